```python
import math
import jax
import jax.numpy as jnp
from jax import lax
import numpy as np

D_MODEL = 2048
BATCH = 2
SEQ = 16384
DEPTH = 2

F32 = jnp.float32
GRID_W = 64
CTX_LEN = 256
HEAD_DIM = 64
ROPE_BASE = 10000.0
NORM_EPS = 1e-6
NEG_INF = -1e30

N_BRANCH = 4
BRANCH_W = D_MODEL // N_BRANCH

HY_CH = BRANCH_W
HY_ORDER = 2
HY_POS_BANDS = 8
HY_POS_FEAT = 1 + 2 * HY_POS_BANDS
HY_FFN = 64
HY_DECAY_MIN = 3.07
HY_DECAY_MAX = 15.35

SWA_HEADS = BRANCH_W // HEAD_DIM
SWA_KV_HEADS = 2
SWA_GROUP = SWA_HEADS // SWA_KV_HEADS
SWA_WINDOW = 128
SWA_BLOCK = 128

RW_HEADS = BRANCH_W // HEAD_DIM
RW_DECAY_RANK = 64
RW_A_RANK = 64
RW_G_RANK = 128
RW_FEAT = 3 * BRANCH_W + RW_DECAY_RANK + RW_A_RANK + RW_G_RANK
RW_SPLITS = (BRANCH_W, 2 * BRANCH_W, 3 * BRANCH_W, 3 * BRANCH_W + RW_DECAY_RANK,
             3 * BRANCH_W + RW_DECAY_RANK + RW_A_RANK)
RW_GN_EPS = 64e-5

DF_HEADS = BRANCH_W // (2 * HEAD_DIM)
DF_BLOCK = 128

MOE_GROUPS = 4
MOE_PER_GROUP = 8
MOE_EXPERTS = MOE_GROUPS * MOE_PER_GROUP
MOE_TOP_K = 2
MOE_HIDDEN = D_MODEL // 2
MOE_BLOCK = 256

A_COLS = 3 * HY_CH
B_COLS = (SWA_HEADS + 2 * SWA_KV_HEADS) * HEAD_DIM
C_COLS = RW_FEAT
D_COLS = 3 * DF_HEADS * 2 * HEAD_DIM
IN_COLS = A_COLS + B_COLS + C_COLS + D_COLS
IN_SPLITS = (A_COLS, A_COLS + B_COLS, A_COLS + B_COLS + C_COLS)

kernel_name = 'hybrid_diffusion_prefix_trunk'


def rms_norm(x, g):
    xf = x.astype(F32)
    y = xf * lax.rsqrt(jnp.mean(xf * xf, axis=-1, keepdims=True) + NORM_EPS)
    return (y * g.astype(F32)).astype(x.dtype)


def modulate(x, g, shift, scale):
    return rms_norm(x, g) * (1 + scale) + shift


def axial_rope_tables(rows):
    n_freq = HEAD_DIM // 4
    inv = ROPE_BASE ** (-jnp.arange(n_freq, dtype=F32) / n_freq)
    row = jnp.repeat(jnp.arange(rows, dtype=F32), GRID_W)
    col = jnp.tile(jnp.arange(GRID_W, dtype=F32), rows)
    ang = jnp.concatenate([row[:, None] * inv, col[:, None] * inv], axis=-1)
    return jnp.cos(ang), jnp.sin(ang)


def apply_rope(x, cos, sin):
    shp = (x.shape[1],) + (1,) * (x.ndim - 3) + (HEAD_DIM // 2,)
    c, s = cos.reshape(shp), sin.reshape(shp)
    x1, x2 = jnp.split(x.astype(F32), 2, axis=-1)
    return jnp.concatenate([x1 * c - x2 * s, x2 * c + x1 * s], axis=-1).astype(x.dtype)


def short_conv3(x, w, b):
    xp = jnp.pad(x, ((0, 0), (1, 1), (0, 0)))
    return xp[:, :-2] * w[0] + xp[:, 1:-1] * w[1] + xp[:, 2:] * w[2] + b


def hyena_filter_spectra(L, f_w1, f_b1, f_w2, f_b2, f_w3, f_b3, f_freq, f_decay):
    t = jnp.arange(L, dtype=F32) / max(L - 1, 1)
    ang = 2 * math.pi * t[:, None] * jnp.arange(1, HY_POS_BANDS + 1, dtype=F32)
    feat = jnp.concatenate([t[:, None], jnp.sin(ang), jnp.cos(ang)], axis=-1)
    freq = f_freq.astype(F32)
    h = jnp.sin(freq * (feat @ f_w1.astype(F32) + f_b1.astype(F32)))
    h = jnp.sin(freq * (h @ f_w2.astype(F32) + f_b2.astype(F32)))
    h = h @ f_w3.astype(F32) + f_b3.astype(F32)
    h = h * jnp.exp(-f_decay.astype(F32) * t[:, None])
    h = h.reshape(L, HY_ORDER, 2, HY_CH)
    h_fwd, h_bwd = h[:, :, 0], h[:, :, 1]
    l1 = jnp.sum(jnp.abs(h_fwd), axis=0) + jnp.sum(jnp.abs(h_bwd[1:]), axis=0)
    kern = jnp.concatenate([h_fwd, jnp.zeros((1, HY_ORDER, HY_CH), F32), h_bwd[1:][::-1]], axis=0) / l1
    return jnp.fft.rfft(kern, axis=0)


def fft_long_conv(z, spec, skip):
    L = z.shape[1]
    zf = jnp.fft.rfft(z, n=2 * L, axis=1)
    y = jnp.fft.irfft(zf * spec[None], n=2 * L, axis=1)[:, :L]
    return y + skip * z


def hyena_branch(pa, conv_w, conv_b, f_w1, f_b1, f_w2, f_b2, f_w3, f_b3, f_freq, f_decay, skip):
    L = pa.shape[1]
    u = short_conv3(pa, conv_w, conv_b).astype(F32)
    v, x1, x2 = jnp.split(u, 3, axis=-1)
    spec = hyena_filter_spectra(L, f_w1, f_b1, f_w2, f_b2, f_w3, f_b3, f_freq, f_decay)
    skip = skip.astype(F32)
    z = x1 * fft_long_conv(v, spec[:, 0], skip[0])
    y = x2 * fft_long_conv(z, spec[:, 1], skip[1])
    return y.astype(pa.dtype)


def swa_project(pb, q_g, k_g):
    B, L = pb.shape[:2]
    q, k, v = jnp.split(pb, [SWA_HEADS * HEAD_DIM, (SWA_HEADS + SWA_KV_HEADS) * HEAD_DIM], axis=-1)
    q = rms_norm(q.reshape(B, L, SWA_HEADS, HEAD_DIM), q_g)
    k = rms_norm(k.reshape(B, L, SWA_KV_HEADS, HEAD_DIM), k_g)
    return q, k, v.reshape(B, L, SWA_KV_HEADS, HEAD_DIM)


def sink_softmax(s, sink):
    m = jnp.maximum(jnp.max(s, axis=-1, keepdims=True), sink)
    p = jnp.exp(s - m)
    return p / (jnp.sum(p, axis=-1, keepdims=True) + jnp.exp(sink - m))


def swa_context(qc, kc, vc, sink):
    B, C = qc.shape[:2]
    qg = qc.reshape(B, C, SWA_KV_HEADS, SWA_GROUP, HEAD_DIM)
    s = jnp.einsum('bqhgd,bkhd->bhgqk', qg, kc).astype(F32) * HEAD_DIM ** -0.5
    p = sink_softmax(s, sink.astype(F32).reshape(1, SWA_KV_HEADS, SWA_GROUP, 1, 1))
    o = jnp.einsum('bhgqk,bkhd->bqhgd', p.astype(vc.dtype), vc)
    return o.reshape(B, C, SWA_HEADS * HEAD_DIM)


def swa_latent(q, k, v, kc, vc, sink):
    B, L = q.shape[:2]
    nb = L // SWA_BLOCK
    qb = q.reshape(B, nb, SWA_BLOCK, SWA_KV_HEADS, SWA_GROUP, HEAD_DIM)

    def band(t):
        tb = t.reshape(B, nb, SWA_BLOCK, SWA_KV_HEADS, HEAD_DIM)
        tp = jnp.pad(tb, ((0, 0), (1, 1), (0, 0), (0, 0), (0, 0)))
        return jnp.concatenate([tp[:, :-2], tp[:, 1:-1], tp[:, 2:]], axis=2)

    kb, vb = band(k), band(v)
    scale = HEAD_DIM ** -0.5
    s_loc = jnp.einsum('bnqhgd,bnkhd->bnhgqk', qb, kb).astype(F32) * scale
    s_ctx = jnp.einsum('bnqhgd,bchd->bnhgqc', qb, kc).astype(F32) * scale
    q_rel = jnp.arange(SWA_BLOCK)[:, None] + SWA_BLOCK
    k_rel = jnp.arange(3 * SWA_BLOCK)[None, :]
    k_abs = (jnp.arange(nb)[:, None, None] - 1) * SWA_BLOCK + k_rel[None]
    valid = (jnp.abs(q_rel - k_rel) <= SWA_WINDOW)[None] & (k_abs >= 0) & (k_abs < L)
    s_loc = jnp.where(valid[None, :, None, None], s_loc, NEG_INF)
    s = jnp.concatenate([s_loc, s_ctx], axis=-1)
    p = sink_softmax(s, sink.astype(F32).reshape(1, 1, SWA_KV_HEADS, SWA_GROUP, 1, 1)).astype(v.dtype)
    o = (jnp.einsum('bnhgqk,bnkhd->bnqhgd', p[..., :3 * SWA_BLOCK], vb)
         + jnp.einsum('bnhgqc,bchd->bnqhgd', p[..., 3 * SWA_BLOCK:], vc))
    return o.reshape(B, L, SWA_HEADS * HEAD_DIM)


def token_shift(x, reverse):
    if reverse:
        return jnp.pad(x, ((0, 0), (0, 1), (0, 0)))[:, 1:]
    return jnp.pad(x, ((0, 0), (1, 0), (0, 0)))[:, :-1]


def head_l2norm(x):
    B, L, C = x.shape
    xh = x.reshape(B, L, RW_HEADS, HEAD_DIM)
    xh = xh * lax.rsqrt(jnp.sum(xh * xh, axis=-1, keepdims=True) + 1e-12)
    return xh.reshape(B, L, C)


def rwkv_prepare(feats, reverse, mu, w0, w2, a0, a2, g2, k_k, k_a):
    xs = feats + mu * (token_shift(feats, reverse) - feats)
    r, k, v, wd, ad, gd = jnp.split(xs, RW_SPLITS, axis=-1)
    decay = jnp.exp(-jnp.exp(-jax.nn.softplus(-(w0 + jnp.tanh(wd) @ w2)) - 0.5))
    a = jax.nn.sigmoid(a0 + ad @ a2)
    g = jax.nn.sigmoid(gd) @ g2
    kk = head_l2norm(k * k_k)
    k = k * (1 + (a - 1) * k_a)
    return r, decay, kk, a, k, v, g


def wkv7_scan(r, decay, kk, a, k, v, state0, reverse):
    B, L = r.shape[:2]

    def to_time(t):
        return jnp.moveaxis(t.reshape(B, L, RW_HEADS, HEAD_DIM), 1, 0)

    xs = (to_time(r), to_time(decay), to_time(-kk), to_time(kk * a), to_time(k), to_time(v))

    def step(S, inp):
        r_t, w_t, nkk_t, kka_t, k_t, v_t = inp
        S = (S * w_t[:, :, None, :]
             + jnp.einsum('bhvk,bhk->bhv', S, nkk_t)[..., None] * kka_t[:, :, None, :]
             + v_t[..., None] * k_t[:, :, None, :])
        return S, jnp.einsum('bhvk,bhk->bhv', S, r_t)

    s_fin, ys = lax.scan(step, state0, xs, reverse=reverse)
    return s_fin, jnp.moveaxis(ys, 0, 1).reshape(B, L, RW_HEADS * HEAD_DIM)


def rwkv_readout(y, r, k, v, g, r_k, ln_w, ln_b):
    B, L, C = y.shape
    yh = y.reshape(B, L, RW_HEADS, HEAD_DIM)
    mean = jnp.mean(yh, axis=-1, keepdims=True)
    var = jnp.mean(jnp.square(yh - mean), axis=-1, keepdims=True)
    yn = ((yh - mean) * lax.rsqrt(var + RW_GN_EPS)).reshape(B, L, C) * ln_w + ln_b
    bonus = jnp.sum((r * k * r_k).reshape(B, L, RW_HEADS, HEAD_DIM), axis=-1, keepdims=True)
    bonus = (bonus * v.reshape(B, L, RW_HEADS, HEAD_DIM)).reshape(B, L, C)
    return (yn + bonus) * g


def rwkv_branch(f_ctx, f_lat, mu, w0, w2, a0, a2, g2, k_k, k_a, r_k, ln_w, ln_b, need_ctx):
    mu, w0, w2, a0, a2, g2, k_k, k_a, r_k, ln_w, ln_b = [
        t.astype(F32) for t in (mu, w0, w2, a0, a2, g2, k_k, k_a, r_k, ln_w, ln_b)]
    f_ctx, f_lat = f_ctx.astype(F32), f_lat.astype(F32)
    B = f_lat.shape[0]
    y_lat = 0.0
    y_ctx = 0.0 if need_ctx else None
    for d in range(2):
        rev = d == 1
        dir_args = (mu[d], w0[d], w2[d], a0[d], a2[d], g2[d], k_k, k_a)
        r_c, w_c, kk_c, a_c, k_c, v_c, g_c = rwkv_prepare(f_ctx, rev, *dir_args)
        s0 = jnp.zeros((B, RW_HEADS, HEAD_DIM, HEAD_DIM), F32)
        s_ctx, o_c = wkv7_scan(r_c, w_c, kk_c, a_c, k_c, v_c, s0, rev)
        r_l, w_l, kk_l, a_l, k_l, v_l, g_l = rwkv_prepare(f_lat, rev, *dir_args)
        _, o_l = wkv7_scan(r_l, w_l, kk_l, a_l, k_l, v_l, s_ctx, rev)
        y_lat = y_lat + rwkv_readout(o_l, r_l, k_l, v_l, g_l, r_k, ln_w, ln_b)
        if need_ctx:
            y_ctx = y_ctx + rwkv_readout(o_c, r_c, k_c, v_c, g_c, r_k, ln_w, ln_b)
    return y_lat, y_ctx


def diff_project(pd, q_g, k_g):
    B, L = pd.shape[:2]
    q, k, v = jnp.split(pd, 3, axis=-1)
    q = rms_norm(q.reshape(B, L, DF_HEADS, 2, HEAD_DIM), q_g)
    k = rms_norm(k.reshape(B, L, DF_HEADS, 2, HEAD_DIM), k_g)
    return q, k, v.reshape(B, L, DF_HEADS, 2 * HEAD_DIM)


def diff_maps(q, k_all, v_all, lam):
    s = jnp.einsum('bqhid,bkhid->bhiqk', q, k_all).astype(F32) * HEAD_DIM ** -0.5
    p = jax.nn.softmax(s, axis=-1)
    a = p[:, :, 0] - lam * p[:, :, 1]
    return jnp.einsum('bhqk,bkhe->bqhe', a.astype(v_all.dtype), v_all)


def diff_latent(q, k, v, kc, vc, lam):
    B, L = q.shape[:2]
    nb = L // DF_BLOCK
    k_all = jnp.concatenate([kc, k], axis=1)
    v_all = jnp.concatenate([vc, v], axis=1)
    qb = jnp.moveaxis(q.reshape(B, nb, DF_BLOCK, DF_HEADS, 2, HEAD_DIM), 1, 0)
    o = lax.map(lambda q_blk: diff_maps(q_blk, k_all, v_all, lam), qb)
    return jnp.moveaxis(o, 0, 1).reshape(B, L, DF_HEADS, 2 * HEAD_DIM)


def diff_readout(o, sub_g, lam_init):
    B, L = o.shape[:2]
    return (rms_norm(o, sub_g) * (1 - lam_init)).reshape(B, L, DF_HEADS * 2 * HEAD_DIM)


def merge_branches(h, ys, w_gate, b_gate, w_branch, w_out):
    acc = 0.0
    for i in range(N_BRANCH):
        gate = jax.nn.sigmoid(h @ w_gate[i] + b_gate[i])
        acc = acc + gate * (ys[i].astype(h.dtype) @ w_branch[i])
    return acc @ w_out


def hier_moe(tokens, rg_w, rg_b, re_w, re_b, w1, w3, w2):
    N, D = tokens.shape
    g_logits = (tokens @ rg_w).astype(F32) + rg_b.astype(F32)
    g_idx = jnp.argmax(g_logits, axis=-1)
    g_prob = jnp.take_along_axis(jax.nn.softmax(g_logits, axis=-1), g_idx[:, None], axis=-1)[:, 0]
    e_logits = ((tokens @ re_w).astype(F32) + re_b.astype(F32)).reshape(N, MOE_GROUPS, MOE_PER_GROUP)
    e_logits = jnp.take_along_axis(e_logits, g_idx[:, None, None], axis=1)[:, 0]
    top_p, top_e = lax.top_k(jax.nn.softmax(e_logits, axis=-1), MOE_TOP_K)
    weights = g_prob[:, None] * top_p / jnp.sum(top_p, axis=-1, keepdims=True)
    flat_e = (g_idx[:, None] * MOE_PER_GROUP + top_e).reshape(-1)
    flat_w = weights.reshape(-1).astype(tokens.dtype)
    flat_tok = jnp.repeat(jnp.arange(N, dtype=jnp.int32), MOE_TOP_K)
    n_assign = N * MOE_TOP_K
    order = jnp.argsort(flat_e)
    se = flat_e[order]
    counts = jnp.bincount(flat_e, length=MOE_EXPERTS)
    padded = (counts + MOE_BLOCK - 1) // MOE_BLOCK * MOE_BLOCK
    pad_end = jnp.cumsum(padded)
    pad_start = pad_end - padded
    start = jnp.cumsum(counts) - counts
    dest = pad_start[se] + jnp.arange(n_assign, dtype=jnp.int32) - start[se]
    n_blocks = -(-n_assign // MOE_BLOCK) + MOE_EXPERTS
    P = n_blocks * MOE_BLOCK
    tok_buf = jnp.full((P,), N, jnp.int32).at[dest].set(flat_tok[order])
    w_buf = jnp.zeros((P,), tokens.dtype).at[dest].set(flat_w[order])
    blk_e = jnp.minimum(jnp.searchsorted(pad_end, jnp.arange(n_blocks) * MOE_BLOCK, side='right'),
                        MOE_EXPERTS - 1)
    x_pad = jnp.concatenate([tokens, jnp.zeros((1, D), tokens.dtype)], axis=0)
    xb = x_pad[tok_buf].reshape(n_blocks, MOE_BLOCK, D)

    def expert_block(args):
        x_blk, e = args
        hdn = jax.nn.silu(x_blk @ w1[e]) * (x_blk @ w3[e])
        return hdn @ w2[e]

    yb = lax.map(expert_block, (xb, blk_e)).reshape(P, D)
    return jax.ops.segment_sum(yb * w_buf[:, None], tok_buf, num_segments=N + 1)[:N]


def setup_inputs(seed: int = 0) -> dict:
    key = jax.random.key(seed)
    keys = iter(jax.random.split(key, 64))

    def nrm(shape, scale):
        return scale * jax.random.normal(next(keys), shape, F32)

    def unif(shape, lo, hi):
        return jax.random.uniform(next(keys), shape, F32, lo, hi)

    D = D_MODEL
    HF = HY_ORDER * 2 * HY_CH
    tap = jnp.array([0.0, 1.0, 0.0], F32)[None, :, None]
    return {
        'x': nrm((BATCH, SEQ, D), 1.0),
        'c': nrm((BATCH, D), 1.0),
        'ctx': nrm((BATCH, CTX_LEN, D), 1.0),
        'c_ctx': nrm((D,), 1.0),
        'mod_w': nrm((DEPTH, D, 6 * D), 0.01),
        'mod_b': nrm((DEPTH, 6 * D), 0.02),
        'norm1_g': 1.0 + nrm((DEPTH, D), 0.05),
        'norm2_g': 1.0 + nrm((DEPTH, D), 0.05),
        'w_in': nrm((DEPTH, D, IN_COLS), D ** -0.5),
        'hy_conv_w': tap + nrm((DEPTH, 3, A_COLS), 0.3),
        'hy_conv_b': nrm((DEPTH, A_COLS), 0.02),
        'hy_f_w1': nrm((DEPTH, HY_POS_FEAT, HY_FFN), HY_POS_FEAT ** -0.5),
        'hy_f_b1': nrm((DEPTH, HY_FFN), 0.1),
        'hy_f_w2': nrm((DEPTH, HY_FFN, HY_FFN), HY_FFN ** -0.5),
        'hy_f_b2': nrm((DEPTH, HY_FFN), 0.1),
        'hy_f_w3': nrm((DEPTH, HY_FFN, HF), HY_FFN ** -0.5),
        'hy_f_b3': nrm((DEPTH, HF), 0.1),
        'hy_f_freq': 1.0 + nrm((DEPTH, HY_FFN), 0.1),
        'hy_f_decay': unif((DEPTH, HF), HY_DECAY_MIN, HY_DECAY_MAX),
        'hy_skip': nrm((DEPTH, HY_ORDER, HY_CH), 0.5),
        'swa_q_g': 1.0 + nrm((DEPTH, HEAD_DIM), 0.05),
        'swa_k_g': 1.0 + nrm((DEPTH, HEAD_DIM), 0.05),
        'swa_sink': nrm((DEPTH, SWA_HEADS), 0.5),
        'rw_mu': unif((DEPTH, 2, RW_FEAT), 0.0, 1.0),
        'rw_w0': unif((DEPTH, 2, BRANCH_W), -3.0, 1.0),
        'rw_w2': nrm((DEPTH, 2, RW_DECAY_RANK, BRANCH_W), 0.05),
        'rw_a0': nrm((DEPTH, 2, BRANCH_W), 0.5),
        'rw_a2': nrm((DEPTH, 2, RW_A_RANK, BRANCH_W), 0.05),
        'rw_g2': nrm((DEPTH, 2, RW_G_RANK, BRANCH_W), RW_G_RANK ** -0.5),
        'rw_k_k': 0.85 + nrm((DEPTH, BRANCH_W), 0.05),
        'rw_k_a': 1.0 + nrm((DEPTH, BRANCH_W), 0.05),
        'rw_r_k': nrm((DEPTH, BRANCH_W), 0.1),
        'rw_ln_w': 1.0 + nrm((DEPTH, BRANCH_W), 0.05),
        'rw_ln_b': nrm((DEPTH, BRANCH_W), 0.02),
        'df_q_g': 1.0 + nrm((DEPTH, HEAD_DIM), 0.05),
        'df_k_g': 1.0 + nrm((DEPTH, HEAD_DIM), 0.05),
        'df_lq1': nrm((DEPTH, HEAD_DIM), 0.1),
        'df_lk1': nrm((DEPTH, HEAD_DIM), 0.1),
        'df_lq2': nrm((DEPTH, HEAD_DIM), 0.1),
        'df_lk2': nrm((DEPTH, HEAD_DIM), 0.1),
        'df_sub_g': 1.0 + nrm((DEPTH, 2 * HEAD_DIM), 0.05),
        'w_gate': nrm((DEPTH, N_BRANCH, D, D), D ** -0.5),
        'b_gate': nrm((DEPTH, N_BRANCH, D), 0.02),
        'w_branch': nrm((DEPTH, N_BRANCH, BRANCH_W, D), BRANCH_W ** -0.5),
        'w_out': nrm((DEPTH, D, D), D ** -0.5),
        'moe_rg_w': nrm((DEPTH, D, MOE_GROUPS), D ** -0.5),
        'moe_rg_b': nrm((DEPTH, MOE_GROUPS), 0.01),
        'moe_re_w': nrm((DEPTH, D, MOE_EXPERTS), D ** -0.5),
        'moe_re_b': nrm((DEPTH, MOE_EXPERTS), 0.01),
        'moe_w1': nrm((DEPTH, MOE_EXPERTS, D, MOE_HIDDEN), D ** -0.5),
        'moe_w3': nrm((DEPTH, MOE_EXPERTS, D, MOE_HIDDEN), D ** -0.5),
        'moe_w2': nrm((DEPTH, MOE_EXPERTS, MOE_HIDDEN, D), MOE_HIDDEN ** -0.5),
    }


def reference(x, c, ctx, c_ctx, mod_w, mod_b, norm1_g, norm2_g, w_in,
              hy_conv_w, hy_conv_b, hy_f_w1, hy_f_b1, hy_f_w2, hy_f_b2, hy_f_w3, hy_f_b3,
              hy_f_freq, hy_f_decay, hy_skip,
              swa_q_g, swa_k_g, swa_sink,
              rw_mu, rw_w0, rw_w2, rw_a0, rw_a2, rw_g2, rw_k_k, rw_k_a, rw_r_k, rw_ln_w, rw_ln_b,
              df_q_g, df_k_g, df_lq1, df_lk1, df_lq2, df_lk2, df_sub_g,
              w_gate, b_gate, w_branch, w_out,
              moe_rg_w, moe_rg_b, moe_re_w, moe_re_b, moe_w1, moe_w3, moe_w2):
    B, L, D = x.shape
    C = ctx.shape[1]
    rows = L // GRID_W
    cos, sin = axial_rope_tables(rows)
    x_lat, x_ctx = x, ctx
    for l in range(DEPTH):
        need_ctx = l < DEPTH - 1
        mod_lat = jax.nn.silu(c) @ mod_w[l] + mod_b[l]
        mod_ctx = jax.nn.silu(c_ctx) @ mod_w[l] + mod_b[l]
        sh1, sc1, g1, sh2, sc2, g2 = jnp.split(mod_lat[:, None, :], 6, axis=-1)
        csh1, csc1, cg1, csh2, csc2, cg2 = jnp.split(mod_ctx, 6, axis=-1)

        h_lat = modulate(x_lat, norm1_g[l], sh1, sc1)
        h_ctx = modulate(x_ctx, norm1_g[l], csh1, csc1)
        pa_l, pb_l, pc_l, pd_l = jnp.split(h_lat @ w_in[l], IN_SPLITS, axis=-1)
        pa_c, pb_c, pc_c, pd_c = jnp.split(h_ctx @ w_in[l], IN_SPLITS, axis=-1)

        hy_args = (hy_conv_w[l], hy_conv_b[l], hy_f_w1[l], hy_f_b1[l], hy_f_w2[l], hy_f_b2[l],
                   hy_f_w3[l], hy_f_b3[l], hy_f_freq[l], hy_f_decay[l], hy_skip[l])
        ya_l = hyena_branch(pa_l, *hy_args)

        q_l, k_l, v_l = swa_project(pb_l, swa_q_g[l], swa_k_g[l])
        q_l, k_l = apply_rope(q_l, cos, sin), apply_rope(k_l, cos, sin)
        q_c, k_c, v_c = swa_project(pb_c, swa_q_g[l], swa_k_g[l])
        yb_l = swa_latent(q_l, k_l, v_l, k_c, v_c, swa_sink[l])

        yc_l, yc_c = rwkv_branch(pc_c, pc_l, rw_mu[l], rw_w0[l], rw_w2[l], rw_a0[l], rw_a2[l],
                                 rw_g2[l], rw_k_k[l], rw_k_a[l], rw_r_k[l], rw_ln_w[l], rw_ln_b[l],
                                 need_ctx)

        lam_init = 0.8 - 0.6 * math.exp(-0.3 * l)
        lam = (jnp.exp(jnp.sum(df_lq1[l].astype(F32) * df_lk1[l].astype(F32)))
               - jnp.exp(jnp.sum(df_lq2[l].astype(F32) * df_lk2[l].astype(F32))) + lam_init)
        dq_l, dk_l, dv_l = diff_project(pd_l, df_q_g[l], df_k_g[l])
        dq_l, dk_l = apply_rope(dq_l, cos, sin), apply_rope(dk_l, cos, sin)
        dq_c, dk_c, dv_c = diff_project(pd_c, df_q_g[l], df_k_g[l])
        yd_l = diff_readout(diff_latent(dq_l, dk_l, dv_l, dk_c, dv_c, lam), df_sub_g[l], lam_init)

        merge_args = (w_gate[l], b_gate[l], w_branch[l], w_out[l])
        x_lat = x_lat + g1 * merge_branches(h_lat, (ya_l, yb_l, yc_l, yd_l), *merge_args)
        if need_ctx:
            ya_c = hyena_branch(pa_c, *hy_args)
            yb_c = swa_context(q_c, k_c, v_c, swa_sink[l])
            yd_c = diff_readout(diff_maps(dq_c, dk_c, dv_c, lam), df_sub_g[l], lam_init)
            x_ctx = x_ctx + cg1 * merge_branches(h_ctx, (ya_c, yb_c, yc_c, yd_c), *merge_args)

        moe_args = (moe_rg_w[l], moe_rg_b[l], moe_re_w[l], moe_re_b[l], moe_w1[l], moe_w3[l], moe_w2[l])
        hm_lat = modulate(x_lat, norm2_g[l], sh2, sc2)
        if need_ctx:
            hm_ctx = modulate(x_ctx, norm2_g[l], csh2, csc2)
            tokens = jnp.concatenate([hm_ctx.reshape(-1, D), hm_lat.reshape(-1, D)], axis=0)
            out = hier_moe(tokens, *moe_args)
            x_ctx = x_ctx + cg2 * out[:B * C].reshape(B, C, D)
            x_lat = x_lat + g2 * out[B * C:].reshape(B, L, D)
        else:
            x_lat = x_lat + g2 * hier_moe(hm_lat.reshape(-1, D), *moe_args).reshape(B, L, D)
    return x_lat
```

```python
import functools
import math

import jax
import jax.numpy as jnp
from jax import lax
from jax.experimental import pallas as pl
from jax.experimental.pallas import tpu as pltpu

F32 = jnp.float32
BF16 = jnp.bfloat16

GRID_W = 64
HEAD_DIM = 64
ROPE_BASE = 10000.0
NORM_EPS = 1e-6
NEG_INF = -1e30
N_BRANCH = 4
HY_ORDER = 2
HY_POS_BANDS = 8
SWA_KV_HEADS = 2
SWA_WINDOW = 128
SWA_BLOCK = 128
RW_DECAY_RANK = 64
RW_A_RANK = 64
RW_G_RANK = 128
RW_GN_EPS = 64e-5
DF_BLOCK = 128
MOE_GROUPS = 4
MOE_PER_GROUP = 8
MOE_EXPERTS = MOE_GROUPS * MOE_PER_GROUP
MOE_TOP_K = 2
MOE_BLOCK = 256

VMEM_LIMIT_BYTES = 56 * 1024 * 1024


def _row_tile(m, pref):
    t = min(pref, m)
    while m % t:
        t //= 2
    return t


def _params(sem):
    return pltpu.CompilerParams(dimension_semantics=sem, vmem_limit_bytes=VMEM_LIMIT_BYTES)


def _norm_proj_body(x_ref, g_ref, sc_ref, sh_ref, w_ref, p_ref, h_ref, h_scr):
    @pl.when(pl.program_id(1) == 0)
    def _():
        x = x_ref[...].astype(F32)
        y = x * lax.rsqrt(jnp.mean(x * x, axis=-1, keepdims=True) + NORM_EPS)
        h = y * g_ref[...] * (1.0 + sc_ref[...]) + sh_ref[...]
        h_scr[...] = h.astype(BF16)
        h_ref[...] = h_scr[...]

    p_ref[...] = jnp.dot(h_scr[...], w_ref[...], preferred_element_type=F32).astype(p_ref.dtype)


def norm_proj(x, g, scale, shift, w, *, rows_per_mod, tm=1024, tn=512, out_dtype=F32):
    m, d = x.shape
    n = w.shape[1]
    tm = _row_tile(rows_per_mod, tm)
    tn = _row_tile(n, tn)
    tiles_per_mod = rows_per_mod // tm
    mod_map = lambda i, j: (i // tiles_per_mod, 0, 0)
    return pl.pallas_call(
        _norm_proj_body,
        grid=(m // tm, n // tn),
        in_specs=[
            pl.BlockSpec((tm, d), lambda i, j: (i, 0)),
            pl.BlockSpec((1, d), lambda i, j: (0, 0)),
            pl.BlockSpec((None, 1, d), mod_map),
            pl.BlockSpec((None, 1, d), mod_map),
            pl.BlockSpec((d, tn), lambda i, j: (0, j)),
        ],
        out_specs=[
            pl.BlockSpec((tm, tn), lambda i, j: (i, j)),
            pl.BlockSpec((tm, d), lambda i, j: (i, 0)),
        ],
        out_shape=[jax.ShapeDtypeStruct((m, n), out_dtype), jax.ShapeDtypeStruct((m, d), BF16)],
        scratch_shapes=[pltpu.VMEM((tm, d), BF16)],
        compiler_params=_params(("parallel", "arbitrary")),
        name="norm_proj",
    )(x, g, scale, shift, w)


def _merge_body(h_ref, y_ref, wg_ref, bg_ref, wb_ref, o_ref, acc_ref):
    b = pl.program_id(2)
    gate = jax.nn.sigmoid(jnp.dot(h_ref[...], wg_ref[...], preferred_element_type=F32) + bg_ref[...])
    val = gate * jnp.dot(y_ref[...], wb_ref[...], preferred_element_type=F32)

    @pl.when(b == 0)
    def _():
        acc_ref[...] = val

    @pl.when(b > 0)
    def _():
        acc_ref[...] += val

    @pl.when(b == N_BRANCH - 1)
    def _():
        o_ref[...] = acc_ref[...].astype(o_ref.dtype)


def merge_gated(h, ys, wg, bg, wb, *, tm=1024, tn=512):
    m, d = h.shape
    w = ys.shape[-1]
    tm = _row_tile(m, tm)
    tn = _row_tile(d, tn)
    return pl.pallas_call(
        _merge_body,
        grid=(m // tm, d // tn, N_BRANCH),
        in_specs=[
            pl.BlockSpec((tm, d), lambda i, j, b: (i, 0)),
            pl.BlockSpec((None, tm, w), lambda i, j, b: (b, i, 0)),
            pl.BlockSpec((None, d, tn), lambda i, j, b: (b, 0, j)),
            pl.BlockSpec((None, 1, tn), lambda i, j, b: (b, 0, j)),
            pl.BlockSpec((None, w, tn), lambda i, j, b: (b, 0, j)),
        ],
        out_specs=pl.BlockSpec((tm, tn), lambda i, j, b: (i, j)),
        out_shape=jax.ShapeDtypeStruct((m, d), BF16),
        scratch_shapes=[pltpu.VMEM((tm, tn), F32)],
        compiler_params=_params(("parallel", "arbitrary", "arbitrary")),
        name="merge_gated",
    )(h, ys, wg, bg, wb)


def _resid_proj_body(x_ref, a_ref, w_ref, gate_ref, o_ref):
    y = jnp.dot(a_ref[...], w_ref[...], preferred_element_type=F32)
    o_ref[...] = (x_ref[...].astype(F32) + gate_ref[...] * y).astype(o_ref.dtype)


def resid_proj(x, a, w, gate, *, rows_per_mod, tm=1024, tn=512):
    m, d = x.shape
    k = a.shape[1]
    tm = _row_tile(rows_per_mod, tm)
    tn = _row_tile(d, tn)
    tiles_per_mod = rows_per_mod // tm
    return pl.pallas_call(
        _resid_proj_body,
        grid=(m // tm, d // tn),
        in_specs=[
            pl.BlockSpec((tm, tn), lambda i, j: (i, j)),
            pl.BlockSpec((tm, k), lambda i, j: (i, 0)),
            pl.BlockSpec((k, tn), lambda i, j: (0, j)),
            pl.BlockSpec((None, 1, tn), lambda i, j: (i // tiles_per_mod, 0, j)),
        ],
        out_specs=pl.BlockSpec((tm, tn), lambda i, j: (i, j)),
        out_shape=jax.ShapeDtypeStruct((m, d), x.dtype),
        compiler_params=_params(("parallel", "arbitrary")),
        name="resid_proj",
    )(x, a, w, gate)


def _moe_body(blk_e_ref, n_used_ref, x_ref, w1_ref, w3_ref, w2_ref, o_ref):
    i = pl.program_id(0)

    @pl.when(i < n_used_ref[0])
    def _():
        x = x_ref[...]
        a = jnp.dot(x, w1_ref[...], preferred_element_type=F32)
        b = jnp.dot(x, w3_ref[...], preferred_element_type=F32)
        hdn = (a * jax.nn.sigmoid(a) * b).astype(BF16)
        o_ref[...] = jnp.dot(hdn, w2_ref[...], preferred_element_type=F32).astype(o_ref.dtype)

    @pl.when(i >= n_used_ref[0])
    def _():
        o_ref[...] = jnp.zeros_like(o_ref)


def moe_experts(xb, blk_e, n_used, w1, w3, w2):
    p, d = xb.shape
    hid = w1.shape[-1]
    n_blocks = p // MOE_BLOCK
    grid_spec = pltpu.PrefetchScalarGridSpec(
        num_scalar_prefetch=2,
        grid=(n_blocks,),
        in_specs=[
            pl.BlockSpec((MOE_BLOCK, d), lambda i, e, n: (i, 0)),
            pl.BlockSpec((None, d, hid), lambda i, e, n: (e[i], 0, 0)),
            pl.BlockSpec((None, d, hid), lambda i, e, n: (e[i], 0, 0)),
            pl.BlockSpec((None, hid, d), lambda i, e, n: (e[i], 0, 0)),
        ],
        out_specs=pl.BlockSpec((MOE_BLOCK, d), lambda i, e, n: (i, 0)),
    )
    return pl.pallas_call(
        _moe_body,
        grid_spec=grid_spec,
        out_shape=jax.ShapeDtypeStruct((p, d), BF16),
        compiler_params=_params(("arbitrary",)),
        name="moe_experts",
    )(blk_e, n_used, xb, w1, w3, w2)


def _rms_norm(x, g):
    xf = x.astype(F32)
    y = xf * lax.rsqrt(jnp.mean(xf * xf, axis=-1, keepdims=True) + NORM_EPS)
    return (y * g.astype(F32)).astype(x.dtype)


def _rope_tables(rows):
    n_freq = HEAD_DIM // 4
    inv = ROPE_BASE ** (-jnp.arange(n_freq, dtype=F32) / n_freq)
    row = jnp.repeat(jnp.arange(rows, dtype=F32), GRID_W)
    col = jnp.tile(jnp.arange(GRID_W, dtype=F32), rows)
    ang = jnp.concatenate([row[:, None] * inv, col[:, None] * inv], axis=-1)
    return jnp.cos(ang), jnp.sin(ang)


def _apply_rope(x, cos, sin):
    shp = (x.shape[1],) + (1,) * (x.ndim - 3) + (HEAD_DIM // 2,)
    c, s = cos.reshape(shp), sin.reshape(shp)
    x1, x2 = jnp.split(x.astype(F32), 2, axis=-1)
    return jnp.concatenate([x1 * c - x2 * s, x2 * c + x1 * s], axis=-1).astype(x.dtype)


def _short_conv3(x, w, b):
    xp = jnp.pad(x, ((0, 0), (1, 1), (0, 0)))
    return xp[:, :-2] * w[0] + xp[:, 1:-1] * w[1] + xp[:, 2:] * w[2] + b


def _hyena_filter_spectra(L, ch, f_w1, f_b1, f_w2, f_b2, f_w3, f_b3, f_freq, f_decay):
    t = jnp.arange(L, dtype=F32) / max(L - 1, 1)
    ang = 2 * math.pi * t[:, None] * jnp.arange(1, HY_POS_BANDS + 1, dtype=F32)
    feat = jnp.concatenate([t[:, None], jnp.sin(ang), jnp.cos(ang)], axis=-1)
    hp = lax.Precision.HIGHEST
    h = jnp.sin(f_freq * (jnp.dot(feat, f_w1, precision=hp) + f_b1))
    h = jnp.sin(f_freq * (jnp.dot(h, f_w2, precision=hp) + f_b2))
    h = jnp.dot(h, f_w3, precision=hp) + f_b3
    h = h * jnp.exp(-f_decay * t[:, None])
    h = h.reshape(L, HY_ORDER, 2, ch)
    h_fwd, h_bwd = h[:, :, 0], h[:, :, 1]
    l1 = jnp.sum(jnp.abs(h_fwd), axis=0) + jnp.sum(jnp.abs(h_bwd[1:]), axis=0)
    kern = jnp.concatenate([h_fwd, jnp.zeros((1, HY_ORDER, ch), F32), h_bwd[1:][::-1]], axis=0) / l1
    return jnp.fft.rfft(kern, axis=0)


def _fft_long_conv(z, spec, skip):
    L = z.shape[1]
    zf = jnp.fft.rfft(z, n=2 * L, axis=1)
    y = jnp.fft.irfft(zf * spec[None], n=2 * L, axis=1)[:, :L]
    return y + skip * z


def _hyena_branch(pa, conv_w, conv_b, spec, skip):
    u = _short_conv3(pa, conv_w, conv_b).astype(F32)
    v, x1, x2 = jnp.split(u, 3, axis=-1)
    z = x1 * _fft_long_conv(v, spec[:, 0], skip[0])
    return x2 * _fft_long_conv(z, spec[:, 1], skip[1])


def _swa_project(pb, q_g, k_g, n_heads):
    B, L = pb.shape[:2]
    q, k, v = jnp.split(pb, [n_heads * HEAD_DIM, (n_heads + SWA_KV_HEADS) * HEAD_DIM], axis=-1)
    q = _rms_norm(q.reshape(B, L, n_heads, HEAD_DIM), q_g)
    k = _rms_norm(k.reshape(B, L, SWA_KV_HEADS, HEAD_DIM), k_g)
    return q, k, v.reshape(B, L, SWA_KV_HEADS, HEAD_DIM)


def _sink_softmax(s, sink):
    m = jnp.maximum(jnp.max(s, axis=-1, keepdims=True), sink)
    p = jnp.exp(s - m)
    return p / (jnp.sum(p, axis=-1, keepdims=True) + jnp.exp(sink - m))


def _swa_context(qc, kc, vc, sink):
    B, C, n_heads = qc.shape[:3]
    grp = n_heads // SWA_KV_HEADS
    qg = qc.reshape(B, C, SWA_KV_HEADS, grp, HEAD_DIM)
    s = jnp.einsum('bqhgd,bkhd->bhgqk', qg, kc).astype(F32) * HEAD_DIM ** -0.5
    p = _sink_softmax(s, sink.astype(F32).reshape(1, SWA_KV_HEADS, grp, 1, 1))
    o = jnp.einsum('bhgqk,bkhd->bqhgd', p.astype(vc.dtype), vc)
    return o.reshape(B, C, n_heads * HEAD_DIM)


def _swa_latent(q, k, v, kc, vc, sink):
    B, L, n_heads = q.shape[:3]
    grp = n_heads // SWA_KV_HEADS
    nb = L // SWA_BLOCK
    qb = q.reshape(B, nb, SWA_BLOCK, SWA_KV_HEADS, grp, HEAD_DIM)

    def band(t):
        tb = t.reshape(B, nb, SWA_BLOCK, SWA_KV_HEADS, HEAD_DIM)
        tp = jnp.pad(tb, ((0, 0), (1, 1), (0, 0), (0, 0), (0, 0)))
        return jnp.concatenate([tp[:, :-2], tp[:, 1:-1], tp[:, 2:]], axis=2)

    kb, vb = band(k), band(v)
    scale = HEAD_DIM ** -0.5
    s_loc = jnp.einsum('bnqhgd,bnkhd->bnhgqk', qb, kb).astype(F32) * scale
    s_ctx = jnp.einsum('bnqhgd,bchd->bnhgqc', qb, kc).astype(F32) * scale
    q_rel = jnp.arange(SWA_BLOCK)[:, None] + SWA_BLOCK
    k_rel = jnp.arange(3 * SWA_BLOCK)[None, :]
    k_abs = (jnp.arange(nb)[:, None, None] - 1) * SWA_BLOCK + k_rel[None]
    valid = (jnp.abs(q_rel - k_rel) <= SWA_WINDOW)[None] & (k_abs >= 0) & (k_abs < L)
    s_loc = jnp.where(valid[None, :, None, None], s_loc, NEG_INF)
    s = jnp.concatenate([s_loc, s_ctx], axis=-1)
    p = _sink_softmax(s, sink.astype(F32).reshape(1, 1, SWA_KV_HEADS, grp, 1, 1)).astype(v.dtype)
    o = (jnp.einsum('bnhgqk,bnkhd->bnqhgd', p[..., :3 * SWA_BLOCK], vb)
         + jnp.einsum('bnhgqc,bchd->bnqhgd', p[..., 3 * SWA_BLOCK:], vc))
    return o.reshape(B, L, n_heads * HEAD_DIM)


def _token_shift(x, reverse):
    if reverse:
        return jnp.pad(x, ((0, 0), (0, 1), (0, 0)))[:, 1:]
    return jnp.pad(x, ((0, 0), (1, 0), (0, 0)))[:, :-1]


def _head_l2norm(x, n_heads):
    B, L, C = x.shape
    xh = x.reshape(B, L, n_heads, HEAD_DIM)
    xh = xh * lax.rsqrt(jnp.sum(xh * xh, axis=-1, keepdims=True) + 1e-12)
    return xh.reshape(B, L, C)


def _rwkv_prepare(feats, reverse, width, mu, w0, w2, a0, a2, g2, k_k, k_a):
    n_heads = width // HEAD_DIM
    xs = feats + mu * (_token_shift(feats, reverse) - feats)
    splits = (width, 2 * width, 3 * width, 3 * width + RW_DECAY_RANK, 3 * width + RW_DECAY_RANK + RW_A_RANK)
    r, k, v, wd, ad, gd = jnp.split(xs, splits, axis=-1)
    decay = jnp.exp(-jnp.exp(-jax.nn.softplus(-(w0 + jnp.tanh(wd) @ w2)) - 0.5))
    a = jax.nn.sigmoid(a0 + ad @ a2)
    g = jax.nn.sigmoid(gd) @ g2
    kk = _head_l2norm(k * k_k, n_heads)
    k = k * (1 + (a - 1) * k_a)
    return r, decay, kk, a, k, v, g


def _wkv7_scan(r, decay, kk, a, k, v, state0, reverse):
    B, L, width = r.shape
    n_heads = width // HEAD_DIM

    def to_time(t):
        return jnp.moveaxis(t.reshape(B, L, n_heads, HEAD_DIM), 1, 0)

    xs = (to_time(r), to_time(decay), to_time(-kk), to_time(kk * a), to_time(k), to_time(v))

    def step(S, inp):
        r_t, w_t, nkk_t, kka_t, k_t, v_t = inp
        S = (S * w_t[:, :, None, :]
             + jnp.einsum('bhvk,bhk->bhv', S, nkk_t)[..., None] * kka_t[:, :, None, :]
             + v_t[..., None] * k_t[:, :, None, :])
        return S, jnp.einsum('bhvk,bhk->bhv', S, r_t)

    s_fin, ys = lax.scan(step, state0, xs, reverse=reverse)
    return s_fin, jnp.moveaxis(ys, 0, 1).reshape(B, L, width)


def _rwkv_readout(y, r, k, v, g, r_k, ln_w, ln_b):
    B, L, C = y.shape
    n_heads = C // HEAD_DIM
    yh = y.reshape(B, L, n_heads, HEAD_DIM)
    mean = jnp.mean(yh, axis=-1, keepdims=True)
    var = jnp.mean(jnp.square(yh - mean), axis=-1, keepdims=True)
    yn = ((yh - mean) * lax.rsqrt(var + RW_GN_EPS)).reshape(B, L, C) * ln_w + ln_b
    bonus = jnp.sum((r * k * r_k).reshape(B, L, n_heads, HEAD_DIM), axis=-1, keepdims=True)
    bonus = (bonus * v.reshape(B, L, n_heads, HEAD_DIM)).reshape(B, L, C)
    return (yn + bonus) * g


def _rwkv_branch(f_ctx, f_lat, width, mu, w0, w2, a0, a2, g2, k_k, k_a, r_k, ln_w, ln_b, need_ctx):
    B = f_lat.shape[0]
    n_heads = width // HEAD_DIM
    y_lat = 0.0
    y_ctx = 0.0 if need_ctx else None
    for d in range(2):
        rev = d == 1
        dir_args = (width, mu[d], w0[d], w2[d], a0[d], a2[d], g2[d], k_k, k_a)
        r_c, w_c, kk_c, a_c, k_c, v_c, g_c = _rwkv_prepare(f_ctx, rev, *dir_args)
        s0 = jnp.zeros((B, n_heads, HEAD_DIM, HEAD_DIM), F32)
        s_ctx, o_c = _wkv7_scan(r_c, w_c, kk_c, a_c, k_c, v_c, s0, rev)
        r_l, w_l, kk_l, a_l, k_l, v_l, g_l = _rwkv_prepare(f_lat, rev, *dir_args)
        _, o_l = _wkv7_scan(r_l, w_l, kk_l, a_l, k_l, v_l, s_ctx, rev)
        y_lat = y_lat + _rwkv_readout(o_l, r_l, k_l, v_l, g_l, r_k, ln_w, ln_b)
        if need_ctx:
            y_ctx = y_ctx + _rwkv_readout(o_c, r_c, k_c, v_c, g_c, r_k, ln_w, ln_b)
    return y_lat, y_ctx


def _diff_project(pd, q_g, k_g):
    B, L, w3 = pd.shape
    n_heads = w3 // (3 * 2 * HEAD_DIM)
    q, k, v = jnp.split(pd, 3, axis=-1)
    q = _rms_norm(q.reshape(B, L, n_heads, 2, HEAD_DIM), q_g)
    k = _rms_norm(k.reshape(B, L, n_heads, 2, HEAD_DIM), k_g)
    return q, k, v.reshape(B, L, n_heads, 2 * HEAD_DIM)


def _diff_maps(q, k_all, v_all, lam):
    s = jnp.einsum('bqhid,bkhid->bhiqk', q, k_all).astype(F32) * HEAD_DIM ** -0.5
    p = jax.nn.softmax(s, axis=-1)
    a = p[:, :, 0] - lam * p[:, :, 1]
    return jnp.einsum('bhqk,bkhe->bqhe', a.astype(v_all.dtype), v_all)


def _diff_latent(q, k, v, kc, vc, lam):
    B, L, n_heads = q.shape[:3]
    nb = L // DF_BLOCK
    k_all = jnp.concatenate([kc, k], axis=1)
    v_all = jnp.concatenate([vc, v], axis=1)
    qb = jnp.moveaxis(q.reshape(B, nb, DF_BLOCK, n_heads, 2, HEAD_DIM), 1, 0)
    o = lax.map(lambda q_blk: _diff_maps(q_blk, k_all, v_all, lam), qb)
    return jnp.moveaxis(o, 0, 1).reshape(B, L, n_heads, 2 * HEAD_DIM)


def _diff_readout(o, sub_g, lam_init):
    B, L = o.shape[:2]
    return (_rms_norm(o, sub_g) * (1 - lam_init)).reshape(B, L, -1)


def _moe_route(logits_g, logits_e, n_tok):
    g_idx = jnp.argmax(logits_g, axis=-1)
    g_prob = jnp.take_along_axis(jax.nn.softmax(logits_g, axis=-1), g_idx[:, None], axis=-1)[:, 0]
    e_logits = logits_e.reshape(n_tok, MOE_GROUPS, MOE_PER_GROUP)
    e_logits = jnp.take_along_axis(e_logits, g_idx[:, None, None], axis=1)[:, 0]
    top_p, top_e = lax.top_k(jax.nn.softmax(e_logits, axis=-1), MOE_TOP_K)
    weights = g_prob[:, None] * top_p / jnp.sum(top_p, axis=-1, keepdims=True)
    flat_e = (g_idx[:, None] * MOE_PER_GROUP + top_e).reshape(-1).astype(jnp.int32)
    flat_tok = jnp.repeat(jnp.arange(n_tok, dtype=jnp.int32), MOE_TOP_K)
    n_assign = n_tok * MOE_TOP_K
    order = jnp.argsort(flat_e)
    se = flat_e[order]
    counts = jnp.bincount(flat_e, length=MOE_EXPERTS)
    padded = (counts + MOE_BLOCK - 1) // MOE_BLOCK * MOE_BLOCK
    pad_end = jnp.cumsum(padded)
    pad_start = pad_end - padded
    start = jnp.cumsum(counts) - counts
    dest = (pad_start[se] + jnp.arange(n_assign, dtype=jnp.int32) - start[se]).astype(jnp.int32)
    n_blocks = -(-n_assign // MOE_BLOCK) + MOE_EXPERTS
    P = n_blocks * MOE_BLOCK
    tok_buf = jnp.full((P,), n_tok, jnp.int32).at[dest].set(flat_tok[order])
    blk_e = jnp.minimum(jnp.searchsorted(pad_end, jnp.arange(n_blocks) * MOE_BLOCK, side='right'),
                        MOE_EXPERTS - 1).astype(jnp.int32)
    n_used = (pad_end[-1] // MOE_BLOCK).astype(jnp.int32).reshape(1)
    slot = jnp.zeros((n_assign,), jnp.int32).at[order].set(dest).reshape(n_tok, MOE_TOP_K)
    return tok_buf, blk_e, n_used, slot, weights


def _hier_moe(tokens_bf16, tokens_f32, rg_w, rg_b, re_w, re_b, w1, w3, w2):
    n_tok, d = tokens_bf16.shape
    hp = lax.Precision.HIGHEST
    logits_g = jnp.dot(tokens_f32, rg_w, precision=hp) + rg_b
    logits_e = jnp.dot(tokens_f32, re_w, precision=hp) + re_b
    tok_buf, blk_e, n_used, slot, weights = _moe_route(logits_g, logits_e, n_tok)
    x_pad = jnp.concatenate([tokens_bf16, jnp.zeros((1, d), BF16)], axis=0)
    xb = x_pad[tok_buf]
    yb = moe_experts(xb, blk_e, n_used, w1, w3, w2)
    wts = weights.astype(F32)
    return (yb[slot[:, 0]].astype(F32) * wts[:, 0:1] + yb[slot[:, 1]].astype(F32) * wts[:, 1:2])


def _modulate(x, g, shift, scale):
    return _rms_norm(x, g) * (1 + scale) + shift


def kernel(x, c, ctx, c_ctx, mod_w, mod_b, norm1_g, norm2_g, w_in, hy_conv_w, hy_conv_b, hy_f_w1, hy_f_b1, hy_f_w2, hy_f_b2, hy_f_w3, hy_f_b3, hy_f_freq, hy_f_decay, hy_skip, swa_q_g, swa_k_g, swa_sink, rw_mu, rw_w0, rw_w2, rw_a0, rw_a2, rw_g2, rw_k_k, rw_k_a, rw_r_k, rw_ln_w, rw_ln_b, df_q_g, df_k_g, df_lq1, df_lk1, df_lq2, df_lk2, df_sub_g, w_gate, b_gate, w_branch, w_out, moe_rg_w, moe_rg_b, moe_re_w, moe_re_b, moe_w1, moe_w3, moe_w2):
    B, L, D = x.shape
    C = ctx.shape[1]
    depth = mod_w.shape[0]
    bw = D // N_BRANCH
    a_cols = 3 * bw
    swa_heads = bw // HEAD_DIM
    b_cols = (swa_heads + 2 * SWA_KV_HEADS) * HEAD_DIM
    c_cols = 3 * bw + RW_DECAY_RANK + RW_A_RANK + RW_G_RANK
    in_splits = (a_cols, a_cols + b_cols, a_cols + b_cols + c_cols)
    rows = L // GRID_W
    cos, sin = _rope_tables(rows)
    hp = lax.Precision.HIGHEST

    x_lat = x.reshape(B * L, D)
    x_ctx = ctx.reshape(B * C, D)
    for l in range(depth):
        need_ctx = l < depth - 1
        mod_lat = jnp.dot(jax.nn.silu(c), mod_w[l], precision=hp) + mod_b[l]
        mod_ctx = jnp.dot(jax.nn.silu(c_ctx)[None], mod_w[l], precision=hp) + mod_b[l]
        sh1, sc1, g1, sh2, sc2, g2 = [t[:, None, :] for t in jnp.split(mod_lat, 6, axis=-1)]
        csh1, csc1, cg1, csh2, csc2, cg2 = [t[:, None, :] for t in jnp.split(mod_ctx, 6, axis=-1)]

        w_in_b = w_in[l].astype(BF16)
        p_lat, h_lat = norm_proj(x_lat, norm1_g[l][None], sc1, sh1, w_in_b, rows_per_mod=L)
        p_ctx, h_ctx = norm_proj(x_ctx, norm1_g[l][None], csc1, csh1, w_in_b, rows_per_mod=B * C)
        p_lat = p_lat.reshape(B, L, -1)
        p_ctx = p_ctx.reshape(B, C, -1)
        pa_l, pb_l, pc_l, pd_l = jnp.split(p_lat, in_splits, axis=-1)
        pa_c, pb_c, pc_c, pd_c = jnp.split(p_ctx, in_splits, axis=-1)

        spec_l = _hyena_filter_spectra(L, bw, hy_f_w1[l], hy_f_b1[l], hy_f_w2[l], hy_f_b2[l], hy_f_w3[l],
                                       hy_f_b3[l], hy_f_freq[l], hy_f_decay[l])
        ya_l = _hyena_branch(pa_l, hy_conv_w[l], hy_conv_b[l], spec_l, hy_skip[l])

        q_l, k_l, v_l = _swa_project(pb_l, swa_q_g[l], swa_k_g[l], swa_heads)
        q_l, k_l = _apply_rope(q_l, cos, sin), _apply_rope(k_l, cos, sin)
        q_c, k_c, v_c = _swa_project(pb_c, swa_q_g[l], swa_k_g[l], swa_heads)
        yb_l = _swa_latent(q_l, k_l, v_l, k_c, v_c, swa_sink[l])

        yc_l, yc_c = _rwkv_branch(pc_c, pc_l, bw, rw_mu[l], rw_w0[l], rw_w2[l], rw_a0[l], rw_a2[l],
                                  rw_g2[l], rw_k_k[l], rw_k_a[l], rw_r_k[l], rw_ln_w[l], rw_ln_b[l], need_ctx)

        lam_init = 0.8 - 0.6 * math.exp(-0.3 * l)
        lam = (jnp.exp(jnp.sum(df_lq1[l] * df_lk1[l])) - jnp.exp(jnp.sum(df_lq2[l] * df_lk2[l])) + lam_init)
        dq_l, dk_l, dv_l = _diff_project(pd_l, df_q_g[l], df_k_g[l])
        dq_l, dk_l = _apply_rope(dq_l, cos, sin), _apply_rope(dk_l, cos, sin)
        dq_c, dk_c, dv_c = _diff_project(pd_c, df_q_g[l], df_k_g[l])
        yd_l = _diff_readout(_diff_latent(dq_l, dk_l, dv_l, dk_c, dv_c, lam), df_sub_g[l], lam_init)

        wg_b = w_gate[l].astype(BF16)
        bg = b_gate[l][:, None, :]
        wb_b = w_branch[l].astype(BF16)
        wo_b = w_out[l].astype(BF16)
        ys_l = jnp.stack([t.reshape(B * L, bw).astype(BF16) for t in (ya_l, yb_l, yc_l, yd_l)])
        acc_l = merge_gated(h_lat, ys_l, wg_b, bg, wb_b)
        x_lat = resid_proj(x_lat, acc_l, wo_b, g1, rows_per_mod=L)
        if need_ctx:
            spec_c = _hyena_filter_spectra(C, bw, hy_f_w1[l], hy_f_b1[l], hy_f_w2[l], hy_f_b2[l], hy_f_w3[l],
                                           hy_f_b3[l], hy_f_freq[l], hy_f_decay[l])
            ya_c = _hyena_branch(pa_c, hy_conv_w[l], hy_conv_b[l], spec_c, hy_skip[l])
            yb_c = _swa_context(q_c, k_c, v_c, swa_sink[l])
            yd_c = _diff_readout(_diff_maps(dq_c, dk_c, dv_c, lam), df_sub_g[l], lam_init)
            ys_c = jnp.stack([t.reshape(B * C, bw).astype(BF16) for t in (ya_c, yb_c, yc_c, yd_c)])
            acc_c = merge_gated(h_ctx, ys_c, wg_b, bg, wb_b)
            x_ctx = resid_proj(x_ctx, acc_c, wo_b, cg1, rows_per_mod=B * C)

        w1_b, w3_b, w2_b = moe_w1[l].astype(BF16), moe_w3[l].astype(BF16), moe_w2[l].astype(BF16)
        moe_args = (moe_rg_w[l], moe_rg_b[l], moe_re_w[l], moe_re_b[l], w1_b, w3_b, w2_b)
        hm_lat = _modulate(x_lat.reshape(B, L, D), norm2_g[l], sh2, sc2).reshape(B * L, D)
        if need_ctx:
            hm_ctx = _modulate(x_ctx.reshape(B, C, D), norm2_g[l], csh2, csc2).reshape(B * C, D)
            tokens = jnp.concatenate([hm_ctx, hm_lat], axis=0)
            out = _hier_moe(tokens.astype(BF16), tokens, *moe_args)
            x_ctx = x_ctx + (cg2 * out[:B * C].reshape(1, B * C, D)).reshape(B * C, D)
            x_lat = x_lat + (g2 * out[B * C:].reshape(B, L, D)).reshape(B * L, D)
        else:
            out = _hier_moe(hm_lat.astype(BF16), hm_lat, *moe_args)
            x_lat = x_lat + (g2 * out.reshape(B, L, D)).reshape(B * L, D)
    return x_lat.reshape(B, L, D)
```

```python
import functools
import math

import jax
import jax.numpy as jnp
from jax import lax
from jax.experimental import pallas as pl
from jax.experimental.pallas import tpu as pltpu

F32 = jnp.float32
BF16 = jnp.bfloat16

GRID_W = 64
HEAD_DIM = 64
ROPE_BASE = 10000.0
NORM_EPS = 1e-6
NEG_INF = -1e30
N_BRANCH = 4
HY_ORDER = 2
HY_POS_BANDS = 8
SWA_KV_HEADS = 2
SWA_WINDOW = 128
SWA_BLOCK = 128
RW_DECAY_RANK = 64
RW_A_RANK = 64
RW_G_RANK = 128
RW_GN_EPS = 64e-5
DF_BLOCK = 128
MOE_GROUPS = 4
MOE_PER_GROUP = 8
MOE_EXPERTS = MOE_GROUPS * MOE_PER_GROUP
MOE_TOP_K = 2
MOE_BLOCK = 256

VMEM_LIMIT_BYTES = 56 * 1024 * 1024


def _row_tile(m, pref):
    t = min(pref, m)
    while m % t:
        t //= 2
    return t


def _params(sem):
    return pltpu.CompilerParams(dimension_semantics=sem, vmem_limit_bytes=VMEM_LIMIT_BYTES)


def _norm_proj_body(x_ref, g_ref, sc_ref, sh_ref, w_ref, p_ref, h_ref, h_scr):
    @pl.when(pl.program_id(1) == 0)
    def _():
        x = x_ref[...].astype(F32)
        y = x * lax.rsqrt(jnp.mean(x * x, axis=-1, keepdims=True) + NORM_EPS)
        h = y * g_ref[...] * (1.0 + sc_ref[...]) + sh_ref[...]
        h_scr[...] = h.astype(BF16)
        h_ref[...] = h_scr[...]

    p_ref[...] = jnp.dot(h_scr[...], w_ref[...], preferred_element_type=F32).astype(p_ref.dtype)


def norm_proj(x, g, scale, shift, w, *, rows_per_mod, tm=1024, tn=512, out_dtype=F32):
    m, d = x.shape
    n = w.shape[1]
    tm = _row_tile(rows_per_mod, tm)
    tn = _row_tile(n, tn)
    tiles_per_mod = rows_per_mod // tm
    mod_map = lambda i, j: (i // tiles_per_mod, 0, 0)
    return pl.pallas_call(
        _norm_proj_body,
        grid=(m // tm, n // tn),
        in_specs=[
            pl.BlockSpec((tm, d), lambda i, j: (i, 0)),
            pl.BlockSpec((1, d), lambda i, j: (0, 0)),
            pl.BlockSpec((None, 1, d), mod_map),
            pl.BlockSpec((None, 1, d), mod_map),
            pl.BlockSpec((d, tn), lambda i, j: (0, j)),
        ],
        out_specs=[
            pl.BlockSpec((tm, tn), lambda i, j: (i, j)),
            pl.BlockSpec((tm, d), lambda i, j: (i, 0)),
        ],
        out_shape=[jax.ShapeDtypeStruct((m, n), out_dtype), jax.ShapeDtypeStruct((m, d), BF16)],
        scratch_shapes=[pltpu.VMEM((tm, d), BF16)],
        compiler_params=_params(("parallel", "arbitrary")),
        name="norm_proj",
    )(x, g, scale, shift, w)


def _merge_body(h_ref, y_ref, wg_ref, bg_ref, wb_ref, o_ref, acc_ref):
    b = pl.program_id(2)
    gate = jax.nn.sigmoid(jnp.dot(h_ref[...], wg_ref[...], preferred_element_type=F32) + bg_ref[...])
    val = gate * jnp.dot(y_ref[...], wb_ref[...], preferred_element_type=F32)

    @pl.when(b == 0)
    def _():
        acc_ref[...] = val

    @pl.when(b > 0)
    def _():
        acc_ref[...] += val

    @pl.when(b == N_BRANCH - 1)
    def _():
        o_ref[...] = acc_ref[...].astype(o_ref.dtype)


def merge_gated(h, ys, wg, bg, wb, *, tm=1024, tn=512):
    m, d = h.shape
    w = ys.shape[-1]
    tm = _row_tile(m, tm)
    tn = _row_tile(d, tn)
    return pl.pallas_call(
        _merge_body,
        grid=(m // tm, d // tn, N_BRANCH),
        in_specs=[
            pl.BlockSpec((tm, d), lambda i, j, b: (i, 0)),
            pl.BlockSpec((None, tm, w), lambda i, j, b: (b, i, 0)),
            pl.BlockSpec((None, d, tn), lambda i, j, b: (b, 0, j)),
            pl.BlockSpec((None, 1, tn), lambda i, j, b: (b, 0, j)),
            pl.BlockSpec((None, w, tn), lambda i, j, b: (b, 0, j)),
        ],
        out_specs=pl.BlockSpec((tm, tn), lambda i, j, b: (i, j)),
        out_shape=jax.ShapeDtypeStruct((m, d), BF16),
        scratch_shapes=[pltpu.VMEM((tm, tn), F32)],
        compiler_params=_params(("parallel", "arbitrary", "arbitrary")),
        name="merge_gated",
    )(h, ys, wg, bg, wb)


def _resid_proj_body(x_ref, a_ref, w_ref, gate_ref, o_ref):
    y = jnp.dot(a_ref[...], w_ref[...], preferred_element_type=F32)
    o_ref[...] = (x_ref[...].astype(F32) + gate_ref[...] * y).astype(o_ref.dtype)


def resid_proj(x, a, w, gate, *, rows_per_mod, tm=1024, tn=512):
    m, d = x.shape
    k = a.shape[1]
    tm = _row_tile(rows_per_mod, tm)
    tn = _row_tile(d, tn)
    tiles_per_mod = rows_per_mod // tm
    return pl.pallas_call(
        _resid_proj_body,
        grid=(m // tm, d // tn),
        in_specs=[
            pl.BlockSpec((tm, tn), lambda i, j: (i, j)),
            pl.BlockSpec((tm, k), lambda i, j: (i, 0)),
            pl.BlockSpec((k, tn), lambda i, j: (0, j)),
            pl.BlockSpec((None, 1, tn), lambda i, j: (i // tiles_per_mod, 0, j)),
        ],
        out_specs=pl.BlockSpec((tm, tn), lambda i, j: (i, j)),
        out_shape=jax.ShapeDtypeStruct((m, d), x.dtype),
        compiler_params=_params(("parallel", "arbitrary")),
        name="resid_proj",
    )(x, a, w, gate)


def _moe_body(blk_e_ref, n_used_ref, x_ref, w1_ref, w3_ref, w2_ref, o_ref):
    i = pl.program_id(0)

    @pl.when(i < n_used_ref[0])
    def _():
        x = x_ref[...]
        a = jnp.dot(x, w1_ref[...], preferred_element_type=F32)
        b = jnp.dot(x, w3_ref[...], preferred_element_type=F32)
        hdn = (a * jax.nn.sigmoid(a) * b).astype(BF16)
        o_ref[...] = jnp.dot(hdn, w2_ref[...], preferred_element_type=F32).astype(o_ref.dtype)

    @pl.when(i >= n_used_ref[0])
    def _():
        o_ref[...] = jnp.zeros_like(o_ref)


def moe_experts(xb, blk_e, n_used, w1, w3, w2):
    p, d = xb.shape
    hid = w1.shape[-1]
    n_blocks = p // MOE_BLOCK
    grid_spec = pltpu.PrefetchScalarGridSpec(
        num_scalar_prefetch=2,
        grid=(n_blocks,),
        in_specs=[
            pl.BlockSpec((MOE_BLOCK, d), lambda i, e, n: (i, 0)),
            pl.BlockSpec((None, d, hid), lambda i, e, n: (e[i], 0, 0)),
            pl.BlockSpec((None, d, hid), lambda i, e, n: (e[i], 0, 0)),
            pl.BlockSpec((None, hid, d), lambda i, e, n: (e[i], 0, 0)),
        ],
        out_specs=pl.BlockSpec((MOE_BLOCK, d), lambda i, e, n: (i, 0)),
    )
    return pl.pallas_call(
        _moe_body,
        grid_spec=grid_spec,
        out_shape=jax.ShapeDtypeStruct((p, d), BF16),
        compiler_params=_params(("arbitrary",)),
        name="moe_experts",
    )(blk_e, n_used, xb, w1, w3, w2)


RW_CHUNK = 64
_HI = lax.Precision.HIGHEST
_NT = (((1,), (1,)), ((), ()))
_TN = (((0,), (0,)), ((), ()))


def _dot(a, b, dims=None):
    if dims is None:
        return jnp.dot(a, b, preferred_element_type=F32, precision=_HI)
    return lax.dot_general(a, b, dims, preferred_element_type=F32, precision=_HI)


def _rwkv_prep_body(r_ref, lw_ref, kk_ref, a_ref, k_ref, v_ref, p_ref, g_ref, q_ref, yl_ref, *, reverse):
    T = r_ref.shape[0]
    hd = HEAD_DIM
    row = lax.broadcasted_iota(jnp.int32, (T, T), 0)
    col = lax.broadcasted_iota(jnp.int32, (T, T), 1)
    if reverse:
        strict, incl = col > row, col >= row
    else:
        strict, incl = col < row, col <= row
    lw = lw_ref[...]
    cum = _dot(incl.astype(F32), lw)
    total = jnp.sum(lw, axis=0, keepdims=True)
    e_in = jnp.exp(cum)
    e_ex = jnp.exp(cum - lw)
    e_ninv = jnp.exp(-cum)
    e_rem = jnp.exp(total - cum)
    gam = jnp.exp(total)
    kk = kk_ref[...]
    kka = kk * a_ref[...]
    k = k_ref[...]
    nt = -kk * e_ex
    rt = r_ref[...] * e_in
    at = kka * e_ninv
    kt = k * e_ninv
    ac = kka * e_rem
    kc = k * e_rem
    v = v_ref[...]
    eye = lax.broadcasted_iota(jnp.int32, (hd, hd), 0) == lax.broadcasted_iota(jnp.int32, (hd, hd), 1)
    n_heads = r_ref.shape[1] // hd
    q_out, yl_out = [], []
    for hh in range(n_heads):
        sl = slice(hh * hd, (hh + 1) * hd)
        n_h, r_h, a_h, k_h, ac_h, kc_h, v_h = nt[:, sl], rt[:, sl], at[:, sl], kt[:, sl], ac[:, sl], kc[:, sl], v[:, sl]
        a_aa = jnp.where(strict, _dot(n_h, a_h, _NT), 0.0)
        a_ak = jnp.where(strict, _dot(n_h, k_h, _NT), 0.0)
        a_ra = jnp.where(incl, _dot(r_h, a_h, _NT), 0.0)
        a_rk = jnp.where(incl, _dot(r_h, k_h, _NT), 0.0)
        x = jnp.concatenate([n_h, _dot(a_ak, v_h)], axis=1)
        pw = a_aa
        steps = max(1, (T - 1).bit_length())
        for it in range(steps):
            x = x + _dot(pw, x)
            if it < steps - 1:
                pw = _dot(pw, pw)
        nh, bu = x[:, :hd], x[:, hd:]
        q_out.append(r_h + _dot(a_ra, nh))
        yl_out.append(_dot(a_ra, bu) + _dot(a_rk, v_h))
        p_ref[hh] = jnp.where(eye, gam[:, sl], 0.0) + _dot(ac_h, nh, _TN)
        g_ref[hh] = _dot(ac_h, bu, _TN) + _dot(kc_h, v_h, _TN)
    q_ref[...] = jnp.concatenate(q_out, axis=1)
    yl_ref[...] = jnp.concatenate(yl_out, axis=1)


def rwkv_chunk_prep(r, logw, kk, a, k, v, *, reverse):
    B, L, W = r.shape
    T = RW_CHUNK
    nh = W // HEAD_DIM
    lanes = 2 * HEAD_DIM
    nc = L // T
    blk = pl.BlockSpec((None, T, lanes), lambda b, h, c: (b, c, h))
    mat = pl.BlockSpec((None, None, 2, HEAD_DIM, HEAD_DIM), lambda b, h, c: (b, c, h, 0, 0))
    mat_shape = jax.ShapeDtypeStruct((B, nc, nh, HEAD_DIM, HEAD_DIM), F32)
    seq_shape = jax.ShapeDtypeStruct((B, L, W), F32)
    return pl.pallas_call(
        functools.partial(_rwkv_prep_body, reverse=reverse),
        grid=(B, W // lanes, nc),
        in_specs=[blk] * 6,
        out_specs=[mat, mat, blk, blk],
        out_shape=[mat_shape, mat_shape, seq_shape, seq_shape],
        compiler_params=_params(("parallel", "parallel", "parallel")),
        name="rwkv_chunk_prep",
    )(r, logw, kk, a, k, v)


def _rwkv_scan_body(p_ref, g_ref, q_ref, yl_ref, z0_ref, y_ref, zf_ref, z_scr):
    c = pl.program_id(1)

    @pl.when(c == 0)
    def _():
        z_scr[...] = z0_ref[...]

    hd = HEAD_DIM
    n_heads = z_scr.shape[0]
    q = q_ref[...]
    ys = []
    for h in range(n_heads):
        z = z_scr[h]
        ys.append(_dot(q[:, h * hd:(h + 1) * hd], z))
        z_scr[h] = _dot(p_ref[h], z) + g_ref[h]
    y_ref[...] = jnp.concatenate(ys, axis=1) + yl_ref[...]

    @pl.when(c == pl.num_programs(1) - 1)
    def _():
        zf_ref[...] = z_scr[...]


def rwkv_chunk_scan(p, g, qh, yl, z0, *, reverse):
    B, nc, nh = p.shape[:3]
    L, W = qh.shape[1:]
    T = L // nc
    cidx = (lambda c: nc - 1 - c) if reverse else (lambda c: c)
    mat = pl.BlockSpec((None, None, nh, HEAD_DIM, HEAD_DIM), lambda b, c: (b, cidx(c), 0, 0, 0))
    seq = pl.BlockSpec((None, T, W), lambda b, c: (b, cidx(c), 0))
    st = pl.BlockSpec((None, nh, HEAD_DIM, HEAD_DIM), lambda b, c: (b, 0, 0, 0))
    return pl.pallas_call(
        _rwkv_scan_body,
        grid=(B, nc),
        in_specs=[mat, mat, seq, seq, st],
        out_specs=[seq, st],
        out_shape=[jax.ShapeDtypeStruct((B, L, W), F32), jax.ShapeDtypeStruct(z0.shape, F32)],
        scratch_shapes=[pltpu.VMEM((nh, HEAD_DIM, HEAD_DIM), F32)],
        compiler_params=_params(("parallel", "arbitrary")),
        name="rwkv_chunk_scan",
    )(p, g, qh, yl, z0)


def _diff_attn_body(lam_ref, q_ref, kt_ref, v_ref, subg_ref, o_ref, q2_scr, m_scr, l_scr, acc_scr, *, out_scale):
    ki = pl.program_id(3)
    tq = q_ref.shape[0]

    @pl.when(ki == 0)
    def _():
        q = q_ref[...]
        lane = lax.broadcasted_iota(jnp.int32, q.shape, 1)
        q2_scr[0:tq, :] = jnp.where(lane < HEAD_DIM, q, jnp.zeros_like(q))
        q2_scr[tq:2 * tq, :] = jnp.where(lane >= HEAD_DIM, q, jnp.zeros_like(q))
        m_scr[...] = jnp.full_like(m_scr, -jnp.inf)
        l_scr[...] = jnp.zeros_like(l_scr)
        acc_scr[...] = jnp.zeros_like(acc_scr)

    s = jnp.dot(q2_scr[...], kt_ref[...], preferred_element_type=F32)
    m_prev = m_scr[...]
    m_new = jnp.maximum(m_prev, jnp.max(s, axis=-1, keepdims=True))
    alpha = jnp.exp(m_prev - m_new)
    p = jnp.exp(s - m_new)
    l_scr[...] = alpha * l_scr[...] + jnp.sum(p, axis=-1, keepdims=True)
    acc_scr[...] = alpha * acc_scr[...] + jnp.dot(p.astype(BF16), v_ref[...], preferred_element_type=F32)
    m_scr[...] = m_new

    @pl.when(ki == pl.num_programs(3) - 1)
    def _():
        o = acc_scr[...] / l_scr[...]
        a = o[0:tq, :] - lam_ref[0, 0] * o[tq:2 * tq, :]
        y = a * lax.rsqrt(jnp.mean(a * a, axis=-1, keepdims=True) + NORM_EPS)
        o_ref[...] = (y * subg_ref[...] * out_scale).astype(o_ref.dtype)


def _key_tile(k, cap):
    best = 128
    t = 128
    while t <= min(k, cap):
        if k % t == 0:
            best = t
        t += 128
    return best


def diff_attention(q, kt, v, sub_g, lam, out_scale, *, tq=512, tk_cap=1280):
    B, L, W = q.shape
    K = kt.shape[2]
    hw = 2 * HEAD_DIM
    tq = _row_tile(L, tq)
    tk = _key_tile(K, tk_cap)
    return pl.pallas_call(
        functools.partial(_diff_attn_body, out_scale=out_scale),
        grid=(B, W // hw, L // tq, K // tk),
        in_specs=[
            pl.BlockSpec(memory_space=pltpu.SMEM),
            pl.BlockSpec((None, tq, hw), lambda b, h, i, j: (b, i, h)),
            pl.BlockSpec((None, hw, tk), lambda b, h, i, j: (b, h, j)),
            pl.BlockSpec((None, tk, hw), lambda b, h, i, j: (b, j, h)),
            pl.BlockSpec((1, hw), lambda b, h, i, j: (0, 0)),
        ],
        out_specs=pl.BlockSpec((None, tq, hw), lambda b, h, i, j: (b, i, h)),
        out_shape=jax.ShapeDtypeStruct((B, L, W), F32),
        scratch_shapes=[pltpu.VMEM((2 * tq, hw), BF16), pltpu.VMEM((2 * tq, 1), F32),
                        pltpu.VMEM((2 * tq, 1), F32), pltpu.VMEM((2 * tq, hw), F32)],
        compiler_params=_params(("parallel", "parallel", "parallel", "arbitrary")),
        name="diff_attention",
    )(lam, q, kt, v, sub_g)


def _swa_finish(parts, vals, sink, o_ref):
    m = sink
    for s in parts:
        m = jnp.maximum(m, jnp.max(s, axis=-1, keepdims=True))
    denom = jnp.exp(sink - m)
    acc = None
    for s, v in zip(parts, vals):
        p = jnp.exp(s - m)
        denom = denom + jnp.sum(p, axis=-1, keepdims=True)
        pv = jnp.dot(p.astype(BF16), v, preferred_element_type=F32)
        acc = pv if acc is None else acc + pv
    o = acc / denom
    o_ref[...] = o.reshape(o_ref.shape).astype(o_ref.dtype)


def _swa_band_body(q_ref, kp_ref, kn_ref, kx_ref, kc_ref, vp_ref, vn_ref, vx_ref, vc_ref, sink_ref, o_ref):
    n = pl.program_id(2)
    nb = pl.num_programs(2)
    grp, blk, hd = q_ref.shape
    q = q_ref[...].reshape(grp * blk, hd)
    iq = lax.broadcasted_iota(jnp.int32, (grp * blk, blk), 0) % blk
    j = lax.broadcasted_iota(jnp.int32, (grp * blk, blk), 1)
    s_prev = jnp.where((iq + blk - j <= SWA_WINDOW) & (n > 0), _dot_nt_bf16(q, kp_ref[...]), NEG_INF)
    s_cur = jnp.where(jnp.abs(iq - j) <= SWA_WINDOW, _dot_nt_bf16(q, kn_ref[...]), NEG_INF)
    s_next = jnp.where((j + blk - iq <= SWA_WINDOW) & (n < nb - 1), _dot_nt_bf16(q, kx_ref[...]), NEG_INF)
    s_ctx = _dot_nt_bf16(q, kc_ref[...])
    _swa_finish([s_prev, s_cur, s_next, s_ctx], [vp_ref[...], vn_ref[...], vx_ref[...], vc_ref[...]],
                sink_ref[...], o_ref)


def _swa_ctx_body(q_ref, kc_ref, vc_ref, sink_ref, o_ref):
    grp, blk, hd = q_ref.shape
    q = q_ref[...].reshape(grp * blk, hd)
    _swa_finish([_dot_nt_bf16(q, kc_ref[...])], [vc_ref[...]], sink_ref[...], o_ref)


def _dot_nt_bf16(a, b):
    return lax.dot_general(a, b, _NT, preferred_element_type=F32)


def swa_attention(q, k, v, kc, vc, sink_rows):
    B, kvh, grp, L, hd = q.shape
    C = kc.shape[2]
    blk = SWA_BLOCK
    nb = L // blk
    qspec = pl.BlockSpec((None, None, grp, blk, hd), lambda b, h, n: (b, h, 0, n, 0))
    prev = pl.BlockSpec((None, None, blk, hd), lambda b, h, n: (b, h, jnp.maximum(n - 1, 0), 0))
    cur = pl.BlockSpec((None, None, blk, hd), lambda b, h, n: (b, h, n, 0))
    nxt = pl.BlockSpec((None, None, blk, hd), lambda b, h, n: (b, h, jnp.minimum(n + 1, nb - 1), 0))
    cspec = pl.BlockSpec((None, None, C, hd), lambda b, h, n: (b, h, 0, 0))
    sspec = pl.BlockSpec((None, grp * blk, 1), lambda b, h, n: (h, 0, 0))
    return pl.pallas_call(
        _swa_band_body,
        grid=(B, kvh, nb),
        in_specs=[qspec, prev, cur, nxt, cspec, prev, cur, nxt, cspec, sspec],
        out_specs=qspec,
        out_shape=jax.ShapeDtypeStruct(q.shape, F32),
        compiler_params=_params(("parallel", "parallel", "parallel")),
        name="swa_attention",
    )(q, k, k, k, kc, v, v, v, vc, sink_rows)


def swa_context_attention(q, kc, vc, sink_rows):
    B, kvh, grp, L, hd = q.shape
    C = kc.shape[2]
    blk = SWA_BLOCK
    qspec = pl.BlockSpec((None, None, grp, blk, hd), lambda b, h, n: (b, h, 0, n, 0))
    cspec = pl.BlockSpec((None, None, C, hd), lambda b, h, n: (b, h, 0, 0))
    sspec = pl.BlockSpec((None, grp * blk, 1), lambda b, h, n: (h, 0, 0))
    return pl.pallas_call(
        _swa_ctx_body,
        grid=(B, kvh, L // blk),
        in_specs=[qspec, cspec, cspec, sspec],
        out_specs=qspec,
        out_shape=jax.ShapeDtypeStruct(q.shape, F32),
        compiler_params=_params(("parallel", "parallel", "parallel")),
        name="swa_context_attention",
    )(q, kc, vc, sink_rows)


def _rms_norm(x, g):
    xf = x.astype(F32)
    y = xf * lax.rsqrt(jnp.mean(xf * xf, axis=-1, keepdims=True) + NORM_EPS)
    return (y * g.astype(F32)).astype(x.dtype)


def _rope_tables(rows):
    n_freq = HEAD_DIM // 4
    inv = ROPE_BASE ** (-jnp.arange(n_freq, dtype=F32) / n_freq)
    row = jnp.repeat(jnp.arange(rows, dtype=F32), GRID_W)
    col = jnp.tile(jnp.arange(GRID_W, dtype=F32), rows)
    ang = jnp.concatenate([row[:, None] * inv, col[:, None] * inv], axis=-1)
    return jnp.cos(ang), jnp.sin(ang)


def _apply_rope(x, cos, sin):
    shp = (x.shape[1],) + (1,) * (x.ndim - 3) + (HEAD_DIM // 2,)
    c, s = cos.reshape(shp), sin.reshape(shp)
    x1, x2 = jnp.split(x.astype(F32), 2, axis=-1)
    return jnp.concatenate([x1 * c - x2 * s, x2 * c + x1 * s], axis=-1).astype(x.dtype)


def _short_conv3(x, w, b):
    xp = jnp.pad(x, ((0, 0), (1, 1), (0, 0)))
    return xp[:, :-2] * w[0] + xp[:, 1:-1] * w[1] + xp[:, 2:] * w[2] + b


def _hyena_filter_spectra(L, ch, f_w1, f_b1, f_w2, f_b2, f_w3, f_b3, f_freq, f_decay):
    t = jnp.arange(L, dtype=F32) / max(L - 1, 1)
    ang = 2 * math.pi * t[:, None] * jnp.arange(1, HY_POS_BANDS + 1, dtype=F32)
    feat = jnp.concatenate([t[:, None], jnp.sin(ang), jnp.cos(ang)], axis=-1)
    hp = lax.Precision.HIGHEST
    h = jnp.sin(f_freq * (jnp.dot(feat, f_w1, precision=hp) + f_b1))
    h = jnp.sin(f_freq * (jnp.dot(h, f_w2, precision=hp) + f_b2))
    h = jnp.dot(h, f_w3, precision=hp) + f_b3
    h = h * jnp.exp(-f_decay * t[:, None])
    h = h.reshape(L, HY_ORDER, 2, ch)
    h_fwd, h_bwd = h[:, :, 0], h[:, :, 1]
    l1 = jnp.sum(jnp.abs(h_fwd), axis=0) + jnp.sum(jnp.abs(h_bwd[1:]), axis=0)
    kern = jnp.concatenate([h_fwd, jnp.zeros((1, HY_ORDER, ch), F32), h_bwd[1:][::-1]], axis=0) / l1
    return jnp.fft.rfft(kern, axis=0)


def _fft_long_conv(z, spec, skip):
    L = z.shape[1]
    zf = jnp.fft.rfft(z, n=2 * L, axis=1)
    y = jnp.fft.irfft(zf * spec[None], n=2 * L, axis=1)[:, :L]
    return y + skip * z


def _hyena_branch(pa, conv_w, conv_b, spec, skip):
    u = _short_conv3(pa, conv_w, conv_b).astype(F32)
    v, x1, x2 = jnp.split(u, 3, axis=-1)
    z = x1 * _fft_long_conv(v, spec[:, 0], skip[0])
    return x2 * _fft_long_conv(z, spec[:, 1], skip[1])


def _swa_project(pb, q_g, k_g, n_heads):
    B, L = pb.shape[:2]
    q, k, v = jnp.split(pb, [n_heads * HEAD_DIM, (n_heads + SWA_KV_HEADS) * HEAD_DIM], axis=-1)
    q = _rms_norm(q.reshape(B, L, n_heads, HEAD_DIM), q_g)
    k = _rms_norm(k.reshape(B, L, SWA_KV_HEADS, HEAD_DIM), k_g)
    return q, k, v.reshape(B, L, SWA_KV_HEADS, HEAD_DIM)


def _sink_softmax(s, sink):
    m = jnp.maximum(jnp.max(s, axis=-1, keepdims=True), sink)
    p = jnp.exp(s - m)
    return p / (jnp.sum(p, axis=-1, keepdims=True) + jnp.exp(sink - m))


def _swa_context(qc, kc, vc, sink):
    B, C, n_heads = qc.shape[:3]
    grp = n_heads // SWA_KV_HEADS
    qg = qc.reshape(B, C, SWA_KV_HEADS, grp, HEAD_DIM)
    s = jnp.einsum('bqhgd,bkhd->bhgqk', qg, kc).astype(F32) * HEAD_DIM ** -0.5
    p = _sink_softmax(s, sink.astype(F32).reshape(1, SWA_KV_HEADS, grp, 1, 1))
    o = jnp.einsum('bhgqk,bkhd->bqhgd', p.astype(vc.dtype), vc)
    return o.reshape(B, C, n_heads * HEAD_DIM)


def _swa_latent(q, k, v, kc, vc, sink):
    B, L, n_heads = q.shape[:3]
    grp = n_heads // SWA_KV_HEADS
    nb = L // SWA_BLOCK
    qb = q.reshape(B, nb, SWA_BLOCK, SWA_KV_HEADS, grp, HEAD_DIM)

    def band(t):
        tb = t.reshape(B, nb, SWA_BLOCK, SWA_KV_HEADS, HEAD_DIM)
        tp = jnp.pad(tb, ((0, 0), (1, 1), (0, 0), (0, 0), (0, 0)))
        return jnp.concatenate([tp[:, :-2], tp[:, 1:-1], tp[:, 2:]], axis=2)

    kb, vb = band(k), band(v)
    scale = HEAD_DIM ** -0.5
    s_loc = jnp.einsum('bnqhgd,bnkhd->bnhgqk', qb, kb).astype(F32) * scale
    s_ctx = jnp.einsum('bnqhgd,bchd->bnhgqc', qb, kc).astype(F32) * scale
    q_rel = jnp.arange(SWA_BLOCK)[:, None] + SWA_BLOCK
    k_rel = jnp.arange(3 * SWA_BLOCK)[None, :]
    k_abs = (jnp.arange(nb)[:, None, None] - 1) * SWA_BLOCK + k_rel[None]
    valid = (jnp.abs(q_rel - k_rel) <= SWA_WINDOW)[None] & (k_abs >= 0) & (k_abs < L)
    s_loc = jnp.where(valid[None, :, None, None], s_loc, NEG_INF)
    s = jnp.concatenate([s_loc, s_ctx], axis=-1)
    p = _sink_softmax(s, sink.astype(F32).reshape(1, 1, SWA_KV_HEADS, grp, 1, 1)).astype(v.dtype)
    o = (jnp.einsum('bnhgqk,bnkhd->bnqhgd', p[..., :3 * SWA_BLOCK], vb)
         + jnp.einsum('bnhgqc,bchd->bnqhgd', p[..., 3 * SWA_BLOCK:], vc))
    return o.reshape(B, L, n_heads * HEAD_DIM)


def _token_shift(x, reverse):
    if reverse:
        return jnp.pad(x, ((0, 0), (0, 1), (0, 0)))[:, 1:]
    return jnp.pad(x, ((0, 0), (1, 0), (0, 0)))[:, :-1]


def _head_l2norm(x, n_heads):
    B, L, C = x.shape
    xh = x.reshape(B, L, n_heads, HEAD_DIM)
    xh = xh * lax.rsqrt(jnp.sum(xh * xh, axis=-1, keepdims=True) + 1e-12)
    return xh.reshape(B, L, C)


def _rwkv_prepare(feats, reverse, width, mu, w0, w2, a0, a2, g2, k_k, k_a):
    n_heads = width // HEAD_DIM
    xs = feats + mu * (_token_shift(feats, reverse) - feats)
    splits = (width, 2 * width, 3 * width, 3 * width + RW_DECAY_RANK, 3 * width + RW_DECAY_RANK + RW_A_RANK)
    r, k, v, wd, ad, gd = jnp.split(xs, splits, axis=-1)
    logw = -jnp.exp(-jax.nn.softplus(-(w0 + jnp.tanh(wd) @ w2)) - 0.5)
    a = jax.nn.sigmoid(a0 + ad @ a2)
    g = jax.nn.sigmoid(gd) @ g2
    kk = _head_l2norm(k * k_k, n_heads)
    k = k * (1 + (a - 1) * k_a)
    return r, logw, kk, a, k, v, g


def _wkv7(r, logw, kk, a, k, v, z0, reverse):
    p, g, qh, yl = rwkv_chunk_prep(r, logw, kk, a, k, v, reverse=reverse)
    y, z_fin = rwkv_chunk_scan(p, g, qh, yl, z0, reverse=reverse)
    return z_fin, y


def _rwkv_readout(y, r, k, v, g, r_k, ln_w, ln_b):
    B, L, C = y.shape
    n_heads = C // HEAD_DIM
    yh = y.reshape(B, L, n_heads, HEAD_DIM)
    mean = jnp.mean(yh, axis=-1, keepdims=True)
    var = jnp.mean(jnp.square(yh - mean), axis=-1, keepdims=True)
    yn = ((yh - mean) * lax.rsqrt(var + RW_GN_EPS)).reshape(B, L, C) * ln_w + ln_b
    bonus = jnp.sum((r * k * r_k).reshape(B, L, n_heads, HEAD_DIM), axis=-1, keepdims=True)
    bonus = (bonus * v.reshape(B, L, n_heads, HEAD_DIM)).reshape(B, L, C)
    return (yn + bonus) * g


def _rwkv_branch(f_ctx, f_lat, width, mu, w0, w2, a0, a2, g2, k_k, k_a, r_k, ln_w, ln_b, need_ctx):
    B = f_lat.shape[0]
    n_heads = width // HEAD_DIM
    y_lat = 0.0
    y_ctx = 0.0 if need_ctx else None
    for d in range(2):
        rev = d == 1
        dir_args = (width, mu[d], w0[d], w2[d], a0[d], a2[d], g2[d], k_k, k_a)
        r_c, w_c, kk_c, a_c, k_c, v_c, g_c = _rwkv_prepare(f_ctx, rev, *dir_args)
        z0 = jnp.zeros((B, n_heads, HEAD_DIM, HEAD_DIM), F32)
        z_ctx, o_c = _wkv7(r_c, w_c, kk_c, a_c, k_c, v_c, z0, rev)
        r_l, w_l, kk_l, a_l, k_l, v_l, g_l = _rwkv_prepare(f_lat, rev, *dir_args)
        _, o_l = _wkv7(r_l, w_l, kk_l, a_l, k_l, v_l, z_ctx, rev)
        y_lat = y_lat + _rwkv_readout(o_l, r_l, k_l, v_l, g_l, r_k, ln_w, ln_b)
        if need_ctx:
            y_ctx = y_ctx + _rwkv_readout(o_c, r_c, k_c, v_c, g_c, r_k, ln_w, ln_b)
    return y_lat, y_ctx


def _diff_project(pd, q_g, k_g):
    B, L, w3 = pd.shape
    n_heads = w3 // (3 * 2 * HEAD_DIM)
    q, k, v = jnp.split(pd, 3, axis=-1)
    q = _rms_norm(q.reshape(B, L, n_heads, 2, HEAD_DIM), q_g)
    k = _rms_norm(k.reshape(B, L, n_heads, 2, HEAD_DIM), k_g)
    return q, k, v.reshape(B, L, n_heads, 2 * HEAD_DIM)


def _diff_maps(q, k_all, v_all, lam):
    s = jnp.einsum('bqhid,bkhid->bhiqk', q, k_all).astype(F32) * HEAD_DIM ** -0.5
    p = jax.nn.softmax(s, axis=-1)
    a = p[:, :, 0] - lam * p[:, :, 1]
    return jnp.einsum('bhqk,bkhe->bqhe', a.astype(v_all.dtype), v_all)


def _diff_latent(q, k, v, kc, vc, lam):
    B, L, n_heads = q.shape[:3]
    nb = L // DF_BLOCK
    k_all = jnp.concatenate([kc, k], axis=1)
    v_all = jnp.concatenate([vc, v], axis=1)
    qb = jnp.moveaxis(q.reshape(B, nb, DF_BLOCK, n_heads, 2, HEAD_DIM), 1, 0)
    o = lax.map(lambda q_blk: _diff_maps(q_blk, k_all, v_all, lam), qb)
    return jnp.moveaxis(o, 0, 1).reshape(B, L, n_heads, 2 * HEAD_DIM)


def _diff_readout(o, sub_g, lam_init):
    B, L = o.shape[:2]
    return (_rms_norm(o, sub_g) * (1 - lam_init)).reshape(B, L, -1)


def _swa_branch(pb_l, pb_c, cos, sin, q_g, k_g, sink, need_ctx):
    B, L = pb_l.shape[:2]
    C = pb_c.shape[1]
    n_heads = pb_l.shape[-1] // HEAD_DIM - 2 * SWA_KV_HEADS
    grp = n_heads // SWA_KV_HEADS
    scale = HEAD_DIM ** -0.5
    q_l, k_l, v_l = _swa_project(pb_l, q_g, k_g, n_heads)
    q_l, k_l = _apply_rope(q_l, cos, sin), _apply_rope(k_l, cos, sin)
    q_c, k_c, v_c = _swa_project(pb_c, q_g, k_g, n_heads)

    def q_layout(q, n):
        return jnp.transpose((q * scale).astype(BF16).reshape(B, n, SWA_KV_HEADS, grp, HEAD_DIM), (0, 2, 3, 1, 4))

    def kv_layout(t):
        return jnp.transpose(t.astype(BF16), (0, 2, 1, 3))

    def o_layout(o, n):
        return jnp.transpose(o, (0, 3, 1, 2, 4)).reshape(B, n, n_heads * HEAD_DIM)

    sink_rows = jnp.repeat(sink.astype(F32).reshape(SWA_KV_HEADS, grp), SWA_BLOCK, axis=1)[..., None]
    kc, vc = kv_layout(k_c), kv_layout(v_c)
    y_l = o_layout(swa_attention(q_layout(q_l, L), kv_layout(k_l), kv_layout(v_l), kc, vc, sink_rows), L)
    y_c = None
    if need_ctx:
        y_c = o_layout(swa_context_attention(q_layout(q_c, C), kc, vc, sink_rows), C)
    return y_l, y_c


def _diff_branch(pd_l, pd_c, cos, sin, q_g, k_g, sub_g, lam, lam_init, need_ctx):
    B, L = pd_l.shape[:2]
    C = pd_c.shape[1]
    W = pd_l.shape[-1] // 3
    scale = HEAD_DIM ** -0.5
    dq_l, dk_l, dv_l = _diff_project(pd_l, q_g, k_g)
    dq_l, dk_l = _apply_rope(dq_l, cos, sin), _apply_rope(dk_l, cos, sin)
    dq_c, dk_c, dv_c = _diff_project(pd_c, q_g, k_g)
    flat = lambda t, n: t.reshape(B, n, W)
    q_l = (flat(dq_l, L) * scale).astype(BF16)
    k_c, v_c = flat(dk_c, C).astype(BF16), flat(dv_c, C).astype(BF16)
    kt_c = jnp.transpose(k_c, (0, 2, 1))
    kt_all = jnp.concatenate([kt_c, jnp.transpose(flat(dk_l, L).astype(BF16), (0, 2, 1))], axis=2)
    v_all = jnp.concatenate([v_c, flat(dv_l, L).astype(BF16)], axis=1)
    sub = sub_g.astype(F32).reshape(1, 2 * HEAD_DIM)
    lam2 = lam.astype(F32).reshape(1, 1)
    y_l = diff_attention(q_l, kt_all, v_all, sub, lam2, 1.0 - lam_init)
    y_c = None
    if need_ctx:
        q_c = (flat(dq_c, C) * scale).astype(BF16)
        y_c = diff_attention(q_c, kt_c, v_c, sub, lam2, 1.0 - lam_init)
    return y_l, y_c


def _moe_route(logits_g, logits_e, n_tok):
    g_idx = jnp.argmax(logits_g, axis=-1)
    g_prob = jnp.take_along_axis(jax.nn.softmax(logits_g, axis=-1), g_idx[:, None], axis=-1)[:, 0]
    e_logits = logits_e.reshape(n_tok, MOE_GROUPS, MOE_PER_GROUP)
    e_logits = jnp.take_along_axis(e_logits, g_idx[:, None, None], axis=1)[:, 0]
    top_p, top_e = lax.top_k(jax.nn.softmax(e_logits, axis=-1), MOE_TOP_K)
    weights = g_prob[:, None] * top_p / jnp.sum(top_p, axis=-1, keepdims=True)
    flat_e = (g_idx[:, None] * MOE_PER_GROUP + top_e).reshape(-1).astype(jnp.int32)
    flat_tok = jnp.repeat(jnp.arange(n_tok, dtype=jnp.int32), MOE_TOP_K)
    n_assign = n_tok * MOE_TOP_K
    order = jnp.argsort(flat_e)
    se = flat_e[order]
    counts = jnp.bincount(flat_e, length=MOE_EXPERTS)
    padded = (counts + MOE_BLOCK - 1) // MOE_BLOCK * MOE_BLOCK
    pad_end = jnp.cumsum(padded)
    pad_start = pad_end - padded
    start = jnp.cumsum(counts) - counts
    dest = (pad_start[se] + jnp.arange(n_assign, dtype=jnp.int32) - start[se]).astype(jnp.int32)
    n_blocks = -(-n_assign // MOE_BLOCK) + MOE_EXPERTS
    P = n_blocks * MOE_BLOCK
    tok_buf = jnp.full((P,), n_tok, jnp.int32).at[dest].set(flat_tok[order])
    blk_e = jnp.minimum(jnp.searchsorted(pad_end, jnp.arange(n_blocks) * MOE_BLOCK, side='right'),
                        MOE_EXPERTS - 1).astype(jnp.int32)
    n_used = (pad_end[-1] // MOE_BLOCK).astype(jnp.int32).reshape(1)
    slot = jnp.zeros((n_assign,), jnp.int32).at[order].set(dest).reshape(n_tok, MOE_TOP_K)
    return tok_buf, blk_e, n_used, slot, weights


def _hier_moe(tokens_bf16, tokens_f32, rg_w, rg_b, re_w, re_b, w1, w3, w2):
    n_tok, d = tokens_bf16.shape
    hp = lax.Precision.HIGHEST
    logits_g = jnp.dot(tokens_f32, rg_w, precision=hp) + rg_b
    logits_e = jnp.dot(tokens_f32, re_w, precision=hp) + re_b
    tok_buf, blk_e, n_used, slot, weights = _moe_route(logits_g, logits_e, n_tok)
    x_pad = jnp.concatenate([tokens_bf16, jnp.zeros((1, d), BF16)], axis=0)
    xb = x_pad[tok_buf]
    yb = moe_experts(xb, blk_e, n_used, w1, w3, w2)
    wts = weights.astype(F32)
    return (yb[slot[:, 0]].astype(F32) * wts[:, 0:1] + yb[slot[:, 1]].astype(F32) * wts[:, 1:2])


def _modulate(x, g, shift, scale):
    return _rms_norm(x, g) * (1 + scale) + shift


def kernel(x, c, ctx, c_ctx, mod_w, mod_b, norm1_g, norm2_g, w_in, hy_conv_w, hy_conv_b, hy_f_w1, hy_f_b1, hy_f_w2, hy_f_b2, hy_f_w3, hy_f_b3, hy_f_freq, hy_f_decay, hy_skip, swa_q_g, swa_k_g, swa_sink, rw_mu, rw_w0, rw_w2, rw_a0, rw_a2, rw_g2, rw_k_k, rw_k_a, rw_r_k, rw_ln_w, rw_ln_b, df_q_g, df_k_g, df_lq1, df_lk1, df_lq2, df_lk2, df_sub_g, w_gate, b_gate, w_branch, w_out, moe_rg_w, moe_rg_b, moe_re_w, moe_re_b, moe_w1, moe_w3, moe_w2):
    B, L, D = x.shape
    C = ctx.shape[1]
    depth = mod_w.shape[0]
    bw = D // N_BRANCH
    a_cols = 3 * bw
    swa_heads = bw // HEAD_DIM
    b_cols = (swa_heads + 2 * SWA_KV_HEADS) * HEAD_DIM
    c_cols = 3 * bw + RW_DECAY_RANK + RW_A_RANK + RW_G_RANK
    in_splits = (a_cols, a_cols + b_cols, a_cols + b_cols + c_cols)
    rows = L // GRID_W
    cos, sin = _rope_tables(rows)
    hp = lax.Precision.HIGHEST

    x_lat = x.reshape(B * L, D)
    x_ctx = ctx.reshape(B * C, D)
    for l in range(depth):
        need_ctx = l < depth - 1
        mod_lat = jnp.dot(jax.nn.silu(c), mod_w[l], precision=hp) + mod_b[l]
        mod_ctx = jnp.dot(jax.nn.silu(c_ctx)[None], mod_w[l], precision=hp) + mod_b[l]
        sh1, sc1, g1, sh2, sc2, g2 = [t[:, None, :] for t in jnp.split(mod_lat, 6, axis=-1)]
        csh1, csc1, cg1, csh2, csc2, cg2 = [t[:, None, :] for t in jnp.split(mod_ctx, 6, axis=-1)]

        w_in_b = w_in[l].astype(BF16)
        p_lat, h_lat = norm_proj(x_lat, norm1_g[l][None], sc1, sh1, w_in_b, rows_per_mod=L)
        p_ctx, h_ctx = norm_proj(x_ctx, norm1_g[l][None], csc1, csh1, w_in_b, rows_per_mod=B * C)
        p_lat = p_lat.reshape(B, L, -1)
        p_ctx = p_ctx.reshape(B, C, -1)
        pa_l, pb_l, pc_l, pd_l = jnp.split(p_lat, in_splits, axis=-1)
        pa_c, pb_c, pc_c, pd_c = jnp.split(p_ctx, in_splits, axis=-1)

        spec_l = _hyena_filter_spectra(L, bw, hy_f_w1[l], hy_f_b1[l], hy_f_w2[l], hy_f_b2[l], hy_f_w3[l],
                                       hy_f_b3[l], hy_f_freq[l], hy_f_decay[l])
        ya_l = _hyena_branch(pa_l, hy_conv_w[l], hy_conv_b[l], spec_l, hy_skip[l])

        yb_l, yb_c = _swa_branch(pb_l, pb_c, cos, sin, swa_q_g[l], swa_k_g[l], swa_sink[l], need_ctx)

        yc_l, yc_c = _rwkv_branch(pc_c, pc_l, bw, rw_mu[l], rw_w0[l], rw_w2[l], rw_a0[l], rw_a2[l],
                                  rw_g2[l], rw_k_k[l], rw_k_a[l], rw_r_k[l], rw_ln_w[l], rw_ln_b[l], need_ctx)

        lam_init = 0.8 - 0.6 * math.exp(-0.3 * l)
        lam = (jnp.exp(jnp.sum(df_lq1[l] * df_lk1[l])) - jnp.exp(jnp.sum(df_lq2[l] * df_lk2[l])) + lam_init)
        yd_l, yd_c = _diff_branch(pd_l, pd_c, cos, sin, df_q_g[l], df_k_g[l], df_sub_g[l], lam, lam_init, need_ctx)

        wg_b = w_gate[l].astype(BF16)
        bg = b_gate[l][:, None, :]
        wb_b = w_branch[l].astype(BF16)
        wo_b = w_out[l].astype(BF16)
        ys_l = jnp.stack([t.reshape(B * L, bw).astype(BF16) for t in (ya_l, yb_l, yc_l, yd_l)])
        acc_l = merge_gated(h_lat, ys_l, wg_b, bg, wb_b)
        x_lat = resid_proj(x_lat, acc_l, wo_b, g1, rows_per_mod=L)
        if need_ctx:
            spec_c = _hyena_filter_spectra(C, bw, hy_f_w1[l], hy_f_b1[l], hy_f_w2[l], hy_f_b2[l], hy_f_w3[l],
                                           hy_f_b3[l], hy_f_freq[l], hy_f_decay[l])
            ya_c = _hyena_branch(pa_c, hy_conv_w[l], hy_conv_b[l], spec_c, hy_skip[l])
            ys_c = jnp.stack([t.reshape(B * C, bw).astype(BF16) for t in (ya_c, yb_c, yc_c, yd_c)])
            acc_c = merge_gated(h_ctx, ys_c, wg_b, bg, wb_b)
            x_ctx = resid_proj(x_ctx, acc_c, wo_b, cg1, rows_per_mod=B * C)

        w1_b, w3_b, w2_b = moe_w1[l].astype(BF16), moe_w3[l].astype(BF16), moe_w2[l].astype(BF16)
        moe_args = (moe_rg_w[l], moe_rg_b[l], moe_re_w[l], moe_re_b[l], w1_b, w3_b, w2_b)
        hm_lat = _modulate(x_lat.reshape(B, L, D), norm2_g[l], sh2, sc2).reshape(B * L, D)
        if need_ctx:
            hm_ctx = _modulate(x_ctx.reshape(B, C, D), norm2_g[l], csh2, csc2).reshape(B * C, D)
            tokens = jnp.concatenate([hm_ctx, hm_lat], axis=0)
            out = _hier_moe(tokens.astype(BF16), tokens, *moe_args)
            x_ctx = x_ctx + (cg2 * out[:B * C].reshape(1, B * C, D)).reshape(B * C, D)
            x_lat = x_lat + (g2 * out[B * C:].reshape(B, L, D)).reshape(B * L, D)
        else:
            out = _hier_moe(hm_lat.astype(BF16), hm_lat, *moe_args)
            x_lat = x_lat + (g2 * out.reshape(B, L, D)).reshape(B * L, D)
    return x_lat.reshape(B, L, D)
```

```python
import functools
import math

import jax
import jax.numpy as jnp
from jax import lax
from jax.experimental import pallas as pl
from jax.experimental.pallas import tpu as pltpu

F32 = jnp.float32
BF16 = jnp.bfloat16

GRID_W = 64
HEAD_DIM = 64
ROPE_BASE = 10000.0
NORM_EPS = 1e-6
NEG_INF = -1e30
N_BRANCH = 4
HY_ORDER = 2
HY_POS_BANDS = 8
SWA_KV_HEADS = 2
SWA_WINDOW = 128
SWA_BLOCK = 128
RW_DECAY_RANK = 64
RW_A_RANK = 64
RW_G_RANK = 128
RW_GN_EPS = 64e-5
DF_BLOCK = 128
DF_ROW_CHUNK = 256
MOE_GROUPS = 4
MOE_PER_GROUP = 8
MOE_EXPERTS = MOE_GROUPS * MOE_PER_GROUP
MOE_TOP_K = 2
MOE_BLOCK = 256

VMEM_LIMIT_BYTES = 56 * 1024 * 1024


def _row_tile(m, pref):
    t = min(pref, m)
    while m % t:
        t //= 2
    return t


def _params(sem):
    return pltpu.CompilerParams(dimension_semantics=sem, vmem_limit_bytes=VMEM_LIMIT_BYTES)


def _norm_proj_body(x_ref, g_ref, sc_ref, sh_ref, w_ref, p_ref, h_ref, h_scr):
    @pl.when(pl.program_id(1) == 0)
    def _():
        x = x_ref[...].astype(F32)
        y = x * lax.rsqrt(jnp.mean(x * x, axis=-1, keepdims=True) + NORM_EPS)
        h = y * g_ref[...] * (1.0 + sc_ref[...]) + sh_ref[...]
        h_scr[...] = h.astype(BF16)
        h_ref[...] = h_scr[...]

    p_ref[...] = jnp.dot(h_scr[...], w_ref[...], preferred_element_type=F32).astype(p_ref.dtype)


def norm_proj(x, g, scale, shift, w, *, rows_per_mod, tm=1024, tn=512, out_dtype=F32):
    m, d = x.shape
    n = w.shape[1]
    tm = _row_tile(rows_per_mod, tm)
    tn = _row_tile(n, tn)
    tiles_per_mod = rows_per_mod // tm
    mod_map = lambda i, j: (i // tiles_per_mod, 0, 0)
    return pl.pallas_call(
        _norm_proj_body,
        grid=(m // tm, n // tn),
        in_specs=[
            pl.BlockSpec((tm, d), lambda i, j: (i, 0)),
            pl.BlockSpec((1, d), lambda i, j: (0, 0)),
            pl.BlockSpec((None, 1, d), mod_map),
            pl.BlockSpec((None, 1, d), mod_map),
            pl.BlockSpec((d, tn), lambda i, j: (0, j)),
        ],
        out_specs=[
            pl.BlockSpec((tm, tn), lambda i, j: (i, j)),
            pl.BlockSpec((tm, d), lambda i, j: (i, 0)),
        ],
        out_shape=[jax.ShapeDtypeStruct((m, n), out_dtype), jax.ShapeDtypeStruct((m, d), BF16)],
        scratch_shapes=[pltpu.VMEM((tm, d), BF16)],
        compiler_params=_params(("parallel", "arbitrary")),
        name="norm_proj",
    )(x, g, scale, shift, w)


def _merge_body(h_ref, y_ref, wg_ref, bg_ref, wb_ref, o_ref, acc_ref):
    b = pl.program_id(2)
    gate = jax.nn.sigmoid(jnp.dot(h_ref[...], wg_ref[...], preferred_element_type=F32) + bg_ref[...])
    val = gate * jnp.dot(y_ref[...], wb_ref[...], preferred_element_type=F32)

    @pl.when(b == 0)
    def _():
        acc_ref[...] = val

    @pl.when(b > 0)
    def _():
        acc_ref[...] += val

    @pl.when(b == N_BRANCH - 1)
    def _():
        o_ref[...] = acc_ref[...].astype(o_ref.dtype)


def merge_gated(h, ys, wg, bg, wb, *, tm=1024, tn=512):
    m, d = h.shape
    w = ys.shape[-1]
    tm = _row_tile(m, tm)
    tn = _row_tile(d, tn)
    return pl.pallas_call(
        _merge_body,
        grid=(m // tm, d // tn, N_BRANCH),
        in_specs=[
            pl.BlockSpec((tm, d), lambda i, j, b: (i, 0)),
            pl.BlockSpec((None, tm, w), lambda i, j, b: (b, i, 0)),
            pl.BlockSpec((None, d, tn), lambda i, j, b: (b, 0, j)),
            pl.BlockSpec((None, 1, tn), lambda i, j, b: (b, 0, j)),
            pl.BlockSpec((None, w, tn), lambda i, j, b: (b, 0, j)),
        ],
        out_specs=pl.BlockSpec((tm, tn), lambda i, j, b: (i, j)),
        out_shape=jax.ShapeDtypeStruct((m, d), BF16),
        scratch_shapes=[pltpu.VMEM((tm, tn), F32)],
        compiler_params=_params(("parallel", "arbitrary", "arbitrary")),
        name="merge_gated",
    )(h, ys, wg, bg, wb)


def _resid_proj_body(x_ref, a_ref, w_ref, gate_ref, o_ref):
    y = jnp.dot(a_ref[...], w_ref[...], preferred_element_type=F32)
    o_ref[...] = (x_ref[...].astype(F32) + gate_ref[...] * y).astype(o_ref.dtype)


def resid_proj(x, a, w, gate, *, rows_per_mod, tm=1024, tn=512):
    m, d = x.shape
    k = a.shape[1]
    tm = _row_tile(rows_per_mod, tm)
    tn = _row_tile(d, tn)
    tiles_per_mod = rows_per_mod // tm
    return pl.pallas_call(
        _resid_proj_body,
        grid=(m // tm, d // tn),
        in_specs=[
            pl.BlockSpec((tm, tn), lambda i, j: (i, j)),
            pl.BlockSpec((tm, k), lambda i, j: (i, 0)),
            pl.BlockSpec((k, tn), lambda i, j: (0, j)),
            pl.BlockSpec((None, 1, tn), lambda i, j: (i // tiles_per_mod, 0, j)),
        ],
        out_specs=pl.BlockSpec((tm, tn), lambda i, j: (i, j)),
        out_shape=jax.ShapeDtypeStruct((m, d), x.dtype),
        compiler_params=_params(("parallel", "arbitrary")),
        name="resid_proj",
    )(x, a, w, gate)


def _moe_body(blk_e_ref, n_used_ref, x_ref, w1_ref, w3_ref, w2_ref, o_ref):
    i = pl.program_id(0)

    @pl.when(i < n_used_ref[0])
    def _():
        x = x_ref[...]
        a = jnp.dot(x, w1_ref[...], preferred_element_type=F32)
        b = jnp.dot(x, w3_ref[...], preferred_element_type=F32)
        hdn = (a * jax.nn.sigmoid(a) * b).astype(BF16)
        o_ref[...] = jnp.dot(hdn, w2_ref[...], preferred_element_type=F32).astype(o_ref.dtype)

    @pl.when(i >= n_used_ref[0])
    def _():
        o_ref[...] = jnp.zeros_like(o_ref)


def moe_experts(xb, blk_e, n_used, w1, w3, w2):
    p, d = xb.shape
    hid = w1.shape[-1]
    n_blocks = p // MOE_BLOCK
    grid_spec = pltpu.PrefetchScalarGridSpec(
        num_scalar_prefetch=2,
        grid=(n_blocks,),
        in_specs=[
            pl.BlockSpec((MOE_BLOCK, d), lambda i, e, n: (i, 0)),
            pl.BlockSpec((None, d, hid), lambda i, e, n: (e[i], 0, 0)),
            pl.BlockSpec((None, d, hid), lambda i, e, n: (e[i], 0, 0)),
            pl.BlockSpec((None, hid, d), lambda i, e, n: (e[i], 0, 0)),
        ],
        out_specs=pl.BlockSpec((MOE_BLOCK, d), lambda i, e, n: (i, 0)),
    )
    return pl.pallas_call(
        _moe_body,
        grid_spec=grid_spec,
        out_shape=jax.ShapeDtypeStruct((p, d), BF16),
        compiler_params=_params(("arbitrary",)),
        name="moe_experts",
    )(blk_e, n_used, xb, w1, w3, w2)


RW_CHUNK = 64
_HI = lax.Precision.HIGHEST
_NT = (((1,), (1,)), ((), ()))
_TN = (((0,), (0,)), ((), ()))


def _dot(a, b, dims=None):
    if dims is None:
        return jnp.dot(a, b, preferred_element_type=F32, precision=_HI)
    return lax.dot_general(a, b, dims, preferred_element_type=F32, precision=_HI)


def _bdot(a, b, dims=None):
    a, b = a.astype(BF16), b.astype(BF16)
    if dims is None:
        return jnp.dot(a, b, preferred_element_type=F32)
    return lax.dot_general(a, b, dims, preferred_element_type=F32)


def _rwkv_prep_body(r_ref, lw_ref, kk_ref, a_ref, k_ref, v_ref, p_ref, g_ref, q_ref, yl_ref, *, reverse):
    T = r_ref.shape[0]
    hd = HEAD_DIM
    row = lax.broadcasted_iota(jnp.int32, (T, T), 0)
    col = lax.broadcasted_iota(jnp.int32, (T, T), 1)
    if reverse:
        strict, incl = col > row, col >= row
    else:
        strict, incl = col < row, col <= row
    lw = lw_ref[...]
    tri = incl.astype(BF16)
    lw1 = lw.astype(BF16)
    res1 = lw - lw1.astype(F32)
    lw2 = res1.astype(BF16)
    lw3 = (res1 - lw2.astype(F32)).astype(BF16)
    cum = (jnp.dot(tri, lw1, preferred_element_type=F32) + jnp.dot(tri, lw2, preferred_element_type=F32)
           + jnp.dot(tri, lw3, preferred_element_type=F32))
    total = jnp.sum(lw, axis=0, keepdims=True)
    e_in = jnp.exp(cum)
    e_ex = jnp.exp(cum - lw)
    e_ninv = jnp.exp(-cum)
    e_rem = jnp.exp(total - cum)
    gam = jnp.exp(total)
    kk = kk_ref[...]
    kka = kk * a_ref[...]
    k = k_ref[...]
    nt = -kk * e_ex
    rt = r_ref[...] * e_in
    at = kka * e_ninv
    kt = k * e_ninv
    ac = kka * e_rem
    kc = k * e_rem
    v = v_ref[...]
    eye = lax.broadcasted_iota(jnp.int32, (hd, hd), 0) == lax.broadcasted_iota(jnp.int32, (hd, hd), 1)
    n_heads = r_ref.shape[1] // hd
    zeros = jnp.zeros((T, hd), F32)
    steps = max(1, (T - 1).bit_length())
    q_out, yl_out = [], []
    for hh in range(n_heads):
        sl = slice(hh * hd, (hh + 1) * hd)
        n_h, r_h, v_h = nt[:, sl], rt[:, sl], v[:, sl]
        big = _bdot(jnp.concatenate([n_h, r_h], axis=0), jnp.concatenate([at[:, sl], kt[:, sl]], axis=0), _NT)
        a_aa = jnp.where(strict, big[:T, :T], 0.0)
        a_ak = jnp.where(strict, big[:T, T:], 0.0)
        a_ra = jnp.where(incl, big[T:, :T], 0.0)
        a_rk = jnp.where(incl, big[T:, T:], 0.0)
        x = jnp.concatenate([n_h, _bdot(a_ak, v_h)], axis=1)
        pw = a_aa
        for it in range(steps):
            if it < steps - 1:
                both = _bdot(pw, jnp.concatenate([x, pw], axis=1))
                x = x + both[:, :2 * hd]
                pw = both[:, 2 * hd:]
            else:
                x = x + _bdot(pw, x)
        rhs = jnp.concatenate([x, jnp.concatenate([zeros, v_h], axis=1)], axis=0)
        top = _bdot(jnp.concatenate([a_ra, a_rk], axis=1), rhs)
        bot = _bdot(jnp.concatenate([ac[:, sl], kc[:, sl]], axis=0), rhs, _TN)
        q_out.append(r_h + top[:, :hd])
        yl_out.append(top[:, hd:])
        p_ref[hh] = jnp.where(eye, gam[:, sl], 0.0) + bot[:, :hd]
        g_ref[hh] = bot[:, hd:]
    q_ref[...] = jnp.concatenate(q_out, axis=1)
    yl_ref[...] = jnp.concatenate(yl_out, axis=1)


def rwkv_chunk_prep(r, logw, kk, a, k, v, *, reverse):
    B, L, W = r.shape
    T = RW_CHUNK
    nh = W // HEAD_DIM
    nc = L // T
    blk = pl.BlockSpec((None, T, W), lambda b, c: (b, c, 0))
    mat = pl.BlockSpec((None, None, nh, HEAD_DIM, HEAD_DIM), lambda b, c: (b, c, 0, 0, 0))
    mat_shape = jax.ShapeDtypeStruct((B, nc, nh, HEAD_DIM, HEAD_DIM), F32)
    seq_shape = jax.ShapeDtypeStruct((B, L, W), F32)
    return pl.pallas_call(
        functools.partial(_rwkv_prep_body, reverse=reverse),
        grid=(B, nc),
        in_specs=[blk] * 6,
        out_specs=[mat, mat, blk, blk],
        out_shape=[mat_shape, mat_shape, seq_shape, seq_shape],
        compiler_params=_params(("parallel", "parallel")),
        name="rwkv_chunk_prep",
    )(r, logw, kk, a, k, v)


def _rwkv_scan_body(p_ref, g_ref, q_ref, yl_ref, z0_ref, y_ref, zf_ref, z_scr):
    c = pl.program_id(1)

    @pl.when(c == 0)
    def _():
        z_scr[...] = z0_ref[...]

    hd = HEAD_DIM
    n_heads = z_scr.shape[0]
    q = q_ref[...]
    ys = []
    for h in range(n_heads):
        z = z_scr[h]
        ys.append(_dot(q[:, h * hd:(h + 1) * hd], z))
        z_scr[h] = _dot(p_ref[h], z) + g_ref[h]
    y_ref[...] = jnp.concatenate(ys, axis=1) + yl_ref[...]

    @pl.when(c == pl.num_programs(1) - 1)
    def _():
        zf_ref[...] = z_scr[...]


def rwkv_chunk_scan(p, g, qh, yl, z0, *, reverse):
    B, nc, nh = p.shape[:3]
    L, W = qh.shape[1:]
    T = L // nc
    cidx = (lambda c: nc - 1 - c) if reverse else (lambda c: c)
    mat = pl.BlockSpec((None, None, nh, HEAD_DIM, HEAD_DIM), lambda b, c: (b, cidx(c), 0, 0, 0))
    seq = pl.BlockSpec((None, T, W), lambda b, c: (b, cidx(c), 0))
    st = pl.BlockSpec((None, nh, HEAD_DIM, HEAD_DIM), lambda b, c: (b, 0, 0, 0))
    return pl.pallas_call(
        _rwkv_scan_body,
        grid=(B, nc),
        in_specs=[mat, mat, seq, seq, st],
        out_specs=[seq, st],
        out_shape=[jax.ShapeDtypeStruct((B, L, W), F32), jax.ShapeDtypeStruct(z0.shape, F32)],
        scratch_shapes=[pltpu.VMEM((nh, HEAD_DIM, HEAD_DIM), F32)],
        compiler_params=_params(("parallel", "arbitrary")),
        name="rwkv_chunk_scan",
    )(p, g, qh, yl, z0)


def _diff_attn_body(lam_ref, q_ref, kt_ref, v_ref, subg_ref, o_ref, q2_scr, m_scr, l_scr, acc_scr, *, out_scale):
    ki = pl.program_id(3)
    tq = q_ref.shape[0]

    @pl.when(ki == 0)
    def _():
        q = q_ref[...]
        lane = lax.broadcasted_iota(jnp.int32, q.shape, 1)
        q2_scr[0:tq, :] = jnp.where(lane < HEAD_DIM, q, jnp.zeros_like(q))
        q2_scr[tq:2 * tq, :] = jnp.where(lane >= HEAD_DIM, q, jnp.zeros_like(q))
        m_scr[...] = jnp.full_like(m_scr, -jnp.inf)
        l_scr[...] = jnp.zeros_like(l_scr)
        acc_scr[...] = jnp.zeros_like(acc_scr)

    kt = kt_ref[...]
    v = v_ref[...]
    rc = min(DF_ROW_CHUNK, 2 * tq)
    for c in range(2 * tq // rc):
        rows = pl.ds(c * rc, rc)
        s = jnp.dot(q2_scr[rows, :], kt, preferred_element_type=F32)
        m_prev = m_scr[rows, :]
        m_new = jnp.maximum(m_prev, jnp.max(s, axis=-1, keepdims=True))
        alpha = jnp.exp2(m_prev - m_new)
        p = jnp.exp2(s - m_new)
        l_scr[rows, :] = alpha * l_scr[rows, :] + jnp.sum(p, axis=-1, keepdims=True)
        acc_scr[rows, :] = alpha * acc_scr[rows, :] + jnp.dot(p.astype(BF16), v, preferred_element_type=F32)
        m_scr[rows, :] = m_new

    @pl.when(ki == pl.num_programs(3) - 1)
    def _():
        o = acc_scr[...] / l_scr[...]
        a = o[0:tq, :] - lam_ref[0, 0] * o[tq:2 * tq, :]
        y = a * lax.rsqrt(jnp.mean(a * a, axis=-1, keepdims=True) + NORM_EPS)
        o_ref[...] = (y * subg_ref[...] * out_scale).astype(o_ref.dtype)


def _key_tile(k, cap):
    best = 128
    t = 128
    while t <= min(k, cap):
        if k % t == 0:
            best = t
        t += 128
    return best


def diff_attention(q, kt, v, sub_g, lam, out_scale, *, tq=512, tk_cap=1280):
    B, L, W = q.shape
    K = kt.shape[2]
    hw = 2 * HEAD_DIM
    tq = _row_tile(L, tq)
    tk = _key_tile(K, tk_cap)
    return pl.pallas_call(
        functools.partial(_diff_attn_body, out_scale=out_scale),
        grid=(B, W // hw, L // tq, K // tk),
        in_specs=[
            pl.BlockSpec(memory_space=pltpu.SMEM),
            pl.BlockSpec((None, tq, hw), lambda b, h, i, j: (b, i, h)),
            pl.BlockSpec((None, hw, tk), lambda b, h, i, j: (b, h, j)),
            pl.BlockSpec((None, tk, hw), lambda b, h, i, j: (b, j, h)),
            pl.BlockSpec((1, hw), lambda b, h, i, j: (0, 0)),
        ],
        out_specs=pl.BlockSpec((None, tq, hw), lambda b, h, i, j: (b, i, h)),
        out_shape=jax.ShapeDtypeStruct((B, L, W), F32),
        scratch_shapes=[pltpu.VMEM((2 * tq, hw), BF16), pltpu.VMEM((2 * tq, 1), F32),
                        pltpu.VMEM((2 * tq, 1), F32), pltpu.VMEM((2 * tq, hw), F32)],
        compiler_params=_params(("parallel", "parallel", "parallel", "arbitrary")),
        name="diff_attention",
    )(lam, q, kt, v, sub_g)


def _swa_finish(parts, vals, sink, o_ref):
    m = sink
    for s in parts:
        m = jnp.maximum(m, jnp.max(s, axis=-1, keepdims=True))
    denom = jnp.exp(sink - m)
    acc = None
    for s, v in zip(parts, vals):
        p = jnp.exp(s - m)
        denom = denom + jnp.sum(p, axis=-1, keepdims=True)
        pv = jnp.dot(p.astype(BF16), v, preferred_element_type=F32)
        acc = pv if acc is None else acc + pv
    o = acc / denom
    o_ref[...] = o.reshape(o_ref.shape).astype(o_ref.dtype)


def _swa_band_body(q_ref, kp_ref, kn_ref, kx_ref, kc_ref, vp_ref, vn_ref, vx_ref, vc_ref, sink_ref, o_ref):
    n = pl.program_id(2)
    nb = pl.num_programs(2)
    grp, blk, hd = q_ref.shape
    q = q_ref[...].reshape(grp * blk, hd)
    iq = lax.broadcasted_iota(jnp.int32, (grp * blk, blk), 0) % blk
    j = lax.broadcasted_iota(jnp.int32, (grp * blk, blk), 1)
    s_prev = jnp.where((iq + blk - j <= SWA_WINDOW) & (n > 0), _dot_nt_bf16(q, kp_ref[...]), NEG_INF)
    s_cur = jnp.where(jnp.abs(iq - j) <= SWA_WINDOW, _dot_nt_bf16(q, kn_ref[...]), NEG_INF)
    s_next = jnp.where((j + blk - iq <= SWA_WINDOW) & (n < nb - 1), _dot_nt_bf16(q, kx_ref[...]), NEG_INF)
    s_ctx = _dot_nt_bf16(q, kc_ref[...])
    _swa_finish([s_prev, s_cur, s_next, s_ctx], [vp_ref[...], vn_ref[...], vx_ref[...], vc_ref[...]],
                sink_ref[...], o_ref)


def _swa_ctx_body(q_ref, kc_ref, vc_ref, sink_ref, o_ref):
    grp, blk, hd = q_ref.shape
    q = q_ref[...].reshape(grp * blk, hd)
    _swa_finish([_dot_nt_bf16(q, kc_ref[...])], [vc_ref[...]], sink_ref[...], o_ref)


def _dot_nt_bf16(a, b):
    return lax.dot_general(a, b, _NT, preferred_element_type=F32)


def swa_attention(q, k, v, kc, vc, sink_rows):
    B, kvh, grp, L, hd = q.shape
    C = kc.shape[2]
    blk = SWA_BLOCK
    nb = L // blk
    qspec = pl.BlockSpec((None, None, grp, blk, hd), lambda b, h, n: (b, h, 0, n, 0))
    prev = pl.BlockSpec((None, None, blk, hd), lambda b, h, n: (b, h, jnp.maximum(n - 1, 0), 0))
    cur = pl.BlockSpec((None, None, blk, hd), lambda b, h, n: (b, h, n, 0))
    nxt = pl.BlockSpec((None, None, blk, hd), lambda b, h, n: (b, h, jnp.minimum(n + 1, nb - 1), 0))
    cspec = pl.BlockSpec((None, None, C, hd), lambda b, h, n: (b, h, 0, 0))
    sspec = pl.BlockSpec((None, grp * blk, 1), lambda b, h, n: (h, 0, 0))
    return pl.pallas_call(
        _swa_band_body,
        grid=(B, kvh, nb),
        in_specs=[qspec, prev, cur, nxt, cspec, prev, cur, nxt, cspec, sspec],
        out_specs=qspec,
        out_shape=jax.ShapeDtypeStruct(q.shape, F32),
        compiler_params=_params(("parallel", "parallel", "parallel")),
        name="swa_attention",
    )(q, k, k, k, kc, v, v, v, vc, sink_rows)


def swa_context_attention(q, kc, vc, sink_rows):
    B, kvh, grp, L, hd = q.shape
    C = kc.shape[2]
    blk = SWA_BLOCK
    qspec = pl.BlockSpec((None, None, grp, blk, hd), lambda b, h, n: (b, h, 0, n, 0))
    cspec = pl.BlockSpec((None, None, C, hd), lambda b, h, n: (b, h, 0, 0))
    sspec = pl.BlockSpec((None, grp * blk, 1), lambda b, h, n: (h, 0, 0))
    return pl.pallas_call(
        _swa_ctx_body,
        grid=(B, kvh, L // blk),
        in_specs=[qspec, cspec, cspec, sspec],
        out_specs=qspec,
        out_shape=jax.ShapeDtypeStruct(q.shape, F32),
        compiler_params=_params(("parallel", "parallel", "parallel")),
        name="swa_context_attention",
    )(q, kc, vc, sink_rows)


def _rms_norm(x, g):
    xf = x.astype(F32)
    y = xf * lax.rsqrt(jnp.mean(xf * xf, axis=-1, keepdims=True) + NORM_EPS)
    return (y * g.astype(F32)).astype(x.dtype)


def _rope_tables(rows):
    n_freq = HEAD_DIM // 4
    inv = ROPE_BASE ** (-jnp.arange(n_freq, dtype=F32) / n_freq)
    row = jnp.repeat(jnp.arange(rows, dtype=F32), GRID_W)
    col = jnp.tile(jnp.arange(GRID_W, dtype=F32), rows)
    ang = jnp.concatenate([row[:, None] * inv, col[:, None] * inv], axis=-1)
    return jnp.cos(ang), jnp.sin(ang)


def _apply_rope(x, cos, sin):
    shp = (x.shape[1],) + (1,) * (x.ndim - 3) + (HEAD_DIM // 2,)
    c, s = cos.reshape(shp), sin.reshape(shp)
    x1, x2 = jnp.split(x.astype(F32), 2, axis=-1)
    return jnp.concatenate([x1 * c - x2 * s, x2 * c + x1 * s], axis=-1).astype(x.dtype)


def _short_conv3(x, w, b):
    xp = jnp.pad(x, ((0, 0), (1, 1), (0, 0)))
    return xp[:, :-2] * w[0] + xp[:, 1:-1] * w[1] + xp[:, 2:] * w[2] + b


def _hyena_filter_spectra(L, ch, f_w1, f_b1, f_w2, f_b2, f_w3, f_b3, f_freq, f_decay):
    t = jnp.arange(L, dtype=F32) / max(L - 1, 1)
    ang = 2 * math.pi * t[:, None] * jnp.arange(1, HY_POS_BANDS + 1, dtype=F32)
    feat = jnp.concatenate([t[:, None], jnp.sin(ang), jnp.cos(ang)], axis=-1)
    hp = lax.Precision.HIGHEST
    h = jnp.sin(f_freq * (jnp.dot(feat, f_w1, precision=hp) + f_b1))
    h = jnp.sin(f_freq * (jnp.dot(h, f_w2, precision=hp) + f_b2))
    h = jnp.dot(h, f_w3, precision=hp) + f_b3
    h = h * jnp.exp(-f_decay * t[:, None])
    h = h.reshape(L, HY_ORDER, 2, ch)
    h_fwd, h_bwd = h[:, :, 0], h[:, :, 1]
    l1 = jnp.sum(jnp.abs(h_fwd), axis=0) + jnp.sum(jnp.abs(h_bwd[1:]), axis=0)
    kern = jnp.concatenate([h_fwd, jnp.zeros((1, HY_ORDER, ch), F32), h_bwd[1:][::-1]], axis=0) / l1
    return jnp.fft.rfft(kern, axis=0)


def _fft_long_conv(z, spec, skip):
    L = z.shape[1]
    zf = jnp.fft.rfft(z, n=2 * L, axis=1)
    y = jnp.fft.irfft(zf * spec[None], n=2 * L, axis=1)[:, :L]
    return y + skip * z


def _hyena_branch(pa, conv_w, conv_b, spec, skip):
    u = _short_conv3(pa, conv_w, conv_b).astype(F32)
    v, x1, x2 = jnp.split(u, 3, axis=-1)
    z = x1 * _fft_long_conv(v, spec[:, 0], skip[0])
    return x2 * _fft_long_conv(z, spec[:, 1], skip[1])


def _swa_project(pb, q_g, k_g, n_heads):
    B, L = pb.shape[:2]
    q, k, v = jnp.split(pb, [n_heads * HEAD_DIM, (n_heads + SWA_KV_HEADS) * HEAD_DIM], axis=-1)
    q = _rms_norm(q.reshape(B, L, n_heads, HEAD_DIM), q_g)
    k = _rms_norm(k.reshape(B, L, SWA_KV_HEADS, HEAD_DIM), k_g)
    return q, k, v.reshape(B, L, SWA_KV_HEADS, HEAD_DIM)


def _sink_softmax(s, sink):
    m = jnp.maximum(jnp.max(s, axis=-1, keepdims=True), sink)
    p = jnp.exp(s - m)
    return p / (jnp.sum(p, axis=-1, keepdims=True) + jnp.exp(sink - m))


def _swa_context(qc, kc, vc, sink):
    B, C, n_heads = qc.shape[:3]
    grp = n_heads // SWA_KV_HEADS
    qg = qc.reshape(B, C, SWA_KV_HEADS, grp, HEAD_DIM)
    s = jnp.einsum('bqhgd,bkhd->bhgqk', qg, kc).astype(F32) * HEAD_DIM ** -0.5
    p = _sink_softmax(s, sink.astype(F32).reshape(1, SWA_KV_HEADS, grp, 1, 1))
    o = jnp.einsum('bhgqk,bkhd->bqhgd', p.astype(vc.dtype), vc)
    return o.reshape(B, C, n_heads * HEAD_DIM)


def _swa_latent(q, k, v, kc, vc, sink):
    B, L, n_heads = q.shape[:3]
    grp = n_heads // SWA_KV_HEADS
    nb = L // SWA_BLOCK
    qb = q.reshape(B, nb, SWA_BLOCK, SWA_KV_HEADS, grp, HEAD_DIM)

    def band(t):
        tb = t.reshape(B, nb, SWA_BLOCK, SWA_KV_HEADS, HEAD_DIM)
        tp = jnp.pad(tb, ((0, 0), (1, 1), (0, 0), (0, 0), (0, 0)))
        return jnp.concatenate([tp[:, :-2], tp[:, 1:-1], tp[:, 2:]], axis=2)

    kb, vb = band(k), band(v)
    scale = HEAD_DIM ** -0.5
    s_loc = jnp.einsum('bnqhgd,bnkhd->bnhgqk', qb, kb).astype(F32) * scale
    s_ctx = jnp.einsum('bnqhgd,bchd->bnhgqc', qb, kc).astype(F32) * scale
    q_rel = jnp.arange(SWA_BLOCK)[:, None] + SWA_BLOCK
    k_rel = jnp.arange(3 * SWA_BLOCK)[None, :]
    k_abs = (jnp.arange(nb)[:, None, None] - 1) * SWA_BLOCK + k_rel[None]
    valid = (jnp.abs(q_rel - k_rel) <= SWA_WINDOW)[None] & (k_abs >= 0) & (k_abs < L)
    s_loc = jnp.where(valid[None, :, None, None], s_loc, NEG_INF)
    s = jnp.concatenate([s_loc, s_ctx], axis=-1)
    p = _sink_softmax(s, sink.astype(F32).reshape(1, 1, SWA_KV_HEADS, grp, 1, 1)).astype(v.dtype)
    o = (jnp.einsum('bnhgqk,bnkhd->bnqhgd', p[..., :3 * SWA_BLOCK], vb)
         + jnp.einsum('bnhgqc,bchd->bnqhgd', p[..., 3 * SWA_BLOCK:], vc))
    return o.reshape(B, L, n_heads * HEAD_DIM)


def _token_shift(x, reverse):
    if reverse:
        return jnp.pad(x, ((0, 0), (0, 1), (0, 0)))[:, 1:]
    return jnp.pad(x, ((0, 0), (1, 0), (0, 0)))[:, :-1]


def _head_l2norm(x, n_heads):
    B, L, C = x.shape
    xh = x.reshape(B, L, n_heads, HEAD_DIM)
    xh = xh * lax.rsqrt(jnp.sum(xh * xh, axis=-1, keepdims=True) + 1e-12)
    return xh.reshape(B, L, C)


def _rwkv_prepare(feats, reverse, width, mu, w0, w2, a0, a2, g2, k_k, k_a):
    n_heads = width // HEAD_DIM
    xs = feats + mu * (_token_shift(feats, reverse) - feats)
    splits = (width, 2 * width, 3 * width, 3 * width + RW_DECAY_RANK, 3 * width + RW_DECAY_RANK + RW_A_RANK)
    r, k, v, wd, ad, gd = jnp.split(xs, splits, axis=-1)
    logw = -jnp.exp(-jax.nn.softplus(-(w0 + jnp.tanh(wd) @ w2)) - 0.5)
    a = jax.nn.sigmoid(a0 + ad @ a2)
    g = jax.nn.sigmoid(gd) @ g2
    kk = _head_l2norm(k * k_k, n_heads)
    k = k * (1 + (a - 1) * k_a)
    return r, logw, kk, a, k, v, g


def _wkv7(r, logw, kk, a, k, v, z0, reverse):
    p, g, qh, yl = rwkv_chunk_prep(r, logw, kk, a, k, v, reverse=reverse)
    y, z_fin = rwkv_chunk_scan(p, g, qh, yl, z0, reverse=reverse)
    return z_fin, y


def _rwkv_readout(y, r, k, v, g, r_k, ln_w, ln_b):
    B, L, C = y.shape
    n_heads = C // HEAD_DIM
    yh = y.reshape(B, L, n_heads, HEAD_DIM)
    mean = jnp.mean(yh, axis=-1, keepdims=True)
    var = jnp.mean(jnp.square(yh - mean), axis=-1, keepdims=True)
    yn = ((yh - mean) * lax.rsqrt(var + RW_GN_EPS)).reshape(B, L, C) * ln_w + ln_b
    bonus = jnp.sum((r * k * r_k).reshape(B, L, n_heads, HEAD_DIM), axis=-1, keepdims=True)
    bonus = (bonus * v.reshape(B, L, n_heads, HEAD_DIM)).reshape(B, L, C)
    return (yn + bonus) * g


def _rwkv_branch(f_ctx, f_lat, width, mu, w0, w2, a0, a2, g2, k_k, k_a, r_k, ln_w, ln_b, need_ctx):
    B = f_lat.shape[0]
    n_heads = width // HEAD_DIM
    y_lat = 0.0
    y_ctx = 0.0 if need_ctx else None
    for d in range(2):
        rev = d == 1
        dir_args = (width, mu[d], w0[d], w2[d], a0[d], a2[d], g2[d], k_k, k_a)
        r_c, w_c, kk_c, a_c, k_c, v_c, g_c = _rwkv_prepare(f_ctx, rev, *dir_args)
        z0 = jnp.zeros((B, n_heads, HEAD_DIM, HEAD_DIM), F32)
        z_ctx, o_c = _wkv7(r_c, w_c, kk_c, a_c, k_c, v_c, z0, rev)
        r_l, w_l, kk_l, a_l, k_l, v_l, g_l = _rwkv_prepare(f_lat, rev, *dir_args)
        _, o_l = _wkv7(r_l, w_l, kk_l, a_l, k_l, v_l, z_ctx, rev)
        y_lat = y_lat + _rwkv_readout(o_l, r_l, k_l, v_l, g_l, r_k, ln_w, ln_b)
        if need_ctx:
            y_ctx = y_ctx + _rwkv_readout(o_c, r_c, k_c, v_c, g_c, r_k, ln_w, ln_b)
    return y_lat, y_ctx


def _diff_project(pd, q_g, k_g):
    B, L, w3 = pd.shape
    n_heads = w3 // (3 * 2 * HEAD_DIM)
    q, k, v = jnp.split(pd, 3, axis=-1)
    q = _rms_norm(q.reshape(B, L, n_heads, 2, HEAD_DIM), q_g)
    k = _rms_norm(k.reshape(B, L, n_heads, 2, HEAD_DIM), k_g)
    return q, k, v.reshape(B, L, n_heads, 2 * HEAD_DIM)


def _diff_maps(q, k_all, v_all, lam):
    s = jnp.einsum('bqhid,bkhid->bhiqk', q, k_all).astype(F32) * HEAD_DIM ** -0.5
    p = jax.nn.softmax(s, axis=-1)
    a = p[:, :, 0] - lam * p[:, :, 1]
    return jnp.einsum('bhqk,bkhe->bqhe', a.astype(v_all.dtype), v_all)


def _diff_latent(q, k, v, kc, vc, lam):
    B, L, n_heads = q.shape[:3]
    nb = L // DF_BLOCK
    k_all = jnp.concatenate([kc, k], axis=1)
    v_all = jnp.concatenate([vc, v], axis=1)
    qb = jnp.moveaxis(q.reshape(B, nb, DF_BLOCK, n_heads, 2, HEAD_DIM), 1, 0)
    o = lax.map(lambda q_blk: _diff_maps(q_blk, k_all, v_all, lam), qb)
    return jnp.moveaxis(o, 0, 1).reshape(B, L, n_heads, 2 * HEAD_DIM)


def _diff_readout(o, sub_g, lam_init):
    B, L = o.shape[:2]
    return (_rms_norm(o, sub_g) * (1 - lam_init)).reshape(B, L, -1)


def _swa_branch(pb_l, pb_c, cos, sin, q_g, k_g, sink, need_ctx):
    B, L = pb_l.shape[:2]
    C = pb_c.shape[1]
    n_heads = pb_l.shape[-1] // HEAD_DIM - 2 * SWA_KV_HEADS
    grp = n_heads // SWA_KV_HEADS
    scale = HEAD_DIM ** -0.5
    q_l, k_l, v_l = _swa_project(pb_l, q_g, k_g, n_heads)
    q_l, k_l = _apply_rope(q_l, cos, sin), _apply_rope(k_l, cos, sin)
    q_c, k_c, v_c = _swa_project(pb_c, q_g, k_g, n_heads)

    def q_layout(q, n):
        return jnp.transpose((q * scale).astype(BF16).reshape(B, n, SWA_KV_HEADS, grp, HEAD_DIM), (0, 2, 3, 1, 4))

    def kv_layout(t):
        return jnp.transpose(t.astype(BF16), (0, 2, 1, 3))

    def o_layout(o, n):
        return jnp.transpose(o, (0, 3, 1, 2, 4)).reshape(B, n, n_heads * HEAD_DIM)

    sink_rows = jnp.repeat(sink.astype(F32).reshape(SWA_KV_HEADS, grp), SWA_BLOCK, axis=1)[..., None]
    kc, vc = kv_layout(k_c), kv_layout(v_c)
    y_l = o_layout(swa_attention(q_layout(q_l, L), kv_layout(k_l), kv_layout(v_l), kc, vc, sink_rows), L)
    y_c = None
    if need_ctx:
        y_c = o_layout(swa_context_attention(q_layout(q_c, C), kc, vc, sink_rows), C)
    return y_l, y_c


def _diff_branch(pd_l, pd_c, cos, sin, q_g, k_g, sub_g, lam, lam_init, need_ctx):
    B, L = pd_l.shape[:2]
    C = pd_c.shape[1]
    W = pd_l.shape[-1] // 3
    scale = HEAD_DIM ** -0.5 * math.log2(math.e)
    dq_l, dk_l, dv_l = _diff_project(pd_l, q_g, k_g)
    dq_l, dk_l = _apply_rope(dq_l, cos, sin), _apply_rope(dk_l, cos, sin)
    dq_c, dk_c, dv_c = _diff_project(pd_c, q_g, k_g)
    flat = lambda t, n: t.reshape(B, n, W)
    q_l = (flat(dq_l, L) * scale).astype(BF16)
    k_c, v_c = flat(dk_c, C).astype(BF16), flat(dv_c, C).astype(BF16)
    kt_c = jnp.transpose(k_c, (0, 2, 1))
    kt_all = jnp.concatenate([kt_c, jnp.transpose(flat(dk_l, L).astype(BF16), (0, 2, 1))], axis=2)
    v_all = jnp.concatenate([v_c, flat(dv_l, L).astype(BF16)], axis=1)
    sub = sub_g.astype(F32).reshape(1, 2 * HEAD_DIM)
    lam2 = lam.astype(F32).reshape(1, 1)
    y_l = diff_attention(q_l, kt_all, v_all, sub, lam2, 1.0 - lam_init)
    y_c = None
    if need_ctx:
        q_c = (flat(dq_c, C) * scale).astype(BF16)
        y_c = diff_attention(q_c, kt_c, v_c, sub, lam2, 1.0 - lam_init)
    return y_l, y_c


def _moe_route(logits_g, logits_e, n_tok):
    g_idx = jnp.argmax(logits_g, axis=-1)
    g_prob = jnp.take_along_axis(jax.nn.softmax(logits_g, axis=-1), g_idx[:, None], axis=-1)[:, 0]
    e_logits = logits_e.reshape(n_tok, MOE_GROUPS, MOE_PER_GROUP)
    e_logits = jnp.take_along_axis(e_logits, g_idx[:, None, None], axis=1)[:, 0]
    top_p, top_e = lax.top_k(jax.nn.softmax(e_logits, axis=-1), MOE_TOP_K)
    weights = g_prob[:, None] * top_p / jnp.sum(top_p, axis=-1, keepdims=True)
    flat_e = (g_idx[:, None] * MOE_PER_GROUP + top_e).reshape(-1).astype(jnp.int32)
    flat_tok = jnp.repeat(jnp.arange(n_tok, dtype=jnp.int32), MOE_TOP_K)
    n_assign = n_tok * MOE_TOP_K
    order = jnp.argsort(flat_e)
    se = flat_e[order]
    counts = jnp.bincount(flat_e, length=MOE_EXPERTS)
    padded = (counts + MOE_BLOCK - 1) // MOE_BLOCK * MOE_BLOCK
    pad_end = jnp.cumsum(padded)
    pad_start = pad_end - padded
    start = jnp.cumsum(counts) - counts
    dest = (pad_start[se] + jnp.arange(n_assign, dtype=jnp.int32) - start[se]).astype(jnp.int32)
    n_blocks = -(-n_assign // MOE_BLOCK) + MOE_EXPERTS
    P = n_blocks * MOE_BLOCK
    tok_buf = jnp.full((P,), n_tok, jnp.int32).at[dest].set(flat_tok[order])
    blk_e = jnp.minimum(jnp.searchsorted(pad_end, jnp.arange(n_blocks) * MOE_BLOCK, side='right'),
                        MOE_EXPERTS - 1).astype(jnp.int32)
    n_used = (pad_end[-1] // MOE_BLOCK).astype(jnp.int32).reshape(1)
    slot = jnp.zeros((n_assign,), jnp.int32).at[order].set(dest).reshape(n_tok, MOE_TOP_K)
    return tok_buf, blk_e, n_used, slot, weights


def _hier_moe(tokens_bf16, tokens_f32, rg_w, rg_b, re_w, re_b, w1, w3, w2):
    n_tok, d = tokens_bf16.shape
    hp = lax.Precision.HIGHEST
    logits_g = jnp.dot(tokens_f32, rg_w, precision=hp) + rg_b
    logits_e = jnp.dot(tokens_f32, re_w, precision=hp) + re_b
    tok_buf, blk_e, n_used, slot, weights = _moe_route(logits_g, logits_e, n_tok)
    x_pad = jnp.concatenate([tokens_bf16, jnp.zeros((1, d), BF16)], axis=0)
    xb = x_pad[tok_buf]
    yb = moe_experts(xb, blk_e, n_used, w1, w3, w2)
    wts = weights.astype(F32)
    return (yb[slot[:, 0]].astype(F32) * wts[:, 0:1] + yb[slot[:, 1]].astype(F32) * wts[:, 1:2])


def _modulate(x, g, shift, scale):
    return _rms_norm(x, g) * (1 + scale) + shift


def kernel(x, c, ctx, c_ctx, mod_w, mod_b, norm1_g, norm2_g, w_in, hy_conv_w, hy_conv_b, hy_f_w1, hy_f_b1, hy_f_w2, hy_f_b2, hy_f_w3, hy_f_b3, hy_f_freq, hy_f_decay, hy_skip, swa_q_g, swa_k_g, swa_sink, rw_mu, rw_w0, rw_w2, rw_a0, rw_a2, rw_g2, rw_k_k, rw_k_a, rw_r_k, rw_ln_w, rw_ln_b, df_q_g, df_k_g, df_lq1, df_lk1, df_lq2, df_lk2, df_sub_g, w_gate, b_gate, w_branch, w_out, moe_rg_w, moe_rg_b, moe_re_w, moe_re_b, moe_w1, moe_w3, moe_w2):
    B, L, D = x.shape
    C = ctx.shape[1]
    depth = mod_w.shape[0]
    bw = D // N_BRANCH
    a_cols = 3 * bw
    swa_heads = bw // HEAD_DIM
    b_cols = (swa_heads + 2 * SWA_KV_HEADS) * HEAD_DIM
    c_cols = 3 * bw + RW_DECAY_RANK + RW_A_RANK + RW_G_RANK
    in_splits = (a_cols, a_cols + b_cols, a_cols + b_cols + c_cols)
    rows = L // GRID_W
    cos, sin = _rope_tables(rows)
    hp = lax.Precision.HIGHEST

    x_lat = x.reshape(B * L, D)
    x_ctx = ctx.reshape(B * C, D)
    for l in range(depth):
        need_ctx = l < depth - 1
        mod_lat = jnp.dot(jax.nn.silu(c), mod_w[l], precision=hp) + mod_b[l]
        mod_ctx = jnp.dot(jax.nn.silu(c_ctx)[None], mod_w[l], precision=hp) + mod_b[l]
        sh1, sc1, g1, sh2, sc2, g2 = [t[:, None, :] for t in jnp.split(mod_lat, 6, axis=-1)]
        csh1, csc1, cg1, csh2, csc2, cg2 = [t[:, None, :] for t in jnp.split(mod_ctx, 6, axis=-1)]

        w_in_b = w_in[l].astype(BF16)
        p_lat, h_lat = norm_proj(x_lat, norm1_g[l][None], sc1, sh1, w_in_b, rows_per_mod=L)
        p_ctx, h_ctx = norm_proj(x_ctx, norm1_g[l][None], csc1, csh1, w_in_b, rows_per_mod=B * C)
        p_lat = p_lat.reshape(B, L, -1)
        p_ctx = p_ctx.reshape(B, C, -1)
        pa_l, pb_l, pc_l, pd_l = jnp.split(p_lat, in_splits, axis=-1)
        pa_c, pb_c, pc_c, pd_c = jnp.split(p_ctx, in_splits, axis=-1)

        spec_l = _hyena_filter_spectra(L, bw, hy_f_w1[l], hy_f_b1[l], hy_f_w2[l], hy_f_b2[l], hy_f_w3[l],
                                       hy_f_b3[l], hy_f_freq[l], hy_f_decay[l])
        ya_l = _hyena_branch(pa_l, hy_conv_w[l], hy_conv_b[l], spec_l, hy_skip[l])

        yb_l, yb_c = _swa_branch(pb_l, pb_c, cos, sin, swa_q_g[l], swa_k_g[l], swa_sink[l], need_ctx)

        yc_l, yc_c = _rwkv_branch(pc_c, pc_l, bw, rw_mu[l], rw_w0[l], rw_w2[l], rw_a0[l], rw_a2[l],
                                  rw_g2[l], rw_k_k[l], rw_k_a[l], rw_r_k[l], rw_ln_w[l], rw_ln_b[l], need_ctx)

        lam_init = 0.8 - 0.6 * math.exp(-0.3 * l)
        lam = (jnp.exp(jnp.sum(df_lq1[l] * df_lk1[l])) - jnp.exp(jnp.sum(df_lq2[l] * df_lk2[l])) + lam_init)
        yd_l, yd_c = _diff_branch(pd_l, pd_c, cos, sin, df_q_g[l], df_k_g[l], df_sub_g[l], lam, lam_init, need_ctx)

        wg_b = w_gate[l].astype(BF16)
        bg = b_gate[l][:, None, :]
        wb_b = w_branch[l].astype(BF16)
        wo_b = w_out[l].astype(BF16)
        ys_l = jnp.stack([t.reshape(B * L, bw).astype(BF16) for t in (ya_l, yb_l, yc_l, yd_l)])
        acc_l = merge_gated(h_lat, ys_l, wg_b, bg, wb_b)
        x_lat = resid_proj(x_lat, acc_l, wo_b, g1, rows_per_mod=L)
        if need_ctx:
            spec_c = _hyena_filter_spectra(C, bw, hy_f_w1[l], hy_f_b1[l], hy_f_w2[l], hy_f_b2[l], hy_f_w3[l],
                                           hy_f_b3[l], hy_f_freq[l], hy_f_decay[l])
            ya_c = _hyena_branch(pa_c, hy_conv_w[l], hy_conv_b[l], spec_c, hy_skip[l])
            ys_c = jnp.stack([t.reshape(B * C, bw).astype(BF16) for t in (ya_c, yb_c, yc_c, yd_c)])
            acc_c = merge_gated(h_ctx, ys_c, wg_b, bg, wb_b)
            x_ctx = resid_proj(x_ctx, acc_c, wo_b, cg1, rows_per_mod=B * C)

        w1_b, w3_b, w2_b = moe_w1[l].astype(BF16), moe_w3[l].astype(BF16), moe_w2[l].astype(BF16)
        moe_args = (moe_rg_w[l], moe_rg_b[l], moe_re_w[l], moe_re_b[l], w1_b, w3_b, w2_b)
        hm_lat = _modulate(x_lat.reshape(B, L, D), norm2_g[l], sh2, sc2).reshape(B * L, D)
        if need_ctx:
            hm_ctx = _modulate(x_ctx.reshape(B, C, D), norm2_g[l], csh2, csc2).reshape(B * C, D)
            tokens = jnp.concatenate([hm_ctx, hm_lat], axis=0)
            out = _hier_moe(tokens.astype(BF16), tokens, *moe_args)
            x_ctx = x_ctx + (cg2 * out[:B * C].reshape(1, B * C, D)).reshape(B * C, D)
            x_lat = x_lat + (g2 * out[B * C:].reshape(B, L, D)).reshape(B * L, D)
        else:
            out = _hier_moe(hm_lat.astype(BF16), hm_lat, *moe_args)
            x_lat = x_lat + (g2 * out.reshape(B, L, D)).reshape(B * L, D)
    return x_lat.reshape(B, L, D)
```

```python
import functools
import math

import jax
import jax.numpy as jnp
from jax import lax
from jax.experimental import pallas as pl
from jax.experimental.pallas import tpu as pltpu

F32 = jnp.float32
BF16 = jnp.bfloat16

GRID_W = 64
HEAD_DIM = 64
ROPE_BASE = 10000.0
NORM_EPS = 1e-6
NEG_INF = -1e30
N_BRANCH = 4
HY_ORDER = 2
HY_POS_BANDS = 8
SWA_KV_HEADS = 2
SWA_WINDOW = 128
SWA_BLOCK = 128
RW_DECAY_RANK = 64
RW_A_RANK = 64
RW_G_RANK = 128
RW_GN_EPS = 64e-5
DF_BLOCK = 128
DF_ROW_CHUNK = 256
MOE_GROUPS = 4
MOE_PER_GROUP = 8
MOE_EXPERTS = MOE_GROUPS * MOE_PER_GROUP
MOE_TOP_K = 2
MOE_BLOCK = 256

VMEM_LIMIT_BYTES = 56 * 1024 * 1024


def _row_tile(m, pref):
    t = min(pref, m)
    while m % t:
        t //= 2
    return t


def _params(sem):
    return pltpu.CompilerParams(dimension_semantics=sem, vmem_limit_bytes=VMEM_LIMIT_BYTES)


def _norm_proj_body(x_ref, g_ref, sc_ref, sh_ref, w_ref, p_ref, h_ref, h_scr):
    @pl.when(pl.program_id(1) == 0)
    def _():
        x = x_ref[...].astype(F32)
        y = x * lax.rsqrt(jnp.mean(x * x, axis=-1, keepdims=True) + NORM_EPS)
        h = y * g_ref[...] * (1.0 + sc_ref[...]) + sh_ref[...]
        h_scr[...] = h.astype(BF16)
        h_ref[...] = h_scr[...]

    p_ref[...] = jnp.dot(h_scr[...], w_ref[...], preferred_element_type=F32).astype(p_ref.dtype)


def norm_proj(x, g, scale, shift, w, *, rows_per_mod, tm=1024, tn=512, out_dtype=F32):
    m, d = x.shape
    n = w.shape[1]
    tm = _row_tile(rows_per_mod, tm)
    tn = _row_tile(n, tn)
    tiles_per_mod = rows_per_mod // tm
    mod_map = lambda i, j: (i // tiles_per_mod, 0, 0)
    return pl.pallas_call(
        _norm_proj_body,
        grid=(m // tm, n // tn),
        in_specs=[
            pl.BlockSpec((tm, d), lambda i, j: (i, 0)),
            pl.BlockSpec((1, d), lambda i, j: (0, 0)),
            pl.BlockSpec((None, 1, d), mod_map),
            pl.BlockSpec((None, 1, d), mod_map),
            pl.BlockSpec((d, tn), lambda i, j: (0, j)),
        ],
        out_specs=[
            pl.BlockSpec((tm, tn), lambda i, j: (i, j)),
            pl.BlockSpec((tm, d), lambda i, j: (i, 0)),
        ],
        out_shape=[jax.ShapeDtypeStruct((m, n), out_dtype), jax.ShapeDtypeStruct((m, d), BF16)],
        scratch_shapes=[pltpu.VMEM((tm, d), BF16)],
        compiler_params=_params(("parallel", "arbitrary")),
        name="norm_proj",
    )(x, g, scale, shift, w)


def _merge_body(h_ref, y_ref, wg_ref, bg_ref, wb_ref, o_ref, acc_ref):
    b = pl.program_id(2)
    gate = jax.nn.sigmoid(jnp.dot(h_ref[...], wg_ref[...], preferred_element_type=F32) + bg_ref[...])
    val = gate * jnp.dot(y_ref[...], wb_ref[...], preferred_element_type=F32)

    @pl.when(b == 0)
    def _():
        acc_ref[...] = val

    @pl.when(b > 0)
    def _():
        acc_ref[...] += val

    @pl.when(b == N_BRANCH - 1)
    def _():
        o_ref[...] = acc_ref[...].astype(o_ref.dtype)


def merge_gated(h, ys, wg, bg, wb, *, tm=1024, tn=512):
    m, d = h.shape
    w = ys.shape[-1]
    tm = _row_tile(m, tm)
    tn = _row_tile(d, tn)
    return pl.pallas_call(
        _merge_body,
        grid=(m // tm, d // tn, N_BRANCH),
        in_specs=[
            pl.BlockSpec((tm, d), lambda i, j, b: (i, 0)),
            pl.BlockSpec((None, tm, w), lambda i, j, b: (b, i, 0)),
            pl.BlockSpec((None, d, tn), lambda i, j, b: (b, 0, j)),
            pl.BlockSpec((None, 1, tn), lambda i, j, b: (b, 0, j)),
            pl.BlockSpec((None, w, tn), lambda i, j, b: (b, 0, j)),
        ],
        out_specs=pl.BlockSpec((tm, tn), lambda i, j, b: (i, j)),
        out_shape=jax.ShapeDtypeStruct((m, d), BF16),
        scratch_shapes=[pltpu.VMEM((tm, tn), F32)],
        compiler_params=_params(("parallel", "arbitrary", "arbitrary")),
        name="merge_gated",
    )(h, ys, wg, bg, wb)


def _resid_proj_body(x_ref, a_ref, w_ref, gate_ref, o_ref):
    y = jnp.dot(a_ref[...], w_ref[...], preferred_element_type=F32)
    o_ref[...] = (x_ref[...].astype(F32) + gate_ref[...] * y).astype(o_ref.dtype)


def resid_proj(x, a, w, gate, *, rows_per_mod, tm=1024, tn=512):
    m, d = x.shape
    k = a.shape[1]
    tm = _row_tile(rows_per_mod, tm)
    tn = _row_tile(d, tn)
    tiles_per_mod = rows_per_mod // tm
    return pl.pallas_call(
        _resid_proj_body,
        grid=(m // tm, d // tn),
        in_specs=[
            pl.BlockSpec((tm, tn), lambda i, j: (i, j)),
            pl.BlockSpec((tm, k), lambda i, j: (i, 0)),
            pl.BlockSpec((k, tn), lambda i, j: (0, j)),
            pl.BlockSpec((None, 1, tn), lambda i, j: (i // tiles_per_mod, 0, j)),
        ],
        out_specs=pl.BlockSpec((tm, tn), lambda i, j: (i, j)),
        out_shape=jax.ShapeDtypeStruct((m, d), x.dtype),
        compiler_params=_params(("parallel", "arbitrary")),
        name="resid_proj",
    )(x, a, w, gate)


def _moe_body(blk_e_ref, n_used_ref, x_ref, w1_ref, w3_ref, w2_ref, o_ref):
    i = pl.program_id(0)

    @pl.when(i < n_used_ref[0])
    def _():
        x = x_ref[...]
        a = jnp.dot(x, w1_ref[...], preferred_element_type=F32)
        b = jnp.dot(x, w3_ref[...], preferred_element_type=F32)
        hdn = (a * jax.nn.sigmoid(a) * b).astype(BF16)
        o_ref[...] = jnp.dot(hdn, w2_ref[...], preferred_element_type=F32).astype(o_ref.dtype)

    @pl.when(i >= n_used_ref[0])
    def _():
        o_ref[...] = jnp.zeros_like(o_ref)


def moe_experts(xb, blk_e, n_used, w1, w3, w2):
    p, d = xb.shape
    hid = w1.shape[-1]
    n_blocks = p // MOE_BLOCK
    grid_spec = pltpu.PrefetchScalarGridSpec(
        num_scalar_prefetch=2,
        grid=(n_blocks,),
        in_specs=[
            pl.BlockSpec((MOE_BLOCK, d), lambda i, e, n: (i, 0)),
            pl.BlockSpec((None, d, hid), lambda i, e, n: (e[i], 0, 0)),
            pl.BlockSpec((None, d, hid), lambda i, e, n: (e[i], 0, 0)),
            pl.BlockSpec((None, hid, d), lambda i, e, n: (e[i], 0, 0)),
        ],
        out_specs=pl.BlockSpec((MOE_BLOCK, d), lambda i, e, n: (i, 0)),
    )
    return pl.pallas_call(
        _moe_body,
        grid_spec=grid_spec,
        out_shape=jax.ShapeDtypeStruct((p, d), BF16),
        compiler_params=_params(("arbitrary",)),
        name="moe_experts",
    )(blk_e, n_used, xb, w1, w3, w2)


RW_CHUNK = 64
_HI = lax.Precision.HIGHEST
_NT = (((1,), (1,)), ((), ()))
_TN = (((0,), (0,)), ((), ()))


def _dot(a, b, dims=None):
    if dims is None:
        return jnp.dot(a, b, preferred_element_type=F32, precision=_HI)
    return lax.dot_general(a, b, dims, preferred_element_type=F32, precision=_HI)


def _bdot(a, b, dims=None):
    a, b = a.astype(BF16), b.astype(BF16)
    if dims is None:
        return jnp.dot(a, b, preferred_element_type=F32)
    return lax.dot_general(a, b, dims, preferred_element_type=F32)


def _rwkv_prep_body(r_ref, lw_ref, kk_ref, a_ref, k_ref, v_ref, p_ref, g_ref, q_ref, yl_ref, *, reverse):
    T = r_ref.shape[0]
    hd = HEAD_DIM
    row = lax.broadcasted_iota(jnp.int32, (T, T), 0)
    col = lax.broadcasted_iota(jnp.int32, (T, T), 1)
    if reverse:
        strict, incl = col > row, col >= row
    else:
        strict, incl = col < row, col <= row
    lw = lw_ref[...]
    tri = incl.astype(BF16)
    lw1 = lw.astype(BF16)
    res1 = lw - lw1.astype(F32)
    lw2 = res1.astype(BF16)
    lw3 = (res1 - lw2.astype(F32)).astype(BF16)
    cum = (jnp.dot(tri, lw1, preferred_element_type=F32) + jnp.dot(tri, lw2, preferred_element_type=F32)
           + jnp.dot(tri, lw3, preferred_element_type=F32))
    total = jnp.sum(lw, axis=0, keepdims=True)
    e_in = jnp.exp(cum)
    e_ex = jnp.exp(cum - lw)
    e_ninv = jnp.exp(-cum)
    e_rem = jnp.exp(total - cum)
    gam = jnp.exp(total)
    kk = kk_ref[...]
    kka = kk * a_ref[...]
    k = k_ref[...]
    nt = -kk * e_ex
    rt = r_ref[...] * e_in
    at = kka * e_ninv
    kt = k * e_ninv
    ac = kka * e_rem
    kc = k * e_rem
    v = v_ref[...]
    eye = lax.broadcasted_iota(jnp.int32, (hd, hd), 0) == lax.broadcasted_iota(jnp.int32, (hd, hd), 1)
    n_heads = r_ref.shape[1] // hd
    zeros = jnp.zeros((T, hd), F32)
    steps = max(1, (T - 1).bit_length())
    heads = range(n_heads)
    sls = [slice(hh * hd, (hh + 1) * hd) for hh in heads]
    bigs = [_bdot(jnp.concatenate([nt[:, sl], rt[:, sl]], axis=0),
                  jnp.concatenate([at[:, sl], kt[:, sl]], axis=0), _NT) for sl in sls]
    a_ak = [jnp.where(strict, big[:T, T:], 0.0) for big in bigs]
    pws = [jnp.where(strict, big[:T, :T], 0.0) for big in bigs]
    lhs_top = [jnp.concatenate([jnp.where(incl, big[T:, :T], 0.0), jnp.where(incl, big[T:, T:], 0.0)], axis=1)
               for big in bigs]
    xs = [jnp.concatenate([nt[:, sl], _bdot(m, v[:, sl])], axis=1) for m, sl in zip(a_ak, sls)]
    for it in range(steps):
        if it < steps - 1:
            boths = [_bdot(pw, jnp.concatenate([x, pw], axis=1)) for pw, x in zip(pws, xs)]
            xs = [x + both[:, :2 * hd] for x, both in zip(xs, boths)]
            pws = [both[:, 2 * hd:] for both in boths]
        else:
            xs = [x + _bdot(pw, x) for pw, x in zip(pws, xs)]
    rhs = [jnp.concatenate([x, jnp.concatenate([zeros, v[:, sl]], axis=1)], axis=0) for x, sl in zip(xs, sls)]
    tops = [_bdot(lt, rh) for lt, rh in zip(lhs_top, rhs)]
    bots = [_bdot(jnp.concatenate([ac[:, sl], kc[:, sl]], axis=0), rh, _TN) for sl, rh in zip(sls, rhs)]
    for hh in heads:
        p_ref[hh] = jnp.where(eye, gam[:, sls[hh]], 0.0) + bots[hh][:, :hd]
        g_ref[hh] = bots[hh][:, hd:]
    q_ref[...] = jnp.concatenate([rt[:, sl] + top[:, :hd] for sl, top in zip(sls, tops)], axis=1)
    yl_ref[...] = jnp.concatenate([top[:, hd:] for top in tops], axis=1)


def rwkv_chunk_prep(r, logw, kk, a, k, v, *, reverse):
    B, L, W = r.shape
    T = RW_CHUNK
    nh = W // HEAD_DIM
    nc = L // T
    blk = pl.BlockSpec((None, T, W), lambda b, c: (b, c, 0))
    mat = pl.BlockSpec((None, None, nh, HEAD_DIM, HEAD_DIM), lambda b, c: (b, c, 0, 0, 0))
    mat_shape = jax.ShapeDtypeStruct((B, nc, nh, HEAD_DIM, HEAD_DIM), F32)
    seq_shape = jax.ShapeDtypeStruct((B, L, W), F32)
    return pl.pallas_call(
        functools.partial(_rwkv_prep_body, reverse=reverse),
        grid=(B, nc),
        in_specs=[blk] * 6,
        out_specs=[mat, mat, blk, blk],
        out_shape=[mat_shape, mat_shape, seq_shape, seq_shape],
        compiler_params=_params(("parallel", "parallel")),
        name="rwkv_chunk_prep",
    )(r, logw, kk, a, k, v)


def _rwkv_scan_body(p_ref, g_ref, q_ref, yl_ref, z0_ref, y_ref, zf_ref, z_scr):
    c = pl.program_id(1)

    @pl.when(c == 0)
    def _():
        z_scr[...] = z0_ref[...]

    hd = HEAD_DIM
    n_heads = z_scr.shape[0]
    q = q_ref[...]
    ys = []
    for h in range(n_heads):
        z = z_scr[h]
        ys.append(_dot(q[:, h * hd:(h + 1) * hd], z))
        z_scr[h] = _dot(p_ref[h], z) + g_ref[h]
    y_ref[...] = jnp.concatenate(ys, axis=1) + yl_ref[...]

    @pl.when(c == pl.num_programs(1) - 1)
    def _():
        zf_ref[...] = z_scr[...]


def rwkv_chunk_scan(p, g, qh, yl, z0, *, reverse):
    B, nc, nh = p.shape[:3]
    L, W = qh.shape[1:]
    T = L // nc
    cidx = (lambda c: nc - 1 - c) if reverse else (lambda c: c)
    mat = pl.BlockSpec((None, None, nh, HEAD_DIM, HEAD_DIM), lambda b, c: (b, cidx(c), 0, 0, 0))
    seq = pl.BlockSpec((None, T, W), lambda b, c: (b, cidx(c), 0))
    st = pl.BlockSpec((None, nh, HEAD_DIM, HEAD_DIM), lambda b, c: (b, 0, 0, 0))
    return pl.pallas_call(
        _rwkv_scan_body,
        grid=(B, nc),
        in_specs=[mat, mat, seq, seq, st],
        out_specs=[seq, st],
        out_shape=[jax.ShapeDtypeStruct((B, L, W), F32), jax.ShapeDtypeStruct(z0.shape, F32)],
        scratch_shapes=[pltpu.VMEM((nh, HEAD_DIM, HEAD_DIM), F32)],
        compiler_params=_params(("parallel", "arbitrary")),
        name="rwkv_chunk_scan",
    )(p, g, qh, yl, z0)


def _diff_attn_body(lam_ref, q_ref, kt_ref, v_ref, subg_ref, o_ref, q2_scr, m_scr, acc_scr, *, out_scale):
    ki = pl.program_id(3)
    tq = q_ref.shape[0]

    @pl.when(ki == 0)
    def _():
        q = q_ref[...]
        lane = lax.broadcasted_iota(jnp.int32, q.shape, 1)
        q2_scr[0:tq, :] = jnp.where(lane < HEAD_DIM, q, jnp.zeros_like(q))
        q2_scr[tq:2 * tq, :] = jnp.where(lane >= HEAD_DIM, q, jnp.zeros_like(q))
        m_scr[...] = jnp.full_like(m_scr, -jnp.inf)
        acc_scr[...] = jnp.zeros_like(acc_scr)

    kt = kt_ref[...]
    v = v_ref[...]
    rc = min(DF_ROW_CHUNK, 2 * tq)
    n_chunks = 2 * tq // rc
    score = lambda c: jnp.dot(q2_scr[pl.ds(c * rc, rc), :], kt, preferred_element_type=F32)
    s_next = score(0)
    for c in range(n_chunks):
        rows = pl.ds(c * rc, rc)
        s = s_next
        if c + 1 < n_chunks:
            s_next = score(c + 1)
        m_prev = m_scr[rows, :]
        m_new = jnp.maximum(m_prev, jnp.max(s, axis=-1, keepdims=True))
        alpha = jnp.exp2(m_prev - m_new)
        p = jnp.exp2(s - m_new)
        acc_scr[rows, :] = alpha * acc_scr[rows, :] + jnp.dot(p.astype(BF16), v, preferred_element_type=F32)
        m_scr[rows, :] = m_new

    @pl.when(ki == pl.num_programs(3) - 1)
    def _():
        hw = o_ref.shape[-1]
        o = acc_scr[:, 0:hw] / acc_scr[:, hw:hw + 1]
        a = o[0:tq, :] - lam_ref[0, 0] * o[tq:2 * tq, :]
        y = a * lax.rsqrt(jnp.mean(a * a, axis=-1, keepdims=True) + NORM_EPS)
        o_ref[...] = (y * subg_ref[...] * out_scale).astype(o_ref.dtype)


def _key_tile(k, cap):
    best = 128
    t = 128
    while t <= min(k, cap):
        if k % t == 0:
            best = t
        t += 128
    return best


def diff_attention(q, kt, v, sub_g, lam, out_scale, *, tq=512, tk_cap=1280):
    B, L, W = q.shape
    K = kt.shape[2]
    hw = 2 * HEAD_DIM
    tq = _row_tile(L, tq)
    tk = _key_tile(K, tk_cap)
    vh = v.reshape(B, K, W // hw, hw)
    v = jnp.concatenate([vh, jnp.ones_like(vh)], axis=-1).reshape(B, K, 2 * W)
    return pl.pallas_call(
        functools.partial(_diff_attn_body, out_scale=out_scale),
        grid=(B, W // hw, L // tq, K // tk),
        in_specs=[
            pl.BlockSpec(memory_space=pltpu.SMEM),
            pl.BlockSpec((None, tq, hw), lambda b, h, i, j: (b, i, h)),
            pl.BlockSpec((None, hw, tk), lambda b, h, i, j: (b, h, j)),
            pl.BlockSpec((None, tk, 2 * hw), lambda b, h, i, j: (b, j, h)),
            pl.BlockSpec((1, hw), lambda b, h, i, j: (0, 0)),
        ],
        out_specs=pl.BlockSpec((None, tq, hw), lambda b, h, i, j: (b, i, h)),
        out_shape=jax.ShapeDtypeStruct((B, L, W), F32),
        scratch_shapes=[pltpu.VMEM((2 * tq, hw), BF16), pltpu.VMEM((2 * tq, 1), F32),
                        pltpu.VMEM((2 * tq, 2 * hw), F32)],
        compiler_params=_params(("parallel", "parallel", "parallel", "arbitrary")),
        name="diff_attention",
    )(lam, q, kt, v, sub_g)


def _swa_finish(parts, vals, sink, o_ref):
    m = sink
    for s in parts:
        m = jnp.maximum(m, jnp.max(s, axis=-1, keepdims=True))
    denom = jnp.exp(sink - m)
    acc = None
    for s, v in zip(parts, vals):
        p = jnp.exp(s - m)
        denom = denom + jnp.sum(p, axis=-1, keepdims=True)
        pv = jnp.dot(p.astype(BF16), v, preferred_element_type=F32)
        acc = pv if acc is None else acc + pv
    o = acc / denom
    o_ref[...] = o.reshape(o_ref.shape).astype(o_ref.dtype)


def _swa_band_body(q_ref, kp_ref, kn_ref, kx_ref, kc_ref, vp_ref, vn_ref, vx_ref, vc_ref, sink_ref, o_ref):
    n = pl.program_id(2)
    nb = pl.num_programs(2)
    grp, blk, hd = q_ref.shape
    q = q_ref[...].reshape(grp * blk, hd)
    iq = lax.broadcasted_iota(jnp.int32, (grp * blk, blk), 0) % blk
    j = lax.broadcasted_iota(jnp.int32, (grp * blk, blk), 1)
    s_prev = jnp.where((iq + blk - j <= SWA_WINDOW) & (n > 0), _dot_nt_bf16(q, kp_ref[...]), NEG_INF)
    s_cur = jnp.where(jnp.abs(iq - j) <= SWA_WINDOW, _dot_nt_bf16(q, kn_ref[...]), NEG_INF)
    s_next = jnp.where((j + blk - iq <= SWA_WINDOW) & (n < nb - 1), _dot_nt_bf16(q, kx_ref[...]), NEG_INF)
    s_ctx = _dot_nt_bf16(q, kc_ref[...])
    _swa_finish([s_prev, s_cur, s_next, s_ctx], [vp_ref[...], vn_ref[...], vx_ref[...], vc_ref[...]],
                sink_ref[...], o_ref)


def _swa_ctx_body(q_ref, kc_ref, vc_ref, sink_ref, o_ref):
    grp, blk, hd = q_ref.shape
    q = q_ref[...].reshape(grp * blk, hd)
    _swa_finish([_dot_nt_bf16(q, kc_ref[...])], [vc_ref[...]], sink_ref[...], o_ref)


def _dot_nt_bf16(a, b):
    return lax.dot_general(a, b, _NT, preferred_element_type=F32)


def swa_attention(q, k, v, kc, vc, sink_rows):
    B, kvh, grp, L, hd = q.shape
    C = kc.shape[2]
    blk = SWA_BLOCK
    nb = L // blk
    qspec = pl.BlockSpec((None, None, grp, blk, hd), lambda b, h, n: (b, h, 0, n, 0))
    prev = pl.BlockSpec((None, None, blk, hd), lambda b, h, n: (b, h, jnp.maximum(n - 1, 0), 0))
    cur = pl.BlockSpec((None, None, blk, hd), lambda b, h, n: (b, h, n, 0))
    nxt = pl.BlockSpec((None, None, blk, hd), lambda b, h, n: (b, h, jnp.minimum(n + 1, nb - 1), 0))
    cspec = pl.BlockSpec((None, None, C, hd), lambda b, h, n: (b, h, 0, 0))
    sspec = pl.BlockSpec((None, grp * blk, 1), lambda b, h, n: (h, 0, 0))
    return pl.pallas_call(
        _swa_band_body,
        grid=(B, kvh, nb),
        in_specs=[qspec, prev, cur, nxt, cspec, prev, cur, nxt, cspec, sspec],
        out_specs=qspec,
        out_shape=jax.ShapeDtypeStruct(q.shape, F32),
        compiler_params=_params(("parallel", "parallel", "parallel")),
        name="swa_attention",
    )(q, k, k, k, kc, v, v, v, vc, sink_rows)


def swa_context_attention(q, kc, vc, sink_rows):
    B, kvh, grp, L, hd = q.shape
    C = kc.shape[2]
    blk = SWA_BLOCK
    qspec = pl.BlockSpec((None, None, grp, blk, hd), lambda b, h, n: (b, h, 0, n, 0))
    cspec = pl.BlockSpec((None, None, C, hd), lambda b, h, n: (b, h, 0, 0))
    sspec = pl.BlockSpec((None, grp * blk, 1), lambda b, h, n: (h, 0, 0))
    return pl.pallas_call(
        _swa_ctx_body,
        grid=(B, kvh, L // blk),
        in_specs=[qspec, cspec, cspec, sspec],
        out_specs=qspec,
        out_shape=jax.ShapeDtypeStruct(q.shape, F32),
        compiler_params=_params(("parallel", "parallel", "parallel")),
        name="swa_context_attention",
    )(q, kc, vc, sink_rows)


def _rms_norm(x, g):
    xf = x.astype(F32)
    y = xf * lax.rsqrt(jnp.mean(xf * xf, axis=-1, keepdims=True) + NORM_EPS)
    return (y * g.astype(F32)).astype(x.dtype)


def _rope_tables(rows):
    n_freq = HEAD_DIM // 4
    inv = ROPE_BASE ** (-jnp.arange(n_freq, dtype=F32) / n_freq)
    row = jnp.repeat(jnp.arange(rows, dtype=F32), GRID_W)
    col = jnp.tile(jnp.arange(GRID_W, dtype=F32), rows)
    ang = jnp.concatenate([row[:, None] * inv, col[:, None] * inv], axis=-1)
    return jnp.cos(ang), jnp.sin(ang)


def _apply_rope(x, cos, sin):
    shp = (x.shape[1],) + (1,) * (x.ndim - 3) + (HEAD_DIM // 2,)
    c, s = cos.reshape(shp), sin.reshape(shp)
    x1, x2 = jnp.split(x.astype(F32), 2, axis=-1)
    return jnp.concatenate([x1 * c - x2 * s, x2 * c + x1 * s], axis=-1).astype(x.dtype)


def _short_conv3(x, w, b):
    xp = jnp.pad(x, ((0, 0), (1, 1), (0, 0)))
    return xp[:, :-2] * w[0] + xp[:, 1:-1] * w[1] + xp[:, 2:] * w[2] + b


def _hyena_filter_spectra(L, ch, f_w1, f_b1, f_w2, f_b2, f_w3, f_b3, f_freq, f_decay):
    t = jnp.arange(L, dtype=F32) / max(L - 1, 1)
    ang = 2 * math.pi * t[:, None] * jnp.arange(1, HY_POS_BANDS + 1, dtype=F32)
    feat = jnp.concatenate([t[:, None], jnp.sin(ang), jnp.cos(ang)], axis=-1)
    hp = lax.Precision.HIGHEST
    h = jnp.sin(f_freq * (jnp.dot(feat, f_w1, precision=hp) + f_b1))
    h = jnp.sin(f_freq * (jnp.dot(h, f_w2, precision=hp) + f_b2))
    h = jnp.dot(h, f_w3, precision=hp) + f_b3
    h = h * jnp.exp(-f_decay * t[:, None])
    h = h.reshape(L, HY_ORDER, 2, ch)
    h_fwd, h_bwd = h[:, :, 0], h[:, :, 1]
    l1 = jnp.sum(jnp.abs(h_fwd), axis=0) + jnp.sum(jnp.abs(h_bwd[1:]), axis=0)
    kern = jnp.concatenate([h_fwd, jnp.zeros((1, HY_ORDER, ch), F32), h_bwd[1:][::-1]], axis=0) / l1
    return jnp.fft.rfft(kern, axis=0)


def _fft_long_conv(z, spec, skip):
    L = z.shape[1]
    zf = jnp.fft.rfft(z, n=2 * L, axis=1)
    y = jnp.fft.irfft(zf * spec[None], n=2 * L, axis=1)[:, :L]
    return y + skip * z


def _hyena_branch(pa, conv_w, conv_b, spec, skip):
    u = _short_conv3(pa, conv_w, conv_b).astype(F32)
    v, x1, x2 = jnp.split(u, 3, axis=-1)
    z = x1 * _fft_long_conv(v, spec[:, 0], skip[0])
    return x2 * _fft_long_conv(z, spec[:, 1], skip[1])


def _swa_project(pb, q_g, k_g, n_heads):
    B, L = pb.shape[:2]
    q, k, v = jnp.split(pb, [n_heads * HEAD_DIM, (n_heads + SWA_KV_HEADS) * HEAD_DIM], axis=-1)
    q = _rms_norm(q.reshape(B, L, n_heads, HEAD_DIM), q_g)
    k = _rms_norm(k.reshape(B, L, SWA_KV_HEADS, HEAD_DIM), k_g)
    return q, k, v.reshape(B, L, SWA_KV_HEADS, HEAD_DIM)


def _sink_softmax(s, sink):
    m = jnp.maximum(jnp.max(s, axis=-1, keepdims=True), sink)
    p = jnp.exp(s - m)
    return p / (jnp.sum(p, axis=-1, keepdims=True) + jnp.exp(sink - m))


def _swa_context(qc, kc, vc, sink):
    B, C, n_heads = qc.shape[:3]
    grp = n_heads // SWA_KV_HEADS
    qg = qc.reshape(B, C, SWA_KV_HEADS, grp, HEAD_DIM)
    s = jnp.einsum('bqhgd,bkhd->bhgqk', qg, kc).astype(F32) * HEAD_DIM ** -0.5
    p = _sink_softmax(s, sink.astype(F32).reshape(1, SWA_KV_HEADS, grp, 1, 1))
    o = jnp.einsum('bhgqk,bkhd->bqhgd', p.astype(vc.dtype), vc)
    return o.reshape(B, C, n_heads * HEAD_DIM)


def _swa_latent(q, k, v, kc, vc, sink):
    B, L, n_heads = q.shape[:3]
    grp = n_heads // SWA_KV_HEADS
    nb = L // SWA_BLOCK
    qb = q.reshape(B, nb, SWA_BLOCK, SWA_KV_HEADS, grp, HEAD_DIM)

    def band(t):
        tb = t.reshape(B, nb, SWA_BLOCK, SWA_KV_HEADS, HEAD_DIM)
        tp = jnp.pad(tb, ((0, 0), (1, 1), (0, 0), (0, 0), (0, 0)))
        return jnp.concatenate([tp[:, :-2], tp[:, 1:-1], tp[:, 2:]], axis=2)

    kb, vb = band(k), band(v)
    scale = HEAD_DIM ** -0.5
    s_loc = jnp.einsum('bnqhgd,bnkhd->bnhgqk', qb, kb).astype(F32) * scale
    s_ctx = jnp.einsum('bnqhgd,bchd->bnhgqc', qb, kc).astype(F32) * scale
    q_rel = jnp.arange(SWA_BLOCK)[:, None] + SWA_BLOCK
    k_rel = jnp.arange(3 * SWA_BLOCK)[None, :]
    k_abs = (jnp.arange(nb)[:, None, None] - 1) * SWA_BLOCK + k_rel[None]
    valid = (jnp.abs(q_rel - k_rel) <= SWA_WINDOW)[None] & (k_abs >= 0) & (k_abs < L)
    s_loc = jnp.where(valid[None, :, None, None], s_loc, NEG_INF)
    s = jnp.concatenate([s_loc, s_ctx], axis=-1)
    p = _sink_softmax(s, sink.astype(F32).reshape(1, 1, SWA_KV_HEADS, grp, 1, 1)).astype(v.dtype)
    o = (jnp.einsum('bnhgqk,bnkhd->bnqhgd', p[..., :3 * SWA_BLOCK], vb)
         + jnp.einsum('bnhgqc,bchd->bnqhgd', p[..., 3 * SWA_BLOCK:], vc))
    return o.reshape(B, L, n_heads * HEAD_DIM)


def _token_shift(x, reverse):
    if reverse:
        return jnp.pad(x, ((0, 0), (0, 1), (0, 0)))[:, 1:]
    return jnp.pad(x, ((0, 0), (1, 0), (0, 0)))[:, :-1]


def _head_l2norm(x, n_heads):
    B, L, C = x.shape
    xh = x.reshape(B, L, n_heads, HEAD_DIM)
    xh = xh * lax.rsqrt(jnp.sum(xh * xh, axis=-1, keepdims=True) + 1e-12)
    return xh.reshape(B, L, C)


def _rwkv_prepare(feats, reverse, width, mu, w0, w2, a0, a2, g2, k_k, k_a):
    n_heads = width // HEAD_DIM
    xs = feats + mu * (_token_shift(feats, reverse) - feats)
    splits = (width, 2 * width, 3 * width, 3 * width + RW_DECAY_RANK, 3 * width + RW_DECAY_RANK + RW_A_RANK)
    r, k, v, wd, ad, gd = jnp.split(xs, splits, axis=-1)
    logw = -jnp.exp(-jax.nn.softplus(-(w0 + jnp.tanh(wd) @ w2)) - 0.5)
    a = jax.nn.sigmoid(a0 + ad @ a2)
    g = jax.nn.sigmoid(gd) @ g2
    kk = _head_l2norm(k * k_k, n_heads)
    k = k * (1 + (a - 1) * k_a)
    return r, logw, kk, a, k, v, g


def _wkv7(r, logw, kk, a, k, v, z0, reverse):
    p, g, qh, yl = rwkv_chunk_prep(r, logw, kk, a, k, v, reverse=reverse)
    y, z_fin = rwkv_chunk_scan(p, g, qh, yl, z0, reverse=reverse)
    return z_fin, y


def _rwkv_readout(y, r, k, v, g, r_k, ln_w, ln_b):
    B, L, C = y.shape
    n_heads = C // HEAD_DIM
    yh = y.reshape(B, L, n_heads, HEAD_DIM)
    mean = jnp.mean(yh, axis=-1, keepdims=True)
    var = jnp.mean(jnp.square(yh - mean), axis=-1, keepdims=True)
    yn = ((yh - mean) * lax.rsqrt(var + RW_GN_EPS)).reshape(B, L, C) * ln_w + ln_b
    bonus = jnp.sum((r * k * r_k).reshape(B, L, n_heads, HEAD_DIM), axis=-1, keepdims=True)
    bonus = (bonus * v.reshape(B, L, n_heads, HEAD_DIM)).reshape(B, L, C)
    return (yn + bonus) * g


def _rwkv_branch(f_ctx, f_lat, width, mu, w0, w2, a0, a2, g2, k_k, k_a, r_k, ln_w, ln_b, need_ctx):
    B = f_lat.shape[0]
    n_heads = width // HEAD_DIM
    y_lat = 0.0
    y_ctx = 0.0 if need_ctx else None
    for d in range(2):
        rev = d == 1
        dir_args = (width, mu[d], w0[d], w2[d], a0[d], a2[d], g2[d], k_k, k_a)
        r_c, w_c, kk_c, a_c, k_c, v_c, g_c = _rwkv_prepare(f_ctx, rev, *dir_args)
        z0 = jnp.zeros((B, n_heads, HEAD_DIM, HEAD_DIM), F32)
        z_ctx, o_c = _wkv7(r_c, w_c, kk_c, a_c, k_c, v_c, z0, rev)
        r_l, w_l, kk_l, a_l, k_l, v_l, g_l = _rwkv_prepare(f_lat, rev, *dir_args)
        _, o_l = _wkv7(r_l, w_l, kk_l, a_l, k_l, v_l, z_ctx, rev)
        y_lat = y_lat + _rwkv_readout(o_l, r_l, k_l, v_l, g_l, r_k, ln_w, ln_b)
        if need_ctx:
            y_ctx = y_ctx + _rwkv_readout(o_c, r_c, k_c, v_c, g_c, r_k, ln_w, ln_b)
    return y_lat, y_ctx


def _diff_project(pd, q_g, k_g):
    B, L, w3 = pd.shape
    n_heads = w3 // (3 * 2 * HEAD_DIM)
    q, k, v = jnp.split(pd, 3, axis=-1)
    q = _rms_norm(q.reshape(B, L, n_heads, 2, HEAD_DIM), q_g)
    k = _rms_norm(k.reshape(B, L, n_heads, 2, HEAD_DIM), k_g)
    return q, k, v.reshape(B, L, n_heads, 2 * HEAD_DIM)


def _diff_maps(q, k_all, v_all, lam):
    s = jnp.einsum('bqhid,bkhid->bhiqk', q, k_all).astype(F32) * HEAD_DIM ** -0.5
    p = jax.nn.softmax(s, axis=-1)
    a = p[:, :, 0] - lam * p[:, :, 1]
    return jnp.einsum('bhqk,bkhe->bqhe', a.astype(v_all.dtype), v_all)


def _diff_latent(q, k, v, kc, vc, lam):
    B, L, n_heads = q.shape[:3]
    nb = L // DF_BLOCK
    k_all = jnp.concatenate([kc, k], axis=1)
    v_all = jnp.concatenate([vc, v], axis=1)
    qb = jnp.moveaxis(q.reshape(B, nb, DF_BLOCK, n_heads, 2, HEAD_DIM), 1, 0)
    o = lax.map(lambda q_blk: _diff_maps(q_blk, k_all, v_all, lam), qb)
    return jnp.moveaxis(o, 0, 1).reshape(B, L, n_heads, 2 * HEAD_DIM)


def _diff_readout(o, sub_g, lam_init):
    B, L = o.shape[:2]
    return (_rms_norm(o, sub_g) * (1 - lam_init)).reshape(B, L, -1)


def _swa_branch(pb_l, pb_c, cos, sin, q_g, k_g, sink, need_ctx):
    B, L = pb_l.shape[:2]
    C = pb_c.shape[1]
    n_heads = pb_l.shape[-1] // HEAD_DIM - 2 * SWA_KV_HEADS
    grp = n_heads // SWA_KV_HEADS
    scale = HEAD_DIM ** -0.5
    q_l, k_l, v_l = _swa_project(pb_l, q_g, k_g, n_heads)
    q_l, k_l = _apply_rope(q_l, cos, sin), _apply_rope(k_l, cos, sin)
    q_c, k_c, v_c = _swa_project(pb_c, q_g, k_g, n_heads)

    def q_layout(q, n):
        return jnp.transpose((q * scale).astype(BF16).reshape(B, n, SWA_KV_HEADS, grp, HEAD_DIM), (0, 2, 3, 1, 4))

    def kv_layout(t):
        return jnp.transpose(t.astype(BF16), (0, 2, 1, 3))

    def o_layout(o, n):
        return jnp.transpose(o, (0, 3, 1, 2, 4)).reshape(B, n, n_heads * HEAD_DIM)

    sink_rows = jnp.repeat(sink.astype(F32).reshape(SWA_KV_HEADS, grp), SWA_BLOCK, axis=1)[..., None]
    kc, vc = kv_layout(k_c), kv_layout(v_c)
    y_l = o_layout(swa_attention(q_layout(q_l, L), kv_layout(k_l), kv_layout(v_l), kc, vc, sink_rows), L)
    y_c = None
    if need_ctx:
        y_c = o_layout(swa_context_attention(q_layout(q_c, C), kc, vc, sink_rows), C)
    return y_l, y_c


def _diff_branch(pd_l, pd_c, cos, sin, q_g, k_g, sub_g, lam, lam_init, need_ctx):
    B, L = pd_l.shape[:2]
    C = pd_c.shape[1]
    W = pd_l.shape[-1] // 3
    scale = HEAD_DIM ** -0.5 * math.log2(math.e)
    dq_l, dk_l, dv_l = _diff_project(pd_l, q_g, k_g)
    dq_l, dk_l = _apply_rope(dq_l, cos, sin), _apply_rope(dk_l, cos, sin)
    dq_c, dk_c, dv_c = _diff_project(pd_c, q_g, k_g)
    flat = lambda t, n: t.reshape(B, n, W)
    q_l = (flat(dq_l, L) * scale).astype(BF16)
    k_c, v_c = flat(dk_c, C).astype(BF16), flat(dv_c, C).astype(BF16)
    kt_c = jnp.transpose(k_c, (0, 2, 1))
    kt_all = jnp.concatenate([kt_c, jnp.transpose(flat(dk_l, L).astype(BF16), (0, 2, 1))], axis=2)
    v_all = jnp.concatenate([v_c, flat(dv_l, L).astype(BF16)], axis=1)
    sub = sub_g.astype(F32).reshape(1, 2 * HEAD_DIM)
    lam2 = lam.astype(F32).reshape(1, 1)
    y_l = diff_attention(q_l, kt_all, v_all, sub, lam2, 1.0 - lam_init)
    y_c = None
    if need_ctx:
        q_c = (flat(dq_c, C) * scale).astype(BF16)
        y_c = diff_attention(q_c, kt_c, v_c, sub, lam2, 1.0 - lam_init)
    return y_l, y_c


def _moe_route(logits_g, logits_e, n_tok):
    g_idx = jnp.argmax(logits_g, axis=-1)
    g_prob = jnp.take_along_axis(jax.nn.softmax(logits_g, axis=-1), g_idx[:, None], axis=-1)[:, 0]
    e_logits = logits_e.reshape(n_tok, MOE_GROUPS, MOE_PER_GROUP)
    e_logits = jnp.take_along_axis(e_logits, g_idx[:, None, None], axis=1)[:, 0]
    top_p, top_e = lax.top_k(jax.nn.softmax(e_logits, axis=-1), MOE_TOP_K)
    weights = g_prob[:, None] * top_p / jnp.sum(top_p, axis=-1, keepdims=True)
    flat_e = (g_idx[:, None] * MOE_PER_GROUP + top_e).reshape(-1).astype(jnp.int32)
    flat_tok = jnp.repeat(jnp.arange(n_tok, dtype=jnp.int32), MOE_TOP_K)
    n_assign = n_tok * MOE_TOP_K
    order = jnp.argsort(flat_e)
    se = flat_e[order]
    counts = jnp.bincount(flat_e, length=MOE_EXPERTS)
    padded = (counts + MOE_BLOCK - 1) // MOE_BLOCK * MOE_BLOCK
    pad_end = jnp.cumsum(padded)
    pad_start = pad_end - padded
    start = jnp.cumsum(counts) - counts
    dest = (pad_start[se] + jnp.arange(n_assign, dtype=jnp.int32) - start[se]).astype(jnp.int32)
    n_blocks = -(-n_assign // MOE_BLOCK) + MOE_EXPERTS
    P = n_blocks * MOE_BLOCK
    tok_buf = jnp.full((P,), n_tok, jnp.int32).at[dest].set(flat_tok[order])
    blk_e = jnp.minimum(jnp.searchsorted(pad_end, jnp.arange(n_blocks) * MOE_BLOCK, side='right'),
                        MOE_EXPERTS - 1).astype(jnp.int32)
    n_used = (pad_end[-1] // MOE_BLOCK).astype(jnp.int32).reshape(1)
    slot = jnp.zeros((n_assign,), jnp.int32).at[order].set(dest).reshape(n_tok, MOE_TOP_K)
    return tok_buf, blk_e, n_used, slot, weights


def _hier_moe(tokens_bf16, tokens_f32, rg_w, rg_b, re_w, re_b, w1, w3, w2):
    n_tok, d = tokens_bf16.shape
    hp = lax.Precision.HIGHEST
    logits_g = jnp.dot(tokens_f32, rg_w, precision=hp) + rg_b
    logits_e = jnp.dot(tokens_f32, re_w, precision=hp) + re_b
    tok_buf, blk_e, n_used, slot, weights = _moe_route(logits_g, logits_e, n_tok)
    x_pad = jnp.concatenate([tokens_bf16, jnp.zeros((1, d), BF16)], axis=0)
    xb = x_pad[tok_buf]
    yb = moe_experts(xb, blk_e, n_used, w1, w3, w2)
    wts = weights.astype(F32)
    return (yb[slot[:, 0]].astype(F32) * wts[:, 0:1] + yb[slot[:, 1]].astype(F32) * wts[:, 1:2])


def _modulate(x, g, shift, scale):
    return _rms_norm(x, g) * (1 + scale) + shift


def kernel(x, c, ctx, c_ctx, mod_w, mod_b, norm1_g, norm2_g, w_in, hy_conv_w, hy_conv_b, hy_f_w1, hy_f_b1, hy_f_w2, hy_f_b2, hy_f_w3, hy_f_b3, hy_f_freq, hy_f_decay, hy_skip, swa_q_g, swa_k_g, swa_sink, rw_mu, rw_w0, rw_w2, rw_a0, rw_a2, rw_g2, rw_k_k, rw_k_a, rw_r_k, rw_ln_w, rw_ln_b, df_q_g, df_k_g, df_lq1, df_lk1, df_lq2, df_lk2, df_sub_g, w_gate, b_gate, w_branch, w_out, moe_rg_w, moe_rg_b, moe_re_w, moe_re_b, moe_w1, moe_w3, moe_w2):
    B, L, D = x.shape
    C = ctx.shape[1]
    depth = mod_w.shape[0]
    bw = D // N_BRANCH
    a_cols = 3 * bw
    swa_heads = bw // HEAD_DIM
    b_cols = (swa_heads + 2 * SWA_KV_HEADS) * HEAD_DIM
    c_cols = 3 * bw + RW_DECAY_RANK + RW_A_RANK + RW_G_RANK
    in_splits = (a_cols, a_cols + b_cols, a_cols + b_cols + c_cols)
    rows = L // GRID_W
    cos, sin = _rope_tables(rows)
    hp = lax.Precision.HIGHEST

    x_lat = x.reshape(B * L, D)
    x_ctx = ctx.reshape(B * C, D)
    for l in range(depth):
        need_ctx = l < depth - 1
        mod_lat = jnp.dot(jax.nn.silu(c), mod_w[l], precision=hp) + mod_b[l]
        mod_ctx = jnp.dot(jax.nn.silu(c_ctx)[None], mod_w[l], precision=hp) + mod_b[l]
        sh1, sc1, g1, sh2, sc2, g2 = [t[:, None, :] for t in jnp.split(mod_lat, 6, axis=-1)]
        csh1, csc1, cg1, csh2, csc2, cg2 = [t[:, None, :] for t in jnp.split(mod_ctx, 6, axis=-1)]

        w_in_b = w_in[l].astype(BF16)
        p_lat, h_lat = norm_proj(x_lat, norm1_g[l][None], sc1, sh1, w_in_b, rows_per_mod=L)
        p_ctx, h_ctx = norm_proj(x_ctx, norm1_g[l][None], csc1, csh1, w_in_b, rows_per_mod=B * C)
        p_lat = p_lat.reshape(B, L, -1)
        p_ctx = p_ctx.reshape(B, C, -1)
        pa_l, pb_l, pc_l, pd_l = jnp.split(p_lat, in_splits, axis=-1)
        pa_c, pb_c, pc_c, pd_c = jnp.split(p_ctx, in_splits, axis=-1)

        spec_l = _hyena_filter_spectra(L, bw, hy_f_w1[l], hy_f_b1[l], hy_f_w2[l], hy_f_b2[l], hy_f_w3[l],
                                       hy_f_b3[l], hy_f_freq[l], hy_f_decay[l])
        ya_l = _hyena_branch(pa_l, hy_conv_w[l], hy_conv_b[l], spec_l, hy_skip[l])

        yb_l, yb_c = _swa_branch(pb_l, pb_c, cos, sin, swa_q_g[l], swa_k_g[l], swa_sink[l], need_ctx)

        yc_l, yc_c = _rwkv_branch(pc_c, pc_l, bw, rw_mu[l], rw_w0[l], rw_w2[l], rw_a0[l], rw_a2[l],
                                  rw_g2[l], rw_k_k[l], rw_k_a[l], rw_r_k[l], rw_ln_w[l], rw_ln_b[l], need_ctx)

        lam_init = 0.8 - 0.6 * math.exp(-0.3 * l)
        lam = (jnp.exp(jnp.sum(df_lq1[l] * df_lk1[l])) - jnp.exp(jnp.sum(df_lq2[l] * df_lk2[l])) + lam_init)
        yd_l, yd_c = _diff_branch(pd_l, pd_c, cos, sin, df_q_g[l], df_k_g[l], df_sub_g[l], lam, lam_init, need_ctx)

        wg_b = w_gate[l].astype(BF16)
        bg = b_gate[l][:, None, :]
        wb_b = w_branch[l].astype(BF16)
        wo_b = w_out[l].astype(BF16)
        ys_l = jnp.stack([t.reshape(B * L, bw).astype(BF16) for t in (ya_l, yb_l, yc_l, yd_l)])
        acc_l = merge_gated(h_lat, ys_l, wg_b, bg, wb_b)
        x_lat = resid_proj(x_lat, acc_l, wo_b, g1, rows_per_mod=L)
        if need_ctx:
            spec_c = _hyena_filter_spectra(C, bw, hy_f_w1[l], hy_f_b1[l], hy_f_w2[l], hy_f_b2[l], hy_f_w3[l],
                                           hy_f_b3[l], hy_f_freq[l], hy_f_decay[l])
            ya_c = _hyena_branch(pa_c, hy_conv_w[l], hy_conv_b[l], spec_c, hy_skip[l])
            ys_c = jnp.stack([t.reshape(B * C, bw).astype(BF16) for t in (ya_c, yb_c, yc_c, yd_c)])
            acc_c = merge_gated(h_ctx, ys_c, wg_b, bg, wb_b)
            x_ctx = resid_proj(x_ctx, acc_c, wo_b, cg1, rows_per_mod=B * C)

        w1_b, w3_b, w2_b = moe_w1[l].astype(BF16), moe_w3[l].astype(BF16), moe_w2[l].astype(BF16)
        moe_args = (moe_rg_w[l], moe_rg_b[l], moe_re_w[l], moe_re_b[l], w1_b, w3_b, w2_b)
        hm_lat = _modulate(x_lat.reshape(B, L, D), norm2_g[l], sh2, sc2).reshape(B * L, D)
        if need_ctx:
            hm_ctx = _modulate(x_ctx.reshape(B, C, D), norm2_g[l], csh2, csc2).reshape(B * C, D)
            tokens = jnp.concatenate([hm_ctx, hm_lat], axis=0)
            out = _hier_moe(tokens.astype(BF16), tokens, *moe_args)
            x_ctx = x_ctx + (cg2 * out[:B * C].reshape(1, B * C, D)).reshape(B * C, D)
            x_lat = x_lat + (g2 * out[B * C:].reshape(B, L, D)).reshape(B * L, D)
        else:
            out = _hier_moe(hm_lat.astype(BF16), hm_lat, *moe_args)
            x_lat = x_lat + (g2 * out.reshape(B, L, D)).reshape(B * L, D)
    return x_lat.reshape(B, L, D)
```

```python
import functools
import math

import jax
import jax.numpy as jnp
from jax import lax
from jax.experimental import pallas as pl
from jax.experimental.pallas import tpu as pltpu

F32 = jnp.float32
BF16 = jnp.bfloat16

GRID_W = 64
HEAD_DIM = 64
ROPE_BASE = 10000.0
NORM_EPS = 1e-6
NEG_INF = -1e30
N_BRANCH = 4
HY_ORDER = 2
HY_POS_BANDS = 8
SWA_KV_HEADS = 2
SWA_WINDOW = 128
SWA_BLOCK = 128
RW_DECAY_RANK = 64
RW_A_RANK = 64
RW_G_RANK = 128
RW_GN_EPS = 64e-5
DF_BLOCK = 128
DF_ROW_CHUNK = 256
MOE_GROUPS = 4
MOE_PER_GROUP = 8
MOE_EXPERTS = MOE_GROUPS * MOE_PER_GROUP
MOE_TOP_K = 2
MOE_BLOCK = 256

VMEM_LIMIT_BYTES = 56 * 1024 * 1024


def _row_tile(m, pref):
    t = min(pref, m)
    while m % t:
        t //= 2
    return t


def _params(sem):
    return pltpu.CompilerParams(dimension_semantics=sem, vmem_limit_bytes=VMEM_LIMIT_BYTES)


def _norm_proj_body(x_ref, g_ref, sc_ref, sh_ref, w_ref, p_ref, h_ref, h_scr):
    @pl.when(pl.program_id(1) == 0)
    def _():
        x = x_ref[...].astype(F32)
        y = x * lax.rsqrt(jnp.mean(x * x, axis=-1, keepdims=True) + NORM_EPS)
        h = y * g_ref[...] * (1.0 + sc_ref[...]) + sh_ref[...]
        h_scr[...] = h.astype(BF16)
        h_ref[...] = h_scr[...]

    p_ref[...] = jnp.dot(h_scr[...], w_ref[...], preferred_element_type=F32).astype(p_ref.dtype)


def norm_proj(x, g, scale, shift, w, *, rows_per_mod, tm=1024, tn=512, out_dtype=F32):
    m, d = x.shape
    n = w.shape[1]
    tm = _row_tile(rows_per_mod, tm)
    tn = _row_tile(n, tn)
    tiles_per_mod = rows_per_mod // tm
    mod_map = lambda i, j: (i // tiles_per_mod, 0, 0)
    return pl.pallas_call(
        _norm_proj_body,
        grid=(m // tm, n // tn),
        in_specs=[
            pl.BlockSpec((tm, d), lambda i, j: (i, 0)),
            pl.BlockSpec((1, d), lambda i, j: (0, 0)),
            pl.BlockSpec((None, 1, d), mod_map),
            pl.BlockSpec((None, 1, d), mod_map),
            pl.BlockSpec((d, tn), lambda i, j: (0, j)),
        ],
        out_specs=[
            pl.BlockSpec((tm, tn), lambda i, j: (i, j)),
            pl.BlockSpec((tm, d), lambda i, j: (i, 0)),
        ],
        out_shape=[jax.ShapeDtypeStruct((m, n), out_dtype), jax.ShapeDtypeStruct((m, d), BF16)],
        scratch_shapes=[pltpu.VMEM((tm, d), BF16)],
        compiler_params=_params(("parallel", "arbitrary")),
        name="norm_proj",
    )(x, g, scale, shift, w)


def _merge_body(h_ref, y_ref, wg_ref, bg_ref, wb_ref, o_ref, acc_ref):
    b = pl.program_id(2)
    gate = jax.nn.sigmoid(jnp.dot(h_ref[...], wg_ref[...], preferred_element_type=F32) + bg_ref[...])
    val = gate * jnp.dot(y_ref[...], wb_ref[...], preferred_element_type=F32)

    @pl.when(b == 0)
    def _():
        acc_ref[...] = val

    @pl.when(b > 0)
    def _():
        acc_ref[...] += val

    @pl.when(b == N_BRANCH - 1)
    def _():
        o_ref[...] = acc_ref[...].astype(o_ref.dtype)


def merge_gated(h, ys, wg, bg, wb, *, tm=1024, tn=512):
    m, d = h.shape
    w = ys.shape[-1]
    tm = _row_tile(m, tm)
    tn = _row_tile(d, tn)
    return pl.pallas_call(
        _merge_body,
        grid=(m // tm, d // tn, N_BRANCH),
        in_specs=[
            pl.BlockSpec((tm, d), lambda i, j, b: (i, 0)),
            pl.BlockSpec((None, tm, w), lambda i, j, b: (b, i, 0)),
            pl.BlockSpec((None, d, tn), lambda i, j, b: (b, 0, j)),
            pl.BlockSpec((None, 1, tn), lambda i, j, b: (b, 0, j)),
            pl.BlockSpec((None, w, tn), lambda i, j, b: (b, 0, j)),
        ],
        out_specs=pl.BlockSpec((tm, tn), lambda i, j, b: (i, j)),
        out_shape=jax.ShapeDtypeStruct((m, d), BF16),
        scratch_shapes=[pltpu.VMEM((tm, tn), F32)],
        compiler_params=_params(("parallel", "arbitrary", "arbitrary")),
        name="merge_gated",
    )(h, ys, wg, bg, wb)


def _resid_proj_body(x_ref, a_ref, w_ref, gate_ref, o_ref):
    y = jnp.dot(a_ref[...], w_ref[...], preferred_element_type=F32)
    o_ref[...] = (x_ref[...].astype(F32) + gate_ref[...] * y).astype(o_ref.dtype)


def resid_proj(x, a, w, gate, *, rows_per_mod, tm=1024, tn=512):
    m, d = x.shape
    k = a.shape[1]
    tm = _row_tile(rows_per_mod, tm)
    tn = _row_tile(d, tn)
    tiles_per_mod = rows_per_mod // tm
    return pl.pallas_call(
        _resid_proj_body,
        grid=(m // tm, d // tn),
        in_specs=[
            pl.BlockSpec((tm, tn), lambda i, j: (i, j)),
            pl.BlockSpec((tm, k), lambda i, j: (i, 0)),
            pl.BlockSpec((k, tn), lambda i, j: (0, j)),
            pl.BlockSpec((None, 1, tn), lambda i, j: (i // tiles_per_mod, 0, j)),
        ],
        out_specs=pl.BlockSpec((tm, tn), lambda i, j: (i, j)),
        out_shape=jax.ShapeDtypeStruct((m, d), x.dtype),
        compiler_params=_params(("parallel", "arbitrary")),
        name="resid_proj",
    )(x, a, w, gate)


def _moe_body(blk_e_ref, n_used_ref, x_ref, w1_ref, w3_ref, w2_ref, o_ref):
    i = pl.program_id(0)

    @pl.when(i < n_used_ref[0])
    def _():
        x = x_ref[...]
        a = jnp.dot(x, w1_ref[...], preferred_element_type=F32)
        b = jnp.dot(x, w3_ref[...], preferred_element_type=F32)
        hdn = (a * jax.nn.sigmoid(a) * b).astype(BF16)
        o_ref[...] = jnp.dot(hdn, w2_ref[...], preferred_element_type=F32).astype(o_ref.dtype)

    @pl.when(i >= n_used_ref[0])
    def _():
        o_ref[...] = jnp.zeros_like(o_ref)


def moe_experts(xb, blk_e, n_used, w1, w3, w2):
    p, d = xb.shape
    hid = w1.shape[-1]
    n_blocks = p // MOE_BLOCK
    grid_spec = pltpu.PrefetchScalarGridSpec(
        num_scalar_prefetch=2,
        grid=(n_blocks,),
        in_specs=[
            pl.BlockSpec((MOE_BLOCK, d), lambda i, e, n: (i, 0)),
            pl.BlockSpec((None, d, hid), lambda i, e, n: (e[i], 0, 0)),
            pl.BlockSpec((None, d, hid), lambda i, e, n: (e[i], 0, 0)),
            pl.BlockSpec((None, hid, d), lambda i, e, n: (e[i], 0, 0)),
        ],
        out_specs=pl.BlockSpec((MOE_BLOCK, d), lambda i, e, n: (i, 0)),
    )
    return pl.pallas_call(
        _moe_body,
        grid_spec=grid_spec,
        out_shape=jax.ShapeDtypeStruct((p, d), BF16),
        compiler_params=_params(("arbitrary",)),
        name="moe_experts",
    )(blk_e, n_used, xb, w1, w3, w2)


RW_CHUNK = 64
_HI = lax.Precision.HIGHEST
_NT = (((1,), (1,)), ((), ()))
_TN = (((0,), (0,)), ((), ()))


def _dot(a, b, dims=None):
    if dims is None:
        return jnp.dot(a, b, preferred_element_type=F32, precision=_HI)
    return lax.dot_general(a, b, dims, preferred_element_type=F32, precision=_HI)


def _bdot(a, b, dims=None):
    a, b = a.astype(BF16), b.astype(BF16)
    if dims is None:
        return jnp.dot(a, b, preferred_element_type=F32)
    return lax.dot_general(a, b, dims, preferred_element_type=F32)


def _rwkv_prep_body(r_ref, lw_ref, kk_ref, a_ref, k_ref, v_ref, p_ref, g_ref, q_ref, yl_ref, *, reverse):
    T = r_ref.shape[0]
    hd = HEAD_DIM
    row = lax.broadcasted_iota(jnp.int32, (T, T), 0)
    col = lax.broadcasted_iota(jnp.int32, (T, T), 1)
    if reverse:
        strict, incl = col > row, col >= row
    else:
        strict, incl = col < row, col <= row
    lw = lw_ref[...]
    tri = incl.astype(BF16)
    lw1 = lw.astype(BF16)
    res1 = lw - lw1.astype(F32)
    lw2 = res1.astype(BF16)
    lw3 = (res1 - lw2.astype(F32)).astype(BF16)
    cum = (jnp.dot(tri, lw1, preferred_element_type=F32) + jnp.dot(tri, lw2, preferred_element_type=F32)
           + jnp.dot(tri, lw3, preferred_element_type=F32))
    total = jnp.sum(lw, axis=0, keepdims=True)
    e_in = jnp.exp(cum)
    e_ex = jnp.exp(cum - lw)
    e_ninv = jnp.exp(-cum)
    e_rem = jnp.exp(total - cum)
    gam = jnp.exp(total)
    kk = kk_ref[...]
    kka = kk * a_ref[...]
    k = k_ref[...]
    nt = -kk * e_ex
    rt = r_ref[...] * e_in
    at = kka * e_ninv
    kt = k * e_ninv
    ac = kka * e_rem
    kc = k * e_rem
    v = v_ref[...]
    eye = lax.broadcasted_iota(jnp.int32, (hd, hd), 0) == lax.broadcasted_iota(jnp.int32, (hd, hd), 1)
    n_heads = r_ref.shape[1] // hd
    zeros = jnp.zeros((T, hd), F32)
    steps = max(1, (T - 1).bit_length())
    heads = range(n_heads)
    sls = [slice(hh * hd, (hh + 1) * hd) for hh in heads]
    bigs = [_bdot(jnp.concatenate([nt[:, sl], rt[:, sl]], axis=0),
                  jnp.concatenate([at[:, sl], kt[:, sl]], axis=0), _NT) for sl in sls]
    a_ak = [jnp.where(strict, big[:T, T:], 0.0) for big in bigs]
    pws = [jnp.where(strict, big[:T, :T], 0.0) for big in bigs]
    lhs_top = [jnp.concatenate([jnp.where(incl, big[T:, :T], 0.0), jnp.where(incl, big[T:, T:], 0.0)], axis=1)
               for big in bigs]
    xs = [jnp.concatenate([nt[:, sl], _bdot(m, v[:, sl])], axis=1) for m, sl in zip(a_ak, sls)]
    for it in range(steps):
        if it < steps - 1:
            boths = [_bdot(pw, jnp.concatenate([x, pw], axis=1)) for pw, x in zip(pws, xs)]
            xs = [x + both[:, :2 * hd] for x, both in zip(xs, boths)]
            pws = [both[:, 2 * hd:] for both in boths]
        else:
            xs = [x + _bdot(pw, x) for pw, x in zip(pws, xs)]
    rhs = [jnp.concatenate([x, jnp.concatenate([zeros, v[:, sl]], axis=1)], axis=0) for x, sl in zip(xs, sls)]
    tops = [_bdot(lt, rh) for lt, rh in zip(lhs_top, rhs)]
    bots = [_bdot(jnp.concatenate([ac[:, sl], kc[:, sl]], axis=0), rh, _TN) for sl, rh in zip(sls, rhs)]
    for hh in heads:
        p_ref[hh] = jnp.where(eye, gam[:, sls[hh]], 0.0) + bots[hh][:, :hd]
        g_ref[hh] = bots[hh][:, hd:]
    q_ref[...] = jnp.concatenate([rt[:, sl] + top[:, :hd] for sl, top in zip(sls, tops)], axis=1)
    yl_ref[...] = jnp.concatenate([top[:, hd:] for top in tops], axis=1)


def rwkv_chunk_prep(r, logw, kk, a, k, v, *, reverse):
    B, L, W = r.shape
    T = RW_CHUNK
    nh = W // HEAD_DIM
    nc = L // T
    blk = pl.BlockSpec((None, T, W), lambda b, c: (b, c, 0))
    mat = pl.BlockSpec((None, None, nh, HEAD_DIM, HEAD_DIM), lambda b, c: (b, c, 0, 0, 0))
    mat_shape = jax.ShapeDtypeStruct((B, nc, nh, HEAD_DIM, HEAD_DIM), F32)
    seq_shape = jax.ShapeDtypeStruct((B, L, W), F32)
    return pl.pallas_call(
        functools.partial(_rwkv_prep_body, reverse=reverse),
        grid=(B, nc),
        in_specs=[blk] * 6,
        out_specs=[mat, mat, blk, blk],
        out_shape=[mat_shape, mat_shape, seq_shape, seq_shape],
        compiler_params=_params(("parallel", "parallel")),
        name="rwkv_chunk_prep",
    )(r, logw, kk, a, k, v)


def _rwkv_scan_body(p_ref, g_ref, q_ref, yl_ref, z0_ref, y_ref, zf_ref, z_scr):
    c = pl.program_id(1)

    @pl.when(c == 0)
    def _():
        z_scr[...] = z0_ref[...]

    hd = HEAD_DIM
    n_heads = z_scr.shape[0]
    q = q_ref[...]
    ys = []
    for h in range(n_heads):
        z = z_scr[h]
        ys.append(_dot(q[:, h * hd:(h + 1) * hd], z))
        z_scr[h] = _dot(p_ref[h], z) + g_ref[h]
    y_ref[...] = jnp.concatenate(ys, axis=1) + yl_ref[...]

    @pl.when(c == pl.num_programs(1) - 1)
    def _():
        zf_ref[...] = z_scr[...]


def rwkv_chunk_scan(p, g, qh, yl, z0, *, reverse):
    B, nc, nh = p.shape[:3]
    L, W = qh.shape[1:]
    T = L // nc
    cidx = (lambda c: nc - 1 - c) if reverse else (lambda c: c)
    mat = pl.BlockSpec((None, None, nh, HEAD_DIM, HEAD_DIM), lambda b, c: (b, cidx(c), 0, 0, 0))
    seq = pl.BlockSpec((None, T, W), lambda b, c: (b, cidx(c), 0))
    st = pl.BlockSpec((None, nh, HEAD_DIM, HEAD_DIM), lambda b, c: (b, 0, 0, 0))
    return pl.pallas_call(
        _rwkv_scan_body,
        grid=(B, nc),
        in_specs=[mat, mat, seq, seq, st],
        out_specs=[seq, st],
        out_shape=[jax.ShapeDtypeStruct((B, L, W), F32), jax.ShapeDtypeStruct(z0.shape, F32)],
        scratch_shapes=[pltpu.VMEM((nh, HEAD_DIM, HEAD_DIM), F32)],
        compiler_params=_params(("parallel", "arbitrary")),
        name="rwkv_chunk_scan",
    )(p, g, qh, yl, z0)


def _diff_attn_body(lam_ref, q_ref, kt_ref, v_ref, subg_ref, o_ref, q2_scr, m_scr, acc_scr, *, out_scale):
    ki = pl.program_id(3)
    tq = q_ref.shape[0]

    @pl.when(ki == 0)
    def _():
        q = q_ref[...]
        lane = lax.broadcasted_iota(jnp.int32, q.shape, 1)
        q2_scr[0:tq, :] = jnp.where(lane < HEAD_DIM, q, jnp.zeros_like(q))
        q2_scr[tq:2 * tq, :] = jnp.where(lane >= HEAD_DIM, q, jnp.zeros_like(q))
        m_scr[...] = jnp.full_like(m_scr, -jnp.inf)
        acc_scr[...] = jnp.zeros_like(acc_scr)

    kt = kt_ref[...]
    v = v_ref[...]
    rc = min(DF_ROW_CHUNK, 2 * tq)
    n_chunks = 2 * tq // rc
    score = lambda c: jnp.dot(q2_scr[pl.ds(c * rc, rc), :], kt, preferred_element_type=F32)
    s_next = score(0)
    for c in range(n_chunks):
        rows = pl.ds(c * rc, rc)
        s = s_next
        if c + 1 < n_chunks:
            s_next = score(c + 1)
        m_prev = m_scr[rows, :]
        m_new = jnp.maximum(m_prev, jnp.max(s, axis=-1, keepdims=True))
        alpha = jnp.exp2(m_prev - m_new)
        p = jnp.exp2(s - m_new)
        acc_scr[rows, :] = alpha * acc_scr[rows, :] + jnp.dot(p.astype(BF16), v, preferred_element_type=F32)
        m_scr[rows, :] = m_new

    @pl.when(ki == pl.num_programs(3) - 1)
    def _():
        hw = o_ref.shape[-1]
        o = acc_scr[:, 0:hw] / acc_scr[:, hw:hw + 1]
        a = o[0:tq, :] - lam_ref[0, 0] * o[tq:2 * tq, :]
        y = a * lax.rsqrt(jnp.mean(a * a, axis=-1, keepdims=True) + NORM_EPS)
        o_ref[...] = (y * subg_ref[...] * out_scale).astype(o_ref.dtype)


def _key_tile(k, cap):
    best = 128
    t = 128
    while t <= min(k, cap):
        if k % t == 0:
            best = t
        t += 128
    return best


def diff_attention(q, kt, v, sub_g, lam, out_scale, *, tq=512, tk_cap=1280):
    B, L, W = q.shape
    K = kt.shape[2]
    hw = 2 * HEAD_DIM
    tq = _row_tile(L, tq)
    tk = _key_tile(K, tk_cap)
    vh = v.reshape(B, K, W // hw, hw)
    v = jnp.concatenate([vh, jnp.ones_like(vh)], axis=-1).reshape(B, K, 2 * W)
    return pl.pallas_call(
        functools.partial(_diff_attn_body, out_scale=out_scale),
        grid=(B, W // hw, L // tq, K // tk),
        in_specs=[
            pl.BlockSpec(memory_space=pltpu.SMEM),
            pl.BlockSpec((None, tq, hw), lambda b, h, i, j: (b, i, h)),
            pl.BlockSpec((None, hw, tk), lambda b, h, i, j: (b, h, j)),
            pl.BlockSpec((None, tk, 2 * hw), lambda b, h, i, j: (b, j, h)),
            pl.BlockSpec((1, hw), lambda b, h, i, j: (0, 0)),
        ],
        out_specs=pl.BlockSpec((None, tq, hw), lambda b, h, i, j: (b, i, h)),
        out_shape=jax.ShapeDtypeStruct((B, L, W), F32),
        scratch_shapes=[pltpu.VMEM((2 * tq, hw), BF16), pltpu.VMEM((2 * tq, 1), F32),
                        pltpu.VMEM((2 * tq, 2 * hw), F32)],
        compiler_params=_params(("parallel", "parallel", "parallel", "arbitrary")),
        name="diff_attention",
    )(lam, q, kt, v, sub_g)


def _swa_finish(parts, vals, sink, o_ref):
    m = sink
    for s in parts:
        m = jnp.maximum(m, jnp.max(s, axis=-1, keepdims=True))
    denom = jnp.exp(sink - m)
    acc = None
    for s, v in zip(parts, vals):
        p = jnp.exp(s - m)
        denom = denom + jnp.sum(p, axis=-1, keepdims=True)
        pv = jnp.dot(p.astype(BF16), v, preferred_element_type=F32)
        acc = pv if acc is None else acc + pv
    o = acc / denom
    o_ref[...] = o.reshape(o_ref.shape).astype(o_ref.dtype)


def _swa_band_body(q_ref, kp_ref, kn_ref, kx_ref, kc_ref, vp_ref, vn_ref, vx_ref, vc_ref, sink_ref, o_ref):
    n = pl.program_id(2)
    nb = pl.num_programs(2)
    grp, blk, hd = q_ref.shape
    q = q_ref[...].reshape(grp * blk, hd)
    iq = lax.broadcasted_iota(jnp.int32, (grp * blk, blk), 0) % blk
    j = lax.broadcasted_iota(jnp.int32, (grp * blk, blk), 1)
    s_prev = jnp.where((iq + blk - j <= SWA_WINDOW) & (n > 0), _dot_nt_bf16(q, kp_ref[...]), NEG_INF)
    s_cur = jnp.where(jnp.abs(iq - j) <= SWA_WINDOW, _dot_nt_bf16(q, kn_ref[...]), NEG_INF)
    s_next = jnp.where((j + blk - iq <= SWA_WINDOW) & (n < nb - 1), _dot_nt_bf16(q, kx_ref[...]), NEG_INF)
    s_ctx = _dot_nt_bf16(q, kc_ref[...])
    _swa_finish([s_prev, s_cur, s_next, s_ctx], [vp_ref[...], vn_ref[...], vx_ref[...], vc_ref[...]],
                sink_ref[...], o_ref)


def _swa_ctx_body(q_ref, kc_ref, vc_ref, sink_ref, o_ref):
    grp, blk, hd = q_ref.shape
    q = q_ref[...].reshape(grp * blk, hd)
    _swa_finish([_dot_nt_bf16(q, kc_ref[...])], [vc_ref[...]], sink_ref[...], o_ref)


def _dot_nt_bf16(a, b):
    return lax.dot_general(a, b, _NT, preferred_element_type=F32)


def swa_attention(q, k, v, kc, vc, sink_rows):
    B, kvh, grp, L, hd = q.shape
    C = kc.shape[2]
    blk = SWA_BLOCK
    nb = L // blk
    qspec = pl.BlockSpec((None, None, grp, blk, hd), lambda b, h, n: (b, h, 0, n, 0))
    prev = pl.BlockSpec((None, None, blk, hd), lambda b, h, n: (b, h, jnp.maximum(n - 1, 0), 0))
    cur = pl.BlockSpec((None, None, blk, hd), lambda b, h, n: (b, h, n, 0))
    nxt = pl.BlockSpec((None, None, blk, hd), lambda b, h, n: (b, h, jnp.minimum(n + 1, nb - 1), 0))
    cspec = pl.BlockSpec((None, None, C, hd), lambda b, h, n: (b, h, 0, 0))
    sspec = pl.BlockSpec((None, grp * blk, 1), lambda b, h, n: (h, 0, 0))
    return pl.pallas_call(
        _swa_band_body,
        grid=(B, kvh, nb),
        in_specs=[qspec, prev, cur, nxt, cspec, prev, cur, nxt, cspec, sspec],
        out_specs=qspec,
        out_shape=jax.ShapeDtypeStruct(q.shape, F32),
        compiler_params=_params(("parallel", "parallel", "parallel")),
        name="swa_attention",
    )(q, k, k, k, kc, v, v, v, vc, sink_rows)


def swa_context_attention(q, kc, vc, sink_rows):
    B, kvh, grp, L, hd = q.shape
    C = kc.shape[2]
    blk = SWA_BLOCK
    qspec = pl.BlockSpec((None, None, grp, blk, hd), lambda b, h, n: (b, h, 0, n, 0))
    cspec = pl.BlockSpec((None, None, C, hd), lambda b, h, n: (b, h, 0, 0))
    sspec = pl.BlockSpec((None, grp * blk, 1), lambda b, h, n: (h, 0, 0))
    return pl.pallas_call(
        _swa_ctx_body,
        grid=(B, kvh, L // blk),
        in_specs=[qspec, cspec, cspec, sspec],
        out_specs=qspec,
        out_shape=jax.ShapeDtypeStruct(q.shape, F32),
        compiler_params=_params(("parallel", "parallel", "parallel")),
        name="swa_context_attention",
    )(q, kc, vc, sink_rows)


HY_COL_GROUP = 8
HY_K2_GROUP = 8


def _hy_fwd_outer_body(x_ref, fc_ref, fsn_ref, o_ref):
    x = x_ref[...].astype(BF16)
    o_ref[0] = jnp.dot(fc_ref[...], x, preferred_element_type=F32).astype(o_ref.dtype)
    o_ref[1] = jnp.dot(fsn_ref[...], x, preferred_element_type=F32).astype(o_ref.dtype)


def _hy_spectral_body(a_ref, m_ref, mi_ref, h_ref, o_ref):
    n1 = a_ref.shape[2]
    for j in range(a_ref.shape[1]):
        b = jnp.concatenate([a_ref[0, j], a_ref[1, j]], axis=0)
        x = jnp.dot(m_ref[j], b, preferred_element_type=F32)
        xr, xi = x[:n1], x[n1:]
        hr, hi = h_ref[0, j], h_ref[1, j]
        y = jnp.concatenate([xr * hr - xi * hi, xr * hi + xi * hr], axis=0).astype(BF16)
        c = jnp.dot(mi_ref[j], y, preferred_element_type=F32)
        o_ref[0, j] = c[:n1].astype(o_ref.dtype)
        o_ref[1, j] = c[n1:].astype(o_ref.dtype)


def _hy_inv_outer_body(c_ref, gc_ref, gsn_ref, z_ref, gate_ref, skip_ref, o_ref):
    y = (jnp.dot(gc_ref[...], c_ref[0], preferred_element_type=F32)
         + jnp.dot(gsn_ref[...], c_ref[1], preferred_element_type=F32))
    o_ref[...] = gate_ref[...] * (y + skip_ref[...] * z_ref[...])


def _hy_tables(n):
    n2 = 1 << (n.bit_length() // 2)
    n1 = n // n2
    two_pi = 2.0 * math.pi
    i2 = jnp.arange(n2, dtype=jnp.int32)
    ang2 = ((i2[:, None] * i2[None, :]) % n2).astype(F32) * (two_pi / n2)
    c2, s2 = jnp.cos(ang2), jnp.sin(ang2)
    half = n2 // 2
    fwd_c, fwd_sn = c2[:, :half].astype(BF16), (-s2[:, :half]).astype(BF16)
    inv_c, inv_sn = c2[:half, :].astype(BF16), (-s2[:half, :]).astype(BF16)
    i1 = jnp.arange(n1, dtype=jnp.int32)
    kk = n2 * i1[None, :, None] + i2[:, None, None]
    th = ((kk * i1[None, None, :]) % n).astype(F32) * (two_pi / n)
    ct, st = jnp.cos(th), jnp.sin(th)
    m_big = jnp.concatenate([jnp.concatenate([ct, st], axis=2), jnp.concatenate([-st, ct], axis=2)], axis=1)
    ctt, stt = jnp.swapaxes(ct, 1, 2), jnp.swapaxes(st, 1, 2)
    mi_big = jnp.concatenate([jnp.concatenate([ctt, -stt], axis=2), jnp.concatenate([stt, ctt], axis=2)], axis=1)
    return n1, n2, fwd_c, fwd_sn, inv_c, inv_sn, m_big.astype(BF16), mi_big.astype(BF16)


def _hy_spectrum_planes(spec, n, n1, n2):
    L = n // 2
    full = jnp.concatenate([spec, jnp.conj(spec[1:L][::-1])], axis=0) * (1.0 / n)
    o, c = full.shape[1:]
    full = jnp.transpose(full.reshape(n1, n2, o, c), (2, 1, 0, 3))
    return jnp.stack([jnp.real(full), jnp.imag(full)], axis=1).astype(F32)


def hyena_long_conv(z, gate, skip, h_planes, tables):
    B, L, C = z.shape
    n1, n2, fwd_c, fwd_sn, inv_c, inv_sn, m_big, mi_big = tables
    half = n2 // 2
    g = min(HY_COL_GROUP, n1)
    kb = min(HY_K2_GROUP, n2)
    gc = g * C
    z2 = z.reshape(B, half, n1 * C)
    gate2 = gate.reshape(B, half, n1 * C)
    plane = jax.ShapeDtypeStruct((B, 2, n2, n1 * C), BF16)
    full_mat = lambda shape: pl.BlockSpec(shape, lambda b, j: (0,) * len(shape))
    a = pl.pallas_call(
        _hy_fwd_outer_body,
        grid=(B, n1 // g),
        in_specs=[pl.BlockSpec((None, half, gc), lambda b, j: (b, 0, j)), full_mat((n2, half)), full_mat((n2, half))],
        out_specs=pl.BlockSpec((None, 2, n2, gc), lambda b, j: (b, 0, 0, j)),
        out_shape=plane,
        compiler_params=_params(("parallel", "parallel")),
        name="hy_fwd_outer",
    )(z2, fwd_c, fwd_sn)
    a5 = a.reshape(B, 2, n2, n1, C)
    cc = pl.pallas_call(
        _hy_spectral_body,
        grid=(B, n2 // kb),
        in_specs=[
            pl.BlockSpec((None, 2, kb, n1, C), lambda b, j: (b, 0, j, 0, 0)),
            pl.BlockSpec((kb, 2 * n1, 2 * n1), lambda b, j: (j, 0, 0)),
            pl.BlockSpec((kb, 2 * n1, 2 * n1), lambda b, j: (j, 0, 0)),
            pl.BlockSpec((2, kb, n1, C), lambda b, j: (0, j, 0, 0)),
        ],
        out_specs=pl.BlockSpec((None, 2, kb, n1, C), lambda b, j: (b, 0, j, 0, 0)),
        out_shape=jax.ShapeDtypeStruct((B, 2, n2, n1, C), BF16),
        compiler_params=_params(("parallel", "parallel")),
        name="hy_spectral",
    )(a5, m_big, mi_big, h_planes)
    c2 = cc.reshape(B, 2, n2, n1 * C)
    skip_t = jnp.tile(skip.astype(F32).reshape(1, C), (1, g))
    out = pl.pallas_call(
        _hy_inv_outer_body,
        grid=(B, n1 // g),
        in_specs=[
            pl.BlockSpec((None, 2, n2, gc), lambda b, j: (b, 0, 0, j)),
            full_mat((half, n2)), full_mat((half, n2)),
            pl.BlockSpec((None, half, gc), lambda b, j: (b, 0, j)),
            pl.BlockSpec((None, half, gc), lambda b, j: (b, 0, j)),
            full_mat((1, gc)),
        ],
        out_specs=pl.BlockSpec((None, half, gc), lambda b, j: (b, 0, j)),
        out_shape=jax.ShapeDtypeStruct((B, half, n1 * C), F32),
        compiler_params=_params(("parallel", "parallel")),
        name="hy_inv_outer",
    )(c2, inv_c, inv_sn, z2, gate2, skip_t)
    return out.reshape(B, L, C)


def _hyena(pa, conv_w, conv_b, f_w1, f_b1, f_w2, f_b2, f_w3, f_b3, f_freq, f_decay, skip):
    B, L, _ = pa.shape
    ch = skip.shape[-1]
    n = 2 * L
    u = _short_conv3(pa, conv_w, conv_b).astype(F32)
    v, x1, x2 = jnp.split(u, 3, axis=-1)
    spec = _hyena_filter_spectra(L, ch, f_w1, f_b1, f_w2, f_b2, f_w3, f_b3, f_freq, f_decay)
    tables = _hy_tables(n)
    h = _hy_spectrum_planes(spec, n, tables[0], tables[1])
    z = hyena_long_conv(v, x1, skip[0], h[0], tables)
    return hyena_long_conv(z, x2, skip[1], h[1], tables)


def _rms_norm(x, g):
    xf = x.astype(F32)
    y = xf * lax.rsqrt(jnp.mean(xf * xf, axis=-1, keepdims=True) + NORM_EPS)
    return (y * g.astype(F32)).astype(x.dtype)


def _rope_tables(rows):
    n_freq = HEAD_DIM // 4
    inv = ROPE_BASE ** (-jnp.arange(n_freq, dtype=F32) / n_freq)
    row = jnp.repeat(jnp.arange(rows, dtype=F32), GRID_W)
    col = jnp.tile(jnp.arange(GRID_W, dtype=F32), rows)
    ang = jnp.concatenate([row[:, None] * inv, col[:, None] * inv], axis=-1)
    return jnp.cos(ang), jnp.sin(ang)


def _apply_rope(x, cos, sin):
    shp = (x.shape[1],) + (1,) * (x.ndim - 3) + (HEAD_DIM // 2,)
    c, s = cos.reshape(shp), sin.reshape(shp)
    x1, x2 = jnp.split(x.astype(F32), 2, axis=-1)
    return jnp.concatenate([x1 * c - x2 * s, x2 * c + x1 * s], axis=-1).astype(x.dtype)


def _short_conv3(x, w, b):
    xp = jnp.pad(x, ((0, 0), (1, 1), (0, 0)))
    return xp[:, :-2] * w[0] + xp[:, 1:-1] * w[1] + xp[:, 2:] * w[2] + b


def _hyena_filter_spectra(L, ch, f_w1, f_b1, f_w2, f_b2, f_w3, f_b3, f_freq, f_decay):
    t = jnp.arange(L, dtype=F32) / max(L - 1, 1)
    ang = 2 * math.pi * t[:, None] * jnp.arange(1, HY_POS_BANDS + 1, dtype=F32)
    feat = jnp.concatenate([t[:, None], jnp.sin(ang), jnp.cos(ang)], axis=-1)
    hp = lax.Precision.HIGHEST
    h = jnp.sin(f_freq * (jnp.dot(feat, f_w1, precision=hp) + f_b1))
    h = jnp.sin(f_freq * (jnp.dot(h, f_w2, precision=hp) + f_b2))
    h = jnp.dot(h, f_w3, precision=hp) + f_b3
    h = h * jnp.exp(-f_decay * t[:, None])
    h = h.reshape(L, HY_ORDER, 2, ch)
    h_fwd, h_bwd = h[:, :, 0], h[:, :, 1]
    l1 = jnp.sum(jnp.abs(h_fwd), axis=0) + jnp.sum(jnp.abs(h_bwd[1:]), axis=0)
    kern = jnp.concatenate([h_fwd, jnp.zeros((1, HY_ORDER, ch), F32), h_bwd[1:][::-1]], axis=0) / l1
    return jnp.fft.rfft(kern, axis=0)


def _fft_long_conv(z, spec, skip):
    L = z.shape[1]
    zf = jnp.fft.rfft(z, n=2 * L, axis=1)
    y = jnp.fft.irfft(zf * spec[None], n=2 * L, axis=1)[:, :L]
    return y + skip * z


def _hyena_branch(pa, conv_w, conv_b, spec, skip):
    u = _short_conv3(pa, conv_w, conv_b).astype(F32)
    v, x1, x2 = jnp.split(u, 3, axis=-1)
    z = x1 * _fft_long_conv(v, spec[:, 0], skip[0])
    return x2 * _fft_long_conv(z, spec[:, 1], skip[1])


def _swa_project(pb, q_g, k_g, n_heads):
    B, L = pb.shape[:2]
    q, k, v = jnp.split(pb, [n_heads * HEAD_DIM, (n_heads + SWA_KV_HEADS) * HEAD_DIM], axis=-1)
    q = _rms_norm(q.reshape(B, L, n_heads, HEAD_DIM), q_g)
    k = _rms_norm(k.reshape(B, L, SWA_KV_HEADS, HEAD_DIM), k_g)
    return q, k, v.reshape(B, L, SWA_KV_HEADS, HEAD_DIM)


def _sink_softmax(s, sink):
    m = jnp.maximum(jnp.max(s, axis=-1, keepdims=True), sink)
    p = jnp.exp(s - m)
    return p / (jnp.sum(p, axis=-1, keepdims=True) + jnp.exp(sink - m))


def _swa_context(qc, kc, vc, sink):
    B, C, n_heads = qc.shape[:3]
    grp = n_heads // SWA_KV_HEADS
    qg = qc.reshape(B, C, SWA_KV_HEADS, grp, HEAD_DIM)
    s = jnp.einsum('bqhgd,bkhd->bhgqk', qg, kc).astype(F32) * HEAD_DIM ** -0.5
    p = _sink_softmax(s, sink.astype(F32).reshape(1, SWA_KV_HEADS, grp, 1, 1))
    o = jnp.einsum('bhgqk,bkhd->bqhgd', p.astype(vc.dtype), vc)
    return o.reshape(B, C, n_heads * HEAD_DIM)


def _swa_latent(q, k, v, kc, vc, sink):
    B, L, n_heads = q.shape[:3]
    grp = n_heads // SWA_KV_HEADS
    nb = L // SWA_BLOCK
    qb = q.reshape(B, nb, SWA_BLOCK, SWA_KV_HEADS, grp, HEAD_DIM)

    def band(t):
        tb = t.reshape(B, nb, SWA_BLOCK, SWA_KV_HEADS, HEAD_DIM)
        tp = jnp.pad(tb, ((0, 0), (1, 1), (0, 0), (0, 0), (0, 0)))
        return jnp.concatenate([tp[:, :-2], tp[:, 1:-1], tp[:, 2:]], axis=2)

    kb, vb = band(k), band(v)
    scale = HEAD_DIM ** -0.5
    s_loc = jnp.einsum('bnqhgd,bnkhd->bnhgqk', qb, kb).astype(F32) * scale
    s_ctx = jnp.einsum('bnqhgd,bchd->bnhgqc', qb, kc).astype(F32) * scale
    q_rel = jnp.arange(SWA_BLOCK)[:, None] + SWA_BLOCK
    k_rel = jnp.arange(3 * SWA_BLOCK)[None, :]
    k_abs = (jnp.arange(nb)[:, None, None] - 1) * SWA_BLOCK + k_rel[None]
    valid = (jnp.abs(q_rel - k_rel) <= SWA_WINDOW)[None] & (k_abs >= 0) & (k_abs < L)
    s_loc = jnp.where(valid[None, :, None, None], s_loc, NEG_INF)
    s = jnp.concatenate([s_loc, s_ctx], axis=-1)
    p = _sink_softmax(s, sink.astype(F32).reshape(1, 1, SWA_KV_HEADS, grp, 1, 1)).astype(v.dtype)
    o = (jnp.einsum('bnhgqk,bnkhd->bnqhgd', p[..., :3 * SWA_BLOCK], vb)
         + jnp.einsum('bnhgqc,bchd->bnqhgd', p[..., 3 * SWA_BLOCK:], vc))
    return o.reshape(B, L, n_heads * HEAD_DIM)


def _token_shift(x, reverse):
    if reverse:
        return jnp.pad(x, ((0, 0), (0, 1), (0, 0)))[:, 1:]
    return jnp.pad(x, ((0, 0), (1, 0), (0, 0)))[:, :-1]


def _head_l2norm(x, n_heads):
    B, L, C = x.shape
    xh = x.reshape(B, L, n_heads, HEAD_DIM)
    xh = xh * lax.rsqrt(jnp.sum(xh * xh, axis=-1, keepdims=True) + 1e-12)
    return xh.reshape(B, L, C)


def _rwkv_prepare(feats, reverse, width, mu, w0, w2, a0, a2, g2, k_k, k_a):
    n_heads = width // HEAD_DIM
    xs = feats + mu * (_token_shift(feats, reverse) - feats)
    splits = (width, 2 * width, 3 * width, 3 * width + RW_DECAY_RANK, 3 * width + RW_DECAY_RANK + RW_A_RANK)
    r, k, v, wd, ad, gd = jnp.split(xs, splits, axis=-1)
    logw = -jnp.exp(-jax.nn.softplus(-(w0 + jnp.tanh(wd) @ w2)) - 0.5)
    a = jax.nn.sigmoid(a0 + ad @ a2)
    g = jax.nn.sigmoid(gd) @ g2
    kk = _head_l2norm(k * k_k, n_heads)
    k = k * (1 + (a - 1) * k_a)
    return r, logw, kk, a, k, v, g


def _wkv7(r, logw, kk, a, k, v, z0, reverse):
    p, g, qh, yl = rwkv_chunk_prep(r, logw, kk, a, k, v, reverse=reverse)
    y, z_fin = rwkv_chunk_scan(p, g, qh, yl, z0, reverse=reverse)
    return z_fin, y


def _rwkv_readout(y, r, k, v, g, r_k, ln_w, ln_b):
    B, L, C = y.shape
    n_heads = C // HEAD_DIM
    yh = y.reshape(B, L, n_heads, HEAD_DIM)
    mean = jnp.mean(yh, axis=-1, keepdims=True)
    var = jnp.mean(jnp.square(yh - mean), axis=-1, keepdims=True)
    yn = ((yh - mean) * lax.rsqrt(var + RW_GN_EPS)).reshape(B, L, C) * ln_w + ln_b
    bonus = jnp.sum((r * k * r_k).reshape(B, L, n_heads, HEAD_DIM), axis=-1, keepdims=True)
    bonus = (bonus * v.reshape(B, L, n_heads, HEAD_DIM)).reshape(B, L, C)
    return (yn + bonus) * g


def _rwkv_branch(f_ctx, f_lat, width, mu, w0, w2, a0, a2, g2, k_k, k_a, r_k, ln_w, ln_b, need_ctx):
    B = f_lat.shape[0]
    n_heads = width // HEAD_DIM
    y_lat = 0.0
    y_ctx = 0.0 if need_ctx else None
    for d in range(2):
        rev = d == 1
        dir_args = (width, mu[d], w0[d], w2[d], a0[d], a2[d], g2[d], k_k, k_a)
        r_c, w_c, kk_c, a_c, k_c, v_c, g_c = _rwkv_prepare(f_ctx, rev, *dir_args)
        z0 = jnp.zeros((B, n_heads, HEAD_DIM, HEAD_DIM), F32)
        z_ctx, o_c = _wkv7(r_c, w_c, kk_c, a_c, k_c, v_c, z0, rev)
        r_l, w_l, kk_l, a_l, k_l, v_l, g_l = _rwkv_prepare(f_lat, rev, *dir_args)
        _, o_l = _wkv7(r_l, w_l, kk_l, a_l, k_l, v_l, z_ctx, rev)
        y_lat = y_lat + _rwkv_readout(o_l, r_l, k_l, v_l, g_l, r_k, ln_w, ln_b)
        if need_ctx:
            y_ctx = y_ctx + _rwkv_readout(o_c, r_c, k_c, v_c, g_c, r_k, ln_w, ln_b)
    return y_lat, y_ctx


def _diff_project(pd, q_g, k_g):
    B, L, w3 = pd.shape
    n_heads = w3 // (3 * 2 * HEAD_DIM)
    q, k, v = jnp.split(pd, 3, axis=-1)
    q = _rms_norm(q.reshape(B, L, n_heads, 2, HEAD_DIM), q_g)
    k = _rms_norm(k.reshape(B, L, n_heads, 2, HEAD_DIM), k_g)
    return q, k, v.reshape(B, L, n_heads, 2 * HEAD_DIM)


def _diff_maps(q, k_all, v_all, lam):
    s = jnp.einsum('bqhid,bkhid->bhiqk', q, k_all).astype(F32) * HEAD_DIM ** -0.5
    p = jax.nn.softmax(s, axis=-1)
    a = p[:, :, 0] - lam * p[:, :, 1]
    return jnp.einsum('bhqk,bkhe->bqhe', a.astype(v_all.dtype), v_all)


def _diff_latent(q, k, v, kc, vc, lam):
    B, L, n_heads = q.shape[:3]
    nb = L // DF_BLOCK
    k_all = jnp.concatenate([kc, k], axis=1)
    v_all = jnp.concatenate([vc, v], axis=1)
    qb = jnp.moveaxis(q.reshape(B, nb, DF_BLOCK, n_heads, 2, HEAD_DIM), 1, 0)
    o = lax.map(lambda q_blk: _diff_maps(q_blk, k_all, v_all, lam), qb)
    return jnp.moveaxis(o, 0, 1).reshape(B, L, n_heads, 2 * HEAD_DIM)


def _diff_readout(o, sub_g, lam_init):
    B, L = o.shape[:2]
    return (_rms_norm(o, sub_g) * (1 - lam_init)).reshape(B, L, -1)


def _swa_branch(pb_l, pb_c, cos, sin, q_g, k_g, sink, need_ctx):
    B, L = pb_l.shape[:2]
    C = pb_c.shape[1]
    n_heads = pb_l.shape[-1] // HEAD_DIM - 2 * SWA_KV_HEADS
    grp = n_heads // SWA_KV_HEADS
    scale = HEAD_DIM ** -0.5
    q_l, k_l, v_l = _swa_project(pb_l, q_g, k_g, n_heads)
    q_l, k_l = _apply_rope(q_l, cos, sin), _apply_rope(k_l, cos, sin)
    q_c, k_c, v_c = _swa_project(pb_c, q_g, k_g, n_heads)

    def q_layout(q, n):
        return jnp.transpose((q * scale).astype(BF16).reshape(B, n, SWA_KV_HEADS, grp, HEAD_DIM), (0, 2, 3, 1, 4))

    def kv_layout(t):
        return jnp.transpose(t.astype(BF16), (0, 2, 1, 3))

    def o_layout(o, n):
        return jnp.transpose(o, (0, 3, 1, 2, 4)).reshape(B, n, n_heads * HEAD_DIM)

    sink_rows = jnp.repeat(sink.astype(F32).reshape(SWA_KV_HEADS, grp), SWA_BLOCK, axis=1)[..., None]
    kc, vc = kv_layout(k_c), kv_layout(v_c)
    y_l = o_layout(swa_attention(q_layout(q_l, L), kv_layout(k_l), kv_layout(v_l), kc, vc, sink_rows), L)
    y_c = None
    if need_ctx:
        y_c = o_layout(swa_context_attention(q_layout(q_c, C), kc, vc, sink_rows), C)
    return y_l, y_c


def _diff_branch(pd_l, pd_c, cos, sin, q_g, k_g, sub_g, lam, lam_init, need_ctx):
    B, L = pd_l.shape[:2]
    C = pd_c.shape[1]
    W = pd_l.shape[-1] // 3
    scale = HEAD_DIM ** -0.5 * math.log2(math.e)
    dq_l, dk_l, dv_l = _diff_project(pd_l, q_g, k_g)
    dq_l, dk_l = _apply_rope(dq_l, cos, sin), _apply_rope(dk_l, cos, sin)
    dq_c, dk_c, dv_c = _diff_project(pd_c, q_g, k_g)
    flat = lambda t, n: t.reshape(B, n, W)
    q_l = (flat(dq_l, L) * scale).astype(BF16)
    k_c, v_c = flat(dk_c, C).astype(BF16), flat(dv_c, C).astype(BF16)
    kt_c = jnp.transpose(k_c, (0, 2, 1))
    kt_all = jnp.concatenate([kt_c, jnp.transpose(flat(dk_l, L).astype(BF16), (0, 2, 1))], axis=2)
    v_all = jnp.concatenate([v_c, flat(dv_l, L).astype(BF16)], axis=1)
    sub = sub_g.astype(F32).reshape(1, 2 * HEAD_DIM)
    lam2 = lam.astype(F32).reshape(1, 1)
    y_l = diff_attention(q_l, kt_all, v_all, sub, lam2, 1.0 - lam_init)
    y_c = None
    if need_ctx:
        q_c = (flat(dq_c, C) * scale).astype(BF16)
        y_c = diff_attention(q_c, kt_c, v_c, sub, lam2, 1.0 - lam_init)
    return y_l, y_c


def _moe_route(logits_g, logits_e, n_tok):
    g_idx = jnp.argmax(logits_g, axis=-1)
    g_prob = jnp.take_along_axis(jax.nn.softmax(logits_g, axis=-1), g_idx[:, None], axis=-1)[:, 0]
    e_logits = logits_e.reshape(n_tok, MOE_GROUPS, MOE_PER_GROUP)
    e_logits = jnp.take_along_axis(e_logits, g_idx[:, None, None], axis=1)[:, 0]
    top_p, top_e = lax.top_k(jax.nn.softmax(e_logits, axis=-1), MOE_TOP_K)
    weights = g_prob[:, None] * top_p / jnp.sum(top_p, axis=-1, keepdims=True)
    flat_e = (g_idx[:, None] * MOE_PER_GROUP + top_e).reshape(-1).astype(jnp.int32)
    flat_tok = jnp.repeat(jnp.arange(n_tok, dtype=jnp.int32), MOE_TOP_K)
    n_assign = n_tok * MOE_TOP_K
    order = jnp.argsort(flat_e)
    se = flat_e[order]
    counts = jnp.bincount(flat_e, length=MOE_EXPERTS)
    padded = (counts + MOE_BLOCK - 1) // MOE_BLOCK * MOE_BLOCK
    pad_end = jnp.cumsum(padded)
    pad_start = pad_end - padded
    start = jnp.cumsum(counts) - counts
    dest = (pad_start[se] + jnp.arange(n_assign, dtype=jnp.int32) - start[se]).astype(jnp.int32)
    n_blocks = -(-n_assign // MOE_BLOCK) + MOE_EXPERTS
    P = n_blocks * MOE_BLOCK
    tok_buf = jnp.full((P,), n_tok, jnp.int32).at[dest].set(flat_tok[order])
    blk_e = jnp.minimum(jnp.searchsorted(pad_end, jnp.arange(n_blocks) * MOE_BLOCK, side='right'),
                        MOE_EXPERTS - 1).astype(jnp.int32)
    n_used = (pad_end[-1] // MOE_BLOCK).astype(jnp.int32).reshape(1)
    slot = jnp.zeros((n_assign,), jnp.int32).at[order].set(dest).reshape(n_tok, MOE_TOP_K)
    return tok_buf, blk_e, n_used, slot, weights


def _hier_moe(tokens_bf16, tokens_f32, rg_w, rg_b, re_w, re_b, w1, w3, w2):
    n_tok, d = tokens_bf16.shape
    hp = lax.Precision.HIGHEST
    logits_g = jnp.dot(tokens_f32, rg_w, precision=hp) + rg_b
    logits_e = jnp.dot(tokens_f32, re_w, precision=hp) + re_b
    tok_buf, blk_e, n_used, slot, weights = _moe_route(logits_g, logits_e, n_tok)
    x_pad = jnp.concatenate([tokens_bf16, jnp.zeros((1, d), BF16)], axis=0)
    xb = x_pad[tok_buf]
    yb = moe_experts(xb, blk_e, n_used, w1, w3, w2)
    wts = weights.astype(F32)
    return (yb[slot[:, 0]].astype(F32) * wts[:, 0:1] + yb[slot[:, 1]].astype(F32) * wts[:, 1:2])


def _modulate(x, g, shift, scale):
    return _rms_norm(x, g) * (1 + scale) + shift


def kernel(x, c, ctx, c_ctx, mod_w, mod_b, norm1_g, norm2_g, w_in, hy_conv_w, hy_conv_b, hy_f_w1, hy_f_b1, hy_f_w2, hy_f_b2, hy_f_w3, hy_f_b3, hy_f_freq, hy_f_decay, hy_skip, swa_q_g, swa_k_g, swa_sink, rw_mu, rw_w0, rw_w2, rw_a0, rw_a2, rw_g2, rw_k_k, rw_k_a, rw_r_k, rw_ln_w, rw_ln_b, df_q_g, df_k_g, df_lq1, df_lk1, df_lq2, df_lk2, df_sub_g, w_gate, b_gate, w_branch, w_out, moe_rg_w, moe_rg_b, moe_re_w, moe_re_b, moe_w1, moe_w3, moe_w2):
    B, L, D = x.shape
    C = ctx.shape[1]
    depth = mod_w.shape[0]
    bw = D // N_BRANCH
    a_cols = 3 * bw
    swa_heads = bw // HEAD_DIM
    b_cols = (swa_heads + 2 * SWA_KV_HEADS) * HEAD_DIM
    c_cols = 3 * bw + RW_DECAY_RANK + RW_A_RANK + RW_G_RANK
    in_splits = (a_cols, a_cols + b_cols, a_cols + b_cols + c_cols)
    rows = L // GRID_W
    cos, sin = _rope_tables(rows)
    hp = lax.Precision.HIGHEST

    x_lat = x.reshape(B * L, D)
    x_ctx = ctx.reshape(B * C, D)
    for l in range(depth):
        need_ctx = l < depth - 1
        mod_lat = jnp.dot(jax.nn.silu(c), mod_w[l], precision=hp) + mod_b[l]
        mod_ctx = jnp.dot(jax.nn.silu(c_ctx)[None], mod_w[l], precision=hp) + mod_b[l]
        sh1, sc1, g1, sh2, sc2, g2 = [t[:, None, :] for t in jnp.split(mod_lat, 6, axis=-1)]
        csh1, csc1, cg1, csh2, csc2, cg2 = [t[:, None, :] for t in jnp.split(mod_ctx, 6, axis=-1)]

        w_in_b = w_in[l].astype(BF16)
        p_lat, h_lat = norm_proj(x_lat, norm1_g[l][None], sc1, sh1, w_in_b, rows_per_mod=L)
        p_ctx, h_ctx = norm_proj(x_ctx, norm1_g[l][None], csc1, csh1, w_in_b, rows_per_mod=B * C)
        p_lat = p_lat.reshape(B, L, -1)
        p_ctx = p_ctx.reshape(B, C, -1)
        pa_l, pb_l, pc_l, pd_l = jnp.split(p_lat, in_splits, axis=-1)
        pa_c, pb_c, pc_c, pd_c = jnp.split(p_ctx, in_splits, axis=-1)

        hy_args = (hy_conv_w[l], hy_conv_b[l], hy_f_w1[l], hy_f_b1[l], hy_f_w2[l], hy_f_b2[l], hy_f_w3[l],
                   hy_f_b3[l], hy_f_freq[l], hy_f_decay[l], hy_skip[l])
        ya_l = _hyena(pa_l, *hy_args)

        yb_l, yb_c = _swa_branch(pb_l, pb_c, cos, sin, swa_q_g[l], swa_k_g[l], swa_sink[l], need_ctx)

        yc_l, yc_c = _rwkv_branch(pc_c, pc_l, bw, rw_mu[l], rw_w0[l], rw_w2[l], rw_a0[l], rw_a2[l],
                                  rw_g2[l], rw_k_k[l], rw_k_a[l], rw_r_k[l], rw_ln_w[l], rw_ln_b[l], need_ctx)

        lam_init = 0.8 - 0.6 * math.exp(-0.3 * l)
        lam = (jnp.exp(jnp.sum(df_lq1[l] * df_lk1[l])) - jnp.exp(jnp.sum(df_lq2[l] * df_lk2[l])) + lam_init)
        yd_l, yd_c = _diff_branch(pd_l, pd_c, cos, sin, df_q_g[l], df_k_g[l], df_sub_g[l], lam, lam_init, need_ctx)

        wg_b = w_gate[l].astype(BF16)
        bg = b_gate[l][:, None, :]
        wb_b = w_branch[l].astype(BF16)
        wo_b = w_out[l].astype(BF16)
        ys_l = jnp.stack([t.reshape(B * L, bw).astype(BF16) for t in (ya_l, yb_l, yc_l, yd_l)])
        acc_l = merge_gated(h_lat, ys_l, wg_b, bg, wb_b)
        x_lat = resid_proj(x_lat, acc_l, wo_b, g1, rows_per_mod=L)
        if need_ctx:
            ya_c = _hyena(pa_c, *hy_args)
            ys_c = jnp.stack([t.reshape(B * C, bw).astype(BF16) for t in (ya_c, yb_c, yc_c, yd_c)])
            acc_c = merge_gated(h_ctx, ys_c, wg_b, bg, wb_b)
            x_ctx = resid_proj(x_ctx, acc_c, wo_b, cg1, rows_per_mod=B * C)

        w1_b, w3_b, w2_b = moe_w1[l].astype(BF16), moe_w3[l].astype(BF16), moe_w2[l].astype(BF16)
        moe_args = (moe_rg_w[l], moe_rg_b[l], moe_re_w[l], moe_re_b[l], w1_b, w3_b, w2_b)
        hm_lat = _modulate(x_lat.reshape(B, L, D), norm2_g[l], sh2, sc2).reshape(B * L, D)
        if need_ctx:
            hm_ctx = _modulate(x_ctx.reshape(B, C, D), norm2_g[l], csh2, csc2).reshape(B * C, D)
            tokens = jnp.concatenate([hm_ctx, hm_lat], axis=0)
            out = _hier_moe(tokens.astype(BF16), tokens, *moe_args)
            x_ctx = x_ctx + (cg2 * out[:B * C].reshape(1, B * C, D)).reshape(B * C, D)
            x_lat = x_lat + (g2 * out[B * C:].reshape(B, L, D)).reshape(B * L, D)
        else:
            out = _hier_moe(hm_lat.astype(BF16), hm_lat, *moe_args)
            x_lat = x_lat + (g2 * out.reshape(B, L, D)).reshape(B * L, D)
    return x_lat.reshape(B, L, D)
```

```python
import functools
import math

import jax
import jax.numpy as jnp
from jax import lax
from jax.experimental import pallas as pl
from jax.experimental.pallas import tpu as pltpu

F32 = jnp.float32
BF16 = jnp.bfloat16

GRID_W = 64
HEAD_DIM = 64
ROPE_BASE = 10000.0
NORM_EPS = 1e-6
NEG_INF = -1e30
N_BRANCH = 4
HY_ORDER = 2
HY_POS_BANDS = 8
SWA_KV_HEADS = 2
SWA_WINDOW = 128
SWA_BLOCK = 128
RW_DECAY_RANK = 64
RW_A_RANK = 64
RW_G_RANK = 128
RW_GN_EPS = 64e-5
DF_BLOCK = 128
DF_ROW_CHUNK = 256
MOE_GROUPS = 4
MOE_PER_GROUP = 8
MOE_EXPERTS = MOE_GROUPS * MOE_PER_GROUP
MOE_TOP_K = 2
MOE_BLOCK = 256

VMEM_LIMIT_BYTES = 56 * 1024 * 1024


def _row_tile(m, pref):
    t = min(pref, m)
    while m % t:
        t //= 2
    return t


def _params(sem):
    return pltpu.CompilerParams(dimension_semantics=sem, vmem_limit_bytes=VMEM_LIMIT_BYTES)


def _norm_proj_body(x_ref, g_ref, sc_ref, sh_ref, w_ref, p_ref, h_ref, h_scr):
    @pl.when(pl.program_id(1) == 0)
    def _():
        x = x_ref[...].astype(F32)
        y = x * lax.rsqrt(jnp.mean(x * x, axis=-1, keepdims=True) + NORM_EPS)
        h = y * g_ref[...] * (1.0 + sc_ref[...]) + sh_ref[...]
        h_scr[...] = h.astype(BF16)
        h_ref[...] = h_scr[...]

    p_ref[...] = jnp.dot(h_scr[...], w_ref[...], preferred_element_type=F32).astype(p_ref.dtype)


def norm_proj(x, g, scale, shift, w, *, rows_per_mod, tm=1024, tn=512, out_dtype=F32):
    m, d = x.shape
    n = w.shape[1]
    tm = _row_tile(rows_per_mod, tm)
    tn = _row_tile(n, tn)
    tiles_per_mod = rows_per_mod // tm
    mod_map = lambda i, j: (i // tiles_per_mod, 0, 0)
    return pl.pallas_call(
        _norm_proj_body,
        grid=(m // tm, n // tn),
        in_specs=[
            pl.BlockSpec((tm, d), lambda i, j: (i, 0)),
            pl.BlockSpec((1, d), lambda i, j: (0, 0)),
            pl.BlockSpec((None, 1, d), mod_map),
            pl.BlockSpec((None, 1, d), mod_map),
            pl.BlockSpec((d, tn), lambda i, j: (0, j)),
        ],
        out_specs=[
            pl.BlockSpec((tm, tn), lambda i, j: (i, j)),
            pl.BlockSpec((tm, d), lambda i, j: (i, 0)),
        ],
        out_shape=[jax.ShapeDtypeStruct((m, n), out_dtype), jax.ShapeDtypeStruct((m, d), BF16)],
        scratch_shapes=[pltpu.VMEM((tm, d), BF16)],
        compiler_params=_params(("parallel", "arbitrary")),
        name="norm_proj",
    )(x, g, scale, shift, w)


def _merge_body(h_ref, y0_ref, y1_ref, y2_ref, y3_ref, wg_ref, bg_ref, wb_ref, o_ref):
    h = h_ref[...]
    acc = None
    for b, y_ref in enumerate((y0_ref, y1_ref, y2_ref, y3_ref)):
        gate = jax.nn.sigmoid(jnp.dot(h, wg_ref[b], preferred_element_type=F32) + bg_ref[b])
        val = gate * jnp.dot(y_ref[...].astype(BF16), wb_ref[b], preferred_element_type=F32)
        acc = val if acc is None else acc + val
    o_ref[...] = acc.astype(o_ref.dtype)


def merge_gated(h, ys, wg, bg, wb, *, tm=1024, tn=256):
    m, d = h.shape
    w = ys[0].shape[-1]
    tm = _row_tile(m, tm)
    tn = _row_tile(d, tn)
    yspec = pl.BlockSpec((tm, w), lambda i, j: (i, 0))
    return pl.pallas_call(
        _merge_body,
        grid=(m // tm, d // tn),
        in_specs=[
            pl.BlockSpec((tm, d), lambda i, j: (i, 0)),
            yspec, yspec, yspec, yspec,
            pl.BlockSpec((N_BRANCH, d, tn), lambda i, j: (0, 0, j)),
            pl.BlockSpec((N_BRANCH, 1, tn), lambda i, j: (0, 0, j)),
            pl.BlockSpec((N_BRANCH, w, tn), lambda i, j: (0, 0, j)),
        ],
        out_specs=pl.BlockSpec((tm, tn), lambda i, j: (i, j)),
        out_shape=jax.ShapeDtypeStruct((m, d), BF16),
        compiler_params=_params(("parallel", "arbitrary")),
        name="merge_gated",
    )(h, *ys, wg, bg, wb)


def _resid_proj_body(x_ref, a_ref, w_ref, gate_ref, o_ref):
    y = jnp.dot(a_ref[...], w_ref[...], preferred_element_type=F32)
    o_ref[...] = (x_ref[...].astype(F32) + gate_ref[...] * y).astype(o_ref.dtype)


def resid_proj(x, a, w, gate, *, rows_per_mod, tm=1024, tn=512):
    m, d = x.shape
    k = a.shape[1]
    tm = _row_tile(rows_per_mod, tm)
    tn = _row_tile(d, tn)
    tiles_per_mod = rows_per_mod // tm
    return pl.pallas_call(
        _resid_proj_body,
        grid=(m // tm, d // tn),
        in_specs=[
            pl.BlockSpec((tm, tn), lambda i, j: (i, j)),
            pl.BlockSpec((tm, k), lambda i, j: (i, 0)),
            pl.BlockSpec((k, tn), lambda i, j: (0, j)),
            pl.BlockSpec((None, 1, tn), lambda i, j: (i // tiles_per_mod, 0, j)),
        ],
        out_specs=pl.BlockSpec((tm, tn), lambda i, j: (i, j)),
        out_shape=jax.ShapeDtypeStruct((m, d), x.dtype),
        compiler_params=_params(("parallel", "arbitrary")),
        name="resid_proj",
    )(x, a, w, gate)


def _moe_body(blk_e_ref, n_used_ref, x_ref, w1_ref, w3_ref, w2_ref, o_ref):
    i = pl.program_id(0)

    @pl.when(i < n_used_ref[0])
    def _():
        x = x_ref[...]
        a = jnp.dot(x, w1_ref[...], preferred_element_type=F32)
        b = jnp.dot(x, w3_ref[...], preferred_element_type=F32)
        hdn = (a * jax.nn.sigmoid(a) * b).astype(BF16)
        o_ref[...] = jnp.dot(hdn, w2_ref[...], preferred_element_type=F32).astype(o_ref.dtype)

    @pl.when(i >= n_used_ref[0])
    def _():
        o_ref[...] = jnp.zeros_like(o_ref)


def moe_experts(xb, blk_e, n_used, w1, w3, w2):
    p, d = xb.shape
    hid = w1.shape[-1]
    n_blocks = p // MOE_BLOCK
    grid_spec = pltpu.PrefetchScalarGridSpec(
        num_scalar_prefetch=2,
        grid=(n_blocks,),
        in_specs=[
            pl.BlockSpec((MOE_BLOCK, d), lambda i, e, n: (i, 0)),
            pl.BlockSpec((None, d, hid), lambda i, e, n: (e[i], 0, 0)),
            pl.BlockSpec((None, d, hid), lambda i, e, n: (e[i], 0, 0)),
            pl.BlockSpec((None, hid, d), lambda i, e, n: (e[i], 0, 0)),
        ],
        out_specs=pl.BlockSpec((MOE_BLOCK, d), lambda i, e, n: (i, 0)),
    )
    return pl.pallas_call(
        _moe_body,
        grid_spec=grid_spec,
        out_shape=jax.ShapeDtypeStruct((p, d), BF16),
        compiler_params=_params(("arbitrary",)),
        name="moe_experts",
    )(blk_e, n_used, xb, w1, w3, w2)


RW_CHUNK = 64
_HI = lax.Precision.HIGHEST
_NT = (((1,), (1,)), ((), ()))
_TN = (((0,), (0,)), ((), ()))


def _dot(a, b, dims=None):
    if dims is None:
        return jnp.dot(a, b, preferred_element_type=F32, precision=_HI)
    return lax.dot_general(a, b, dims, preferred_element_type=F32, precision=_HI)


def _bdot(a, b, dims=None):
    a, b = a.astype(BF16), b.astype(BF16)
    if dims is None:
        return jnp.dot(a, b, preferred_element_type=F32)
    return lax.dot_general(a, b, dims, preferred_element_type=F32)


def _rwkv_prep_body(r_ref, lw_ref, kk_ref, a_ref, k_ref, v_ref, p_ref, g_ref, q_ref, yl_ref, *, reverse):
    T = r_ref.shape[0]
    hd = HEAD_DIM
    row = lax.broadcasted_iota(jnp.int32, (T, T), 0)
    col = lax.broadcasted_iota(jnp.int32, (T, T), 1)
    if reverse:
        strict, incl = col > row, col >= row
    else:
        strict, incl = col < row, col <= row
    lw = lw_ref[...]
    tri = incl.astype(BF16)
    lw1 = lw.astype(BF16)
    res1 = lw - lw1.astype(F32)
    lw2 = res1.astype(BF16)
    lw3 = (res1 - lw2.astype(F32)).astype(BF16)
    cum = (jnp.dot(tri, lw1, preferred_element_type=F32) + jnp.dot(tri, lw2, preferred_element_type=F32)
           + jnp.dot(tri, lw3, preferred_element_type=F32))
    total = jnp.sum(lw, axis=0, keepdims=True)
    e_in = jnp.exp(cum)
    e_ex = jnp.exp(cum - lw)
    e_ninv = jnp.exp(-cum)
    e_rem = jnp.exp(total - cum)
    gam = jnp.exp(total)
    kk = kk_ref[...]
    kka = kk * a_ref[...]
    k = k_ref[...]
    nt = -kk * e_ex
    rt = r_ref[...] * e_in
    at = kka * e_ninv
    kt = k * e_ninv
    ac = kka * e_rem
    kc = k * e_rem
    v = v_ref[...]
    eye = lax.broadcasted_iota(jnp.int32, (hd, hd), 0) == lax.broadcasted_iota(jnp.int32, (hd, hd), 1)
    n_heads = r_ref.shape[1] // hd
    zeros = jnp.zeros((T, hd), F32)
    steps = max(1, (T - 1).bit_length())
    heads = range(n_heads)
    sls = [slice(hh * hd, (hh + 1) * hd) for hh in heads]
    bigs = [_bdot(jnp.concatenate([nt[:, sl], rt[:, sl]], axis=0),
                  jnp.concatenate([at[:, sl], kt[:, sl]], axis=0), _NT) for sl in sls]
    a_ak = [jnp.where(strict, big[:T, T:], 0.0) for big in bigs]
    pws = [jnp.where(strict, big[:T, :T], 0.0) for big in bigs]
    lhs_top = [jnp.concatenate([jnp.where(incl, big[T:, :T], 0.0), jnp.where(incl, big[T:, T:], 0.0)], axis=1)
               for big in bigs]
    xs = [jnp.concatenate([nt[:, sl], _bdot(m, v[:, sl])], axis=1) for m, sl in zip(a_ak, sls)]
    for it in range(steps):
        if it < steps - 1:
            boths = [_bdot(pw, jnp.concatenate([x, pw], axis=1)) for pw, x in zip(pws, xs)]
            xs = [x + both[:, :2 * hd] for x, both in zip(xs, boths)]
            pws = [both[:, 2 * hd:] for both in boths]
        else:
            xs = [x + _bdot(pw, x) for pw, x in zip(pws, xs)]
    rhs = [jnp.concatenate([x, jnp.concatenate([zeros, v[:, sl]], axis=1)], axis=0) for x, sl in zip(xs, sls)]
    tops = [_bdot(lt, rh) for lt, rh in zip(lhs_top, rhs)]
    bots = [_bdot(jnp.concatenate([ac[:, sl], kc[:, sl]], axis=0), rh, _TN) for sl, rh in zip(sls, rhs)]
    for hh in heads:
        p_ref[hh] = jnp.where(eye, gam[:, sls[hh]], 0.0) + bots[hh][:, :hd]
        g_ref[hh] = bots[hh][:, hd:]
    q_ref[...] = jnp.concatenate([rt[:, sl] + top[:, :hd] for sl, top in zip(sls, tops)], axis=1)
    yl_ref[...] = jnp.concatenate([top[:, hd:] for top in tops], axis=1)


def rwkv_chunk_prep(r, logw, kk, a, k, v, *, reverse):
    B, L, W = r.shape
    T = RW_CHUNK
    nh = W // HEAD_DIM
    nc = L // T
    blk = pl.BlockSpec((None, T, W), lambda b, c: (b, c, 0))
    mat = pl.BlockSpec((None, None, nh, HEAD_DIM, HEAD_DIM), lambda b, c: (b, c, 0, 0, 0))
    mat_shape = jax.ShapeDtypeStruct((B, nc, nh, HEAD_DIM, HEAD_DIM), F32)
    seq_shape = jax.ShapeDtypeStruct((B, L, W), F32)
    return pl.pallas_call(
        functools.partial(_rwkv_prep_body, reverse=reverse),
        grid=(B, nc),
        in_specs=[blk] * 6,
        out_specs=[mat, mat, blk, blk],
        out_shape=[mat_shape, mat_shape, seq_shape, seq_shape],
        compiler_params=_params(("parallel", "parallel")),
        name="rwkv_chunk_prep",
    )(r, logw, kk, a, k, v)


def _rwkv_scan_body(p_ref, g_ref, q_ref, yl_ref, z0_ref, y_ref, zf_ref, z_scr):
    c = pl.program_id(1)

    @pl.when(c == 0)
    def _():
        z_scr[...] = z0_ref[...]

    hd = HEAD_DIM
    n_heads = z_scr.shape[0]
    q = q_ref[...]
    ys = []
    for h in range(n_heads):
        z = z_scr[h]
        ys.append(_dot(q[:, h * hd:(h + 1) * hd], z))
        z_scr[h] = _dot(p_ref[h], z) + g_ref[h]
    y_ref[...] = jnp.concatenate(ys, axis=1) + yl_ref[...]

    @pl.when(c == pl.num_programs(1) - 1)
    def _():
        zf_ref[...] = z_scr[...]


def rwkv_chunk_scan(p, g, qh, yl, z0, *, reverse):
    B, nc, nh = p.shape[:3]
    L, W = qh.shape[1:]
    T = L // nc
    cidx = (lambda c: nc - 1 - c) if reverse else (lambda c: c)
    mat = pl.BlockSpec((None, None, nh, HEAD_DIM, HEAD_DIM), lambda b, c: (b, cidx(c), 0, 0, 0))
    seq = pl.BlockSpec((None, T, W), lambda b, c: (b, cidx(c), 0))
    st = pl.BlockSpec((None, nh, HEAD_DIM, HEAD_DIM), lambda b, c: (b, 0, 0, 0))
    return pl.pallas_call(
        _rwkv_scan_body,
        grid=(B, nc),
        in_specs=[mat, mat, seq, seq, st],
        out_specs=[seq, st],
        out_shape=[jax.ShapeDtypeStruct((B, L, W), F32), jax.ShapeDtypeStruct(z0.shape, F32)],
        scratch_shapes=[pltpu.VMEM((nh, HEAD_DIM, HEAD_DIM), F32)],
        compiler_params=_params(("parallel", "arbitrary")),
        name="rwkv_chunk_scan",
    )(p, g, qh, yl, z0)


def _head_prep_body(*refs, rope):
    if rope:
        x_ref, g_ref, mk_ref, sc_ref, cos_ref, sin_ref, o_ref = refs
    else:
        x_ref, g_ref, mk_ref, sc_ref, o_ref = refs
    x = x_ref[...]
    tm, wb = x.shape
    lt = 2 * HEAD_DIM
    gid = lax.broadcasted_iota(jnp.int32, (lt, lt), 0) // HEAD_DIM
    gjd = lax.broadcasted_iota(jnp.int32, (lt, lt), 1) // HEAD_DIM
    bd = (gid == gjd).astype(BF16)
    xx = x * x
    hi = xx.astype(BF16)
    lo = (xx - hi.astype(F32)).astype(BF16)
    ms = jnp.concatenate(
        [jnp.dot(hi[:, t * lt:(t + 1) * lt], bd, preferred_element_type=F32)
         + jnp.dot(lo[:, t * lt:(t + 1) * lt], bd, preferred_element_type=F32) for t in range(wb // lt)], axis=1)
    y = x * lax.rsqrt(ms * (1.0 / HEAD_DIM) + NORM_EPS) * g_ref[...]
    if rope:
        half = HEAD_DIM // 2
        lane = lax.broadcasted_iota(jnp.int32, (tm, wb), 1)
        partner = jnp.where(lane % HEAD_DIM < half, pltpu.roll(y, wb - half, axis=1), pltpu.roll(y, half, axis=1))
        y = y * cos_ref[...] + partner * sin_ref[...]
    y = jnp.where(mk_ref[...] > 0.0, y, x)
    o_ref[...] = (y * sc_ref[...]).astype(o_ref.dtype)


def head_prep(p, col0, wb, gain, mask, scale, tables, rows_per_seq):
    m = p.shape[0]
    nj = gain.shape[0]
    tm = _row_tile(rows_per_seq, 512)
    cb0 = col0 // wb
    tiles_per_seq = rows_per_seq // tm
    par = pl.BlockSpec((None, 1, wb), lambda i, j: (j, 0, 0))
    in_specs = [pl.BlockSpec((tm, wb), lambda i, j: (i, cb0 + j)), par, par, par]
    args = [p, gain, mask, scale]
    if tables is not None:
        tab = pl.BlockSpec((tm, wb), lambda i, j: (i % tiles_per_seq, 0))
        in_specs += [tab, tab]
        args += list(tables)
    return pl.pallas_call(
        functools.partial(_head_prep_body, rope=tables is not None),
        grid=(m // tm, nj),
        in_specs=in_specs,
        out_specs=pl.BlockSpec((tm, wb), lambda i, j: (i, j)),
        out_shape=jax.ShapeDtypeStruct((m, nj * wb), BF16),
        compiler_params=_params(("parallel", "parallel")),
        name="head_prep",
    )(*args)


def _diff_attn_body(lam_ref, q_ref, kt_ref, v_ref, subg_ref, o_ref, q2_scr, m_scr, acc_scr, *, out_scale):
    ki = pl.program_id(3)
    tq = q_ref.shape[0]

    @pl.when(ki == 0)
    def _():
        q = q_ref[...]
        lane = lax.broadcasted_iota(jnp.int32, q.shape, 1)
        q2_scr[0:tq, :] = jnp.where(lane < HEAD_DIM, q, jnp.zeros_like(q))
        q2_scr[tq:2 * tq, :] = jnp.where(lane >= HEAD_DIM, q, jnp.zeros_like(q))
        m_scr[...] = jnp.full_like(m_scr, -jnp.inf)
        acc_scr[...] = jnp.zeros_like(acc_scr)

    kt = kt_ref[...]
    v = jnp.concatenate([v_ref[...], jnp.ones(v_ref.shape, BF16)], axis=1)
    rc = min(DF_ROW_CHUNK, 2 * tq)
    n_chunks = 2 * tq // rc
    score = lambda c: jnp.dot(q2_scr[pl.ds(c * rc, rc), :], kt, preferred_element_type=F32)
    s_next = score(0)
    for c in range(n_chunks):
        rows = pl.ds(c * rc, rc)
        s = s_next
        if c + 1 < n_chunks:
            s_next = score(c + 1)
        m_prev = m_scr[rows, :]
        m_new = jnp.maximum(m_prev, jnp.max(s, axis=-1, keepdims=True))
        alpha = jnp.exp2(m_prev - m_new)
        p = jnp.exp2(s - m_new)
        acc_scr[rows, :] = alpha * acc_scr[rows, :] + jnp.dot(p.astype(BF16), v, preferred_element_type=F32)
        m_scr[rows, :] = m_new

    @pl.when(ki == pl.num_programs(3) - 1)
    def _():
        hw = o_ref.shape[-1]
        o = acc_scr[:, 0:hw] / acc_scr[:, hw:hw + 1]
        a = o[0:tq, :] - lam_ref[0, 0] * o[tq:2 * tq, :]
        y = a * lax.rsqrt(jnp.mean(a * a, axis=-1, keepdims=True) + NORM_EPS)
        o_ref[...] = (y * subg_ref[...] * out_scale).astype(o_ref.dtype)


def _key_tile(k, cap):
    best = 128
    t = 128
    while t <= min(k, cap):
        if k % t == 0:
            best = t
        t += 128
    return best


def diff_attention(q, q_blk0, kt, v, v_blk0, sub_g, lam, out_scale, *, tq=512, tk_cap=1280):
    B, L = q.shape[:2]
    W, K = kt.shape[1:]
    hw = 2 * HEAD_DIM
    tq = _row_tile(L, tq)
    tk = _key_tile(K, tk_cap)
    return pl.pallas_call(
        functools.partial(_diff_attn_body, out_scale=out_scale),
        grid=(B, W // hw, L // tq, K // tk),
        in_specs=[
            pl.BlockSpec(memory_space=pltpu.SMEM),
            pl.BlockSpec((None, tq, hw), lambda b, h, i, j: (b, i, q_blk0 + h)),
            pl.BlockSpec((None, hw, tk), lambda b, h, i, j: (b, h, j)),
            pl.BlockSpec((None, tk, hw), lambda b, h, i, j: (b, j, v_blk0 + h)),
            pl.BlockSpec((1, hw), lambda b, h, i, j: (0, 0)),
        ],
        out_specs=pl.BlockSpec((None, tq, hw), lambda b, h, i, j: (b, i, h)),
        out_shape=jax.ShapeDtypeStruct((B, L, W), F32),
        scratch_shapes=[pltpu.VMEM((2 * tq, hw), BF16), pltpu.VMEM((2 * tq, 1), F32),
                        pltpu.VMEM((2 * tq, 2 * hw), F32)],
        compiler_params=_params(("parallel", "parallel", "parallel", "arbitrary")),
        name="diff_attention",
    )(lam, q, kt, v, sub_g)


def _swa_finish(parts, vals, sink, o_ref):
    m = sink
    for s in parts:
        m = jnp.maximum(m, jnp.max(s, axis=-1, keepdims=True))
    denom = jnp.exp(sink - m)
    acc = None
    for s, v in zip(parts, vals):
        p = jnp.exp(s - m)
        denom = denom + jnp.sum(p, axis=-1, keepdims=True)
        pv = jnp.dot(p.astype(BF16), v, preferred_element_type=F32)
        acc = pv if acc is None else acc + pv
    o = acc / denom
    o_ref[...] = o.reshape(o_ref.shape).astype(o_ref.dtype)


def _swa_band_body(q_ref, kp_ref, kn_ref, kx_ref, kc_ref, vp_ref, vn_ref, vx_ref, vc_ref, sink_ref, o_ref):
    n = pl.program_id(2)
    nb = pl.num_programs(2)
    grp, blk, hd = q_ref.shape
    q = q_ref[...].reshape(grp * blk, hd)
    iq = lax.broadcasted_iota(jnp.int32, (grp * blk, blk), 0) % blk
    j = lax.broadcasted_iota(jnp.int32, (grp * blk, blk), 1)
    s_prev = jnp.where((iq + blk - j <= SWA_WINDOW) & (n > 0), _dot_nt_bf16(q, kp_ref[...]), NEG_INF)
    s_cur = jnp.where(jnp.abs(iq - j) <= SWA_WINDOW, _dot_nt_bf16(q, kn_ref[...]), NEG_INF)
    s_next = jnp.where((j + blk - iq <= SWA_WINDOW) & (n < nb - 1), _dot_nt_bf16(q, kx_ref[...]), NEG_INF)
    s_ctx = _dot_nt_bf16(q, kc_ref[...])
    _swa_finish([s_prev, s_cur, s_next, s_ctx], [vp_ref[...], vn_ref[...], vx_ref[...], vc_ref[...]],
                sink_ref[...], o_ref)


def _swa_ctx_body(q_ref, kc_ref, vc_ref, sink_ref, o_ref):
    grp, blk, hd = q_ref.shape
    q = q_ref[...].reshape(grp * blk, hd)
    _swa_finish([_dot_nt_bf16(q, kc_ref[...])], [vc_ref[...]], sink_ref[...], o_ref)


def _dot_nt_bf16(a, b):
    return lax.dot_general(a, b, _NT, preferred_element_type=F32)


def swa_attention(q, k, v, kc, vc, sink_rows):
    B, kvh, grp, L, hd = q.shape
    C = kc.shape[2]
    blk = SWA_BLOCK
    nb = L // blk
    qspec = pl.BlockSpec((None, None, grp, blk, hd), lambda b, h, n: (b, h, 0, n, 0))
    prev = pl.BlockSpec((None, None, blk, hd), lambda b, h, n: (b, h, jnp.maximum(n - 1, 0), 0))
    cur = pl.BlockSpec((None, None, blk, hd), lambda b, h, n: (b, h, n, 0))
    nxt = pl.BlockSpec((None, None, blk, hd), lambda b, h, n: (b, h, jnp.minimum(n + 1, nb - 1), 0))
    cspec = pl.BlockSpec((None, None, C, hd), lambda b, h, n: (b, h, 0, 0))
    sspec = pl.BlockSpec((None, grp * blk, 1), lambda b, h, n: (h, 0, 0))
    return pl.pallas_call(
        _swa_band_body,
        grid=(B, kvh, nb),
        in_specs=[qspec, prev, cur, nxt, cspec, prev, cur, nxt, cspec, sspec],
        out_specs=qspec,
        out_shape=jax.ShapeDtypeStruct(q.shape, F32),
        compiler_params=_params(("parallel", "parallel", "parallel")),
        name="swa_attention",
    )(q, k, k, k, kc, v, v, v, vc, sink_rows)


def swa_context_attention(q, kc, vc, sink_rows):
    B, kvh, grp, L, hd = q.shape
    C = kc.shape[2]
    blk = SWA_BLOCK
    qspec = pl.BlockSpec((None, None, grp, blk, hd), lambda b, h, n: (b, h, 0, n, 0))
    cspec = pl.BlockSpec((None, None, C, hd), lambda b, h, n: (b, h, 0, 0))
    sspec = pl.BlockSpec((None, grp * blk, 1), lambda b, h, n: (h, 0, 0))
    return pl.pallas_call(
        _swa_ctx_body,
        grid=(B, kvh, L // blk),
        in_specs=[qspec, cspec, cspec, sspec],
        out_specs=qspec,
        out_shape=jax.ShapeDtypeStruct(q.shape, F32),
        compiler_params=_params(("parallel", "parallel", "parallel")),
        name="swa_context_attention",
    )(q, kc, vc, sink_rows)


HY_COL_GROUP = 8
HY_K2_GROUP = 8


def _hy_fwd_outer_body(x_ref, fc_ref, fsn_ref, o_ref):
    x = x_ref[...].astype(BF16)
    o_ref[0] = jnp.dot(fc_ref[...], x, preferred_element_type=F32).astype(o_ref.dtype)
    o_ref[1] = jnp.dot(fsn_ref[...], x, preferred_element_type=F32).astype(o_ref.dtype)


def _hy_spectral_body(a_ref, m_ref, mi_ref, h_ref, o_ref):
    n1 = a_ref.shape[2]
    for j in range(a_ref.shape[1]):
        b = jnp.concatenate([a_ref[0, j], a_ref[1, j]], axis=0)
        x = jnp.dot(m_ref[j], b, preferred_element_type=F32)
        xr, xi = x[:n1], x[n1:]
        hr, hi = h_ref[0, j], h_ref[1, j]
        y = jnp.concatenate([xr * hr - xi * hi, xr * hi + xi * hr], axis=0).astype(BF16)
        c = jnp.dot(mi_ref[j], y, preferred_element_type=F32)
        o_ref[0, j] = c[:n1].astype(o_ref.dtype)
        o_ref[1, j] = c[n1:].astype(o_ref.dtype)


def _hy_inv_outer_body(c_ref, gc_ref, gsn_ref, z_ref, gate_ref, skip_ref, o_ref):
    y = (jnp.dot(gc_ref[...], c_ref[0], preferred_element_type=F32)
         + jnp.dot(gsn_ref[...], c_ref[1], preferred_element_type=F32))
    o_ref[...] = gate_ref[...] * (y + skip_ref[...] * z_ref[...])


def _hy_tables(n):
    n2 = 1 << (n.bit_length() // 2)
    n1 = n // n2
    two_pi = 2.0 * math.pi
    i2 = jnp.arange(n2, dtype=jnp.int32)
    ang2 = ((i2[:, None] * i2[None, :]) % n2).astype(F32) * (two_pi / n2)
    c2, s2 = jnp.cos(ang2), jnp.sin(ang2)
    half = n2 // 2
    fwd_c, fwd_sn = c2[:, :half].astype(BF16), (-s2[:, :half]).astype(BF16)
    inv_c, inv_sn = c2[:half, :].astype(BF16), (-s2[:half, :]).astype(BF16)
    i1 = jnp.arange(n1, dtype=jnp.int32)
    kk = n2 * i1[None, :, None] + i2[:, None, None]
    th = ((kk * i1[None, None, :]) % n).astype(F32) * (two_pi / n)
    ct, st = jnp.cos(th), jnp.sin(th)
    m_big = jnp.concatenate([jnp.concatenate([ct, st], axis=2), jnp.concatenate([-st, ct], axis=2)], axis=1)
    ctt, stt = jnp.swapaxes(ct, 1, 2), jnp.swapaxes(st, 1, 2)
    mi_big = jnp.concatenate([jnp.concatenate([ctt, -stt], axis=2), jnp.concatenate([stt, ctt], axis=2)], axis=1)
    return n1, n2, fwd_c, fwd_sn, inv_c, inv_sn, m_big.astype(BF16), mi_big.astype(BF16)


def _hy_spectrum_planes(spec, n, n1, n2):
    L = n // 2
    full = jnp.concatenate([spec, jnp.conj(spec[1:L][::-1])], axis=0) * (1.0 / n)
    o, c = full.shape[1:]
    full = jnp.transpose(full.reshape(n1, n2, o, c), (2, 1, 0, 3))
    return jnp.stack([jnp.real(full), jnp.imag(full)], axis=1).astype(F32)


def hyena_long_conv(z, gate, skip, h_planes, tables):
    B, L, C = z.shape
    n1, n2, fwd_c, fwd_sn, inv_c, inv_sn, m_big, mi_big = tables
    half = n2 // 2
    g = min(HY_COL_GROUP, n1)
    kb = min(HY_K2_GROUP, n2)
    gc = g * C
    z2 = z.reshape(B, half, n1 * C)
    gate2 = gate.reshape(B, half, n1 * C)
    plane = jax.ShapeDtypeStruct((B, 2, n2, n1 * C), BF16)
    full_mat = lambda shape: pl.BlockSpec(shape, lambda b, j: (0,) * len(shape))
    a = pl.pallas_call(
        _hy_fwd_outer_body,
        grid=(B, n1 // g),
        in_specs=[pl.BlockSpec((None, half, gc), lambda b, j: (b, 0, j)), full_mat((n2, half)), full_mat((n2, half))],
        out_specs=pl.BlockSpec((None, 2, n2, gc), lambda b, j: (b, 0, 0, j)),
        out_shape=plane,
        compiler_params=_params(("parallel", "parallel")),
        name="hy_fwd_outer",
    )(z2, fwd_c, fwd_sn)
    a5 = a.reshape(B, 2, n2, n1, C)
    cc = pl.pallas_call(
        _hy_spectral_body,
        grid=(B, n2 // kb),
        in_specs=[
            pl.BlockSpec((None, 2, kb, n1, C), lambda b, j: (b, 0, j, 0, 0)),
            pl.BlockSpec((kb, 2 * n1, 2 * n1), lambda b, j: (j, 0, 0)),
            pl.BlockSpec((kb, 2 * n1, 2 * n1), lambda b, j: (j, 0, 0)),
            pl.BlockSpec((2, kb, n1, C), lambda b, j: (0, j, 0, 0)),
        ],
        out_specs=pl.BlockSpec((None, 2, kb, n1, C), lambda b, j: (b, 0, j, 0, 0)),
        out_shape=jax.ShapeDtypeStruct((B, 2, n2, n1, C), BF16),
        compiler_params=_params(("parallel", "parallel")),
        name="hy_spectral",
    )(a5, m_big, mi_big, h_planes)
    c2 = cc.reshape(B, 2, n2, n1 * C)
    skip_t = jnp.tile(skip.astype(F32).reshape(1, C), (1, g))
    out = pl.pallas_call(
        _hy_inv_outer_body,
        grid=(B, n1 // g),
        in_specs=[
            pl.BlockSpec((None, 2, n2, gc), lambda b, j: (b, 0, 0, j)),
            full_mat((half, n2)), full_mat((half, n2)),
            pl.BlockSpec((None, half, gc), lambda b, j: (b, 0, j)),
            pl.BlockSpec((None, half, gc), lambda b, j: (b, 0, j)),
            full_mat((1, gc)),
        ],
        out_specs=pl.BlockSpec((None, half, gc), lambda b, j: (b, 0, j)),
        out_shape=jax.ShapeDtypeStruct((B, half, n1 * C), F32),
        compiler_params=_params(("parallel", "parallel")),
        name="hy_inv_outer",
    )(c2, inv_c, inv_sn, z2, gate2, skip_t)
    return out.reshape(B, L, C)


def _hyena(pa, conv_w, conv_b, f_w1, f_b1, f_w2, f_b2, f_w3, f_b3, f_freq, f_decay, skip):
    B, L, _ = pa.shape
    ch = skip.shape[-1]
    n = 2 * L
    u = _short_conv3(pa, conv_w, conv_b).astype(F32)
    v, x1, x2 = jnp.split(u, 3, axis=-1)
    spec = _hyena_filter_spectra(L, ch, f_w1, f_b1, f_w2, f_b2, f_w3, f_b3, f_freq, f_decay)
    tables = _hy_tables(n)
    h = _hy_spectrum_planes(spec, n, tables[0], tables[1])
    z = hyena_long_conv(v, x1, skip[0], h[0], tables)
    return hyena_long_conv(z, x2, skip[1], h[1], tables)


def _rms_norm(x, g):
    xf = x.astype(F32)
    y = xf * lax.rsqrt(jnp.mean(xf * xf, axis=-1, keepdims=True) + NORM_EPS)
    return (y * g.astype(F32)).astype(x.dtype)


def _rope_tables(rows):
    n_freq = HEAD_DIM // 4
    inv = ROPE_BASE ** (-jnp.arange(n_freq, dtype=F32) / n_freq)
    row = jnp.repeat(jnp.arange(rows, dtype=F32), GRID_W)
    col = jnp.tile(jnp.arange(GRID_W, dtype=F32), rows)
    ang = jnp.concatenate([row[:, None] * inv, col[:, None] * inv], axis=-1)
    return jnp.cos(ang), jnp.sin(ang)


def _apply_rope(x, cos, sin):
    shp = (x.shape[1],) + (1,) * (x.ndim - 3) + (HEAD_DIM // 2,)
    c, s = cos.reshape(shp), sin.reshape(shp)
    x1, x2 = jnp.split(x.astype(F32), 2, axis=-1)
    return jnp.concatenate([x1 * c - x2 * s, x2 * c + x1 * s], axis=-1).astype(x.dtype)


def _short_conv3(x, w, b):
    xp = jnp.pad(x, ((0, 0), (1, 1), (0, 0)))
    return xp[:, :-2] * w[0] + xp[:, 1:-1] * w[1] + xp[:, 2:] * w[2] + b


def _hyena_filter_spectra(L, ch, f_w1, f_b1, f_w2, f_b2, f_w3, f_b3, f_freq, f_decay):
    t = jnp.arange(L, dtype=F32) / max(L - 1, 1)
    ang = 2 * math.pi * t[:, None] * jnp.arange(1, HY_POS_BANDS + 1, dtype=F32)
    feat = jnp.concatenate([t[:, None], jnp.sin(ang), jnp.cos(ang)], axis=-1)
    hp = lax.Precision.HIGHEST
    h = jnp.sin(f_freq * (jnp.dot(feat, f_w1, precision=hp) + f_b1))
    h = jnp.sin(f_freq * (jnp.dot(h, f_w2, precision=hp) + f_b2))
    h = jnp.dot(h, f_w3, precision=hp) + f_b3
    h = h * jnp.exp(-f_decay * t[:, None])
    h = h.reshape(L, HY_ORDER, 2, ch)
    h_fwd, h_bwd = h[:, :, 0], h[:, :, 1]
    l1 = jnp.sum(jnp.abs(h_fwd), axis=0) + jnp.sum(jnp.abs(h_bwd[1:]), axis=0)
    kern = jnp.concatenate([h_fwd, jnp.zeros((1, HY_ORDER, ch), F32), h_bwd[1:][::-1]], axis=0) / l1
    return jnp.fft.rfft(kern, axis=0)


def _fft_long_conv(z, spec, skip):
    L = z.shape[1]
    zf = jnp.fft.rfft(z, n=2 * L, axis=1)
    y = jnp.fft.irfft(zf * spec[None], n=2 * L, axis=1)[:, :L]
    return y + skip * z


def _hyena_branch(pa, conv_w, conv_b, spec, skip):
    u = _short_conv3(pa, conv_w, conv_b).astype(F32)
    v, x1, x2 = jnp.split(u, 3, axis=-1)
    z = x1 * _fft_long_conv(v, spec[:, 0], skip[0])
    return x2 * _fft_long_conv(z, spec[:, 1], skip[1])


def _swa_project(pb, q_g, k_g, n_heads):
    B, L = pb.shape[:2]
    q, k, v = jnp.split(pb, [n_heads * HEAD_DIM, (n_heads + SWA_KV_HEADS) * HEAD_DIM], axis=-1)
    q = _rms_norm(q.reshape(B, L, n_heads, HEAD_DIM), q_g)
    k = _rms_norm(k.reshape(B, L, SWA_KV_HEADS, HEAD_DIM), k_g)
    return q, k, v.reshape(B, L, SWA_KV_HEADS, HEAD_DIM)


def _sink_softmax(s, sink):
    m = jnp.maximum(jnp.max(s, axis=-1, keepdims=True), sink)
    p = jnp.exp(s - m)
    return p / (jnp.sum(p, axis=-1, keepdims=True) + jnp.exp(sink - m))


def _swa_context(qc, kc, vc, sink):
    B, C, n_heads = qc.shape[:3]
    grp = n_heads // SWA_KV_HEADS
    qg = qc.reshape(B, C, SWA_KV_HEADS, grp, HEAD_DIM)
    s = jnp.einsum('bqhgd,bkhd->bhgqk', qg, kc).astype(F32) * HEAD_DIM ** -0.5
    p = _sink_softmax(s, sink.astype(F32).reshape(1, SWA_KV_HEADS, grp, 1, 1))
    o = jnp.einsum('bhgqk,bkhd->bqhgd', p.astype(vc.dtype), vc)
    return o.reshape(B, C, n_heads * HEAD_DIM)


def _swa_latent(q, k, v, kc, vc, sink):
    B, L, n_heads = q.shape[:3]
    grp = n_heads // SWA_KV_HEADS
    nb = L // SWA_BLOCK
    qb = q.reshape(B, nb, SWA_BLOCK, SWA_KV_HEADS, grp, HEAD_DIM)

    def band(t):
        tb = t.reshape(B, nb, SWA_BLOCK, SWA_KV_HEADS, HEAD_DIM)
        tp = jnp.pad(tb, ((0, 0), (1, 1), (0, 0), (0, 0), (0, 0)))
        return jnp.concatenate([tp[:, :-2], tp[:, 1:-1], tp[:, 2:]], axis=2)

    kb, vb = band(k), band(v)
    scale = HEAD_DIM ** -0.5
    s_loc = jnp.einsum('bnqhgd,bnkhd->bnhgqk', qb, kb).astype(F32) * scale
    s_ctx = jnp.einsum('bnqhgd,bchd->bnhgqc', qb, kc).astype(F32) * scale
    q_rel = jnp.arange(SWA_BLOCK)[:, None] + SWA_BLOCK
    k_rel = jnp.arange(3 * SWA_BLOCK)[None, :]
    k_abs = (jnp.arange(nb)[:, None, None] - 1) * SWA_BLOCK + k_rel[None]
    valid = (jnp.abs(q_rel - k_rel) <= SWA_WINDOW)[None] & (k_abs >= 0) & (k_abs < L)
    s_loc = jnp.where(valid[None, :, None, None], s_loc, NEG_INF)
    s = jnp.concatenate([s_loc, s_ctx], axis=-1)
    p = _sink_softmax(s, sink.astype(F32).reshape(1, 1, SWA_KV_HEADS, grp, 1, 1)).astype(v.dtype)
    o = (jnp.einsum('bnhgqk,bnkhd->bnqhgd', p[..., :3 * SWA_BLOCK], vb)
         + jnp.einsum('bnhgqc,bchd->bnqhgd', p[..., 3 * SWA_BLOCK:], vc))
    return o.reshape(B, L, n_heads * HEAD_DIM)


def _token_shift(x, reverse):
    if reverse:
        return jnp.pad(x, ((0, 0), (0, 1), (0, 0)))[:, 1:]
    return jnp.pad(x, ((0, 0), (1, 0), (0, 0)))[:, :-1]


def _head_l2norm(x, n_heads):
    B, L, C = x.shape
    xh = x.reshape(B, L, n_heads, HEAD_DIM)
    xh = xh * lax.rsqrt(jnp.sum(xh * xh, axis=-1, keepdims=True) + 1e-12)
    return xh.reshape(B, L, C)


def _rwkv_prepare(feats, reverse, width, mu, w0, w2, a0, a2, g2, k_k, k_a):
    n_heads = width // HEAD_DIM
    xs = feats + mu * (_token_shift(feats, reverse) - feats)
    splits = (width, 2 * width, 3 * width, 3 * width + RW_DECAY_RANK, 3 * width + RW_DECAY_RANK + RW_A_RANK)
    r, k, v, wd, ad, gd = jnp.split(xs, splits, axis=-1)
    logw = -jnp.exp(-jax.nn.softplus(-(w0 + jnp.tanh(wd) @ w2)) - 0.5)
    a = jax.nn.sigmoid(a0 + ad @ a2)
    g = jax.nn.sigmoid(gd) @ g2
    kk = _head_l2norm(k * k_k, n_heads)
    k = k * (1 + (a - 1) * k_a)
    return r, logw, kk, a, k, v, g


def _wkv7(r, logw, kk, a, k, v, z0, reverse):
    p, g, qh, yl = rwkv_chunk_prep(r, logw, kk, a, k, v, reverse=reverse)
    y, z_fin = rwkv_chunk_scan(p, g, qh, yl, z0, reverse=reverse)
    return z_fin, y


def _rwkv_readout(y, r, k, v, g, r_k, ln_w, ln_b):
    B, L, C = y.shape
    n_heads = C // HEAD_DIM
    yh = y.reshape(B, L, n_heads, HEAD_DIM)
    mean = jnp.mean(yh, axis=-1, keepdims=True)
    var = jnp.mean(jnp.square(yh - mean), axis=-1, keepdims=True)
    yn = ((yh - mean) * lax.rsqrt(var + RW_GN_EPS)).reshape(B, L, C) * ln_w + ln_b
    bonus = jnp.sum((r * k * r_k).reshape(B, L, n_heads, HEAD_DIM), axis=-1, keepdims=True)
    bonus = (bonus * v.reshape(B, L, n_heads, HEAD_DIM)).reshape(B, L, C)
    return (yn + bonus) * g


def _rwkv_branch(f_ctx, f_lat, width, mu, w0, w2, a0, a2, g2, k_k, k_a, r_k, ln_w, ln_b, need_ctx):
    B = f_lat.shape[0]
    n_heads = width // HEAD_DIM
    y_lat = 0.0
    y_ctx = 0.0 if need_ctx else None
    for d in range(2):
        rev = d == 1
        dir_args = (width, mu[d], w0[d], w2[d], a0[d], a2[d], g2[d], k_k, k_a)
        r_c, w_c, kk_c, a_c, k_c, v_c, g_c = _rwkv_prepare(f_ctx, rev, *dir_args)
        z0 = jnp.zeros((B, n_heads, HEAD_DIM, HEAD_DIM), F32)
        z_ctx, o_c = _wkv7(r_c, w_c, kk_c, a_c, k_c, v_c, z0, rev)
        r_l, w_l, kk_l, a_l, k_l, v_l, g_l = _rwkv_prepare(f_lat, rev, *dir_args)
        _, o_l = _wkv7(r_l, w_l, kk_l, a_l, k_l, v_l, z_ctx, rev)
        y_lat = y_lat + _rwkv_readout(o_l, r_l, k_l, v_l, g_l, r_k, ln_w, ln_b)
        if need_ctx:
            y_ctx = y_ctx + _rwkv_readout(o_c, r_c, k_c, v_c, g_c, r_k, ln_w, ln_b)
    return y_lat, y_ctx


def _diff_project(pd, q_g, k_g):
    B, L, w3 = pd.shape
    n_heads = w3 // (3 * 2 * HEAD_DIM)
    q, k, v = jnp.split(pd, 3, axis=-1)
    q = _rms_norm(q.reshape(B, L, n_heads, 2, HEAD_DIM), q_g)
    k = _rms_norm(k.reshape(B, L, n_heads, 2, HEAD_DIM), k_g)
    return q, k, v.reshape(B, L, n_heads, 2 * HEAD_DIM)


def _diff_maps(q, k_all, v_all, lam):
    s = jnp.einsum('bqhid,bkhid->bhiqk', q, k_all).astype(F32) * HEAD_DIM ** -0.5
    p = jax.nn.softmax(s, axis=-1)
    a = p[:, :, 0] - lam * p[:, :, 1]
    return jnp.einsum('bhqk,bkhe->bqhe', a.astype(v_all.dtype), v_all)


def _diff_latent(q, k, v, kc, vc, lam):
    B, L, n_heads = q.shape[:3]
    nb = L // DF_BLOCK
    k_all = jnp.concatenate([kc, k], axis=1)
    v_all = jnp.concatenate([vc, v], axis=1)
    qb = jnp.moveaxis(q.reshape(B, nb, DF_BLOCK, n_heads, 2, HEAD_DIM), 1, 0)
    o = lax.map(lambda q_blk: _diff_maps(q_blk, k_all, v_all, lam), qb)
    return jnp.moveaxis(o, 0, 1).reshape(B, L, n_heads, 2 * HEAD_DIM)


def _diff_readout(o, sub_g, lam_init):
    B, L = o.shape[:2]
    return (_rms_norm(o, sub_g) * (1 - lam_init)).reshape(B, L, -1)


def _swa_branch(pb_l, pb_c, cos, sin, q_g, k_g, sink, need_ctx):
    B, L = pb_l.shape[:2]
    C = pb_c.shape[1]
    n_heads = pb_l.shape[-1] // HEAD_DIM - 2 * SWA_KV_HEADS
    grp = n_heads // SWA_KV_HEADS
    scale = HEAD_DIM ** -0.5
    q_l, k_l, v_l = _swa_project(pb_l, q_g, k_g, n_heads)
    q_l, k_l = _apply_rope(q_l, cos, sin), _apply_rope(k_l, cos, sin)
    q_c, k_c, v_c = _swa_project(pb_c, q_g, k_g, n_heads)

    def q_layout(q, n):
        return jnp.transpose((q * scale).astype(BF16).reshape(B, n, SWA_KV_HEADS, grp, HEAD_DIM), (0, 2, 3, 1, 4))

    def kv_layout(t):
        return jnp.transpose(t.astype(BF16), (0, 2, 1, 3))

    def o_layout(o, n):
        return jnp.transpose(o, (0, 3, 1, 2, 4)).reshape(B, n, n_heads * HEAD_DIM)

    sink_rows = jnp.repeat(sink.astype(F32).reshape(SWA_KV_HEADS, grp), SWA_BLOCK, axis=1)[..., None]
    kc, vc = kv_layout(k_c), kv_layout(v_c)
    y_l = o_layout(swa_attention(q_layout(q_l, L), kv_layout(k_l), kv_layout(v_l), kc, vc, sink_rows), L)
    y_c = None
    if need_ctx:
        y_c = o_layout(swa_context_attention(q_layout(q_c, C), kc, vc, sink_rows), C)
    return y_l, y_c


def _diff_branch(pd_l, pd_c, cos, sin, q_g, k_g, sub_g, lam, lam_init, need_ctx):
    B, L = pd_l.shape[:2]
    C = pd_c.shape[1]
    W = pd_l.shape[-1] // 3
    scale = HEAD_DIM ** -0.5 * math.log2(math.e)
    dq_l, dk_l, dv_l = _diff_project(pd_l, q_g, k_g)
    dq_l, dk_l = _apply_rope(dq_l, cos, sin), _apply_rope(dk_l, cos, sin)
    dq_c, dk_c, dv_c = _diff_project(pd_c, q_g, k_g)
    flat = lambda t, n: t.reshape(B, n, W)
    q_l = (flat(dq_l, L) * scale).astype(BF16)
    k_c, v_c = flat(dk_c, C).astype(BF16), flat(dv_c, C).astype(BF16)
    kt_c = jnp.transpose(k_c, (0, 2, 1))
    kt_all = jnp.concatenate([kt_c, jnp.transpose(flat(dk_l, L).astype(BF16), (0, 2, 1))], axis=2)
    v_all = jnp.concatenate([v_c, flat(dv_l, L).astype(BF16)], axis=1)
    sub = sub_g.astype(F32).reshape(1, 2 * HEAD_DIM)
    lam2 = lam.astype(F32).reshape(1, 1)
    y_l = diff_attention(q_l, kt_all, v_all, sub, lam2, 1.0 - lam_init)
    y_c = None
    if need_ctx:
        q_c = (flat(dq_c, C) * scale).astype(BF16)
        y_c = diff_attention(q_c, kt_c, v_c, sub, lam2, 1.0 - lam_init)
    return y_l, y_c


def _rope_lane_tables(rows, wb):
    cos, sin = _rope_tables(rows)
    reps = wb // (HEAD_DIM // 2)
    sign = jnp.where(jnp.arange(wb) % HEAD_DIM < HEAD_DIM // 2, -1.0, 1.0).astype(F32)
    return jnp.tile(cos, (1, reps)), jnp.tile(sin, (1, reps)) * sign


def _lane_params(parts, wb):
    cols = [jnp.tile(jnp.asarray(val, F32).reshape(-1), n // jnp.asarray(val).size) for val, n in parts]
    return jnp.concatenate(cols).reshape(-1, 1, wb)


def _swa_branch2(p_l, p_c, col0, B, L, C, tables, q_g, k_g, sink, need_ctx):
    n_heads = sink.shape[0]
    grp = n_heads // SWA_KV_HEADS
    qw, kw = n_heads * HEAD_DIM, SWA_KV_HEADS * HEAD_DIM
    wb = 2 * kw
    gain = _lane_params([(q_g, qw), (k_g, kw), (1.0, kw)], wb)
    mask = _lane_params([(1.0, qw), (1.0, kw), (0.0, kw)], wb)
    scale = _lane_params([(HEAD_DIM ** -0.5, qw), (1.0, kw), (1.0, kw)], wb)
    o_l = head_prep(p_l, col0, wb, gain, mask, scale, tables, L).reshape(B, L, qw + 2 * kw)
    o_c = head_prep(p_c, col0, wb, gain, mask, scale, None, B * C).reshape(B, C, qw + 2 * kw)

    def q_layout(o, n):
        return jnp.transpose(o[:, :, :qw].reshape(B, n, SWA_KV_HEADS, grp, HEAD_DIM), (0, 2, 3, 1, 4))

    def kv_layout(t, n):
        return jnp.transpose(t.reshape(B, n, SWA_KV_HEADS, HEAD_DIM), (0, 2, 1, 3))

    def o_layout(o, n):
        return jnp.transpose(o, (0, 3, 1, 2, 4)).reshape(B * n, qw)

    sink_rows = jnp.repeat(sink.astype(F32).reshape(SWA_KV_HEADS, grp), SWA_BLOCK, axis=1)[..., None]
    kc, vc = kv_layout(o_c[:, :, qw:qw + kw], C), kv_layout(o_c[:, :, qw + kw:], C)
    y_l = o_layout(swa_attention(q_layout(o_l, L), kv_layout(o_l[:, :, qw:qw + kw], L),
                                 kv_layout(o_l[:, :, qw + kw:], L), kc, vc, sink_rows), L)
    y_c = o_layout(swa_context_attention(q_layout(o_c, C), kc, vc, sink_rows), C) if need_ctx else None
    return y_l, y_c


def _diff_branch2(p_l, p_c, col0, B, L, C, tables, q_g, k_g, sub_g, lam, lam_init, need_ctx):
    W = tables[0].shape[1]
    hw = 2 * HEAD_DIM
    q_scale = HEAD_DIM ** -0.5 * math.log2(math.e)
    gain = _lane_params([(q_g, W), (k_g, W), (1.0, W)], W)
    mask = _lane_params([(1.0, W), (1.0, W), (0.0, W)], W)
    scale = _lane_params([(q_scale, W), (1.0, W), (1.0, W)], W)
    o_l = head_prep(p_l, col0, W, gain, mask, scale, tables, L).reshape(B, L, 3 * W)
    o_c = head_prep(p_c, col0, W, gain, mask, scale, None, B * C).reshape(B, C, 3 * W)
    kv_all = jnp.concatenate([o_c, o_l], axis=1)
    kt_all = jnp.transpose(kv_all[:, :, W:2 * W], (0, 2, 1))
    sub = sub_g.astype(F32).reshape(1, hw)
    lam2 = lam.astype(F32).reshape(1, 1)
    v_blk0 = 2 * W // hw
    y_l = diff_attention(o_l, 0, kt_all, kv_all, v_blk0, sub, lam2, 1.0 - lam_init).reshape(B * L, W)
    y_c = None
    if need_ctx:
        y_c = diff_attention(o_c, 0, kt_all[:, :, :C], o_c, v_blk0, sub, lam2, 1.0 - lam_init).reshape(B * C, W)
    return y_l, y_c


def _moe_route(logits_g, logits_e, n_tok):
    g_idx = jnp.argmax(logits_g, axis=-1)
    g_prob = jnp.take_along_axis(jax.nn.softmax(logits_g, axis=-1), g_idx[:, None], axis=-1)[:, 0]
    e_logits = logits_e.reshape(n_tok, MOE_GROUPS, MOE_PER_GROUP)
    e_logits = jnp.take_along_axis(e_logits, g_idx[:, None, None], axis=1)[:, 0]
    top_p, top_e = lax.top_k(jax.nn.softmax(e_logits, axis=-1), MOE_TOP_K)
    weights = g_prob[:, None] * top_p / jnp.sum(top_p, axis=-1, keepdims=True)
    flat_e = (g_idx[:, None] * MOE_PER_GROUP + top_e).reshape(-1).astype(jnp.int32)
    flat_tok = jnp.repeat(jnp.arange(n_tok, dtype=jnp.int32), MOE_TOP_K)
    n_assign = n_tok * MOE_TOP_K
    order = jnp.argsort(flat_e)
    se = flat_e[order]
    counts = jnp.bincount(flat_e, length=MOE_EXPERTS)
    padded = (counts + MOE_BLOCK - 1) // MOE_BLOCK * MOE_BLOCK
    pad_end = jnp.cumsum(padded)
    pad_start = pad_end - padded
    start = jnp.cumsum(counts) - counts
    dest = (pad_start[se] + jnp.arange(n_assign, dtype=jnp.int32) - start[se]).astype(jnp.int32)
    n_blocks = -(-n_assign // MOE_BLOCK) + MOE_EXPERTS
    P = n_blocks * MOE_BLOCK
    tok_buf = jnp.full((P,), n_tok, jnp.int32).at[dest].set(flat_tok[order])
    blk_e = jnp.minimum(jnp.searchsorted(pad_end, jnp.arange(n_blocks) * MOE_BLOCK, side='right'),
                        MOE_EXPERTS - 1).astype(jnp.int32)
    n_used = (pad_end[-1] // MOE_BLOCK).astype(jnp.int32).reshape(1)
    slot = jnp.zeros((n_assign,), jnp.int32).at[order].set(dest).reshape(n_tok, MOE_TOP_K)
    return tok_buf, blk_e, n_used, slot, weights


def _hier_moe(tokens_bf16, tokens_f32, rg_w, rg_b, re_w, re_b, w1, w3, w2):
    n_tok, d = tokens_bf16.shape
    hp = lax.Precision.HIGHEST
    logits_g = jnp.dot(tokens_f32, rg_w, precision=hp) + rg_b
    logits_e = jnp.dot(tokens_f32, re_w, precision=hp) + re_b
    tok_buf, blk_e, n_used, slot, weights = _moe_route(logits_g, logits_e, n_tok)
    x_pad = jnp.concatenate([tokens_bf16, jnp.zeros((1, d), BF16)], axis=0)
    xb = x_pad[tok_buf]
    yb = moe_experts(xb, blk_e, n_used, w1, w3, w2)
    wts = weights.astype(F32)
    return (yb[slot[:, 0]].astype(F32) * wts[:, 0:1] + yb[slot[:, 1]].astype(F32) * wts[:, 1:2])


def _modulate(x, g, shift, scale):
    return _rms_norm(x, g) * (1 + scale) + shift


def kernel(x, c, ctx, c_ctx, mod_w, mod_b, norm1_g, norm2_g, w_in, hy_conv_w, hy_conv_b, hy_f_w1, hy_f_b1, hy_f_w2, hy_f_b2, hy_f_w3, hy_f_b3, hy_f_freq, hy_f_decay, hy_skip, swa_q_g, swa_k_g, swa_sink, rw_mu, rw_w0, rw_w2, rw_a0, rw_a2, rw_g2, rw_k_k, rw_k_a, rw_r_k, rw_ln_w, rw_ln_b, df_q_g, df_k_g, df_lq1, df_lk1, df_lq2, df_lk2, df_sub_g, w_gate, b_gate, w_branch, w_out, moe_rg_w, moe_rg_b, moe_re_w, moe_re_b, moe_w1, moe_w3, moe_w2):
    B, L, D = x.shape
    C = ctx.shape[1]
    depth = mod_w.shape[0]
    bw = D // N_BRANCH
    a_cols = 3 * bw
    swa_heads = bw // HEAD_DIM
    b_cols = (swa_heads + 2 * SWA_KV_HEADS) * HEAD_DIM
    c_cols = 3 * bw + RW_DECAY_RANK + RW_A_RANK + RW_G_RANK
    in_splits = (a_cols, a_cols + b_cols, a_cols + b_cols + c_cols)
    rows = L // GRID_W
    swa_tables = _rope_lane_tables(rows, 2 * SWA_KV_HEADS * HEAD_DIM)
    diff_tables = _rope_lane_tables(rows, bw)
    hp = lax.Precision.HIGHEST

    x_lat = x.reshape(B * L, D)
    x_ctx = ctx.reshape(B * C, D)
    for l in range(depth):
        need_ctx = l < depth - 1
        mod_lat = jnp.dot(jax.nn.silu(c), mod_w[l], precision=hp) + mod_b[l]
        mod_ctx = jnp.dot(jax.nn.silu(c_ctx)[None], mod_w[l], precision=hp) + mod_b[l]
        sh1, sc1, g1, sh2, sc2, g2 = [t[:, None, :] for t in jnp.split(mod_lat, 6, axis=-1)]
        csh1, csc1, cg1, csh2, csc2, cg2 = [t[:, None, :] for t in jnp.split(mod_ctx, 6, axis=-1)]

        w_in_b = w_in[l].astype(BF16)
        p2_lat, h_lat = norm_proj(x_lat, norm1_g[l][None], sc1, sh1, w_in_b, rows_per_mod=L)
        p2_ctx, h_ctx = norm_proj(x_ctx, norm1_g[l][None], csc1, csh1, w_in_b, rows_per_mod=B * C)
        p_lat = p2_lat.reshape(B, L, -1)
        p_ctx = p2_ctx.reshape(B, C, -1)
        pa_l, pc_l = p_lat[..., :in_splits[0]], p_lat[..., in_splits[1]:in_splits[2]]
        pa_c, pc_c = p_ctx[..., :in_splits[0]], p_ctx[..., in_splits[1]:in_splits[2]]

        hy_args = (hy_conv_w[l], hy_conv_b[l], hy_f_w1[l], hy_f_b1[l], hy_f_w2[l], hy_f_b2[l], hy_f_w3[l],
                   hy_f_b3[l], hy_f_freq[l], hy_f_decay[l], hy_skip[l])
        ya_l = _hyena(pa_l, *hy_args)

        yb_l, yb_c = _swa_branch2(p2_lat, p2_ctx, in_splits[0], B, L, C, swa_tables, swa_q_g[l], swa_k_g[l],
                                  swa_sink[l], need_ctx)

        yc_l, yc_c = _rwkv_branch(pc_c, pc_l, bw, rw_mu[l], rw_w0[l], rw_w2[l], rw_a0[l], rw_a2[l],
                                  rw_g2[l], rw_k_k[l], rw_k_a[l], rw_r_k[l], rw_ln_w[l], rw_ln_b[l], need_ctx)

        lam_init = 0.8 - 0.6 * math.exp(-0.3 * l)
        lam = (jnp.exp(jnp.sum(df_lq1[l] * df_lk1[l])) - jnp.exp(jnp.sum(df_lq2[l] * df_lk2[l])) + lam_init)
        yd_l, yd_c = _diff_branch2(p2_lat, p2_ctx, in_splits[2], B, L, C, diff_tables, df_q_g[l], df_k_g[l],
                                   df_sub_g[l], lam, lam_init, need_ctx)

        wg_b = w_gate[l].astype(BF16)
        bg = b_gate[l][:, None, :]
        wb_b = w_branch[l].astype(BF16)
        wo_b = w_out[l].astype(BF16)
        ys_l = [t.reshape(B * L, bw) for t in (ya_l, yb_l, yc_l, yd_l)]
        acc_l = merge_gated(h_lat, ys_l, wg_b, bg, wb_b)
        x_lat = resid_proj(x_lat, acc_l, wo_b, g1, rows_per_mod=L)
        if need_ctx:
            ya_c = _hyena(pa_c, *hy_args)
            ys_c = [t.reshape(B * C, bw) for t in (ya_c, yb_c, yc_c, yd_c)]
            acc_c = merge_gated(h_ctx, ys_c, wg_b, bg, wb_b)
            x_ctx = resid_proj(x_ctx, acc_c, wo_b, cg1, rows_per_mod=B * C)

        w1_b, w3_b, w2_b = moe_w1[l].astype(BF16), moe_w3[l].astype(BF16), moe_w2[l].astype(BF16)
        moe_args = (moe_rg_w[l], moe_rg_b[l], moe_re_w[l], moe_re_b[l], w1_b, w3_b, w2_b)
        hm_lat = _modulate(x_lat.reshape(B, L, D), norm2_g[l], sh2, sc2).reshape(B * L, D)
        if need_ctx:
            hm_ctx = _modulate(x_ctx.reshape(B, C, D), norm2_g[l], csh2, csc2).reshape(B * C, D)
            tokens = jnp.concatenate([hm_ctx, hm_lat], axis=0)
            out = _hier_moe(tokens.astype(BF16), tokens, *moe_args)
            x_ctx = x_ctx + (cg2 * out[:B * C].reshape(1, B * C, D)).reshape(B * C, D)
            x_lat = x_lat + (g2 * out[B * C:].reshape(B, L, D)).reshape(B * L, D)
        else:
            out = _hier_moe(hm_lat.astype(BF16), hm_lat, *moe_args)
            x_lat = x_lat + (g2 * out.reshape(B, L, D)).reshape(B * L, D)
    return x_lat.reshape(B, L, D)
```

```python
import functools
import math

import jax
import jax.numpy as jnp
from jax import lax
from jax.experimental import pallas as pl
from jax.experimental.pallas import tpu as pltpu

F32 = jnp.float32
BF16 = jnp.bfloat16

GRID_W = 64
HEAD_DIM = 64
ROPE_BASE = 10000.0
NORM_EPS = 1e-6
NEG_INF = -1e30
N_BRANCH = 4
HY_ORDER = 2
HY_POS_BANDS = 8
SWA_KV_HEADS = 2
SWA_WINDOW = 128
SWA_BLOCK = 128
RW_DECAY_RANK = 64
RW_A_RANK = 64
RW_G_RANK = 128
RW_GN_EPS = 64e-5
DF_BLOCK = 128
DF_ROW_CHUNK = 256
MOE_GROUPS = 4
MOE_PER_GROUP = 8
MOE_EXPERTS = MOE_GROUPS * MOE_PER_GROUP
MOE_TOP_K = 2
MOE_BLOCK = 256

VMEM_LIMIT_BYTES = 56 * 1024 * 1024


def _row_tile(m, pref):
    t = min(pref, m)
    while m % t:
        t //= 2
    return t


def _params(sem):
    return pltpu.CompilerParams(dimension_semantics=sem, vmem_limit_bytes=VMEM_LIMIT_BYTES)


def _norm_proj_body(x_ref, g_ref, sc_ref, sh_ref, w_ref, p_ref, h_ref, h_scr):
    @pl.when(pl.program_id(1) == 0)
    def _():
        x = x_ref[...].astype(F32)
        y = x * lax.rsqrt(jnp.mean(x * x, axis=-1, keepdims=True) + NORM_EPS)
        h = y * g_ref[...] * (1.0 + sc_ref[...]) + sh_ref[...]
        h_scr[...] = h.astype(BF16)
        h_ref[...] = h_scr[...]

    p_ref[...] = jnp.dot(h_scr[...], w_ref[...], preferred_element_type=F32).astype(p_ref.dtype)


def norm_proj(x, g, scale, shift, w, *, rows_per_mod, tm=1024, tn=512, out_dtype=F32):
    m, d = x.shape
    n = w.shape[1]
    tm = _row_tile(rows_per_mod, tm)
    tn = _row_tile(n, tn)
    tiles_per_mod = rows_per_mod // tm
    mod_map = lambda i, j: (i // tiles_per_mod, 0, 0)
    return pl.pallas_call(
        _norm_proj_body,
        grid=(m // tm, n // tn),
        in_specs=[
            pl.BlockSpec((tm, d), lambda i, j: (i, 0)),
            pl.BlockSpec((1, d), lambda i, j: (0, 0)),
            pl.BlockSpec((None, 1, d), mod_map),
            pl.BlockSpec((None, 1, d), mod_map),
            pl.BlockSpec((d, tn), lambda i, j: (0, j)),
        ],
        out_specs=[
            pl.BlockSpec((tm, tn), lambda i, j: (i, j)),
            pl.BlockSpec((tm, d), lambda i, j: (i, 0)),
        ],
        out_shape=[jax.ShapeDtypeStruct((m, n), out_dtype), jax.ShapeDtypeStruct((m, d), BF16)],
        scratch_shapes=[pltpu.VMEM((tm, d), BF16)],
        compiler_params=_params(("parallel", "arbitrary")),
        name="norm_proj",
    )(x, g, scale, shift, w)


def _merge_body(h_ref, y0_ref, y1_ref, y2_ref, y3_ref, wg_ref, bg_ref, wb_ref, o_ref):
    h = h_ref[...]
    acc = None
    for b, y_ref in enumerate((y0_ref, y1_ref, y2_ref, y3_ref)):
        gate = jax.nn.sigmoid(jnp.dot(h, wg_ref[b], preferred_element_type=F32) + bg_ref[b])
        val = gate * jnp.dot(y_ref[...].astype(BF16), wb_ref[b], preferred_element_type=F32)
        acc = val if acc is None else acc + val
    o_ref[...] = acc.astype(o_ref.dtype)


def merge_gated(h, ys, wg, bg, wb, *, tm=1024, tn=256):
    m, d = h.shape
    w = ys[0].shape[-1]
    tm = _row_tile(m, tm)
    tn = _row_tile(d, tn)
    yspec = pl.BlockSpec((tm, w), lambda i, j: (i, 0))
    return pl.pallas_call(
        _merge_body,
        grid=(m // tm, d // tn),
        in_specs=[
            pl.BlockSpec((tm, d), lambda i, j: (i, 0)),
            yspec, yspec, yspec, yspec,
            pl.BlockSpec((N_BRANCH, d, tn), lambda i, j: (0, 0, j)),
            pl.BlockSpec((N_BRANCH, 1, tn), lambda i, j: (0, 0, j)),
            pl.BlockSpec((N_BRANCH, w, tn), lambda i, j: (0, 0, j)),
        ],
        out_specs=pl.BlockSpec((tm, tn), lambda i, j: (i, j)),
        out_shape=jax.ShapeDtypeStruct((m, d), BF16),
        compiler_params=_params(("parallel", "arbitrary")),
        name="merge_gated",
    )(h, *ys, wg, bg, wb)


def _resid_proj_body(x_ref, a_ref, w_ref, gate_ref, o_ref):
    y = jnp.dot(a_ref[...], w_ref[...], preferred_element_type=F32)
    o_ref[...] = (x_ref[...].astype(F32) + gate_ref[...] * y).astype(o_ref.dtype)


def resid_proj(x, a, w, gate, *, rows_per_mod, tm=1024, tn=512):
    m, d = x.shape
    k = a.shape[1]
    tm = _row_tile(rows_per_mod, tm)
    tn = _row_tile(d, tn)
    tiles_per_mod = rows_per_mod // tm
    return pl.pallas_call(
        _resid_proj_body,
        grid=(m // tm, d // tn),
        in_specs=[
            pl.BlockSpec((tm, tn), lambda i, j: (i, j)),
            pl.BlockSpec((tm, k), lambda i, j: (i, 0)),
            pl.BlockSpec((k, tn), lambda i, j: (0, j)),
            pl.BlockSpec((None, 1, tn), lambda i, j: (i // tiles_per_mod, 0, j)),
        ],
        out_specs=pl.BlockSpec((tm, tn), lambda i, j: (i, j)),
        out_shape=jax.ShapeDtypeStruct((m, d), x.dtype),
        compiler_params=_params(("parallel", "arbitrary")),
        name="resid_proj",
    )(x, a, w, gate)


def _moe_body(blk_e_ref, n_used_ref, x_ref, w1_ref, w3_ref, w2_ref, o_ref):
    i = pl.program_id(0)

    @pl.when(i < n_used_ref[0])
    def _():
        x = x_ref[...]
        a = jnp.dot(x, w1_ref[...], preferred_element_type=F32)
        b = jnp.dot(x, w3_ref[...], preferred_element_type=F32)
        hdn = (a * jax.nn.sigmoid(a) * b).astype(BF16)
        o_ref[...] = jnp.dot(hdn, w2_ref[...], preferred_element_type=F32).astype(o_ref.dtype)

    @pl.when(i >= n_used_ref[0])
    def _():
        o_ref[...] = jnp.zeros_like(o_ref)


def moe_experts(xb, blk_e, n_used, w1, w3, w2):
    p, d = xb.shape
    hid = w1.shape[-1]
    n_blocks = p // MOE_BLOCK
    grid_spec = pltpu.PrefetchScalarGridSpec(
        num_scalar_prefetch=2,
        grid=(n_blocks,),
        in_specs=[
            pl.BlockSpec((MOE_BLOCK, d), lambda i, e, n: (i, 0)),
            pl.BlockSpec((None, d, hid), lambda i, e, n: (e[i], 0, 0)),
            pl.BlockSpec((None, d, hid), lambda i, e, n: (e[i], 0, 0)),
            pl.BlockSpec((None, hid, d), lambda i, e, n: (e[i], 0, 0)),
        ],
        out_specs=pl.BlockSpec((MOE_BLOCK, d), lambda i, e, n: (i, 0)),
    )
    return pl.pallas_call(
        _moe_body,
        grid_spec=grid_spec,
        out_shape=jax.ShapeDtypeStruct((p, d), BF16),
        compiler_params=_params(("arbitrary",)),
        name="moe_experts",
    )(blk_e, n_used, xb, w1, w3, w2)


RW_CHUNK = 64
_HI = lax.Precision.HIGHEST
_NT = (((1,), (1,)), ((), ()))
_TN = (((0,), (0,)), ((), ()))


def _dot(a, b, dims=None):
    if dims is None:
        return jnp.dot(a, b, preferred_element_type=F32, precision=_HI)
    return lax.dot_general(a, b, dims, preferred_element_type=F32, precision=_HI)


def _bdot(a, b, dims=None):
    a, b = a.astype(BF16), b.astype(BF16)
    if dims is None:
        return jnp.dot(a, b, preferred_element_type=F32)
    return lax.dot_general(a, b, dims, preferred_element_type=F32)


def _rwkv_prep_body(r_ref, lw_ref, kk_ref, a_ref, k_ref, v_ref, p_ref, g_ref, q_ref, yl_ref, *, reverse):
    T = r_ref.shape[0]
    hd = HEAD_DIM
    row = lax.broadcasted_iota(jnp.int32, (T, T), 0)
    col = lax.broadcasted_iota(jnp.int32, (T, T), 1)
    if reverse:
        strict, incl = col > row, col >= row
    else:
        strict, incl = col < row, col <= row
    lw = lw_ref[...]
    tri = incl.astype(BF16)
    lw1 = lw.astype(BF16)
    res1 = lw - lw1.astype(F32)
    lw2 = res1.astype(BF16)
    lw3 = (res1 - lw2.astype(F32)).astype(BF16)
    cum = (jnp.dot(tri, lw1, preferred_element_type=F32) + jnp.dot(tri, lw2, preferred_element_type=F32)
           + jnp.dot(tri, lw3, preferred_element_type=F32))
    total = jnp.sum(lw, axis=0, keepdims=True)
    e_in = jnp.exp(cum)
    e_ex = jnp.exp(cum - lw)
    e_ninv = jnp.exp(-cum)
    e_rem = jnp.exp(total - cum)
    gam = jnp.exp(total)
    kk = kk_ref[...]
    kka = kk * a_ref[...]
    k = k_ref[...]
    nt = -kk * e_ex
    rt = r_ref[...] * e_in
    at = kka * e_ninv
    kt = k * e_ninv
    ac = kka * e_rem
    kc = k * e_rem
    v = v_ref[...]
    eye = lax.broadcasted_iota(jnp.int32, (hd, hd), 0) == lax.broadcasted_iota(jnp.int32, (hd, hd), 1)
    n_heads = r_ref.shape[1] // hd
    zeros = jnp.zeros((T, hd), F32)
    steps = max(1, (T - 1).bit_length())
    heads = range(n_heads)
    sls = [slice(hh * hd, (hh + 1) * hd) for hh in heads]
    bigs = [_bdot(jnp.concatenate([nt[:, sl], rt[:, sl]], axis=0),
                  jnp.concatenate([at[:, sl], kt[:, sl]], axis=0), _NT) for sl in sls]
    a_ak = [jnp.where(strict, big[:T, T:], 0.0) for big in bigs]
    pws = [jnp.where(strict, big[:T, :T], 0.0) for big in bigs]
    lhs_top = [jnp.concatenate([jnp.where(incl, big[T:, :T], 0.0), jnp.where(incl, big[T:, T:], 0.0)], axis=1)
               for big in bigs]
    xs = [jnp.concatenate([nt[:, sl], _bdot(m, v[:, sl])], axis=1) for m, sl in zip(a_ak, sls)]
    for it in range(steps):
        if it < steps - 1:
            boths = [_bdot(pw, jnp.concatenate([x, pw], axis=1)) for pw, x in zip(pws, xs)]
            xs = [x + both[:, :2 * hd] for x, both in zip(xs, boths)]
            pws = [both[:, 2 * hd:] for both in boths]
        else:
            xs = [x + _bdot(pw, x) for pw, x in zip(pws, xs)]
    rhs = [jnp.concatenate([x, jnp.concatenate([zeros, v[:, sl]], axis=1)], axis=0) for x, sl in zip(xs, sls)]
    tops = [_bdot(lt, rh) for lt, rh in zip(lhs_top, rhs)]
    bots = [_bdot(jnp.concatenate([ac[:, sl], kc[:, sl]], axis=0), rh, _TN) for sl, rh in zip(sls, rhs)]
    for hh in heads:
        p_ref[hh] = jnp.where(eye, gam[:, sls[hh]], 0.0) + bots[hh][:, :hd]
        g_ref[hh] = bots[hh][:, hd:]
    q_ref[...] = jnp.concatenate([rt[:, sl] + top[:, :hd] for sl, top in zip(sls, tops)], axis=1)
    yl_ref[...] = jnp.concatenate([top[:, hd:] for top in tops], axis=1)


def rwkv_chunk_prep(r, logw, kk, a, k, v, *, reverse):
    B, L, W = r.shape
    T = RW_CHUNK
    nh = W // HEAD_DIM
    nc = L // T
    blk = pl.BlockSpec((None, T, W), lambda b, c: (b, c, 0))
    mat = pl.BlockSpec((None, None, nh, HEAD_DIM, HEAD_DIM), lambda b, c: (b, c, 0, 0, 0))
    mat_shape = jax.ShapeDtypeStruct((B, nc, nh, HEAD_DIM, HEAD_DIM), F32)
    seq_shape = jax.ShapeDtypeStruct((B, L, W), F32)
    return pl.pallas_call(
        functools.partial(_rwkv_prep_body, reverse=reverse),
        grid=(B, nc),
        in_specs=[blk] * 6,
        out_specs=[mat, mat, blk, blk],
        out_shape=[mat_shape, mat_shape, seq_shape, seq_shape],
        compiler_params=_params(("parallel", "parallel")),
        name="rwkv_chunk_prep",
    )(r, logw, kk, a, k, v)


def _rwkv_scan_body(p_ref, g_ref, q_ref, yl_ref, z0_ref, y_ref, zf_ref, z_scr):
    c = pl.program_id(1)

    @pl.when(c == 0)
    def _():
        z_scr[...] = z0_ref[...]

    hd = HEAD_DIM
    n_heads = z_scr.shape[0]
    q = q_ref[...]
    ys = []
    for h in range(n_heads):
        z = z_scr[h]
        ys.append(_dot(q[:, h * hd:(h + 1) * hd], z))
        z_scr[h] = _dot(p_ref[h], z) + g_ref[h]
    y_ref[...] = jnp.concatenate(ys, axis=1) + yl_ref[...]

    @pl.when(c == pl.num_programs(1) - 1)
    def _():
        zf_ref[...] = z_scr[...]


def rwkv_chunk_scan(p, g, qh, yl, z0, *, reverse):
    B, nc, nh = p.shape[:3]
    L, W = qh.shape[1:]
    T = L // nc
    cidx = (lambda c: nc - 1 - c) if reverse else (lambda c: c)
    mat = pl.BlockSpec((None, None, nh, HEAD_DIM, HEAD_DIM), lambda b, c: (b, cidx(c), 0, 0, 0))
    seq = pl.BlockSpec((None, T, W), lambda b, c: (b, cidx(c), 0))
    st = pl.BlockSpec((None, nh, HEAD_DIM, HEAD_DIM), lambda b, c: (b, 0, 0, 0))
    return pl.pallas_call(
        _rwkv_scan_body,
        grid=(B, nc),
        in_specs=[mat, mat, seq, seq, st],
        out_specs=[seq, st],
        out_shape=[jax.ShapeDtypeStruct((B, L, W), F32), jax.ShapeDtypeStruct(z0.shape, F32)],
        scratch_shapes=[pltpu.VMEM((nh, HEAD_DIM, HEAD_DIM), F32)],
        compiler_params=_params(("parallel", "arbitrary")),
        name="rwkv_chunk_scan",
    )(p, g, qh, yl, z0)


def _head_sum(x):
    lt = 2 * HEAD_DIM
    gi = lax.broadcasted_iota(jnp.int32, (lt, lt), 0) // HEAD_DIM
    gj = lax.broadcasted_iota(jnp.int32, (lt, lt), 1) // HEAD_DIM
    bd = (gi == gj).astype(BF16)
    hi = x.astype(BF16)
    lo = (x - hi.astype(F32)).astype(BF16)
    tiles = [jnp.dot(hi[:, t * lt:(t + 1) * lt], bd, preferred_element_type=F32)
             + jnp.dot(lo[:, t * lt:(t + 1) * lt], bd, preferred_element_type=F32) for t in range(x.shape[1] // lt)]
    return tiles[0] if len(tiles) == 1 else jnp.concatenate(tiles, axis=1)


def _rwkv_pre_body(x_ref, xp_ref, xn_ref, mu_ref, wt_ref, w0_ref, a0_ref, kk_ref, ka_ref, *out_refs, tiles_per_seq):
    i = pl.program_id(0)
    x = x_ref[...]
    tm, feat = x.shape
    w = kk_ref.shape[-1]
    tail = feat - 3 * w
    row = lax.broadcasted_iota(jnp.int32, (tm, feat), 0)
    pos = i % tiles_per_seq
    prev_row = jnp.where(pos == 0, 0.0, xp_ref[7:8, :])
    next_row = jnp.where(pos == tiles_per_seq - 1, 0.0, xn_ref[0:1, :])
    lane = lax.broadcasted_iota(jnp.int32, (tm, tail), 1)
    kscale = kk_ref[...]
    kmix = ka_ref[...]
    for d in range(2):
        if d == 0:
            sh = jnp.where(row == 0, prev_row, pltpu.roll(x, 1, axis=0))
        else:
            sh = jnp.where(row == tm - 1, next_row, pltpu.roll(x, tm - 1, axis=0))
        xs = x + mu_ref[d] * (sh - x)
        r, k, v, t = xs[:, :w], xs[:, w:2 * w], xs[:, 2 * w:3 * w], xs[:, 3 * w:]
        act = jnp.where(lane < RW_DECAY_RANK, jnp.tanh(t),
                        jnp.where(lane < RW_DECAY_RANK + RW_A_RANK, t, jax.nn.sigmoid(t)))
        hi = act.astype(BF16)
        lo = (act - hi.astype(F32)).astype(BF16)
        z = jnp.dot(hi, wt_ref[d], preferred_element_type=F32) + jnp.dot(lo, wt_ref[d], preferred_element_type=F32)
        logw = -math.exp(-0.5) * jax.nn.sigmoid(w0_ref[d] + z[:, :w])
        a = jax.nn.sigmoid(a0_ref[d] + z[:, w:2 * w])
        g = z[:, 2 * w:]
        kx = k * kscale
        kk = kx * lax.rsqrt(_head_sum(kx * kx) + 1e-12)
        k2 = k * (1.0 + (a - 1.0) * kmix)
        for ref, val in zip(out_refs[7 * d:7 * d + 7], (r, logw, kk, a, k2, v, g)):
            ref[...] = val


def rwkv_pre(feats, mu, wt, w0, a0, k_k, k_a, rows_per_seq, width):
    m, feat = feats.shape
    tm = _row_tile(rows_per_seq, 256)
    tiles_per_seq = rows_per_seq // tm
    r8 = tm // 8
    full = lambda shape: pl.BlockSpec(shape, lambda i: (0,) * len(shape))
    outs = pl.pallas_call(
        functools.partial(_rwkv_pre_body, tiles_per_seq=tiles_per_seq),
        grid=(m // tm,),
        in_specs=[
            pl.BlockSpec((tm, feat), lambda i: (i, 0)),
            pl.BlockSpec((8, feat), lambda i: (jnp.maximum(i * r8 - 1, 0), 0)),
            pl.BlockSpec((8, feat), lambda i: (jnp.minimum((i + 1) * r8, m // 8 - 1), 0)),
            full(mu.shape), full(wt.shape), full(w0.shape), full(a0.shape), full(k_k.shape), full(k_a.shape),
        ],
        out_specs=[pl.BlockSpec((tm, width), lambda i: (i, 0))] * 14,
        out_shape=[jax.ShapeDtypeStruct((m, width), F32)] * 14,
        compiler_params=_params(("parallel",)),
        name="rwkv_pre",
    )(feats, feats, feats, mu, wt, w0, a0, k_k, k_a)
    return outs[:7], outs[7:]


def _rwkv_readout_body(*refs):
    dirs, (rk_ref, lnw_ref, lnb_ref, o_ref) = (refs[0:5], refs[5:10]), refs[10:]
    inv = 1.0 / HEAD_DIM
    acc = None
    for y_ref, r_ref, k_ref, v_ref, g_ref in dirs:
        y = y_ref[...]
        c = y - _head_sum(y) * inv
        yn = c * lax.rsqrt(_head_sum(c * c) * inv + RW_GN_EPS) * lnw_ref[...] + lnb_ref[...]
        bonus = _head_sum(r_ref[...] * k_ref[...] * rk_ref[...]) * v_ref[...]
        val = (yn + bonus) * g_ref[...]
        acc = val if acc is None else acc + val
    o_ref[...] = acc


def rwkv_readout(fwd, bwd, r_k, ln_w, ln_b):
    m, w = fwd[0].shape
    tm = _row_tile(m, 512)
    blk = pl.BlockSpec((tm, w), lambda i: (i, 0))
    par = pl.BlockSpec((1, w), lambda i: (0, 0))
    return pl.pallas_call(
        _rwkv_readout_body,
        grid=(m // tm,),
        in_specs=[blk] * 10 + [par] * 3,
        out_specs=blk,
        out_shape=jax.ShapeDtypeStruct((m, w), F32),
        compiler_params=_params(("parallel",)),
        name="rwkv_readout",
    )(*fwd, *bwd, r_k, ln_w, ln_b)


def _head_prep_body(*refs, rope):
    if rope:
        x_ref, g_ref, mk_ref, sc_ref, cos_ref, sin_ref, o_ref = refs
    else:
        x_ref, g_ref, mk_ref, sc_ref, o_ref = refs
    x = x_ref[...]
    tm, wb = x.shape
    lt = 2 * HEAD_DIM
    gid = lax.broadcasted_iota(jnp.int32, (lt, lt), 0) // HEAD_DIM
    gjd = lax.broadcasted_iota(jnp.int32, (lt, lt), 1) // HEAD_DIM
    bd = (gid == gjd).astype(BF16)
    xx = x * x
    hi = xx.astype(BF16)
    lo = (xx - hi.astype(F32)).astype(BF16)
    ms = jnp.concatenate(
        [jnp.dot(hi[:, t * lt:(t + 1) * lt], bd, preferred_element_type=F32)
         + jnp.dot(lo[:, t * lt:(t + 1) * lt], bd, preferred_element_type=F32) for t in range(wb // lt)], axis=1)
    y = x * lax.rsqrt(ms * (1.0 / HEAD_DIM) + NORM_EPS) * g_ref[...]
    if rope:
        half = HEAD_DIM // 2
        lane = lax.broadcasted_iota(jnp.int32, (tm, wb), 1)
        partner = jnp.where(lane % HEAD_DIM < half, pltpu.roll(y, wb - half, axis=1), pltpu.roll(y, half, axis=1))
        y = y * cos_ref[...] + partner * sin_ref[...]
    y = jnp.where(mk_ref[...] > 0.0, y, x)
    o_ref[...] = (y * sc_ref[...]).astype(o_ref.dtype)


def head_prep(p, col0, wb, gain, mask, scale, tables, rows_per_seq):
    m = p.shape[0]
    nj = gain.shape[0]
    tm = _row_tile(rows_per_seq, 512)
    cb0 = col0 // wb
    tiles_per_seq = rows_per_seq // tm
    par = pl.BlockSpec((None, 1, wb), lambda i, j: (j, 0, 0))
    in_specs = [pl.BlockSpec((tm, wb), lambda i, j: (i, cb0 + j)), par, par, par]
    args = [p, gain, mask, scale]
    if tables is not None:
        tab = pl.BlockSpec((tm, wb), lambda i, j: (i % tiles_per_seq, 0))
        in_specs += [tab, tab]
        args += list(tables)
    return pl.pallas_call(
        functools.partial(_head_prep_body, rope=tables is not None),
        grid=(m // tm, nj),
        in_specs=in_specs,
        out_specs=pl.BlockSpec((tm, wb), lambda i, j: (i, j)),
        out_shape=jax.ShapeDtypeStruct((m, nj * wb), BF16),
        compiler_params=_params(("parallel", "parallel")),
        name="head_prep",
    )(*args)


def _diff_attn_body(lam_ref, q_ref, kt_ref, v_ref, subg_ref, o_ref, q2_scr, m_scr, acc_scr, *, out_scale):
    ki = pl.program_id(3)
    tq = q_ref.shape[0]

    @pl.when(ki == 0)
    def _():
        q = q_ref[...]
        lane = lax.broadcasted_iota(jnp.int32, q.shape, 1)
        q2_scr[0:tq, :] = jnp.where(lane < HEAD_DIM, q, jnp.zeros_like(q))
        q2_scr[tq:2 * tq, :] = jnp.where(lane >= HEAD_DIM, q, jnp.zeros_like(q))
        m_scr[...] = jnp.full_like(m_scr, -jnp.inf)
        acc_scr[...] = jnp.zeros_like(acc_scr)

    kt = kt_ref[...]
    v = jnp.concatenate([v_ref[...], jnp.ones(v_ref.shape, BF16)], axis=1)
    rc = min(DF_ROW_CHUNK, 2 * tq)
    n_chunks = 2 * tq // rc
    score = lambda c: jnp.dot(q2_scr[pl.ds(c * rc, rc), :], kt, preferred_element_type=F32)
    s_next = score(0)
    for c in range(n_chunks):
        rows = pl.ds(c * rc, rc)
        s = s_next
        if c + 1 < n_chunks:
            s_next = score(c + 1)
        m_prev = m_scr[rows, :]
        m_new = jnp.maximum(m_prev, jnp.max(s, axis=-1, keepdims=True))
        alpha = jnp.exp2(m_prev - m_new)
        p = jnp.exp2(s - m_new)
        acc_scr[rows, :] = alpha * acc_scr[rows, :] + jnp.dot(p.astype(BF16), v, preferred_element_type=F32)
        m_scr[rows, :] = m_new

    @pl.when(ki == pl.num_programs(3) - 1)
    def _():
        hw = o_ref.shape[-1]
        o = acc_scr[:, 0:hw] / acc_scr[:, hw:hw + 1]
        a = o[0:tq, :] - lam_ref[0, 0] * o[tq:2 * tq, :]
        y = a * lax.rsqrt(jnp.mean(a * a, axis=-1, keepdims=True) + NORM_EPS)
        o_ref[...] = (y * subg_ref[...] * out_scale).astype(o_ref.dtype)


def _key_tile(k, cap):
    best = 128
    t = 128
    while t <= min(k, cap):
        if k % t == 0:
            best = t
        t += 128
    return best


def diff_attention(q, q_blk0, kt, v, v_blk0, sub_g, lam, out_scale, *, tq=512, tk_cap=1280):
    B, L = q.shape[:2]
    W, K = kt.shape[1:]
    hw = 2 * HEAD_DIM
    tq = _row_tile(L, tq)
    tk = _key_tile(K, tk_cap)
    return pl.pallas_call(
        functools.partial(_diff_attn_body, out_scale=out_scale),
        grid=(B, W // hw, L // tq, K // tk),
        in_specs=[
            pl.BlockSpec(memory_space=pltpu.SMEM),
            pl.BlockSpec((None, tq, hw), lambda b, h, i, j: (b, i, q_blk0 + h)),
            pl.BlockSpec((None, hw, tk), lambda b, h, i, j: (b, h, j)),
            pl.BlockSpec((None, tk, hw), lambda b, h, i, j: (b, j, v_blk0 + h)),
            pl.BlockSpec((1, hw), lambda b, h, i, j: (0, 0)),
        ],
        out_specs=pl.BlockSpec((None, tq, hw), lambda b, h, i, j: (b, i, h)),
        out_shape=jax.ShapeDtypeStruct((B, L, W), F32),
        scratch_shapes=[pltpu.VMEM((2 * tq, hw), BF16), pltpu.VMEM((2 * tq, 1), F32),
                        pltpu.VMEM((2 * tq, 2 * hw), F32)],
        compiler_params=_params(("parallel", "parallel", "parallel", "arbitrary")),
        name="diff_attention",
    )(lam, q, kt, v, sub_g)


def _swa_finish(parts, vals, sink, o_ref):
    m = sink
    for s in parts:
        m = jnp.maximum(m, jnp.max(s, axis=-1, keepdims=True))
    denom = jnp.exp(sink - m)
    acc = None
    for s, v in zip(parts, vals):
        p = jnp.exp(s - m)
        denom = denom + jnp.sum(p, axis=-1, keepdims=True)
        pv = jnp.dot(p.astype(BF16), v, preferred_element_type=F32)
        acc = pv if acc is None else acc + pv
    o = acc / denom
    o_ref[...] = o.reshape(o_ref.shape).astype(o_ref.dtype)


def _swa_band_body(q_ref, kp_ref, kn_ref, kx_ref, kc_ref, vp_ref, vn_ref, vx_ref, vc_ref, sink_ref, o_ref):
    n = pl.program_id(2)
    nb = pl.num_programs(2)
    grp, blk, hd = q_ref.shape
    q = q_ref[...].reshape(grp * blk, hd)
    iq = lax.broadcasted_iota(jnp.int32, (grp * blk, blk), 0) % blk
    j = lax.broadcasted_iota(jnp.int32, (grp * blk, blk), 1)
    s_prev = jnp.where((iq + blk - j <= SWA_WINDOW) & (n > 0), _dot_nt_bf16(q, kp_ref[...]), NEG_INF)
    s_cur = jnp.where(jnp.abs(iq - j) <= SWA_WINDOW, _dot_nt_bf16(q, kn_ref[...]), NEG_INF)
    s_next = jnp.where((j + blk - iq <= SWA_WINDOW) & (n < nb - 1), _dot_nt_bf16(q, kx_ref[...]), NEG_INF)
    s_ctx = _dot_nt_bf16(q, kc_ref[...])
    _swa_finish([s_prev, s_cur, s_next, s_ctx], [vp_ref[...], vn_ref[...], vx_ref[...], vc_ref[...]],
                sink_ref[...], o_ref)


def _swa_ctx_body(q_ref, kc_ref, vc_ref, sink_ref, o_ref):
    grp, blk, hd = q_ref.shape
    q = q_ref[...].reshape(grp * blk, hd)
    _swa_finish([_dot_nt_bf16(q, kc_ref[...])], [vc_ref[...]], sink_ref[...], o_ref)


def _dot_nt_bf16(a, b):
    return lax.dot_general(a, b, _NT, preferred_element_type=F32)


def swa_attention(q, k, v, kc, vc, sink_rows):
    B, kvh, grp, L, hd = q.shape
    C = kc.shape[2]
    blk = SWA_BLOCK
    nb = L // blk
    qspec = pl.BlockSpec((None, None, grp, blk, hd), lambda b, h, n: (b, h, 0, n, 0))
    prev = pl.BlockSpec((None, None, blk, hd), lambda b, h, n: (b, h, jnp.maximum(n - 1, 0), 0))
    cur = pl.BlockSpec((None, None, blk, hd), lambda b, h, n: (b, h, n, 0))
    nxt = pl.BlockSpec((None, None, blk, hd), lambda b, h, n: (b, h, jnp.minimum(n + 1, nb - 1), 0))
    cspec = pl.BlockSpec((None, None, C, hd), lambda b, h, n: (b, h, 0, 0))
    sspec = pl.BlockSpec((None, grp * blk, 1), lambda b, h, n: (h, 0, 0))
    return pl.pallas_call(
        _swa_band_body,
        grid=(B, kvh, nb),
        in_specs=[qspec, prev, cur, nxt, cspec, prev, cur, nxt, cspec, sspec],
        out_specs=qspec,
        out_shape=jax.ShapeDtypeStruct(q.shape, F32),
        compiler_params=_params(("parallel", "parallel", "parallel")),
        name="swa_attention",
    )(q, k, k, k, kc, v, v, v, vc, sink_rows)


def swa_context_attention(q, kc, vc, sink_rows):
    B, kvh, grp, L, hd = q.shape
    C = kc.shape[2]
    blk = SWA_BLOCK
    qspec = pl.BlockSpec((None, None, grp, blk, hd), lambda b, h, n: (b, h, 0, n, 0))
    cspec = pl.BlockSpec((None, None, C, hd), lambda b, h, n: (b, h, 0, 0))
    sspec = pl.BlockSpec((None, grp * blk, 1), lambda b, h, n: (h, 0, 0))
    return pl.pallas_call(
        _swa_ctx_body,
        grid=(B, kvh, L // blk),
        in_specs=[qspec, cspec, cspec, sspec],
        out_specs=qspec,
        out_shape=jax.ShapeDtypeStruct(q.shape, F32),
        compiler_params=_params(("parallel", "parallel", "parallel")),
        name="swa_context_attention",
    )(q, kc, vc, sink_rows)


HY_COL_GROUP = 8
HY_K2_GROUP = 8


def _hy_fwd_outer_body(x_ref, fc_ref, fsn_ref, o_ref):
    x = x_ref[...].astype(BF16)
    o_ref[0] = jnp.dot(fc_ref[...], x, preferred_element_type=F32).astype(o_ref.dtype)
    o_ref[1] = jnp.dot(fsn_ref[...], x, preferred_element_type=F32).astype(o_ref.dtype)


def _hy_spectral_body(a_ref, m_ref, mi_ref, h_ref, o_ref):
    n1 = a_ref.shape[2]
    for j in range(a_ref.shape[1]):
        b = jnp.concatenate([a_ref[0, j], a_ref[1, j]], axis=0)
        x = jnp.dot(m_ref[j], b, preferred_element_type=F32)
        xr, xi = x[:n1], x[n1:]
        hr, hi = h_ref[0, j], h_ref[1, j]
        y = jnp.concatenate([xr * hr - xi * hi, xr * hi + xi * hr], axis=0).astype(BF16)
        c = jnp.dot(mi_ref[j], y, preferred_element_type=F32)
        o_ref[0, j] = c[:n1].astype(o_ref.dtype)
        o_ref[1, j] = c[n1:].astype(o_ref.dtype)


def _hy_inv_outer_body(c_ref, gc_ref, gsn_ref, z_ref, gate_ref, skip_ref, o_ref):
    y = (jnp.dot(gc_ref[...], c_ref[0], preferred_element_type=F32)
         + jnp.dot(gsn_ref[...], c_ref[1], preferred_element_type=F32))
    o_ref[...] = gate_ref[...] * (y + skip_ref[...] * z_ref[...])


def _hy_tables(n):
    n2 = 1 << (n.bit_length() // 2)
    n1 = n // n2
    two_pi = 2.0 * math.pi
    i2 = jnp.arange(n2, dtype=jnp.int32)
    ang2 = ((i2[:, None] * i2[None, :]) % n2).astype(F32) * (two_pi / n2)
    c2, s2 = jnp.cos(ang2), jnp.sin(ang2)
    half = n2 // 2
    fwd_c, fwd_sn = c2[:, :half].astype(BF16), (-s2[:, :half]).astype(BF16)
    inv_c, inv_sn = c2[:half, :].astype(BF16), (-s2[:half, :]).astype(BF16)
    i1 = jnp.arange(n1, dtype=jnp.int32)
    kk = n2 * i1[None, :, None] + i2[:, None, None]
    th = ((kk * i1[None, None, :]) % n).astype(F32) * (two_pi / n)
    ct, st = jnp.cos(th), jnp.sin(th)
    m_big = jnp.concatenate([jnp.concatenate([ct, st], axis=2), jnp.concatenate([-st, ct], axis=2)], axis=1)
    ctt, stt = jnp.swapaxes(ct, 1, 2), jnp.swapaxes(st, 1, 2)
    mi_big = jnp.concatenate([jnp.concatenate([ctt, -stt], axis=2), jnp.concatenate([stt, ctt], axis=2)], axis=1)
    return n1, n2, fwd_c, fwd_sn, inv_c, inv_sn, m_big.astype(BF16), mi_big.astype(BF16)


def _hy_spectrum_planes(spec, n, n1, n2):
    L = n // 2
    full = jnp.concatenate([spec, jnp.conj(spec[1:L][::-1])], axis=0) * (1.0 / n)
    o, c = full.shape[1:]
    full = jnp.transpose(full.reshape(n1, n2, o, c), (2, 1, 0, 3))
    return jnp.stack([jnp.real(full), jnp.imag(full)], axis=1).astype(F32)


def hyena_long_conv(z, gate, skip, h_planes, tables):
    B, L, C = z.shape
    n1, n2, fwd_c, fwd_sn, inv_c, inv_sn, m_big, mi_big = tables
    half = n2 // 2
    g = min(HY_COL_GROUP, n1)
    kb = min(HY_K2_GROUP, n2)
    gc = g * C
    z2 = z.reshape(B, half, n1 * C)
    gate2 = gate.reshape(B, half, n1 * C)
    plane = jax.ShapeDtypeStruct((B, 2, n2, n1 * C), BF16)
    full_mat = lambda shape: pl.BlockSpec(shape, lambda b, j: (0,) * len(shape))
    a = pl.pallas_call(
        _hy_fwd_outer_body,
        grid=(B, n1 // g),
        in_specs=[pl.BlockSpec((None, half, gc), lambda b, j: (b, 0, j)), full_mat((n2, half)), full_mat((n2, half))],
        out_specs=pl.BlockSpec((None, 2, n2, gc), lambda b, j: (b, 0, 0, j)),
        out_shape=plane,
        compiler_params=_params(("parallel", "parallel")),
        name="hy_fwd_outer",
    )(z2, fwd_c, fwd_sn)
    a5 = a.reshape(B, 2, n2, n1, C)
    cc = pl.pallas_call(
        _hy_spectral_body,
        grid=(B, n2 // kb),
        in_specs=[
            pl.BlockSpec((None, 2, kb, n1, C), lambda b, j: (b, 0, j, 0, 0)),
            pl.BlockSpec((kb, 2 * n1, 2 * n1), lambda b, j: (j, 0, 0)),
            pl.BlockSpec((kb, 2 * n1, 2 * n1), lambda b, j: (j, 0, 0)),
            pl.BlockSpec((2, kb, n1, C), lambda b, j: (0, j, 0, 0)),
        ],
        out_specs=pl.BlockSpec((None, 2, kb, n1, C), lambda b, j: (b, 0, j, 0, 0)),
        out_shape=jax.ShapeDtypeStruct((B, 2, n2, n1, C), BF16),
        compiler_params=_params(("parallel", "parallel")),
        name="hy_spectral",
    )(a5, m_big, mi_big, h_planes)
    c2 = cc.reshape(B, 2, n2, n1 * C)
    skip_t = jnp.tile(skip.astype(F32).reshape(1, C), (1, g))
    out = pl.pallas_call(
        _hy_inv_outer_body,
        grid=(B, n1 // g),
        in_specs=[
            pl.BlockSpec((None, 2, n2, gc), lambda b, j: (b, 0, 0, j)),
            full_mat((half, n2)), full_mat((half, n2)),
            pl.BlockSpec((None, half, gc), lambda b, j: (b, 0, j)),
            pl.BlockSpec((None, half, gc), lambda b, j: (b, 0, j)),
            full_mat((1, gc)),
        ],
        out_specs=pl.BlockSpec((None, half, gc), lambda b, j: (b, 0, j)),
        out_shape=jax.ShapeDtypeStruct((B, half, n1 * C), F32),
        compiler_params=_params(("parallel", "parallel")),
        name="hy_inv_outer",
    )(c2, inv_c, inv_sn, z2, gate2, skip_t)
    return out.reshape(B, L, C)


def _split2(x):
    hi = x.astype(BF16)
    return hi, (x - hi.astype(F32)).astype(BF16)


def _hy_filt_outer_body(x_ref, fc_ref, fsn_ref, o_ref):
    hi, lo = _split2(x_ref[...])
    for plane, f_ref in enumerate((fc_ref, fsn_ref)):
        o_ref[plane] = (jnp.dot(f_ref[...], hi, preferred_element_type=F32)
                        + jnp.dot(f_ref[...], lo, preferred_element_type=F32))


def _hy_filt_inner_body(a_ref, m_ref, o_ref):
    n1 = a_ref.shape[2]
    n_order, ch = o_ref.shape[0], o_ref.shape[-1]
    for j in range(a_ref.shape[1]):
        hi, lo = _split2(jnp.concatenate([a_ref[0, j], a_ref[1, j]], axis=0))
        x = jnp.dot(m_ref[j], hi, preferred_element_type=F32) + jnp.dot(m_ref[j], lo, preferred_element_type=F32)
        xr, xi = x[:n1], x[n1:]
        for o in range(n_order):
            f, b = slice(2 * o * ch, (2 * o + 1) * ch), slice((2 * o + 1) * ch, (2 * o + 2) * ch)
            o_ref[o, 0, j] = xr[:, f] + xr[:, b]
            o_ref[o, 1, j] = xi[:, f] - xi[:, b]


def hyena_filter_planes(hs, tables, n_order, ch):
    L, hf = hs.shape
    n1, n2, fwd_c, fwd_sn, _, _, m_big, _ = tables
    half = n2 // 2
    g = min(2, n1)
    kb = min(2, n2)
    full_mat = lambda shape: pl.BlockSpec(shape, lambda j: (0,) * len(shape))
    a = pl.pallas_call(
        _hy_filt_outer_body,
        grid=(n1 // g,),
        in_specs=[pl.BlockSpec((half, g * hf), lambda j: (0, j)), full_mat((n2, half)), full_mat((n2, half))],
        out_specs=pl.BlockSpec((2, n2, g * hf), lambda j: (0, 0, j)),
        out_shape=jax.ShapeDtypeStruct((2, n2, n1 * hf), F32),
        compiler_params=_params(("parallel",)),
        name="hy_filt_outer",
    )(hs.reshape(half, n1 * hf), fwd_c, fwd_sn)
    return pl.pallas_call(
        _hy_filt_inner_body,
        grid=(n2 // kb,),
        in_specs=[pl.BlockSpec((2, kb, n1, hf), lambda j: (0, j, 0, 0)),
                  pl.BlockSpec((kb, 2 * n1, 2 * n1), lambda j: (j, 0, 0))],
        out_specs=pl.BlockSpec((n_order, 2, kb, n1, ch), lambda j: (0, 0, j, 0, 0)),
        out_shape=jax.ShapeDtypeStruct((n_order, 2, n2, n1, ch), F32),
        compiler_params=_params(("parallel",)),
        name="hy_filt_inner",
    )(a.reshape(2, n2, n1, hf), m_big)


def _hyena_filter_taps(L, ch, f_w1, f_b1, f_w2, f_b2, f_w3, f_b3, f_freq, f_decay):
    t = jnp.arange(L, dtype=F32) / max(L - 1, 1)
    ang = 2 * math.pi * t[:, None] * jnp.arange(1, HY_POS_BANDS + 1, dtype=F32)
    feat = jnp.concatenate([t[:, None], jnp.sin(ang), jnp.cos(ang)], axis=-1)
    hp = lax.Precision.HIGHEST
    h = jnp.sin(f_freq * (jnp.dot(feat, f_w1, precision=hp) + f_b1))
    h = jnp.sin(f_freq * (jnp.dot(h, f_w2, precision=hp) + f_b2))
    h = jnp.dot(h, f_w3, precision=hp) + f_b3
    h = h * jnp.exp(-f_decay * t[:, None])
    h = h.reshape(L, HY_ORDER, 2, ch)
    keep = jnp.ones((L, 1, 2, 1), F32).at[0, 0, 1, 0].set(0.0)
    h = h * keep
    l1 = jnp.sum(jnp.abs(h), axis=(0, 2), keepdims=True)
    return (h / (l1 * (2.0 * L))).reshape(L, HY_ORDER * 2 * ch)


def _hyena(pa, conv_w, conv_b, f_w1, f_b1, f_w2, f_b2, f_w3, f_b3, f_freq, f_decay, skip):
    B, L, _ = pa.shape
    ch = skip.shape[-1]
    n = 2 * L
    u = _short_conv3(pa, conv_w, conv_b).astype(F32)
    v, x1, x2 = jnp.split(u, 3, axis=-1)
    tables = _hy_tables(n)
    taps = _hyena_filter_taps(L, ch, f_w1, f_b1, f_w2, f_b2, f_w3, f_b3, f_freq, f_decay)
    h = hyena_filter_planes(taps, tables, HY_ORDER, ch)
    z = hyena_long_conv(v, x1, skip[0], h[0], tables)
    return hyena_long_conv(z, x2, skip[1], h[1], tables)


def _rms_norm(x, g):
    xf = x.astype(F32)
    y = xf * lax.rsqrt(jnp.mean(xf * xf, axis=-1, keepdims=True) + NORM_EPS)
    return (y * g.astype(F32)).astype(x.dtype)


def _rope_tables(rows):
    n_freq = HEAD_DIM // 4
    inv = ROPE_BASE ** (-jnp.arange(n_freq, dtype=F32) / n_freq)
    row = jnp.repeat(jnp.arange(rows, dtype=F32), GRID_W)
    col = jnp.tile(jnp.arange(GRID_W, dtype=F32), rows)
    ang = jnp.concatenate([row[:, None] * inv, col[:, None] * inv], axis=-1)
    return jnp.cos(ang), jnp.sin(ang)


def _apply_rope(x, cos, sin):
    shp = (x.shape[1],) + (1,) * (x.ndim - 3) + (HEAD_DIM // 2,)
    c, s = cos.reshape(shp), sin.reshape(shp)
    x1, x2 = jnp.split(x.astype(F32), 2, axis=-1)
    return jnp.concatenate([x1 * c - x2 * s, x2 * c + x1 * s], axis=-1).astype(x.dtype)


def _short_conv3(x, w, b):
    xp = jnp.pad(x, ((0, 0), (1, 1), (0, 0)))
    return xp[:, :-2] * w[0] + xp[:, 1:-1] * w[1] + xp[:, 2:] * w[2] + b


def _hyena_filter_spectra(L, ch, f_w1, f_b1, f_w2, f_b2, f_w3, f_b3, f_freq, f_decay):
    t = jnp.arange(L, dtype=F32) / max(L - 1, 1)
    ang = 2 * math.pi * t[:, None] * jnp.arange(1, HY_POS_BANDS + 1, dtype=F32)
    feat = jnp.concatenate([t[:, None], jnp.sin(ang), jnp.cos(ang)], axis=-1)
    hp = lax.Precision.HIGHEST
    h = jnp.sin(f_freq * (jnp.dot(feat, f_w1, precision=hp) + f_b1))
    h = jnp.sin(f_freq * (jnp.dot(h, f_w2, precision=hp) + f_b2))
    h = jnp.dot(h, f_w3, precision=hp) + f_b3
    h = h * jnp.exp(-f_decay * t[:, None])
    h = h.reshape(L, HY_ORDER, 2, ch)
    h_fwd, h_bwd = h[:, :, 0], h[:, :, 1]
    l1 = jnp.sum(jnp.abs(h_fwd), axis=0) + jnp.sum(jnp.abs(h_bwd[1:]), axis=0)
    kern = jnp.concatenate([h_fwd, jnp.zeros((1, HY_ORDER, ch), F32), h_bwd[1:][::-1]], axis=0) / l1
    return jnp.fft.rfft(kern, axis=0)


def _fft_long_conv(z, spec, skip):
    L = z.shape[1]
    zf = jnp.fft.rfft(z, n=2 * L, axis=1)
    y = jnp.fft.irfft(zf * spec[None], n=2 * L, axis=1)[:, :L]
    return y + skip * z


def _hyena_branch(pa, conv_w, conv_b, spec, skip):
    u = _short_conv3(pa, conv_w, conv_b).astype(F32)
    v, x1, x2 = jnp.split(u, 3, axis=-1)
    z = x1 * _fft_long_conv(v, spec[:, 0], skip[0])
    return x2 * _fft_long_conv(z, spec[:, 1], skip[1])


def _swa_project(pb, q_g, k_g, n_heads):
    B, L = pb.shape[:2]
    q, k, v = jnp.split(pb, [n_heads * HEAD_DIM, (n_heads + SWA_KV_HEADS) * HEAD_DIM], axis=-1)
    q = _rms_norm(q.reshape(B, L, n_heads, HEAD_DIM), q_g)
    k = _rms_norm(k.reshape(B, L, SWA_KV_HEADS, HEAD_DIM), k_g)
    return q, k, v.reshape(B, L, SWA_KV_HEADS, HEAD_DIM)


def _sink_softmax(s, sink):
    m = jnp.maximum(jnp.max(s, axis=-1, keepdims=True), sink)
    p = jnp.exp(s - m)
    return p / (jnp.sum(p, axis=-1, keepdims=True) + jnp.exp(sink - m))


def _swa_context(qc, kc, vc, sink):
    B, C, n_heads = qc.shape[:3]
    grp = n_heads // SWA_KV_HEADS
    qg = qc.reshape(B, C, SWA_KV_HEADS, grp, HEAD_DIM)
    s = jnp.einsum('bqhgd,bkhd->bhgqk', qg, kc).astype(F32) * HEAD_DIM ** -0.5
    p = _sink_softmax(s, sink.astype(F32).reshape(1, SWA_KV_HEADS, grp, 1, 1))
    o = jnp.einsum('bhgqk,bkhd->bqhgd', p.astype(vc.dtype), vc)
    return o.reshape(B, C, n_heads * HEAD_DIM)


def _swa_latent(q, k, v, kc, vc, sink):
    B, L, n_heads = q.shape[:3]
    grp = n_heads // SWA_KV_HEADS
    nb = L // SWA_BLOCK
    qb = q.reshape(B, nb, SWA_BLOCK, SWA_KV_HEADS, grp, HEAD_DIM)

    def band(t):
        tb = t.reshape(B, nb, SWA_BLOCK, SWA_KV_HEADS, HEAD_DIM)
        tp = jnp.pad(tb, ((0, 0), (1, 1), (0, 0), (0, 0), (0, 0)))
        return jnp.concatenate([tp[:, :-2], tp[:, 1:-1], tp[:, 2:]], axis=2)

    kb, vb = band(k), band(v)
    scale = HEAD_DIM ** -0.5
    s_loc = jnp.einsum('bnqhgd,bnkhd->bnhgqk', qb, kb).astype(F32) * scale
    s_ctx = jnp.einsum('bnqhgd,bchd->bnhgqc', qb, kc).astype(F32) * scale
    q_rel = jnp.arange(SWA_BLOCK)[:, None] + SWA_BLOCK
    k_rel = jnp.arange(3 * SWA_BLOCK)[None, :]
    k_abs = (jnp.arange(nb)[:, None, None] - 1) * SWA_BLOCK + k_rel[None]
    valid = (jnp.abs(q_rel - k_rel) <= SWA_WINDOW)[None] & (k_abs >= 0) & (k_abs < L)
    s_loc = jnp.where(valid[None, :, None, None], s_loc, NEG_INF)
    s = jnp.concatenate([s_loc, s_ctx], axis=-1)
    p = _sink_softmax(s, sink.astype(F32).reshape(1, 1, SWA_KV_HEADS, grp, 1, 1)).astype(v.dtype)
    o = (jnp.einsum('bnhgqk,bnkhd->bnqhgd', p[..., :3 * SWA_BLOCK], vb)
         + jnp.einsum('bnhgqc,bchd->bnqhgd', p[..., 3 * SWA_BLOCK:], vc))
    return o.reshape(B, L, n_heads * HEAD_DIM)


def _token_shift(x, reverse):
    if reverse:
        return jnp.pad(x, ((0, 0), (0, 1), (0, 0)))[:, 1:]
    return jnp.pad(x, ((0, 0), (1, 0), (0, 0)))[:, :-1]


def _head_l2norm(x, n_heads):
    B, L, C = x.shape
    xh = x.reshape(B, L, n_heads, HEAD_DIM)
    xh = xh * lax.rsqrt(jnp.sum(xh * xh, axis=-1, keepdims=True) + 1e-12)
    return xh.reshape(B, L, C)


def _rwkv_prepare(feats, reverse, width, mu, w0, w2, a0, a2, g2, k_k, k_a):
    n_heads = width // HEAD_DIM
    xs = feats + mu * (_token_shift(feats, reverse) - feats)
    splits = (width, 2 * width, 3 * width, 3 * width + RW_DECAY_RANK, 3 * width + RW_DECAY_RANK + RW_A_RANK)
    r, k, v, wd, ad, gd = jnp.split(xs, splits, axis=-1)
    logw = -jnp.exp(-jax.nn.softplus(-(w0 + jnp.tanh(wd) @ w2)) - 0.5)
    a = jax.nn.sigmoid(a0 + ad @ a2)
    g = jax.nn.sigmoid(gd) @ g2
    kk = _head_l2norm(k * k_k, n_heads)
    k = k * (1 + (a - 1) * k_a)
    return r, logw, kk, a, k, v, g


def _wkv7(r, logw, kk, a, k, v, z0, reverse):
    p, g, qh, yl = rwkv_chunk_prep(r, logw, kk, a, k, v, reverse=reverse)
    y, z_fin = rwkv_chunk_scan(p, g, qh, yl, z0, reverse=reverse)
    return z_fin, y


def _rwkv_readout(y, r, k, v, g, r_k, ln_w, ln_b):
    B, L, C = y.shape
    n_heads = C // HEAD_DIM
    yh = y.reshape(B, L, n_heads, HEAD_DIM)
    mean = jnp.mean(yh, axis=-1, keepdims=True)
    var = jnp.mean(jnp.square(yh - mean), axis=-1, keepdims=True)
    yn = ((yh - mean) * lax.rsqrt(var + RW_GN_EPS)).reshape(B, L, C) * ln_w + ln_b
    bonus = jnp.sum((r * k * r_k).reshape(B, L, n_heads, HEAD_DIM), axis=-1, keepdims=True)
    bonus = (bonus * v.reshape(B, L, n_heads, HEAD_DIM)).reshape(B, L, C)
    return (yn + bonus) * g


def _rwkv_branch(f_ctx, f_lat, width, mu, w0, w2, a0, a2, g2, k_k, k_a, r_k, ln_w, ln_b, need_ctx):
    B = f_lat.shape[0]
    n_heads = width // HEAD_DIM
    y_lat = 0.0
    y_ctx = 0.0 if need_ctx else None
    for d in range(2):
        rev = d == 1
        dir_args = (width, mu[d], w0[d], w2[d], a0[d], a2[d], g2[d], k_k, k_a)
        r_c, w_c, kk_c, a_c, k_c, v_c, g_c = _rwkv_prepare(f_ctx, rev, *dir_args)
        z0 = jnp.zeros((B, n_heads, HEAD_DIM, HEAD_DIM), F32)
        z_ctx, o_c = _wkv7(r_c, w_c, kk_c, a_c, k_c, v_c, z0, rev)
        r_l, w_l, kk_l, a_l, k_l, v_l, g_l = _rwkv_prepare(f_lat, rev, *dir_args)
        _, o_l = _wkv7(r_l, w_l, kk_l, a_l, k_l, v_l, z_ctx, rev)
        y_lat = y_lat + _rwkv_readout(o_l, r_l, k_l, v_l, g_l, r_k, ln_w, ln_b)
        if need_ctx:
            y_ctx = y_ctx + _rwkv_readout(o_c, r_c, k_c, v_c, g_c, r_k, ln_w, ln_b)
    return y_lat, y_ctx


def _rwkv_branch2(f_ctx, f_lat, B, L, C, width, mu, w0, w2, a0, a2, g2, k_k, k_a, r_k, ln_w, ln_b, need_ctx):
    n_heads = width // HEAD_DIM
    feat = f_lat.shape[1]
    dr, ar = RW_DECAY_RANK, RW_A_RANK
    wt = jnp.zeros((2, feat - 3 * width, 3 * width), F32)
    wt = wt.at[:, :dr, :width].set(w2).at[:, dr:dr + ar, width:2 * width].set(a2).at[:, dr + ar:, 2 * width:].set(g2)
    row = lambda t: t.astype(F32).reshape(1, width)
    args = (mu.astype(F32).reshape(2, 1, feat), wt.astype(BF16), w0.astype(F32).reshape(2, 1, width),
            a0.astype(F32).reshape(2, 1, width), row(k_k), row(k_a))
    pre_c = rwkv_pre(f_ctx, *args, C, width)
    pre_l = rwkv_pre(f_lat, *args, L, width)
    read_l, read_c = [], []
    for d in range(2):
        rev = d == 1
        seq = lambda ts, n: [t.reshape(B, n, width) for t in ts[:6]]
        z0 = jnp.zeros((B, n_heads, HEAD_DIM, HEAD_DIM), F32)
        z_ctx, o_c = _wkv7(*seq(pre_c[d], C), z0, rev)
        _, o_l = _wkv7(*seq(pre_l[d], L), z_ctx, rev)
        pick = lambda o, ts, n: (o.reshape(B * n, width), ts[0], ts[4], ts[5], ts[6])
        read_l.append(pick(o_l, pre_l[d], L))
        read_c.append(pick(o_c, pre_c[d], C))
    tail = (row(r_k), row(ln_w), row(ln_b))
    y_lat = rwkv_readout(read_l[0], read_l[1], *tail)
    y_ctx = rwkv_readout(read_c[0], read_c[1], *tail) if need_ctx else None
    return y_lat, y_ctx


def _diff_project(pd, q_g, k_g):
    B, L, w3 = pd.shape
    n_heads = w3 // (3 * 2 * HEAD_DIM)
    q, k, v = jnp.split(pd, 3, axis=-1)
    q = _rms_norm(q.reshape(B, L, n_heads, 2, HEAD_DIM), q_g)
    k = _rms_norm(k.reshape(B, L, n_heads, 2, HEAD_DIM), k_g)
    return q, k, v.reshape(B, L, n_heads, 2 * HEAD_DIM)


def _diff_maps(q, k_all, v_all, lam):
    s = jnp.einsum('bqhid,bkhid->bhiqk', q, k_all).astype(F32) * HEAD_DIM ** -0.5
    p = jax.nn.softmax(s, axis=-1)
    a = p[:, :, 0] - lam * p[:, :, 1]
    return jnp.einsum('bhqk,bkhe->bqhe', a.astype(v_all.dtype), v_all)


def _diff_latent(q, k, v, kc, vc, lam):
    B, L, n_heads = q.shape[:3]
    nb = L // DF_BLOCK
    k_all = jnp.concatenate([kc, k], axis=1)
    v_all = jnp.concatenate([vc, v], axis=1)
    qb = jnp.moveaxis(q.reshape(B, nb, DF_BLOCK, n_heads, 2, HEAD_DIM), 1, 0)
    o = lax.map(lambda q_blk: _diff_maps(q_blk, k_all, v_all, lam), qb)
    return jnp.moveaxis(o, 0, 1).reshape(B, L, n_heads, 2 * HEAD_DIM)


def _diff_readout(o, sub_g, lam_init):
    B, L = o.shape[:2]
    return (_rms_norm(o, sub_g) * (1 - lam_init)).reshape(B, L, -1)


def _swa_branch(pb_l, pb_c, cos, sin, q_g, k_g, sink, need_ctx):
    B, L = pb_l.shape[:2]
    C = pb_c.shape[1]
    n_heads = pb_l.shape[-1] // HEAD_DIM - 2 * SWA_KV_HEADS
    grp = n_heads // SWA_KV_HEADS
    scale = HEAD_DIM ** -0.5
    q_l, k_l, v_l = _swa_project(pb_l, q_g, k_g, n_heads)
    q_l, k_l = _apply_rope(q_l, cos, sin), _apply_rope(k_l, cos, sin)
    q_c, k_c, v_c = _swa_project(pb_c, q_g, k_g, n_heads)

    def q_layout(q, n):
        return jnp.transpose((q * scale).astype(BF16).reshape(B, n, SWA_KV_HEADS, grp, HEAD_DIM), (0, 2, 3, 1, 4))

    def kv_layout(t):
        return jnp.transpose(t.astype(BF16), (0, 2, 1, 3))

    def o_layout(o, n):
        return jnp.transpose(o, (0, 3, 1, 2, 4)).reshape(B, n, n_heads * HEAD_DIM)

    sink_rows = jnp.repeat(sink.astype(F32).reshape(SWA_KV_HEADS, grp), SWA_BLOCK, axis=1)[..., None]
    kc, vc = kv_layout(k_c), kv_layout(v_c)
    y_l = o_layout(swa_attention(q_layout(q_l, L), kv_layout(k_l), kv_layout(v_l), kc, vc, sink_rows), L)
    y_c = None
    if need_ctx:
        y_c = o_layout(swa_context_attention(q_layout(q_c, C), kc, vc, sink_rows), C)
    return y_l, y_c


def _diff_branch(pd_l, pd_c, cos, sin, q_g, k_g, sub_g, lam, lam_init, need_ctx):
    B, L = pd_l.shape[:2]
    C = pd_c.shape[1]
    W = pd_l.shape[-1] // 3
    scale = HEAD_DIM ** -0.5 * math.log2(math.e)
    dq_l, dk_l, dv_l = _diff_project(pd_l, q_g, k_g)
    dq_l, dk_l = _apply_rope(dq_l, cos, sin), _apply_rope(dk_l, cos, sin)
    dq_c, dk_c, dv_c = _diff_project(pd_c, q_g, k_g)
    flat = lambda t, n: t.reshape(B, n, W)
    q_l = (flat(dq_l, L) * scale).astype(BF16)
    k_c, v_c = flat(dk_c, C).astype(BF16), flat(dv_c, C).astype(BF16)
    kt_c = jnp.transpose(k_c, (0, 2, 1))
    kt_all = jnp.concatenate([kt_c, jnp.transpose(flat(dk_l, L).astype(BF16), (0, 2, 1))], axis=2)
    v_all = jnp.concatenate([v_c, flat(dv_l, L).astype(BF16)], axis=1)
    sub = sub_g.astype(F32).reshape(1, 2 * HEAD_DIM)
    lam2 = lam.astype(F32).reshape(1, 1)
    y_l = diff_attention(q_l, kt_all, v_all, sub, lam2, 1.0 - lam_init)
    y_c = None
    if need_ctx:
        q_c = (flat(dq_c, C) * scale).astype(BF16)
        y_c = diff_attention(q_c, kt_c, v_c, sub, lam2, 1.0 - lam_init)
    return y_l, y_c


def _rope_lane_tables(rows, wb):
    cos, sin = _rope_tables(rows)
    reps = wb // (HEAD_DIM // 2)
    sign = jnp.where(jnp.arange(wb) % HEAD_DIM < HEAD_DIM // 2, -1.0, 1.0).astype(F32)
    return jnp.tile(cos, (1, reps)), jnp.tile(sin, (1, reps)) * sign


def _lane_params(parts, wb):
    cols = [jnp.tile(jnp.asarray(val, F32).reshape(-1), n // jnp.asarray(val).size) for val, n in parts]
    return jnp.concatenate(cols).reshape(-1, 1, wb)


def _swa_branch2(p_l, p_c, col0, B, L, C, tables, q_g, k_g, sink, need_ctx):
    n_heads = sink.shape[0]
    grp = n_heads // SWA_KV_HEADS
    qw, kw = n_heads * HEAD_DIM, SWA_KV_HEADS * HEAD_DIM
    wb = 2 * kw
    gain = _lane_params([(q_g, qw), (k_g, kw), (1.0, kw)], wb)
    mask = _lane_params([(1.0, qw), (1.0, kw), (0.0, kw)], wb)
    scale = _lane_params([(HEAD_DIM ** -0.5, qw), (1.0, kw), (1.0, kw)], wb)
    o_l = head_prep(p_l, col0, wb, gain, mask, scale, tables, L).reshape(B, L, qw + 2 * kw)
    o_c = head_prep(p_c, col0, wb, gain, mask, scale, None, B * C).reshape(B, C, qw + 2 * kw)

    def q_layout(o, n):
        return jnp.transpose(o[:, :, :qw].reshape(B, n, SWA_KV_HEADS, grp, HEAD_DIM), (0, 2, 3, 1, 4))

    def kv_layout(t, n):
        return jnp.transpose(t.reshape(B, n, SWA_KV_HEADS, HEAD_DIM), (0, 2, 1, 3))

    def o_layout(o, n):
        return jnp.transpose(o, (0, 3, 1, 2, 4)).reshape(B * n, qw)

    sink_rows = jnp.repeat(sink.astype(F32).reshape(SWA_KV_HEADS, grp), SWA_BLOCK, axis=1)[..., None]
    kc, vc = kv_layout(o_c[:, :, qw:qw + kw], C), kv_layout(o_c[:, :, qw + kw:], C)
    y_l = o_layout(swa_attention(q_layout(o_l, L), kv_layout(o_l[:, :, qw:qw + kw], L),
                                 kv_layout(o_l[:, :, qw + kw:], L), kc, vc, sink_rows), L)
    y_c = o_layout(swa_context_attention(q_layout(o_c, C), kc, vc, sink_rows), C) if need_ctx else None
    return y_l, y_c


def _diff_branch2(p_l, p_c, col0, B, L, C, tables, q_g, k_g, sub_g, lam, lam_init, need_ctx):
    W = tables[0].shape[1]
    hw = 2 * HEAD_DIM
    q_scale = HEAD_DIM ** -0.5 * math.log2(math.e)
    gain = _lane_params([(q_g, W), (k_g, W), (1.0, W)], W)
    mask = _lane_params([(1.0, W), (1.0, W), (0.0, W)], W)
    scale = _lane_params([(q_scale, W), (1.0, W), (1.0, W)], W)
    o_l = head_prep(p_l, col0, W, gain, mask, scale, tables, L).reshape(B, L, 3 * W)
    o_c = head_prep(p_c, col0, W, gain, mask, scale, None, B * C).reshape(B, C, 3 * W)
    kv_all = jnp.concatenate([o_c, o_l], axis=1)
    kt_all = jnp.transpose(kv_all[:, :, W:2 * W], (0, 2, 1))
    sub = sub_g.astype(F32).reshape(1, hw)
    lam2 = lam.astype(F32).reshape(1, 1)
    v_blk0 = 2 * W // hw
    y_l = diff_attention(o_l, 0, kt_all, kv_all, v_blk0, sub, lam2, 1.0 - lam_init).reshape(B * L, W)
    y_c = None
    if need_ctx:
        y_c = diff_attention(o_c, 0, kt_all[:, :, :C], o_c, v_blk0, sub, lam2, 1.0 - lam_init).reshape(B * C, W)
    return y_l, y_c


def _moe_route(logits_g, logits_e, n_tok):
    g_idx = jnp.argmax(logits_g, axis=-1)
    g_prob = jnp.take_along_axis(jax.nn.softmax(logits_g, axis=-1), g_idx[:, None], axis=-1)[:, 0]
    e_logits = logits_e.reshape(n_tok, MOE_GROUPS, MOE_PER_GROUP)
    e_logits = jnp.take_along_axis(e_logits, g_idx[:, None, None], axis=1)[:, 0]
    top_p, top_e = lax.top_k(jax.nn.softmax(e_logits, axis=-1), MOE_TOP_K)
    weights = g_prob[:, None] * top_p / jnp.sum(top_p, axis=-1, keepdims=True)
    flat_e = (g_idx[:, None] * MOE_PER_GROUP + top_e).reshape(-1).astype(jnp.int32)
    flat_tok = jnp.repeat(jnp.arange(n_tok, dtype=jnp.int32), MOE_TOP_K)
    n_assign = n_tok * MOE_TOP_K
    order = jnp.argsort(flat_e)
    se = flat_e[order]
    counts = jnp.bincount(flat_e, length=MOE_EXPERTS)
    padded = (counts + MOE_BLOCK - 1) // MOE_BLOCK * MOE_BLOCK
    pad_end = jnp.cumsum(padded)
    pad_start = pad_end - padded
    start = jnp.cumsum(counts) - counts
    dest = (pad_start[se] + jnp.arange(n_assign, dtype=jnp.int32) - start[se]).astype(jnp.int32)
    n_blocks = -(-n_assign // MOE_BLOCK) + MOE_EXPERTS
    P = n_blocks * MOE_BLOCK
    tok_buf = jnp.full((P,), n_tok, jnp.int32).at[dest].set(flat_tok[order])
    blk_e = jnp.minimum(jnp.searchsorted(pad_end, jnp.arange(n_blocks) * MOE_BLOCK, side='right'),
                        MOE_EXPERTS - 1).astype(jnp.int32)
    n_used = (pad_end[-1] // MOE_BLOCK).astype(jnp.int32).reshape(1)
    slot = jnp.zeros((n_assign,), jnp.int32).at[order].set(dest).reshape(n_tok, MOE_TOP_K)
    return tok_buf, blk_e, n_used, slot, weights


def _hier_moe(tokens_bf16, tokens_f32, rg_w, rg_b, re_w, re_b, w1, w3, w2):
    n_tok, d = tokens_bf16.shape
    hp = lax.Precision.HIGHEST
    logits_g = jnp.dot(tokens_f32, rg_w, precision=hp) + rg_b
    logits_e = jnp.dot(tokens_f32, re_w, precision=hp) + re_b
    tok_buf, blk_e, n_used, slot, weights = _moe_route(logits_g, logits_e, n_tok)
    x_pad = jnp.concatenate([tokens_bf16, jnp.zeros((1, d), BF16)], axis=0)
    xb = x_pad[tok_buf]
    yb = moe_experts(xb, blk_e, n_used, w1, w3, w2)
    wts = weights.astype(F32)
    return (yb[slot[:, 0]].astype(F32) * wts[:, 0:1] + yb[slot[:, 1]].astype(F32) * wts[:, 1:2])


def _modulate(x, g, shift, scale):
    return _rms_norm(x, g) * (1 + scale) + shift


def kernel(x, c, ctx, c_ctx, mod_w, mod_b, norm1_g, norm2_g, w_in, hy_conv_w, hy_conv_b, hy_f_w1, hy_f_b1, hy_f_w2, hy_f_b2, hy_f_w3, hy_f_b3, hy_f_freq, hy_f_decay, hy_skip, swa_q_g, swa_k_g, swa_sink, rw_mu, rw_w0, rw_w2, rw_a0, rw_a2, rw_g2, rw_k_k, rw_k_a, rw_r_k, rw_ln_w, rw_ln_b, df_q_g, df_k_g, df_lq1, df_lk1, df_lq2, df_lk2, df_sub_g, w_gate, b_gate, w_branch, w_out, moe_rg_w, moe_rg_b, moe_re_w, moe_re_b, moe_w1, moe_w3, moe_w2):
    B, L, D = x.shape
    C = ctx.shape[1]
    depth = mod_w.shape[0]
    bw = D // N_BRANCH
    a_cols = 3 * bw
    swa_heads = bw // HEAD_DIM
    b_cols = (swa_heads + 2 * SWA_KV_HEADS) * HEAD_DIM
    c_cols = 3 * bw + RW_DECAY_RANK + RW_A_RANK + RW_G_RANK
    in_splits = (a_cols, a_cols + b_cols, a_cols + b_cols + c_cols)
    rows = L // GRID_W
    swa_tables = _rope_lane_tables(rows, 2 * SWA_KV_HEADS * HEAD_DIM)
    diff_tables = _rope_lane_tables(rows, bw)
    hp = lax.Precision.HIGHEST

    x_lat = x.reshape(B * L, D)
    x_ctx = ctx.reshape(B * C, D)
    for l in range(depth):
        need_ctx = l < depth - 1
        mod_lat = jnp.dot(jax.nn.silu(c), mod_w[l], precision=hp) + mod_b[l]
        mod_ctx = jnp.dot(jax.nn.silu(c_ctx)[None], mod_w[l], precision=hp) + mod_b[l]
        sh1, sc1, g1, sh2, sc2, g2 = [t[:, None, :] for t in jnp.split(mod_lat, 6, axis=-1)]
        csh1, csc1, cg1, csh2, csc2, cg2 = [t[:, None, :] for t in jnp.split(mod_ctx, 6, axis=-1)]

        w_in_b = w_in[l].astype(BF16)
        p2_lat, h_lat = norm_proj(x_lat, norm1_g[l][None], sc1, sh1, w_in_b, rows_per_mod=L)
        p2_ctx, h_ctx = norm_proj(x_ctx, norm1_g[l][None], csc1, csh1, w_in_b, rows_per_mod=B * C)
        p_lat = p2_lat.reshape(B, L, -1)
        p_ctx = p2_ctx.reshape(B, C, -1)
        pa_l, pc_l = p_lat[..., :in_splits[0]], p_lat[..., in_splits[1]:in_splits[2]]
        pa_c, pc_c = p_ctx[..., :in_splits[0]], p_ctx[..., in_splits[1]:in_splits[2]]

        hy_args = (hy_conv_w[l], hy_conv_b[l], hy_f_w1[l], hy_f_b1[l], hy_f_w2[l], hy_f_b2[l], hy_f_w3[l],
                   hy_f_b3[l], hy_f_freq[l], hy_f_decay[l], hy_skip[l])
        ya_l = _hyena(pa_l, *hy_args)

        yb_l, yb_c = _swa_branch2(p2_lat, p2_ctx, in_splits[0], B, L, C, swa_tables, swa_q_g[l], swa_k_g[l],
                                  swa_sink[l], need_ctx)

        yc_l, yc_c = _rwkv_branch2(p2_ctx[:, in_splits[1]:in_splits[2]], p2_lat[:, in_splits[1]:in_splits[2]],
                                   B, L, C, bw, rw_mu[l], rw_w0[l], rw_w2[l], rw_a0[l], rw_a2[l],
                                   rw_g2[l], rw_k_k[l], rw_k_a[l], rw_r_k[l], rw_ln_w[l], rw_ln_b[l], need_ctx)

        lam_init = 0.8 - 0.6 * math.exp(-0.3 * l)
        lam = (jnp.exp(jnp.sum(df_lq1[l] * df_lk1[l])) - jnp.exp(jnp.sum(df_lq2[l] * df_lk2[l])) + lam_init)
        yd_l, yd_c = _diff_branch2(p2_lat, p2_ctx, in_splits[2], B, L, C, diff_tables, df_q_g[l], df_k_g[l],
                                   df_sub_g[l], lam, lam_init, need_ctx)

        wg_b = w_gate[l].astype(BF16)
        bg = b_gate[l][:, None, :]
        wb_b = w_branch[l].astype(BF16)
        wo_b = w_out[l].astype(BF16)
        ys_l = [t.reshape(B * L, bw) for t in (ya_l, yb_l, yc_l, yd_l)]
        acc_l = merge_gated(h_lat, ys_l, wg_b, bg, wb_b)
        x_lat = resid_proj(x_lat, acc_l, wo_b, g1, rows_per_mod=L)
        if need_ctx:
            ya_c = _hyena(pa_c, *hy_args)
            ys_c = [t.reshape(B * C, bw) for t in (ya_c, yb_c, yc_c, yd_c)]
            acc_c = merge_gated(h_ctx, ys_c, wg_b, bg, wb_b)
            x_ctx = resid_proj(x_ctx, acc_c, wo_b, cg1, rows_per_mod=B * C)

        w1_b, w3_b, w2_b = moe_w1[l].astype(BF16), moe_w3[l].astype(BF16), moe_w2[l].astype(BF16)
        moe_args = (moe_rg_w[l], moe_rg_b[l], moe_re_w[l], moe_re_b[l], w1_b, w3_b, w2_b)
        hm_lat = _modulate(x_lat.reshape(B, L, D), norm2_g[l], sh2, sc2).reshape(B * L, D)
        if need_ctx:
            hm_ctx = _modulate(x_ctx.reshape(B, C, D), norm2_g[l], csh2, csc2).reshape(B * C, D)
            tokens = jnp.concatenate([hm_ctx, hm_lat], axis=0)
            out = _hier_moe(tokens.astype(BF16), tokens, *moe_args)
            x_ctx = x_ctx + (cg2 * out[:B * C].reshape(1, B * C, D)).reshape(B * C, D)
            x_lat = x_lat + (g2 * out[B * C:].reshape(B, L, D)).reshape(B * L, D)
        else:
            out = _hier_moe(hm_lat.astype(BF16), hm_lat, *moe_args)
            x_lat = x_lat + (g2 * out.reshape(B, L, D)).reshape(B * L, D)
    return x_lat.reshape(B, L, D)
```

```python
import functools
import math

import jax
import jax.numpy as jnp
from jax import lax
from jax.experimental import pallas as pl
from jax.experimental.pallas import tpu as pltpu

F32 = jnp.float32
BF16 = jnp.bfloat16

GRID_W = 64
HEAD_DIM = 64
ROPE_BASE = 10000.0
NORM_EPS = 1e-6
NEG_INF = -1e30
N_BRANCH = 4
HY_ORDER = 2
HY_POS_BANDS = 8
SWA_KV_HEADS = 2
SWA_WINDOW = 128
SWA_BLOCK = 128
RW_DECAY_RANK = 64
RW_A_RANK = 64
RW_G_RANK = 128
RW_GN_EPS = 64e-5
DF_BLOCK = 128
DF_ROW_CHUNK = 256
MOE_GROUPS = 4
MOE_PER_GROUP = 8
MOE_EXPERTS = MOE_GROUPS * MOE_PER_GROUP
MOE_TOP_K = 2
MOE_BLOCK = 256

VMEM_LIMIT_BYTES = 56 * 1024 * 1024


def _row_tile(m, pref):
    t = min(pref, m)
    while m % t:
        t //= 2
    return t


def _params(sem):
    return pltpu.CompilerParams(dimension_semantics=sem, vmem_limit_bytes=VMEM_LIMIT_BYTES)


def _norm_proj_body(x_ref, g_ref, sc_ref, sh_ref, w_ref, p_ref, h_ref, h_scr):
    @pl.when(pl.program_id(1) == 0)
    def _():
        x = x_ref[...].astype(F32)
        y = x * lax.rsqrt(jnp.mean(x * x, axis=-1, keepdims=True) + NORM_EPS)
        h = y * g_ref[...] * (1.0 + sc_ref[...]) + sh_ref[...]
        h_scr[...] = h.astype(BF16)
        h_ref[...] = h_scr[...]

    p_ref[...] = jnp.dot(h_scr[...], w_ref[...], preferred_element_type=F32).astype(p_ref.dtype)


def norm_proj(x, g, scale, shift, w, *, rows_per_mod, tm=1024, tn=512, out_dtype=F32):
    m, d = x.shape
    n = w.shape[1]
    tm = _row_tile(rows_per_mod, tm)
    tn = _row_tile(n, tn)
    tiles_per_mod = rows_per_mod // tm
    mod_map = lambda i, j: (i // tiles_per_mod, 0, 0)
    return pl.pallas_call(
        _norm_proj_body,
        grid=(m // tm, n // tn),
        in_specs=[
            pl.BlockSpec((tm, d), lambda i, j: (i, 0)),
            pl.BlockSpec((1, d), lambda i, j: (0, 0)),
            pl.BlockSpec((None, 1, d), mod_map),
            pl.BlockSpec((None, 1, d), mod_map),
            pl.BlockSpec((d, tn), lambda i, j: (0, j)),
        ],
        out_specs=[
            pl.BlockSpec((tm, tn), lambda i, j: (i, j)),
            pl.BlockSpec((tm, d), lambda i, j: (i, 0)),
        ],
        out_shape=[jax.ShapeDtypeStruct((m, n), out_dtype), jax.ShapeDtypeStruct((m, d), BF16)],
        scratch_shapes=[pltpu.VMEM((tm, d), BF16)],
        compiler_params=_params(("parallel", "arbitrary")),
        name="norm_proj",
    )(x, g, scale, shift, w)


ROUTER_LANES = 128


def _norm_route_body(x_ref, g_ref, sc_ref, sh_ref, wr_ref, br_ref, h_ref, lg_ref):
    x = x_ref[...].astype(F32)
    y = x * lax.rsqrt(jnp.mean(x * x, axis=-1, keepdims=True) + NORM_EPS)
    h = y * g_ref[...] * (1.0 + sc_ref[...]) + sh_ref[...]
    h_ref[...] = h.astype(h_ref.dtype)
    lg_ref[...] = jnp.dot(h, wr_ref[...], preferred_element_type=F32, precision=lax.Precision.HIGHEST) + br_ref[...]


def norm_route(x, g, scale, shift, wr, br, *, rows_per_mod, tm=512):
    m, d = x.shape
    tm = _row_tile(rows_per_mod, tm)
    tiles_per_mod = rows_per_mod // tm
    mod_map = lambda i: (i // tiles_per_mod, 0, 0)
    return pl.pallas_call(
        _norm_route_body,
        grid=(m // tm,),
        in_specs=[
            pl.BlockSpec((tm, d), lambda i: (i, 0)),
            pl.BlockSpec((1, d), lambda i: (0, 0)),
            pl.BlockSpec((None, 1, d), mod_map),
            pl.BlockSpec((None, 1, d), mod_map),
            pl.BlockSpec((d, ROUTER_LANES), lambda i: (0, 0)),
            pl.BlockSpec((1, ROUTER_LANES), lambda i: (0, 0)),
        ],
        out_specs=[pl.BlockSpec((tm, d), lambda i: (i, 0)), pl.BlockSpec((tm, ROUTER_LANES), lambda i: (i, 0))],
        out_shape=[jax.ShapeDtypeStruct((m, d), BF16), jax.ShapeDtypeStruct((m, ROUTER_LANES), F32)],
        compiler_params=_params(("parallel",)),
        name="norm_route",
    )(x, g, scale, shift, wr, br)


def _merge_body(h_ref, y0_ref, y1_ref, y2_ref, y3_ref, wg_ref, bg_ref, wb_ref, o_ref):
    h = h_ref[...]
    acc = None
    for b, y_ref in enumerate((y0_ref, y1_ref, y2_ref, y3_ref)):
        gate = jax.nn.sigmoid(jnp.dot(h, wg_ref[b], preferred_element_type=F32) + bg_ref[b])
        val = gate * jnp.dot(y_ref[...].astype(BF16), wb_ref[b], preferred_element_type=F32)
        acc = val if acc is None else acc + val
    o_ref[...] = acc.astype(o_ref.dtype)


def merge_gated(h, ys, wg, bg, wb, *, tm=1024, tn=256):
    m, d = h.shape
    w = ys[0].shape[-1]
    tm = _row_tile(m, tm)
    tn = _row_tile(d, tn)
    yspec = pl.BlockSpec((tm, w), lambda i, j: (i, 0))
    return pl.pallas_call(
        _merge_body,
        grid=(m // tm, d // tn),
        in_specs=[
            pl.BlockSpec((tm, d), lambda i, j: (i, 0)),
            yspec, yspec, yspec, yspec,
            pl.BlockSpec((N_BRANCH, d, tn), lambda i, j: (0, 0, j)),
            pl.BlockSpec((N_BRANCH, 1, tn), lambda i, j: (0, 0, j)),
            pl.BlockSpec((N_BRANCH, w, tn), lambda i, j: (0, 0, j)),
        ],
        out_specs=pl.BlockSpec((tm, tn), lambda i, j: (i, j)),
        out_shape=jax.ShapeDtypeStruct((m, d), BF16),
        compiler_params=_params(("parallel", "arbitrary")),
        name="merge_gated",
    )(h, *ys, wg, bg, wb)


def _resid_proj_body(x_ref, a_ref, w_ref, gate_ref, o_ref):
    y = jnp.dot(a_ref[...], w_ref[...], preferred_element_type=F32)
    o_ref[...] = (x_ref[...].astype(F32) + gate_ref[...] * y).astype(o_ref.dtype)


def resid_proj(x, a, w, gate, *, rows_per_mod, tm=1024, tn=512):
    m, d = x.shape
    k = a.shape[1]
    tm = _row_tile(rows_per_mod, tm)
    tn = _row_tile(d, tn)
    tiles_per_mod = rows_per_mod // tm
    return pl.pallas_call(
        _resid_proj_body,
        grid=(m // tm, d // tn),
        in_specs=[
            pl.BlockSpec((tm, tn), lambda i, j: (i, j)),
            pl.BlockSpec((tm, k), lambda i, j: (i, 0)),
            pl.BlockSpec((k, tn), lambda i, j: (0, j)),
            pl.BlockSpec((None, 1, tn), lambda i, j: (i // tiles_per_mod, 0, j)),
        ],
        out_specs=pl.BlockSpec((tm, tn), lambda i, j: (i, j)),
        out_shape=jax.ShapeDtypeStruct((m, d), x.dtype),
        compiler_params=_params(("parallel", "arbitrary")),
        name="resid_proj",
    )(x, a, w, gate)


def _moe_body(blk_e_ref, n_used_ref, x_ref, w1_ref, w3_ref, w2_ref, o_ref):
    i = pl.program_id(0)

    @pl.when(i < n_used_ref[0])
    def _():
        x = x_ref[...]
        a = jnp.dot(x, w1_ref[...], preferred_element_type=F32)
        b = jnp.dot(x, w3_ref[...], preferred_element_type=F32)
        hdn = (a * jax.nn.sigmoid(a) * b).astype(BF16)
        o_ref[...] = jnp.dot(hdn, w2_ref[...], preferred_element_type=F32).astype(o_ref.dtype)

    @pl.when(i >= n_used_ref[0])
    def _():
        o_ref[...] = jnp.zeros_like(o_ref)


def moe_experts(xb, blk_e, n_used, w1, w3, w2):
    p, d = xb.shape
    hid = w1.shape[-1]
    n_blocks = p // MOE_BLOCK
    grid_spec = pltpu.PrefetchScalarGridSpec(
        num_scalar_prefetch=2,
        grid=(n_blocks,),
        in_specs=[
            pl.BlockSpec((MOE_BLOCK, d), lambda i, e, n: (i, 0)),
            pl.BlockSpec((None, d, hid), lambda i, e, n: (e[i], 0, 0)),
            pl.BlockSpec((None, d, hid), lambda i, e, n: (e[i], 0, 0)),
            pl.BlockSpec((None, hid, d), lambda i, e, n: (e[i], 0, 0)),
        ],
        out_specs=pl.BlockSpec((MOE_BLOCK, d), lambda i, e, n: (i, 0)),
    )
    return pl.pallas_call(
        _moe_body,
        grid_spec=grid_spec,
        out_shape=jax.ShapeDtypeStruct((p, d), BF16),
        compiler_params=_params(("arbitrary",)),
        name="moe_experts",
    )(blk_e, n_used, xb, w1, w3, w2)


RW_CHUNK = 64
_HI = lax.Precision.HIGHEST
_NT = (((1,), (1,)), ((), ()))
_TN = (((0,), (0,)), ((), ()))


def _dot(a, b, dims=None):
    if dims is None:
        return jnp.dot(a, b, preferred_element_type=F32, precision=_HI)
    return lax.dot_general(a, b, dims, preferred_element_type=F32, precision=_HI)


def _bdot(a, b, dims=None):
    a, b = a.astype(BF16), b.astype(BF16)
    if dims is None:
        return jnp.dot(a, b, preferred_element_type=F32)
    return lax.dot_general(a, b, dims, preferred_element_type=F32)


def _rwkv_prep_body(r_ref, lw_ref, kk_ref, a_ref, k_ref, v_ref, p_ref, g_ref, q_ref, yl_ref, *, reverse):
    T = r_ref.shape[0]
    hd = HEAD_DIM
    row = lax.broadcasted_iota(jnp.int32, (T, T), 0)
    col = lax.broadcasted_iota(jnp.int32, (T, T), 1)
    if reverse:
        strict, incl = col > row, col >= row
    else:
        strict, incl = col < row, col <= row
    lw = lw_ref[...]
    tri = incl.astype(BF16)
    lw1 = lw.astype(BF16)
    res1 = lw - lw1.astype(F32)
    lw2 = res1.astype(BF16)
    lw3 = (res1 - lw2.astype(F32)).astype(BF16)
    cum = (jnp.dot(tri, lw1, preferred_element_type=F32) + jnp.dot(tri, lw2, preferred_element_type=F32)
           + jnp.dot(tri, lw3, preferred_element_type=F32))
    total = jnp.sum(lw, axis=0, keepdims=True)
    e_in = jnp.exp(cum)
    e_ex = jnp.exp(cum - lw)
    e_ninv = jnp.exp(-cum)
    e_rem = jnp.exp(total - cum)
    gam = jnp.exp(total)
    kk = kk_ref[...]
    kka = kk * a_ref[...]
    k = k_ref[...]
    nt = -kk * e_ex
    rt = r_ref[...] * e_in
    at = kka * e_ninv
    kt = k * e_ninv
    ac = kka * e_rem
    kc = k * e_rem
    v = v_ref[...]
    eye = lax.broadcasted_iota(jnp.int32, (hd, hd), 0) == lax.broadcasted_iota(jnp.int32, (hd, hd), 1)
    n_heads = r_ref.shape[1] // hd
    zeros = jnp.zeros((T, hd), F32)
    steps = max(1, (T - 1).bit_length())
    heads = range(n_heads)
    sls = [slice(hh * hd, (hh + 1) * hd) for hh in heads]
    bigs = [_bdot(jnp.concatenate([nt[:, sl], rt[:, sl]], axis=0),
                  jnp.concatenate([at[:, sl], kt[:, sl]], axis=0), _NT) for sl in sls]
    a_ak = [jnp.where(strict, big[:T, T:], 0.0) for big in bigs]
    pws = [jnp.where(strict, big[:T, :T], 0.0) for big in bigs]
    lhs_top = [jnp.concatenate([jnp.where(incl, big[T:, :T], 0.0), jnp.where(incl, big[T:, T:], 0.0)], axis=1)
               for big in bigs]
    xs = [jnp.concatenate([nt[:, sl], _bdot(m, v[:, sl])], axis=1) for m, sl in zip(a_ak, sls)]
    for it in range(steps):
        if it < steps - 1:
            boths = [_bdot(pw, jnp.concatenate([x, pw], axis=1)) for pw, x in zip(pws, xs)]
            xs = [x + both[:, :2 * hd] for x, both in zip(xs, boths)]
            pws = [both[:, 2 * hd:] for both in boths]
        else:
            xs = [x + _bdot(pw, x) for pw, x in zip(pws, xs)]
    rhs = [jnp.concatenate([x, jnp.concatenate([zeros, v[:, sl]], axis=1)], axis=0) for x, sl in zip(xs, sls)]
    tops = [_bdot(lt, rh) for lt, rh in zip(lhs_top, rhs)]
    bots = [_bdot(jnp.concatenate([ac[:, sl], kc[:, sl]], axis=0), rh, _TN) for sl, rh in zip(sls, rhs)]
    for hh in heads:
        p_ref[hh] = jnp.where(eye, gam[:, sls[hh]], 0.0) + bots[hh][:, :hd]
        g_ref[hh] = bots[hh][:, hd:]
    q_ref[...] = jnp.concatenate([rt[:, sl] + top[:, :hd] for sl, top in zip(sls, tops)], axis=1)
    yl_ref[...] = jnp.concatenate([top[:, hd:] for top in tops], axis=1)


def rwkv_chunk_prep(r, logw, kk, a, k, v, *, reverse):
    B, L, W = r.shape
    T = RW_CHUNK
    nh = W // HEAD_DIM
    nc = L // T
    blk = pl.BlockSpec((None, T, W), lambda b, c: (b, c, 0))
    mat = pl.BlockSpec((None, None, nh, HEAD_DIM, HEAD_DIM), lambda b, c: (b, c, 0, 0, 0))
    mat_shape = jax.ShapeDtypeStruct((B, nc, nh, HEAD_DIM, HEAD_DIM), F32)
    seq_shape = jax.ShapeDtypeStruct((B, L, W), F32)
    return pl.pallas_call(
        functools.partial(_rwkv_prep_body, reverse=reverse),
        grid=(B, nc),
        in_specs=[blk] * 6,
        out_specs=[mat, mat, blk, blk],
        out_shape=[mat_shape, mat_shape, seq_shape, seq_shape],
        compiler_params=_params(("parallel", "parallel")),
        name="rwkv_chunk_prep",
    )(r, logw, kk, a, k, v)


def _rwkv_scan_body(p_ref, g_ref, q_ref, yl_ref, z0_ref, y_ref, zf_ref, z_scr):
    c = pl.program_id(1)

    @pl.when(c == 0)
    def _():
        z_scr[...] = z0_ref[...]

    hd = HEAD_DIM
    n_heads = z_scr.shape[0]
    q = q_ref[...]
    ys = []
    for h in range(n_heads):
        z = z_scr[h]
        ys.append(_dot(q[:, h * hd:(h + 1) * hd], z))
        z_scr[h] = _dot(p_ref[h], z) + g_ref[h]
    y_ref[...] = jnp.concatenate(ys, axis=1) + yl_ref[...]

    @pl.when(c == pl.num_programs(1) - 1)
    def _():
        zf_ref[...] = z_scr[...]


def rwkv_chunk_scan(p, g, qh, yl, z0, *, reverse):
    B, nc, nh = p.shape[:3]
    L, W = qh.shape[1:]
    T = L // nc
    cidx = (lambda c: nc - 1 - c) if reverse else (lambda c: c)
    mat = pl.BlockSpec((None, None, nh, HEAD_DIM, HEAD_DIM), lambda b, c: (b, cidx(c), 0, 0, 0))
    seq = pl.BlockSpec((None, T, W), lambda b, c: (b, cidx(c), 0))
    st = pl.BlockSpec((None, nh, HEAD_DIM, HEAD_DIM), lambda b, c: (b, 0, 0, 0))
    return pl.pallas_call(
        _rwkv_scan_body,
        grid=(B, nc),
        in_specs=[mat, mat, seq, seq, st],
        out_specs=[seq, st],
        out_shape=[jax.ShapeDtypeStruct((B, L, W), F32), jax.ShapeDtypeStruct(z0.shape, F32)],
        scratch_shapes=[pltpu.VMEM((nh, HEAD_DIM, HEAD_DIM), F32)],
        compiler_params=_params(("parallel", "arbitrary")),
        name="rwkv_chunk_scan",
    )(p, g, qh, yl, z0)


def _head_sum(x):
    lt = 2 * HEAD_DIM
    gi = lax.broadcasted_iota(jnp.int32, (lt, lt), 0) // HEAD_DIM
    gj = lax.broadcasted_iota(jnp.int32, (lt, lt), 1) // HEAD_DIM
    bd = (gi == gj).astype(BF16)
    hi = x.astype(BF16)
    lo = (x - hi.astype(F32)).astype(BF16)
    tiles = [jnp.dot(hi[:, t * lt:(t + 1) * lt], bd, preferred_element_type=F32)
             + jnp.dot(lo[:, t * lt:(t + 1) * lt], bd, preferred_element_type=F32) for t in range(x.shape[1] // lt)]
    return tiles[0] if len(tiles) == 1 else jnp.concatenate(tiles, axis=1)


def _rwkv_pre_body(x_ref, xp_ref, xn_ref, mu_ref, wt_ref, w0_ref, a0_ref, kk_ref, ka_ref, *out_refs, tiles_per_seq):
    i = pl.program_id(0)
    x = x_ref[...]
    tm, feat = x.shape
    w = kk_ref.shape[-1]
    tail = feat - 3 * w
    row = lax.broadcasted_iota(jnp.int32, (tm, feat), 0)
    pos = i % tiles_per_seq
    prev_row = jnp.where(pos == 0, 0.0, xp_ref[7:8, :])
    next_row = jnp.where(pos == tiles_per_seq - 1, 0.0, xn_ref[0:1, :])
    lane = lax.broadcasted_iota(jnp.int32, (tm, tail), 1)
    kscale = kk_ref[...]
    kmix = ka_ref[...]
    for d in range(2):
        if d == 0:
            sh = jnp.where(row == 0, prev_row, pltpu.roll(x, 1, axis=0))
        else:
            sh = jnp.where(row == tm - 1, next_row, pltpu.roll(x, tm - 1, axis=0))
        xs = x + mu_ref[d] * (sh - x)
        r, k, v, t = xs[:, :w], xs[:, w:2 * w], xs[:, 2 * w:3 * w], xs[:, 3 * w:]
        act = jnp.where(lane < RW_DECAY_RANK, jnp.tanh(t),
                        jnp.where(lane < RW_DECAY_RANK + RW_A_RANK, t, jax.nn.sigmoid(t)))
        hi = act.astype(BF16)
        lo = (act - hi.astype(F32)).astype(BF16)
        z = jnp.dot(hi, wt_ref[d], preferred_element_type=F32) + jnp.dot(lo, wt_ref[d], preferred_element_type=F32)
        logw = -math.exp(-0.5) * jax.nn.sigmoid(w0_ref[d] + z[:, :w])
        a = jax.nn.sigmoid(a0_ref[d] + z[:, w:2 * w])
        g = z[:, 2 * w:]
        kx = k * kscale
        kk = kx * lax.rsqrt(_head_sum(kx * kx) + 1e-12)
        k2 = k * (1.0 + (a - 1.0) * kmix)
        for ref, val in zip(out_refs[7 * d:7 * d + 7], (r, logw, kk, a, k2, v, g)):
            ref[...] = val


def rwkv_pre(feats, mu, wt, w0, a0, k_k, k_a, rows_per_seq, width):
    m = feats.shape[0]
    feat = mu.shape[-1]
    tm = _row_tile(rows_per_seq, 256)
    tiles_per_seq = rows_per_seq // tm
    r8 = tm // 8
    full = lambda shape: pl.BlockSpec(shape, lambda i: (0,) * len(shape))
    outs = pl.pallas_call(
        functools.partial(_rwkv_pre_body, tiles_per_seq=tiles_per_seq),
        grid=(m // tm,),
        in_specs=[
            pl.BlockSpec((tm, feat), lambda i: (i, 0)),
            pl.BlockSpec((8, feat), lambda i: (jnp.maximum(i * r8 - 1, 0), 0)),
            pl.BlockSpec((8, feat), lambda i: (jnp.minimum((i + 1) * r8, m // 8 - 1), 0)),
            full(mu.shape), full(wt.shape), full(w0.shape), full(a0.shape), full(k_k.shape), full(k_a.shape),
        ],
        out_specs=[pl.BlockSpec((tm, width), lambda i: (i, 0))] * 14,
        out_shape=[jax.ShapeDtypeStruct((m, width), F32)] * 14,
        compiler_params=_params(("parallel",)),
        name="rwkv_pre",
    )(feats, feats, feats, mu, wt, w0, a0, k_k, k_a)
    return outs[:7], outs[7:]


def _rwkv_readout_body(*refs):
    dirs, (rk_ref, lnw_ref, lnb_ref, o_ref) = (refs[0:5], refs[5:10]), refs[10:]
    inv = 1.0 / HEAD_DIM
    acc = None
    for y_ref, r_ref, k_ref, v_ref, g_ref in dirs:
        y = y_ref[...]
        c = y - _head_sum(y) * inv
        yn = c * lax.rsqrt(_head_sum(c * c) * inv + RW_GN_EPS) * lnw_ref[...] + lnb_ref[...]
        bonus = _head_sum(r_ref[...] * k_ref[...] * rk_ref[...]) * v_ref[...]
        val = (yn + bonus) * g_ref[...]
        acc = val if acc is None else acc + val
    o_ref[...] = acc


def rwkv_readout(fwd, bwd, r_k, ln_w, ln_b):
    m, w = fwd[0].shape
    tm = _row_tile(m, 512)
    blk = pl.BlockSpec((tm, w), lambda i: (i, 0))
    par = pl.BlockSpec((1, w), lambda i: (0, 0))
    return pl.pallas_call(
        _rwkv_readout_body,
        grid=(m // tm,),
        in_specs=[blk] * 10 + [par] * 3,
        out_specs=blk,
        out_shape=jax.ShapeDtypeStruct((m, w), F32),
        compiler_params=_params(("parallel",)),
        name="rwkv_readout",
    )(*fwd, *bwd, r_k, ln_w, ln_b)


def _head_prep_body(*refs, rope):
    if rope:
        x_ref, g_ref, mk_ref, sc_ref, cos_ref, sin_ref, o_ref = refs
    else:
        x_ref, g_ref, mk_ref, sc_ref, o_ref = refs
    x = x_ref[...]
    tm, wb = x.shape
    lt = 2 * HEAD_DIM
    gid = lax.broadcasted_iota(jnp.int32, (lt, lt), 0) // HEAD_DIM
    gjd = lax.broadcasted_iota(jnp.int32, (lt, lt), 1) // HEAD_DIM
    bd = (gid == gjd).astype(BF16)
    xx = x * x
    hi = xx.astype(BF16)
    lo = (xx - hi.astype(F32)).astype(BF16)
    ms = jnp.concatenate(
        [jnp.dot(hi[:, t * lt:(t + 1) * lt], bd, preferred_element_type=F32)
         + jnp.dot(lo[:, t * lt:(t + 1) * lt], bd, preferred_element_type=F32) for t in range(wb // lt)], axis=1)
    y = x * lax.rsqrt(ms * (1.0 / HEAD_DIM) + NORM_EPS) * g_ref[...]
    if rope:
        half = HEAD_DIM // 2
        lane = lax.broadcasted_iota(jnp.int32, (tm, wb), 1)
        partner = jnp.where(lane % HEAD_DIM < half, pltpu.roll(y, wb - half, axis=1), pltpu.roll(y, half, axis=1))
        y = y * cos_ref[...] + partner * sin_ref[...]
    y = jnp.where(mk_ref[...] > 0.0, y, x)
    o_ref[...] = (y * sc_ref[...]).astype(o_ref.dtype)


def head_prep(p, col0, wb, gain, mask, scale, tables, rows_per_seq):
    m = p.shape[0]
    nj = gain.shape[0]
    tm = _row_tile(rows_per_seq, 512)
    cb0 = col0 // wb
    tiles_per_seq = rows_per_seq // tm
    par = pl.BlockSpec((None, 1, wb), lambda i, j: (j, 0, 0))
    in_specs = [pl.BlockSpec((tm, wb), lambda i, j: (i, cb0 + j)), par, par, par]
    args = [p, gain, mask, scale]
    if tables is not None:
        tab = pl.BlockSpec((tm, wb), lambda i, j: (i % tiles_per_seq, 0))
        in_specs += [tab, tab]
        args += list(tables)
    return pl.pallas_call(
        functools.partial(_head_prep_body, rope=tables is not None),
        grid=(m // tm, nj),
        in_specs=in_specs,
        out_specs=pl.BlockSpec((tm, wb), lambda i, j: (i, j)),
        out_shape=jax.ShapeDtypeStruct((m, nj * wb), BF16),
        compiler_params=_params(("parallel", "parallel")),
        name="head_prep",
    )(*args)


def _diff_attn_body(lam_ref, q_ref, kt_ref, v_ref, subg_ref, o_ref, q2_scr, m_scr, acc_scr, *, out_scale):
    ki = pl.program_id(3)
    tq = q_ref.shape[0]

    @pl.when(ki == 0)
    def _():
        q = q_ref[...]
        lane = lax.broadcasted_iota(jnp.int32, q.shape, 1)
        q2_scr[0:tq, :] = jnp.where(lane < HEAD_DIM, q, jnp.zeros_like(q))
        q2_scr[tq:2 * tq, :] = jnp.where(lane >= HEAD_DIM, q, jnp.zeros_like(q))
        m_scr[...] = jnp.full_like(m_scr, -jnp.inf)
        acc_scr[...] = jnp.zeros_like(acc_scr)

    kt = kt_ref[...]
    v = jnp.concatenate([v_ref[...], jnp.ones(v_ref.shape, BF16)], axis=1)
    rc = min(DF_ROW_CHUNK, 2 * tq)
    n_chunks = 2 * tq // rc
    score = lambda c: jnp.dot(q2_scr[pl.ds(c * rc, rc), :], kt, preferred_element_type=F32)
    s_next = score(0)
    for c in range(n_chunks):
        rows = pl.ds(c * rc, rc)
        s = s_next
        if c + 1 < n_chunks:
            s_next = score(c + 1)
        m_prev = m_scr[rows, :]
        m_new = jnp.maximum(m_prev, jnp.max(s, axis=-1, keepdims=True))
        alpha = jnp.exp2(m_prev - m_new)
        p = jnp.exp2(s - m_new)
        acc_scr[rows, :] = alpha * acc_scr[rows, :] + jnp.dot(p.astype(BF16), v, preferred_element_type=F32)
        m_scr[rows, :] = m_new

    @pl.when(ki == pl.num_programs(3) - 1)
    def _():
        hw = o_ref.shape[-1]
        o = acc_scr[:, 0:hw] / acc_scr[:, hw:hw + 1]
        a = o[0:tq, :] - lam_ref[0, 0] * o[tq:2 * tq, :]
        y = a * lax.rsqrt(jnp.mean(a * a, axis=-1, keepdims=True) + NORM_EPS)
        o_ref[...] = (y * subg_ref[...] * out_scale).astype(o_ref.dtype)


def _key_tile(k, cap):
    best = 128
    t = 128
    while t <= min(k, cap):
        if k % t == 0:
            best = t
        t += 128
    return best


def diff_attention(q, q_blk0, kt, v, v_blk0, sub_g, lam, out_scale, *, tq=512, tk_cap=1280):
    B, L = q.shape[:2]
    W, K = kt.shape[1:]
    hw = 2 * HEAD_DIM
    tq = _row_tile(L, tq)
    tk = _key_tile(K, tk_cap)
    return pl.pallas_call(
        functools.partial(_diff_attn_body, out_scale=out_scale),
        grid=(B, W // hw, L // tq, K // tk),
        in_specs=[
            pl.BlockSpec(memory_space=pltpu.SMEM),
            pl.BlockSpec((None, tq, hw), lambda b, h, i, j: (b, i, q_blk0 + h)),
            pl.BlockSpec((None, hw, tk), lambda b, h, i, j: (b, h, j)),
            pl.BlockSpec((None, tk, hw), lambda b, h, i, j: (b, j, v_blk0 + h)),
            pl.BlockSpec((1, hw), lambda b, h, i, j: (0, 0)),
        ],
        out_specs=pl.BlockSpec((None, tq, hw), lambda b, h, i, j: (b, i, h)),
        out_shape=jax.ShapeDtypeStruct((B, L, W), F32),
        scratch_shapes=[pltpu.VMEM((2 * tq, hw), BF16), pltpu.VMEM((2 * tq, 1), F32),
                        pltpu.VMEM((2 * tq, 2 * hw), F32)],
        compiler_params=_params(("parallel", "parallel", "parallel", "arbitrary")),
        name="diff_attention",
    )(lam, q, kt, v, sub_g)


def _swa_finish(parts, vals, sink, o_ref):
    m = sink
    for s in parts:
        m = jnp.maximum(m, jnp.max(s, axis=-1, keepdims=True))
    denom = jnp.exp(sink - m)
    acc = None
    for s, v in zip(parts, vals):
        p = jnp.exp(s - m)
        denom = denom + jnp.sum(p, axis=-1, keepdims=True)
        pv = jnp.dot(p.astype(BF16), v, preferred_element_type=F32)
        acc = pv if acc is None else acc + pv
    o = acc / denom
    o_ref[...] = o.reshape(o_ref.shape).astype(o_ref.dtype)


def _swa_band_body(q_ref, kp_ref, kn_ref, kx_ref, kc_ref, vp_ref, vn_ref, vx_ref, vc_ref, sink_ref, o_ref):
    n = pl.program_id(2)
    nb = pl.num_programs(2)
    grp, blk, hd = q_ref.shape
    q = q_ref[...].reshape(grp * blk, hd)
    iq = lax.broadcasted_iota(jnp.int32, (grp * blk, blk), 0) % blk
    j = lax.broadcasted_iota(jnp.int32, (grp * blk, blk), 1)
    s_prev = jnp.where((iq + blk - j <= SWA_WINDOW) & (n > 0), _dot_nt_bf16(q, kp_ref[...]), NEG_INF)
    s_cur = jnp.where(jnp.abs(iq - j) <= SWA_WINDOW, _dot_nt_bf16(q, kn_ref[...]), NEG_INF)
    s_next = jnp.where((j + blk - iq <= SWA_WINDOW) & (n < nb - 1), _dot_nt_bf16(q, kx_ref[...]), NEG_INF)
    s_ctx = _dot_nt_bf16(q, kc_ref[...])
    _swa_finish([s_prev, s_cur, s_next, s_ctx], [vp_ref[...], vn_ref[...], vx_ref[...], vc_ref[...]],
                sink_ref[...], o_ref)


def _swa_ctx_body(q_ref, kc_ref, vc_ref, sink_ref, o_ref):
    grp, blk, hd = q_ref.shape
    q = q_ref[...].reshape(grp * blk, hd)
    _swa_finish([_dot_nt_bf16(q, kc_ref[...])], [vc_ref[...]], sink_ref[...], o_ref)


def _dot_nt_bf16(a, b):
    return lax.dot_general(a, b, _NT, preferred_element_type=F32)


def swa_attention(q, k, v, kc, vc, sink_rows):
    B, kvh, grp, L, hd = q.shape
    C = kc.shape[2]
    blk = SWA_BLOCK
    nb = L // blk
    qspec = pl.BlockSpec((None, None, grp, blk, hd), lambda b, h, n: (b, h, 0, n, 0))
    prev = pl.BlockSpec((None, None, blk, hd), lambda b, h, n: (b, h, jnp.maximum(n - 1, 0), 0))
    cur = pl.BlockSpec((None, None, blk, hd), lambda b, h, n: (b, h, n, 0))
    nxt = pl.BlockSpec((None, None, blk, hd), lambda b, h, n: (b, h, jnp.minimum(n + 1, nb - 1), 0))
    cspec = pl.BlockSpec((None, None, C, hd), lambda b, h, n: (b, h, 0, 0))
    sspec = pl.BlockSpec((None, grp * blk, 1), lambda b, h, n: (h, 0, 0))
    return pl.pallas_call(
        _swa_band_body,
        grid=(B, kvh, nb),
        in_specs=[qspec, prev, cur, nxt, cspec, prev, cur, nxt, cspec, sspec],
        out_specs=qspec,
        out_shape=jax.ShapeDtypeStruct(q.shape, F32),
        compiler_params=_params(("parallel", "parallel", "parallel")),
        name="swa_attention",
    )(q, k, k, k, kc, v, v, v, vc, sink_rows)


def swa_context_attention(q, kc, vc, sink_rows):
    B, kvh, grp, L, hd = q.shape
    C = kc.shape[2]
    blk = SWA_BLOCK
    qspec = pl.BlockSpec((None, None, grp, blk, hd), lambda b, h, n: (b, h, 0, n, 0))
    cspec = pl.BlockSpec((None, None, C, hd), lambda b, h, n: (b, h, 0, 0))
    sspec = pl.BlockSpec((None, grp * blk, 1), lambda b, h, n: (h, 0, 0))
    return pl.pallas_call(
        _swa_ctx_body,
        grid=(B, kvh, L // blk),
        in_specs=[qspec, cspec, cspec, sspec],
        out_specs=qspec,
        out_shape=jax.ShapeDtypeStruct(q.shape, F32),
        compiler_params=_params(("parallel", "parallel", "parallel")),
        name="swa_context_attention",
    )(q, kc, vc, sink_rows)


HY_COL_GROUP = 8
HY_K2_GROUP = 8


def _hy_fwd_outer_body(x_ref, fc_ref, fsn_ref, o_ref):
    half, g, ch = x_ref.shape
    for j in range(g):
        x = x_ref[:, j, :].astype(BF16)
        cols = slice(j * ch, (j + 1) * ch)
        o_ref[0, :, cols] = jnp.dot(fc_ref[...], x, preferred_element_type=F32).astype(o_ref.dtype)
        o_ref[1, :, cols] = jnp.dot(fsn_ref[...], x, preferred_element_type=F32).astype(o_ref.dtype)


def _hy_spectral_body(a_ref, m_ref, mi_ref, h_ref, o_ref):
    n1 = a_ref.shape[2]
    for j in range(a_ref.shape[1]):
        b = jnp.concatenate([a_ref[0, j], a_ref[1, j]], axis=0)
        x = jnp.dot(m_ref[j], b, preferred_element_type=F32)
        xr, xi = x[:n1], x[n1:]
        hr, hi = h_ref[0, j], h_ref[1, j]
        y = jnp.concatenate([xr * hr - xi * hi, xr * hi + xi * hr], axis=0).astype(BF16)
        c = jnp.dot(mi_ref[j], y, preferred_element_type=F32)
        o_ref[0, j] = c[:n1].astype(o_ref.dtype)
        o_ref[1, j] = c[n1:].astype(o_ref.dtype)


def _hy_inv_outer_body(c_ref, gc_ref, gsn_ref, z_ref, gate_ref, skip_ref, o_ref):
    half, g, ch = z_ref.shape
    y = (jnp.dot(gc_ref[...], c_ref[0], preferred_element_type=F32)
         + jnp.dot(gsn_ref[...], c_ref[1], preferred_element_type=F32))
    for j in range(g):
        o_ref[:, j, :] = gate_ref[:, j, :] * (y[:, j * ch:(j + 1) * ch] + skip_ref[...] * z_ref[:, j, :])


def _hy_tables(n):
    n2 = 1 << (n.bit_length() // 2)
    n1 = n // n2
    two_pi = 2.0 * math.pi
    i2 = jnp.arange(n2, dtype=jnp.int32)
    ang2 = ((i2[:, None] * i2[None, :]) % n2).astype(F32) * (two_pi / n2)
    c2, s2 = jnp.cos(ang2), jnp.sin(ang2)
    half = n2 // 2
    fwd_c, fwd_sn = c2[:, :half].astype(BF16), (-s2[:, :half]).astype(BF16)
    inv_c, inv_sn = c2[:half, :].astype(BF16), (-s2[:half, :]).astype(BF16)
    i1 = jnp.arange(n1, dtype=jnp.int32)
    kk = n2 * i1[None, :, None] + i2[:, None, None]
    th = ((kk * i1[None, None, :]) % n).astype(F32) * (two_pi / n)
    ct, st = jnp.cos(th), jnp.sin(th)
    m_big = jnp.concatenate([jnp.concatenate([ct, st], axis=2), jnp.concatenate([-st, ct], axis=2)], axis=1)
    ctt, stt = jnp.swapaxes(ct, 1, 2), jnp.swapaxes(st, 1, 2)
    mi_big = jnp.concatenate([jnp.concatenate([ctt, -stt], axis=2), jnp.concatenate([stt, ctt], axis=2)], axis=1)
    return n1, n2, fwd_c, fwd_sn, inv_c, inv_sn, m_big.astype(BF16), mi_big.astype(BF16)


def _hy_spectrum_planes(spec, n, n1, n2):
    L = n // 2
    full = jnp.concatenate([spec, jnp.conj(spec[1:L][::-1])], axis=0) * (1.0 / n)
    o, c = full.shape[1:]
    full = jnp.transpose(full.reshape(n1, n2, o, c), (2, 1, 0, 3))
    return jnp.stack([jnp.real(full), jnp.imag(full)], axis=1).astype(F32)


def hyena_long_conv(z, gate, skip, h_planes, tables):
    B, L, C = z.shape
    n1, n2, fwd_c, fwd_sn, inv_c, inv_sn, m_big, mi_big = tables
    half = n2 // 2
    g = min(HY_COL_GROUP, n1)
    kb = min(HY_K2_GROUP, n2)
    gc = g * C
    z4 = z.reshape(B, half, n1, C)
    gate4 = gate.reshape(B, half, n1, C)
    seq_blk = pl.BlockSpec((None, half, g, C), lambda b, j: (b, 0, j, 0))
    plane = jax.ShapeDtypeStruct((B, 2, n2, n1 * C), BF16)
    plane_blk = pl.BlockSpec((None, 2, n2, gc), lambda b, j: (b, 0, 0, j))
    full_mat = lambda shape: pl.BlockSpec(shape, lambda b, j: (0,) * len(shape))
    a = pl.pallas_call(
        _hy_fwd_outer_body,
        grid=(B, n1 // g),
        in_specs=[seq_blk, full_mat((n2, half)), full_mat((n2, half))],
        out_specs=plane_blk,
        out_shape=plane,
        compiler_params=_params(("parallel", "parallel")),
        name="hy_fwd_outer",
    )(z4, fwd_c, fwd_sn)
    cc = pl.pallas_call(
        _hy_spectral_body,
        grid=(B, n2 // kb),
        in_specs=[
            pl.BlockSpec((None, 2, kb, n1, C), lambda b, j: (b, 0, j, 0, 0)),
            pl.BlockSpec((kb, 2 * n1, 2 * n1), lambda b, j: (j, 0, 0)),
            pl.BlockSpec((kb, 2 * n1, 2 * n1), lambda b, j: (j, 0, 0)),
            pl.BlockSpec((2, kb, n1, C), lambda b, j: (0, j, 0, 0)),
        ],
        out_specs=pl.BlockSpec((None, 2, kb, n1, C), lambda b, j: (b, 0, j, 0, 0)),
        out_shape=jax.ShapeDtypeStruct((B, 2, n2, n1, C), BF16),
        compiler_params=_params(("parallel", "parallel")),
        name="hy_spectral",
    )(a.reshape(B, 2, n2, n1, C), m_big, mi_big, h_planes)
    skip_row = skip.astype(F32).reshape(1, C)
    out = pl.pallas_call(
        _hy_inv_outer_body,
        grid=(B, n1 // g),
        in_specs=[
            plane_blk,
            full_mat((half, n2)), full_mat((half, n2)),
            seq_blk, seq_blk, full_mat((1, C)),
        ],
        out_specs=seq_blk,
        out_shape=jax.ShapeDtypeStruct((B, half, n1, C), F32),
        compiler_params=_params(("parallel", "parallel")),
        name="hy_inv_outer",
    )(cc.reshape(B, 2, n2, n1 * C), inv_c, inv_sn, z4, gate4, skip_row)
    return out.reshape(B, L, C)


def _split2(x):
    hi = x.astype(BF16)
    return hi, (x - hi.astype(F32)).astype(BF16)


def _hy_filt_outer_body(x_ref, fc_ref, fsn_ref, o_ref):
    hi, lo = _split2(x_ref[...])
    for plane, f_ref in enumerate((fc_ref, fsn_ref)):
        o_ref[plane] = (jnp.dot(f_ref[...], hi, preferred_element_type=F32)
                        + jnp.dot(f_ref[...], lo, preferred_element_type=F32))


def _hy_filt_inner_body(a_ref, m_ref, o_ref):
    n1 = a_ref.shape[2]
    n_order, ch = o_ref.shape[0], o_ref.shape[-1]
    for j in range(a_ref.shape[1]):
        hi, lo = _split2(jnp.concatenate([a_ref[0, j], a_ref[1, j]], axis=0))
        x = jnp.dot(m_ref[j], hi, preferred_element_type=F32) + jnp.dot(m_ref[j], lo, preferred_element_type=F32)
        xr, xi = x[:n1], x[n1:]
        for o in range(n_order):
            f, b = slice(2 * o * ch, (2 * o + 1) * ch), slice((2 * o + 1) * ch, (2 * o + 2) * ch)
            o_ref[o, 0, j] = xr[:, f] + xr[:, b]
            o_ref[o, 1, j] = xi[:, f] - xi[:, b]


def hyena_filter_planes(hs, tables, n_order, ch):
    L, hf = hs.shape
    n1, n2, fwd_c, fwd_sn, _, _, m_big, _ = tables
    half = n2 // 2
    g = min(2, n1)
    kb = min(2, n2)
    full_mat = lambda shape: pl.BlockSpec(shape, lambda j: (0,) * len(shape))
    a = pl.pallas_call(
        _hy_filt_outer_body,
        grid=(n1 // g,),
        in_specs=[pl.BlockSpec((half, g * hf), lambda j: (0, j)), full_mat((n2, half)), full_mat((n2, half))],
        out_specs=pl.BlockSpec((2, n2, g * hf), lambda j: (0, 0, j)),
        out_shape=jax.ShapeDtypeStruct((2, n2, n1 * hf), F32),
        compiler_params=_params(("parallel",)),
        name="hy_filt_outer",
    )(hs.reshape(half, n1 * hf), fwd_c, fwd_sn)
    return pl.pallas_call(
        _hy_filt_inner_body,
        grid=(n2 // kb,),
        in_specs=[pl.BlockSpec((2, kb, n1, hf), lambda j: (0, j, 0, 0)),
                  pl.BlockSpec((kb, 2 * n1, 2 * n1), lambda j: (j, 0, 0))],
        out_specs=pl.BlockSpec((n_order, 2, kb, n1, ch), lambda j: (0, 0, j, 0, 0)),
        out_shape=jax.ShapeDtypeStruct((n_order, 2, n2, n1, ch), F32),
        compiler_params=_params(("parallel",)),
        name="hy_filt_inner",
    )(a.reshape(2, n2, n1, hf), m_big)


def _hyena_filter_taps(L, ch, f_w1, f_b1, f_w2, f_b2, f_w3, f_b3, f_freq, f_decay):
    t = jnp.arange(L, dtype=F32) / max(L - 1, 1)
    ang = 2 * math.pi * t[:, None] * jnp.arange(1, HY_POS_BANDS + 1, dtype=F32)
    feat = jnp.concatenate([t[:, None], jnp.sin(ang), jnp.cos(ang)], axis=-1)
    hp = lax.Precision.HIGHEST
    h = jnp.sin(f_freq * (jnp.dot(feat, f_w1, precision=hp) + f_b1))
    h = jnp.sin(f_freq * (jnp.dot(h, f_w2, precision=hp) + f_b2))
    h = jnp.dot(h, f_w3, precision=hp) + f_b3
    h = h * jnp.exp(-f_decay * t[:, None])
    h = h.reshape(L, HY_ORDER, 2, ch)
    keep = jnp.ones((L, 1, 2, 1), F32).at[0, 0, 1, 0].set(0.0)
    h = h * keep
    l1 = jnp.sum(jnp.abs(h), axis=(0, 2), keepdims=True)
    return (h / (l1 * (2.0 * L))).reshape(L, HY_ORDER * 2 * ch)


def _hyena(pa, conv_w, conv_b, f_w1, f_b1, f_w2, f_b2, f_w3, f_b3, f_freq, f_decay, skip):
    B, L, _ = pa.shape
    ch = skip.shape[-1]
    n = 2 * L
    u = _short_conv3(pa, conv_w, conv_b).astype(F32)
    v, x1, x2 = jnp.split(u, 3, axis=-1)
    tables = _hy_tables(n)
    taps = _hyena_filter_taps(L, ch, f_w1, f_b1, f_w2, f_b2, f_w3, f_b3, f_freq, f_decay)
    h = hyena_filter_planes(taps, tables, HY_ORDER, ch)
    z = hyena_long_conv(v, x1, skip[0], h[0], tables)
    return hyena_long_conv(z, x2, skip[1], h[1], tables)


def _rms_norm(x, g):
    xf = x.astype(F32)
    y = xf * lax.rsqrt(jnp.mean(xf * xf, axis=-1, keepdims=True) + NORM_EPS)
    return (y * g.astype(F32)).astype(x.dtype)


def _rope_tables(rows):
    n_freq = HEAD_DIM // 4
    inv = ROPE_BASE ** (-jnp.arange(n_freq, dtype=F32) / n_freq)
    row = jnp.repeat(jnp.arange(rows, dtype=F32), GRID_W)
    col = jnp.tile(jnp.arange(GRID_W, dtype=F32), rows)
    ang = jnp.concatenate([row[:, None] * inv, col[:, None] * inv], axis=-1)
    return jnp.cos(ang), jnp.sin(ang)


def _apply_rope(x, cos, sin):
    shp = (x.shape[1],) + (1,) * (x.ndim - 3) + (HEAD_DIM // 2,)
    c, s = cos.reshape(shp), sin.reshape(shp)
    x1, x2 = jnp.split(x.astype(F32), 2, axis=-1)
    return jnp.concatenate([x1 * c - x2 * s, x2 * c + x1 * s], axis=-1).astype(x.dtype)


def _short_conv3(x, w, b):
    xp = jnp.pad(x, ((0, 0), (1, 1), (0, 0)))
    return xp[:, :-2] * w[0] + xp[:, 1:-1] * w[1] + xp[:, 2:] * w[2] + b


def _hyena_filter_spectra(L, ch, f_w1, f_b1, f_w2, f_b2, f_w3, f_b3, f_freq, f_decay):
    t = jnp.arange(L, dtype=F32) / max(L - 1, 1)
    ang = 2 * math.pi * t[:, None] * jnp.arange(1, HY_POS_BANDS + 1, dtype=F32)
    feat = jnp.concatenate([t[:, None], jnp.sin(ang), jnp.cos(ang)], axis=-1)
    hp = lax.Precision.HIGHEST
    h = jnp.sin(f_freq * (jnp.dot(feat, f_w1, precision=hp) + f_b1))
    h = jnp.sin(f_freq * (jnp.dot(h, f_w2, precision=hp) + f_b2))
    h = jnp.dot(h, f_w3, precision=hp) + f_b3
    h = h * jnp.exp(-f_decay * t[:, None])
    h = h.reshape(L, HY_ORDER, 2, ch)
    h_fwd, h_bwd = h[:, :, 0], h[:, :, 1]
    l1 = jnp.sum(jnp.abs(h_fwd), axis=0) + jnp.sum(jnp.abs(h_bwd[1:]), axis=0)
    kern = jnp.concatenate([h_fwd, jnp.zeros((1, HY_ORDER, ch), F32), h_bwd[1:][::-1]], axis=0) / l1
    return jnp.fft.rfft(kern, axis=0)


def _fft_long_conv(z, spec, skip):
    L = z.shape[1]
    zf = jnp.fft.rfft(z, n=2 * L, axis=1)
    y = jnp.fft.irfft(zf * spec[None], n=2 * L, axis=1)[:, :L]
    return y + skip * z


def _hyena_branch(pa, conv_w, conv_b, spec, skip):
    u = _short_conv3(pa, conv_w, conv_b).astype(F32)
    v, x1, x2 = jnp.split(u, 3, axis=-1)
    z = x1 * _fft_long_conv(v, spec[:, 0], skip[0])
    return x2 * _fft_long_conv(z, spec[:, 1], skip[1])


def _swa_project(pb, q_g, k_g, n_heads):
    B, L = pb.shape[:2]
    q, k, v = jnp.split(pb, [n_heads * HEAD_DIM, (n_heads + SWA_KV_HEADS) * HEAD_DIM], axis=-1)
    q = _rms_norm(q.reshape(B, L, n_heads, HEAD_DIM), q_g)
    k = _rms_norm(k.reshape(B, L, SWA_KV_HEADS, HEAD_DIM), k_g)
    return q, k, v.reshape(B, L, SWA_KV_HEADS, HEAD_DIM)


def _sink_softmax(s, sink):
    m = jnp.maximum(jnp.max(s, axis=-1, keepdims=True), sink)
    p = jnp.exp(s - m)
    return p / (jnp.sum(p, axis=-1, keepdims=True) + jnp.exp(sink - m))


def _swa_context(qc, kc, vc, sink):
    B, C, n_heads = qc.shape[:3]
    grp = n_heads // SWA_KV_HEADS
    qg = qc.reshape(B, C, SWA_KV_HEADS, grp, HEAD_DIM)
    s = jnp.einsum('bqhgd,bkhd->bhgqk', qg, kc).astype(F32) * HEAD_DIM ** -0.5
    p = _sink_softmax(s, sink.astype(F32).reshape(1, SWA_KV_HEADS, grp, 1, 1))
    o = jnp.einsum('bhgqk,bkhd->bqhgd', p.astype(vc.dtype), vc)
    return o.reshape(B, C, n_heads * HEAD_DIM)


def _swa_latent(q, k, v, kc, vc, sink):
    B, L, n_heads = q.shape[:3]
    grp = n_heads // SWA_KV_HEADS
    nb = L // SWA_BLOCK
    qb = q.reshape(B, nb, SWA_BLOCK, SWA_KV_HEADS, grp, HEAD_DIM)

    def band(t):
        tb = t.reshape(B, nb, SWA_BLOCK, SWA_KV_HEADS, HEAD_DIM)
        tp = jnp.pad(tb, ((0, 0), (1, 1), (0, 0), (0, 0), (0, 0)))
        return jnp.concatenate([tp[:, :-2], tp[:, 1:-1], tp[:, 2:]], axis=2)

    kb, vb = band(k), band(v)
    scale = HEAD_DIM ** -0.5
    s_loc = jnp.einsum('bnqhgd,bnkhd->bnhgqk', qb, kb).astype(F32) * scale
    s_ctx = jnp.einsum('bnqhgd,bchd->bnhgqc', qb, kc).astype(F32) * scale
    q_rel = jnp.arange(SWA_BLOCK)[:, None] + SWA_BLOCK
    k_rel = jnp.arange(3 * SWA_BLOCK)[None, :]
    k_abs = (jnp.arange(nb)[:, None, None] - 1) * SWA_BLOCK + k_rel[None]
    valid = (jnp.abs(q_rel - k_rel) <= SWA_WINDOW)[None] & (k_abs >= 0) & (k_abs < L)
    s_loc = jnp.where(valid[None, :, None, None], s_loc, NEG_INF)
    s = jnp.concatenate([s_loc, s_ctx], axis=-1)
    p = _sink_softmax(s, sink.astype(F32).reshape(1, 1, SWA_KV_HEADS, grp, 1, 1)).astype(v.dtype)
    o = (jnp.einsum('bnhgqk,bnkhd->bnqhgd', p[..., :3 * SWA_BLOCK], vb)
         + jnp.einsum('bnhgqc,bchd->bnqhgd', p[..., 3 * SWA_BLOCK:], vc))
    return o.reshape(B, L, n_heads * HEAD_DIM)


def _token_shift(x, reverse):
    if reverse:
        return jnp.pad(x, ((0, 0), (0, 1), (0, 0)))[:, 1:]
    return jnp.pad(x, ((0, 0), (1, 0), (0, 0)))[:, :-1]


def _head_l2norm(x, n_heads):
    B, L, C = x.shape
    xh = x.reshape(B, L, n_heads, HEAD_DIM)
    xh = xh * lax.rsqrt(jnp.sum(xh * xh, axis=-1, keepdims=True) + 1e-12)
    return xh.reshape(B, L, C)


def _rwkv_prepare(feats, reverse, width, mu, w0, w2, a0, a2, g2, k_k, k_a):
    n_heads = width // HEAD_DIM
    xs = feats + mu * (_token_shift(feats, reverse) - feats)
    splits = (width, 2 * width, 3 * width, 3 * width + RW_DECAY_RANK, 3 * width + RW_DECAY_RANK + RW_A_RANK)
    r, k, v, wd, ad, gd = jnp.split(xs, splits, axis=-1)
    logw = -jnp.exp(-jax.nn.softplus(-(w0 + jnp.tanh(wd) @ w2)) - 0.5)
    a = jax.nn.sigmoid(a0 + ad @ a2)
    g = jax.nn.sigmoid(gd) @ g2
    kk = _head_l2norm(k * k_k, n_heads)
    k = k * (1 + (a - 1) * k_a)
    return r, logw, kk, a, k, v, g


def _wkv7(r, logw, kk, a, k, v, z0, reverse):
    p, g, qh, yl = rwkv_chunk_prep(r, logw, kk, a, k, v, reverse=reverse)
    y, z_fin = rwkv_chunk_scan(p, g, qh, yl, z0, reverse=reverse)
    return z_fin, y


def _rwkv_readout(y, r, k, v, g, r_k, ln_w, ln_b):
    B, L, C = y.shape
    n_heads = C // HEAD_DIM
    yh = y.reshape(B, L, n_heads, HEAD_DIM)
    mean = jnp.mean(yh, axis=-1, keepdims=True)
    var = jnp.mean(jnp.square(yh - mean), axis=-1, keepdims=True)
    yn = ((yh - mean) * lax.rsqrt(var + RW_GN_EPS)).reshape(B, L, C) * ln_w + ln_b
    bonus = jnp.sum((r * k * r_k).reshape(B, L, n_heads, HEAD_DIM), axis=-1, keepdims=True)
    bonus = (bonus * v.reshape(B, L, n_heads, HEAD_DIM)).reshape(B, L, C)
    return (yn + bonus) * g


def _rwkv_branch(f_ctx, f_lat, width, mu, w0, w2, a0, a2, g2, k_k, k_a, r_k, ln_w, ln_b, need_ctx):
    B = f_lat.shape[0]
    n_heads = width // HEAD_DIM
    y_lat = 0.0
    y_ctx = 0.0 if need_ctx else None
    for d in range(2):
        rev = d == 1
        dir_args = (width, mu[d], w0[d], w2[d], a0[d], a2[d], g2[d], k_k, k_a)
        r_c, w_c, kk_c, a_c, k_c, v_c, g_c = _rwkv_prepare(f_ctx, rev, *dir_args)
        z0 = jnp.zeros((B, n_heads, HEAD_DIM, HEAD_DIM), F32)
        z_ctx, o_c = _wkv7(r_c, w_c, kk_c, a_c, k_c, v_c, z0, rev)
        r_l, w_l, kk_l, a_l, k_l, v_l, g_l = _rwkv_prepare(f_lat, rev, *dir_args)
        _, o_l = _wkv7(r_l, w_l, kk_l, a_l, k_l, v_l, z_ctx, rev)
        y_lat = y_lat + _rwkv_readout(o_l, r_l, k_l, v_l, g_l, r_k, ln_w, ln_b)
        if need_ctx:
            y_ctx = y_ctx + _rwkv_readout(o_c, r_c, k_c, v_c, g_c, r_k, ln_w, ln_b)
    return y_lat, y_ctx


def _rwkv_branch2(f_ctx, f_lat, B, L, C, width, mu, w0, w2, a0, a2, g2, k_k, k_a, r_k, ln_w, ln_b, need_ctx):
    n_heads = width // HEAD_DIM
    feat = mu.shape[-1]
    dr, ar = RW_DECAY_RANK, RW_A_RANK
    wt = jnp.zeros((2, feat - 3 * width, 3 * width), F32)
    wt = wt.at[:, :dr, :width].set(w2).at[:, dr:dr + ar, width:2 * width].set(a2).at[:, dr + ar:, 2 * width:].set(g2)
    row = lambda t: t.astype(F32).reshape(1, width)
    args = (mu.astype(F32).reshape(2, 1, feat), wt.astype(BF16), w0.astype(F32).reshape(2, 1, width),
            a0.astype(F32).reshape(2, 1, width), row(k_k), row(k_a))
    pre_c = rwkv_pre(f_ctx, *args, C, width)
    pre_l = rwkv_pre(f_lat, *args, L, width)
    read_l, read_c = [], []
    for d in range(2):
        rev = d == 1
        seq = lambda ts, n: [t.reshape(B, n, width) for t in ts[:6]]
        z0 = jnp.zeros((B, n_heads, HEAD_DIM, HEAD_DIM), F32)
        z_ctx, o_c = _wkv7(*seq(pre_c[d], C), z0, rev)
        _, o_l = _wkv7(*seq(pre_l[d], L), z_ctx, rev)
        pick = lambda o, ts, n: (o.reshape(B * n, width), ts[0], ts[4], ts[5], ts[6])
        read_l.append(pick(o_l, pre_l[d], L))
        read_c.append(pick(o_c, pre_c[d], C))
    tail = (row(r_k), row(ln_w), row(ln_b))
    y_lat = rwkv_readout(read_l[0], read_l[1], *tail)
    y_ctx = rwkv_readout(read_c[0], read_c[1], *tail) if need_ctx else None
    return y_lat, y_ctx


def _diff_project(pd, q_g, k_g):
    B, L, w3 = pd.shape
    n_heads = w3 // (3 * 2 * HEAD_DIM)
    q, k, v = jnp.split(pd, 3, axis=-1)
    q = _rms_norm(q.reshape(B, L, n_heads, 2, HEAD_DIM), q_g)
    k = _rms_norm(k.reshape(B, L, n_heads, 2, HEAD_DIM), k_g)
    return q, k, v.reshape(B, L, n_heads, 2 * HEAD_DIM)


def _diff_maps(q, k_all, v_all, lam):
    s = jnp.einsum('bqhid,bkhid->bhiqk', q, k_all).astype(F32) * HEAD_DIM ** -0.5
    p = jax.nn.softmax(s, axis=-1)
    a = p[:, :, 0] - lam * p[:, :, 1]
    return jnp.einsum('bhqk,bkhe->bqhe', a.astype(v_all.dtype), v_all)


def _diff_latent(q, k, v, kc, vc, lam):
    B, L, n_heads = q.shape[:3]
    nb = L // DF_BLOCK
    k_all = jnp.concatenate([kc, k], axis=1)
    v_all = jnp.concatenate([vc, v], axis=1)
    qb = jnp.moveaxis(q.reshape(B, nb, DF_BLOCK, n_heads, 2, HEAD_DIM), 1, 0)
    o = lax.map(lambda q_blk: _diff_maps(q_blk, k_all, v_all, lam), qb)
    return jnp.moveaxis(o, 0, 1).reshape(B, L, n_heads, 2 * HEAD_DIM)


def _diff_readout(o, sub_g, lam_init):
    B, L = o.shape[:2]
    return (_rms_norm(o, sub_g) * (1 - lam_init)).reshape(B, L, -1)


def _swa_branch(pb_l, pb_c, cos, sin, q_g, k_g, sink, need_ctx):
    B, L = pb_l.shape[:2]
    C = pb_c.shape[1]
    n_heads = pb_l.shape[-1] // HEAD_DIM - 2 * SWA_KV_HEADS
    grp = n_heads // SWA_KV_HEADS
    scale = HEAD_DIM ** -0.5
    q_l, k_l, v_l = _swa_project(pb_l, q_g, k_g, n_heads)
    q_l, k_l = _apply_rope(q_l, cos, sin), _apply_rope(k_l, cos, sin)
    q_c, k_c, v_c = _swa_project(pb_c, q_g, k_g, n_heads)

    def q_layout(q, n):
        return jnp.transpose((q * scale).astype(BF16).reshape(B, n, SWA_KV_HEADS, grp, HEAD_DIM), (0, 2, 3, 1, 4))

    def kv_layout(t):
        return jnp.transpose(t.astype(BF16), (0, 2, 1, 3))

    def o_layout(o, n):
        return jnp.transpose(o, (0, 3, 1, 2, 4)).reshape(B, n, n_heads * HEAD_DIM)

    sink_rows = jnp.repeat(sink.astype(F32).reshape(SWA_KV_HEADS, grp), SWA_BLOCK, axis=1)[..., None]
    kc, vc = kv_layout(k_c), kv_layout(v_c)
    y_l = o_layout(swa_attention(q_layout(q_l, L), kv_layout(k_l), kv_layout(v_l), kc, vc, sink_rows), L)
    y_c = None
    if need_ctx:
        y_c = o_layout(swa_context_attention(q_layout(q_c, C), kc, vc, sink_rows), C)
    return y_l, y_c


def _diff_branch(pd_l, pd_c, cos, sin, q_g, k_g, sub_g, lam, lam_init, need_ctx):
    B, L = pd_l.shape[:2]
    C = pd_c.shape[1]
    W = pd_l.shape[-1] // 3
    scale = HEAD_DIM ** -0.5 * math.log2(math.e)
    dq_l, dk_l, dv_l = _diff_project(pd_l, q_g, k_g)
    dq_l, dk_l = _apply_rope(dq_l, cos, sin), _apply_rope(dk_l, cos, sin)
    dq_c, dk_c, dv_c = _diff_project(pd_c, q_g, k_g)
    flat = lambda t, n: t.reshape(B, n, W)
    q_l = (flat(dq_l, L) * scale).astype(BF16)
    k_c, v_c = flat(dk_c, C).astype(BF16), flat(dv_c, C).astype(BF16)
    kt_c = jnp.transpose(k_c, (0, 2, 1))
    kt_all = jnp.concatenate([kt_c, jnp.transpose(flat(dk_l, L).astype(BF16), (0, 2, 1))], axis=2)
    v_all = jnp.concatenate([v_c, flat(dv_l, L).astype(BF16)], axis=1)
    sub = sub_g.astype(F32).reshape(1, 2 * HEAD_DIM)
    lam2 = lam.astype(F32).reshape(1, 1)
    y_l = diff_attention(q_l, kt_all, v_all, sub, lam2, 1.0 - lam_init)
    y_c = None
    if need_ctx:
        q_c = (flat(dq_c, C) * scale).astype(BF16)
        y_c = diff_attention(q_c, kt_c, v_c, sub, lam2, 1.0 - lam_init)
    return y_l, y_c


def _rope_lane_tables(rows, wb):
    cos, sin = _rope_tables(rows)
    reps = wb // (HEAD_DIM // 2)
    sign = jnp.where(jnp.arange(wb) % HEAD_DIM < HEAD_DIM // 2, -1.0, 1.0).astype(F32)
    return jnp.tile(cos, (1, reps)), jnp.tile(sin, (1, reps)) * sign


def _lane_params(parts, wb):
    cols = [jnp.tile(jnp.asarray(val, F32).reshape(-1), n // jnp.asarray(val).size) for val, n in parts]
    return jnp.concatenate(cols).reshape(-1, 1, wb)


def _swa_branch2(p_l, p_c, col0, B, L, C, tables, q_g, k_g, sink, need_ctx):
    n_heads = sink.shape[0]
    grp = n_heads // SWA_KV_HEADS
    qw, kw = n_heads * HEAD_DIM, SWA_KV_HEADS * HEAD_DIM
    wb = 2 * kw
    gain = _lane_params([(q_g, qw), (k_g, kw), (1.0, kw)], wb)
    mask = _lane_params([(1.0, qw), (1.0, kw), (0.0, kw)], wb)
    scale = _lane_params([(HEAD_DIM ** -0.5, qw), (1.0, kw), (1.0, kw)], wb)
    o_l = head_prep(p_l, col0, wb, gain, mask, scale, tables, L).reshape(B, L, qw + 2 * kw)
    o_c = head_prep(p_c, col0, wb, gain, mask, scale, None, B * C).reshape(B, C, qw + 2 * kw)

    def q_layout(o, n):
        return jnp.transpose(o[:, :, :qw].reshape(B, n, SWA_KV_HEADS, grp, HEAD_DIM), (0, 2, 3, 1, 4))

    def kv_layout(t, n):
        return jnp.transpose(t.reshape(B, n, SWA_KV_HEADS, HEAD_DIM), (0, 2, 1, 3))

    def o_layout(o, n):
        return jnp.transpose(o, (0, 3, 1, 2, 4)).reshape(B * n, qw)

    sink_rows = jnp.repeat(sink.astype(F32).reshape(SWA_KV_HEADS, grp), SWA_BLOCK, axis=1)[..., None]
    kc, vc = kv_layout(o_c[:, :, qw:qw + kw], C), kv_layout(o_c[:, :, qw + kw:], C)
    y_l = o_layout(swa_attention(q_layout(o_l, L), kv_layout(o_l[:, :, qw:qw + kw], L),
                                 kv_layout(o_l[:, :, qw + kw:], L), kc, vc, sink_rows), L)
    y_c = o_layout(swa_context_attention(q_layout(o_c, C), kc, vc, sink_rows), C) if need_ctx else None
    return y_l, y_c


def _diff_branch2(p_l, p_c, col0, B, L, C, tables, q_g, k_g, sub_g, lam, lam_init, need_ctx):
    W = tables[0].shape[1]
    hw = 2 * HEAD_DIM
    q_scale = HEAD_DIM ** -0.5 * math.log2(math.e)
    gain = _lane_params([(q_g, W), (k_g, W), (1.0, W)], W)
    mask = _lane_params([(1.0, W), (1.0, W), (0.0, W)], W)
    scale = _lane_params([(q_scale, W), (1.0, W), (1.0, W)], W)
    o_l = head_prep(p_l, col0, W, gain, mask, scale, tables, L).reshape(B, L, 3 * W)
    o_c = head_prep(p_c, col0, W, gain, mask, scale, None, B * C).reshape(B, C, 3 * W)
    kv_all = jnp.concatenate([o_c, o_l], axis=1)
    kt_all = jnp.transpose(kv_all[:, :, W:2 * W], (0, 2, 1))
    sub = sub_g.astype(F32).reshape(1, hw)
    lam2 = lam.astype(F32).reshape(1, 1)
    v_blk0 = 2 * W // hw
    y_l = diff_attention(o_l, 0, kt_all, kv_all, v_blk0, sub, lam2, 1.0 - lam_init).reshape(B * L, W)
    y_c = None
    if need_ctx:
        y_c = diff_attention(o_c, 0, kt_all[:, :, :C], o_c, v_blk0, sub, lam2, 1.0 - lam_init).reshape(B * C, W)
    return y_l, y_c


def _moe_route(logits_g, logits_e, n_tok):
    g_idx = jnp.argmax(logits_g, axis=-1)
    g_prob = jnp.take_along_axis(jax.nn.softmax(logits_g, axis=-1), g_idx[:, None], axis=-1)[:, 0]
    e_logits = logits_e.reshape(n_tok, MOE_GROUPS, MOE_PER_GROUP)
    e_logits = jnp.take_along_axis(e_logits, g_idx[:, None, None], axis=1)[:, 0]
    top_p, top_e = lax.top_k(jax.nn.softmax(e_logits, axis=-1), MOE_TOP_K)
    weights = g_prob[:, None] * top_p / jnp.sum(top_p, axis=-1, keepdims=True)
    flat_e = (g_idx[:, None] * MOE_PER_GROUP + top_e).reshape(-1).astype(jnp.int32)
    n_assign = n_tok * MOE_TOP_K
    order = jnp.argsort(flat_e).astype(jnp.int32)
    se = flat_e[order]
    counts = jnp.sum(flat_e[:, None] == jnp.arange(MOE_EXPERTS, dtype=jnp.int32)[None, :], axis=0, dtype=jnp.int32)
    padded = (counts + MOE_BLOCK - 1) // MOE_BLOCK * MOE_BLOCK
    pad_end = jnp.cumsum(padded)
    pad_start = pad_end - padded
    start = jnp.cumsum(counts) - counts
    dest = (pad_start[se] + jnp.arange(n_assign, dtype=jnp.int32) - start[se]).astype(jnp.int32)
    n_blocks = -(-n_assign // MOE_BLOCK) + MOE_EXPERTS
    P = n_blocks * MOE_BLOCK
    p = jnp.arange(P, dtype=jnp.int32)
    e_p = jnp.minimum(jnp.sum(p[:, None] >= pad_end[None, :], axis=1, dtype=jnp.int32), MOE_EXPERTS - 1)
    off = p - pad_start[e_p]
    src = jnp.clip(start[e_p] + off, 0, n_assign - 1)
    tok_buf = jnp.where(off < counts[e_p], order[src] // MOE_TOP_K, n_tok).astype(jnp.int32)
    blk_e = e_p[::MOE_BLOCK]
    n_used = (pad_end[-1] // MOE_BLOCK).astype(jnp.int32).reshape(1)
    _, slot = lax.sort((order, dest), num_keys=1)
    return tok_buf, blk_e, n_used, slot.reshape(n_tok, MOE_TOP_K), weights


def _hier_moe(tokens_bf16, logits, w1, w3, w2):
    n_tok, d = tokens_bf16.shape
    logits_g = logits[:, :MOE_GROUPS]
    logits_e = logits[:, MOE_GROUPS:MOE_GROUPS + MOE_EXPERTS]
    tok_buf, blk_e, n_used, slot, weights = _moe_route(logits_g, logits_e, n_tok)
    x_pad = jnp.concatenate([tokens_bf16, jnp.zeros((1, d), BF16)], axis=0)
    xb = x_pad[tok_buf]
    yb = moe_experts(xb, blk_e, n_used, w1, w3, w2)
    wts = weights.astype(F32)
    return (yb[slot[:, 0]].astype(F32) * wts[:, 0:1] + yb[slot[:, 1]].astype(F32) * wts[:, 1:2])


def _modulate(x, g, shift, scale):
    return _rms_norm(x, g) * (1 + scale) + shift


def kernel(x, c, ctx, c_ctx, mod_w, mod_b, norm1_g, norm2_g, w_in, hy_conv_w, hy_conv_b, hy_f_w1, hy_f_b1, hy_f_w2, hy_f_b2, hy_f_w3, hy_f_b3, hy_f_freq, hy_f_decay, hy_skip, swa_q_g, swa_k_g, swa_sink, rw_mu, rw_w0, rw_w2, rw_a0, rw_a2, rw_g2, rw_k_k, rw_k_a, rw_r_k, rw_ln_w, rw_ln_b, df_q_g, df_k_g, df_lq1, df_lk1, df_lq2, df_lk2, df_sub_g, w_gate, b_gate, w_branch, w_out, moe_rg_w, moe_rg_b, moe_re_w, moe_re_b, moe_w1, moe_w3, moe_w2):
    B, L, D = x.shape
    C = ctx.shape[1]
    depth = mod_w.shape[0]
    bw = D // N_BRANCH
    a_cols = 3 * bw
    swa_heads = bw // HEAD_DIM
    b_cols = (swa_heads + 2 * SWA_KV_HEADS) * HEAD_DIM
    c_cols = 3 * bw + RW_DECAY_RANK + RW_A_RANK + RW_G_RANK
    in_splits = (a_cols, a_cols + b_cols, a_cols + b_cols + c_cols)
    a_off, b_off, d_off = c_cols, c_cols + a_cols, c_cols + a_cols + b_cols
    rows = L // GRID_W
    swa_tables = _rope_lane_tables(rows, 2 * SWA_KV_HEADS * HEAD_DIM)
    diff_tables = _rope_lane_tables(rows, bw)
    hp = lax.Precision.HIGHEST

    x_lat = x.reshape(B * L, D)
    x_ctx = ctx.reshape(B * C, D)
    for l in range(depth):
        need_ctx = l < depth - 1
        mod_lat = jnp.dot(jax.nn.silu(c), mod_w[l], precision=hp) + mod_b[l]
        mod_ctx = jnp.dot(jax.nn.silu(c_ctx)[None], mod_w[l], precision=hp) + mod_b[l]
        sh1, sc1, g1, sh2, sc2, g2 = [t[:, None, :] for t in jnp.split(mod_lat, 6, axis=-1)]
        csh1, csc1, cg1, csh2, csc2, cg2 = [t[:, None, :] for t in jnp.split(mod_ctx, 6, axis=-1)]

        w_l = w_in[l]
        w_in_b = jnp.concatenate([w_l[:, in_splits[1]:in_splits[2]], w_l[:, :in_splits[1]], w_l[:, in_splits[2]:]],
                                 axis=1).astype(BF16)
        p2_lat, h_lat = norm_proj(x_lat, norm1_g[l][None], sc1, sh1, w_in_b, rows_per_mod=L)
        p2_ctx, h_ctx = norm_proj(x_ctx, norm1_g[l][None], csc1, csh1, w_in_b, rows_per_mod=B * C)
        pa_l = p2_lat.reshape(B, L, -1)[..., a_off:a_off + a_cols]
        pa_c = p2_ctx.reshape(B, C, -1)[..., a_off:a_off + a_cols]

        hy_args = (hy_conv_w[l], hy_conv_b[l], hy_f_w1[l], hy_f_b1[l], hy_f_w2[l], hy_f_b2[l], hy_f_w3[l],
                   hy_f_b3[l], hy_f_freq[l], hy_f_decay[l], hy_skip[l])
        ya_l = _hyena(pa_l, *hy_args)

        yb_l, yb_c = _swa_branch2(p2_lat, p2_ctx, b_off, B, L, C, swa_tables, swa_q_g[l], swa_k_g[l],
                                  swa_sink[l], need_ctx)

        yc_l, yc_c = _rwkv_branch2(p2_ctx, p2_lat, B, L, C, bw, rw_mu[l], rw_w0[l], rw_w2[l], rw_a0[l], rw_a2[l],
                                   rw_g2[l], rw_k_k[l], rw_k_a[l], rw_r_k[l], rw_ln_w[l], rw_ln_b[l], need_ctx)

        lam_init = 0.8 - 0.6 * math.exp(-0.3 * l)
        lam = (jnp.exp(jnp.sum(df_lq1[l] * df_lk1[l])) - jnp.exp(jnp.sum(df_lq2[l] * df_lk2[l])) + lam_init)
        yd_l, yd_c = _diff_branch2(p2_lat, p2_ctx, d_off, B, L, C, diff_tables, df_q_g[l], df_k_g[l],
                                   df_sub_g[l], lam, lam_init, need_ctx)

        wg_b = w_gate[l].astype(BF16)
        bg = b_gate[l][:, None, :]
        wb_b = w_branch[l].astype(BF16)
        wo_b = w_out[l].astype(BF16)
        ys_l = [t.reshape(B * L, bw) for t in (ya_l, yb_l, yc_l, yd_l)]
        acc_l = merge_gated(h_lat, ys_l, wg_b, bg, wb_b)
        x_lat = resid_proj(x_lat, acc_l, wo_b, g1, rows_per_mod=L)
        if need_ctx:
            ya_c = _hyena(pa_c, *hy_args)
            ys_c = [t.reshape(B * C, bw) for t in (ya_c, yb_c, yc_c, yd_c)]
            acc_c = merge_gated(h_ctx, ys_c, wg_b, bg, wb_b)
            x_ctx = resid_proj(x_ctx, acc_c, wo_b, cg1, rows_per_mod=B * C)

        w1_b, w3_b, w2_b = moe_w1[l].astype(BF16), moe_w3[l].astype(BF16), moe_w2[l].astype(BF16)
        n_route = MOE_GROUPS + MOE_EXPERTS
        wr = jnp.zeros((D, ROUTER_LANES), F32).at[:, :MOE_GROUPS].set(moe_rg_w[l]).at[:, MOE_GROUPS:n_route].set(moe_re_w[l])
        br = jnp.zeros((1, ROUTER_LANES), F32).at[0, :MOE_GROUPS].set(moe_rg_b[l]).at[0, MOE_GROUPS:n_route].set(moe_re_b[l])
        g2n = norm2_g[l][None]
        hm_lat, lg_lat = norm_route(x_lat, g2n, sc2, sh2, wr, br, rows_per_mod=L)
        if need_ctx:
            hm_ctx, lg_ctx = norm_route(x_ctx, g2n, csc2, csh2, wr, br, rows_per_mod=B * C)
            out = _hier_moe(jnp.concatenate([hm_ctx, hm_lat], axis=0), jnp.concatenate([lg_ctx, lg_lat], axis=0),
                            w1_b, w3_b, w2_b)
            x_ctx = x_ctx + (cg2 * out[:B * C].reshape(1, B * C, D)).reshape(B * C, D)
            x_lat = x_lat + (g2 * out[B * C:].reshape(B, L, D)).reshape(B * L, D)
        else:
            out = _hier_moe(hm_lat, lg_lat, w1_b, w3_b, w2_b)
            x_lat = x_lat + (g2 * out.reshape(B, L, D)).reshape(B * L, D)
    return x_lat.reshape(B, L, D)
```

```python
import functools
import math

import jax
import jax.numpy as jnp
from jax import lax
from jax.experimental import pallas as pl
from jax.experimental.pallas import tpu as pltpu

F32 = jnp.float32
BF16 = jnp.bfloat16

GRID_W = 64
HEAD_DIM = 64
ROPE_BASE = 10000.0
NORM_EPS = 1e-6
NEG_INF = -1e30
N_BRANCH = 4
HY_ORDER = 2
HY_POS_BANDS = 8
SWA_KV_HEADS = 2
SWA_WINDOW = 128
SWA_BLOCK = 128
RW_DECAY_RANK = 64
RW_A_RANK = 64
RW_G_RANK = 128
RW_GN_EPS = 64e-5
DF_BLOCK = 128
DF_ROW_CHUNK = 256
MOE_GROUPS = 4
MOE_PER_GROUP = 8
MOE_EXPERTS = MOE_GROUPS * MOE_PER_GROUP
MOE_TOP_K = 2
MOE_BLOCK = 256

VMEM_LIMIT_BYTES = 56 * 1024 * 1024


def _row_tile(m, pref):
    t = min(pref, m)
    while m % t:
        t //= 2
    return t


def _params(sem):
    return pltpu.CompilerParams(dimension_semantics=sem, vmem_limit_bytes=VMEM_LIMIT_BYTES)


def _norm_proj_body(x_ref, g_ref, sc_ref, sh_ref, w_ref, p_ref, h_ref, h_scr):
    @pl.when(pl.program_id(1) == 0)
    def _():
        x = x_ref[...].astype(F32)
        y = x * lax.rsqrt(jnp.mean(x * x, axis=-1, keepdims=True) + NORM_EPS)
        h = y * g_ref[...] * (1.0 + sc_ref[...]) + sh_ref[...]
        h_scr[...] = h.astype(BF16)
        h_ref[...] = h_scr[...]

    p_ref[...] = jnp.dot(h_scr[...], w_ref[...], preferred_element_type=F32).astype(p_ref.dtype)


def norm_proj(x, g, scale, shift, w, *, rows_per_mod, tm=1024, tn=512, out_dtype=F32):
    m, d = x.shape
    n = w.shape[1]
    tm = _row_tile(rows_per_mod, tm)
    tn = _row_tile(n, tn)
    tiles_per_mod = rows_per_mod // tm
    mod_map = lambda i, j: (i // tiles_per_mod, 0, 0)
    return pl.pallas_call(
        _norm_proj_body,
        grid=(m // tm, n // tn),
        in_specs=[
            pl.BlockSpec((tm, d), lambda i, j: (i, 0)),
            pl.BlockSpec((1, d), lambda i, j: (0, 0)),
            pl.BlockSpec((None, 1, d), mod_map),
            pl.BlockSpec((None, 1, d), mod_map),
            pl.BlockSpec((d, tn), lambda i, j: (0, j)),
        ],
        out_specs=[
            pl.BlockSpec((tm, tn), lambda i, j: (i, j)),
            pl.BlockSpec((tm, d), lambda i, j: (i, 0)),
        ],
        out_shape=[jax.ShapeDtypeStruct((m, n), out_dtype), jax.ShapeDtypeStruct((m, d), BF16)],
        scratch_shapes=[pltpu.VMEM((tm, d), BF16)],
        compiler_params=_params(("parallel", "arbitrary")),
        name="norm_proj",
    )(x, g, scale, shift, w)


ROUTER_LANES = 128


def _norm_route_body(x_ref, g_ref, sc_ref, sh_ref, wr_ref, br_ref, h_ref, lg_ref):
    x = x_ref[...].astype(F32)
    y = x * lax.rsqrt(jnp.mean(x * x, axis=-1, keepdims=True) + NORM_EPS)
    h = y * g_ref[...] * (1.0 + sc_ref[...]) + sh_ref[...]
    h_ref[...] = h.astype(h_ref.dtype)
    lg_ref[...] = jnp.dot(h, wr_ref[...], preferred_element_type=F32, precision=lax.Precision.HIGHEST) + br_ref[...]


def norm_route(x, g, scale, shift, wr, br, *, rows_per_mod, tm=512):
    m, d = x.shape
    tm = _row_tile(rows_per_mod, tm)
    tiles_per_mod = rows_per_mod // tm
    mod_map = lambda i: (i // tiles_per_mod, 0, 0)
    return pl.pallas_call(
        _norm_route_body,
        grid=(m // tm,),
        in_specs=[
            pl.BlockSpec((tm, d), lambda i: (i, 0)),
            pl.BlockSpec((1, d), lambda i: (0, 0)),
            pl.BlockSpec((None, 1, d), mod_map),
            pl.BlockSpec((None, 1, d), mod_map),
            pl.BlockSpec((d, ROUTER_LANES), lambda i: (0, 0)),
            pl.BlockSpec((1, ROUTER_LANES), lambda i: (0, 0)),
        ],
        out_specs=[pl.BlockSpec((tm, d), lambda i: (i, 0)), pl.BlockSpec((tm, ROUTER_LANES), lambda i: (i, 0))],
        out_shape=[jax.ShapeDtypeStruct((m, d), BF16), jax.ShapeDtypeStruct((m, ROUTER_LANES), F32)],
        compiler_params=_params(("parallel",)),
        name="norm_route",
    )(x, g, scale, shift, wr, br)


def _merge_body(h_ref, y0_ref, y1_ref, y2_ref, y3_ref, wg_ref, bg_ref, wb_ref, o_ref):
    h = h_ref[...]
    acc = None
    for b, y_ref in enumerate((y0_ref, y1_ref, y2_ref, y3_ref)):
        gate = jax.nn.sigmoid(jnp.dot(h, wg_ref[b], preferred_element_type=F32) + bg_ref[b])
        val = gate * jnp.dot(y_ref[...].astype(BF16), wb_ref[b], preferred_element_type=F32)
        acc = val if acc is None else acc + val
    o_ref[...] = acc.astype(o_ref.dtype)


def merge_gated(h, ys, wg, bg, wb, *, tm=1024, tn=256):
    m, d = h.shape
    w = ys[0].shape[-1]
    tm = _row_tile(m, tm)
    tn = _row_tile(d, tn)
    yspec = pl.BlockSpec((tm, w), lambda i, j: (i, 0))
    return pl.pallas_call(
        _merge_body,
        grid=(m // tm, d // tn),
        in_specs=[
            pl.BlockSpec((tm, d), lambda i, j: (i, 0)),
            yspec, yspec, yspec, yspec,
            pl.BlockSpec((N_BRANCH, d, tn), lambda i, j: (0, 0, j)),
            pl.BlockSpec((N_BRANCH, 1, tn), lambda i, j: (0, 0, j)),
            pl.BlockSpec((N_BRANCH, w, tn), lambda i, j: (0, 0, j)),
        ],
        out_specs=pl.BlockSpec((tm, tn), lambda i, j: (i, j)),
        out_shape=jax.ShapeDtypeStruct((m, d), BF16),
        compiler_params=_params(("parallel", "arbitrary")),
        name="merge_gated",
    )(h, *ys, wg, bg, wb)


def _resid_proj_body(x_ref, a_ref, w_ref, gate_ref, o_ref):
    y = jnp.dot(a_ref[...], w_ref[...], preferred_element_type=F32)
    o_ref[...] = (x_ref[...].astype(F32) + gate_ref[...] * y).astype(o_ref.dtype)


def resid_proj(x, a, w, gate, *, rows_per_mod, tm=1024, tn=512):
    m, d = x.shape
    k = a.shape[1]
    tm = _row_tile(rows_per_mod, tm)
    tn = _row_tile(d, tn)
    tiles_per_mod = rows_per_mod // tm
    return pl.pallas_call(
        _resid_proj_body,
        grid=(m // tm, d // tn),
        in_specs=[
            pl.BlockSpec((tm, tn), lambda i, j: (i, j)),
            pl.BlockSpec((tm, k), lambda i, j: (i, 0)),
            pl.BlockSpec((k, tn), lambda i, j: (0, j)),
            pl.BlockSpec((None, 1, tn), lambda i, j: (i // tiles_per_mod, 0, j)),
        ],
        out_specs=pl.BlockSpec((tm, tn), lambda i, j: (i, j)),
        out_shape=jax.ShapeDtypeStruct((m, d), x.dtype),
        compiler_params=_params(("parallel", "arbitrary")),
        name="resid_proj",
    )(x, a, w, gate)


def _moe_body(blk_e_ref, n_used_ref, x_ref, w1_ref, w3_ref, w2_ref, o_ref):
    i = pl.program_id(0)

    @pl.when(i < n_used_ref[0])
    def _():
        x = x_ref[...]
        a = jnp.dot(x, w1_ref[...], preferred_element_type=F32)
        b = jnp.dot(x, w3_ref[...], preferred_element_type=F32)
        hdn = (a * jax.nn.sigmoid(a) * b).astype(BF16)
        o_ref[...] = jnp.dot(hdn, w2_ref[...], preferred_element_type=F32).astype(o_ref.dtype)

    @pl.when(i >= n_used_ref[0])
    def _():
        o_ref[...] = jnp.zeros_like(o_ref)


def moe_experts(xb, blk_e, n_used, w1, w3, w2):
    p, d = xb.shape
    hid = w1.shape[-1]
    n_blocks = p // MOE_BLOCK
    grid_spec = pltpu.PrefetchScalarGridSpec(
        num_scalar_prefetch=2,
        grid=(n_blocks,),
        in_specs=[
            pl.BlockSpec((MOE_BLOCK, d), lambda i, e, n: (i, 0)),
            pl.BlockSpec((None, d, hid), lambda i, e, n: (e[i], 0, 0)),
            pl.BlockSpec((None, d, hid), lambda i, e, n: (e[i], 0, 0)),
            pl.BlockSpec((None, hid, d), lambda i, e, n: (e[i], 0, 0)),
        ],
        out_specs=pl.BlockSpec((MOE_BLOCK, d), lambda i, e, n: (i, 0)),
    )
    return pl.pallas_call(
        _moe_body,
        grid_spec=grid_spec,
        out_shape=jax.ShapeDtypeStruct((p, d), BF16),
        compiler_params=_params(("arbitrary",)),
        name="moe_experts",
    )(blk_e, n_used, xb, w1, w3, w2)


RW_CHUNK = 64
_HI = lax.Precision.HIGHEST
_NT = (((1,), (1,)), ((), ()))
_TN = (((0,), (0,)), ((), ()))


def _dot(a, b, dims=None):
    if dims is None:
        return jnp.dot(a, b, preferred_element_type=F32, precision=_HI)
    return lax.dot_general(a, b, dims, preferred_element_type=F32, precision=_HI)


def _bdot(a, b, dims=None):
    a, b = a.astype(BF16), b.astype(BF16)
    if dims is None:
        return jnp.dot(a, b, preferred_element_type=F32)
    return lax.dot_general(a, b, dims, preferred_element_type=F32)


def _rwkv_prep_body(r_ref, lw_ref, kk_ref, a_ref, k_ref, v_ref, p_ref, g_ref, q_ref, yl_ref, *, reverse):
    T = r_ref.shape[0]
    hd = HEAD_DIM
    row = lax.broadcasted_iota(jnp.int32, (T, T), 0)
    col = lax.broadcasted_iota(jnp.int32, (T, T), 1)
    if reverse:
        strict, incl = col > row, col >= row
    else:
        strict, incl = col < row, col <= row
    lw = lw_ref[...]
    tri = incl.astype(BF16)
    lw1 = lw.astype(BF16)
    res1 = lw - lw1.astype(F32)
    lw2 = res1.astype(BF16)
    lw3 = (res1 - lw2.astype(F32)).astype(BF16)
    cum = (jnp.dot(tri, lw1, preferred_element_type=F32) + jnp.dot(tri, lw2, preferred_element_type=F32)
           + jnp.dot(tri, lw3, preferred_element_type=F32))
    total = jnp.sum(lw, axis=0, keepdims=True)
    e_in = jnp.exp(cum)
    e_ex = jnp.exp(cum - lw)
    e_ninv = jnp.exp(-cum)
    e_rem = jnp.exp(total - cum)
    gam = jnp.exp(total)
    kk = kk_ref[...]
    kka = kk * a_ref[...]
    k = k_ref[...]
    nt = -kk * e_ex
    rt = r_ref[...] * e_in
    at = kka * e_ninv
    kt = k * e_ninv
    ac = kka * e_rem
    kc = k * e_rem
    v = v_ref[...]
    eye = lax.broadcasted_iota(jnp.int32, (hd, hd), 0) == lax.broadcasted_iota(jnp.int32, (hd, hd), 1)
    n_heads = r_ref.shape[1] // hd
    zeros = jnp.zeros((T, hd), F32)
    steps = max(1, (T - 1).bit_length())
    heads = range(n_heads)
    sls = [slice(hh * hd, (hh + 1) * hd) for hh in heads]
    bigs = [_bdot(jnp.concatenate([nt[:, sl], rt[:, sl]], axis=0),
                  jnp.concatenate([at[:, sl], kt[:, sl]], axis=0), _NT) for sl in sls]
    a_ak = [jnp.where(strict, big[:T, T:], 0.0) for big in bigs]
    pws = [jnp.where(strict, big[:T, :T], 0.0) for big in bigs]
    lhs_top = [jnp.concatenate([jnp.where(incl, big[T:, :T], 0.0), jnp.where(incl, big[T:, T:], 0.0)], axis=1)
               for big in bigs]
    xs = [jnp.concatenate([nt[:, sl], _bdot(m, v[:, sl])], axis=1) for m, sl in zip(a_ak, sls)]
    for it in range(steps):
        if it < steps - 1:
            boths = [_bdot(pw, jnp.concatenate([x, pw], axis=1)) for pw, x in zip(pws, xs)]
            xs = [x + both[:, :2 * hd] for x, both in zip(xs, boths)]
            pws = [both[:, 2 * hd:] for both in boths]
        else:
            xs = [x + _bdot(pw, x) for pw, x in zip(pws, xs)]
    rhs = [jnp.concatenate([x, jnp.concatenate([zeros, v[:, sl]], axis=1)], axis=0) for x, sl in zip(xs, sls)]
    tops = [_bdot(lt, rh) for lt, rh in zip(lhs_top, rhs)]
    bots = [_bdot(jnp.concatenate([ac[:, sl], kc[:, sl]], axis=0), rh, _TN) for sl, rh in zip(sls, rhs)]
    for hh in heads:
        p_ref[hh] = jnp.where(eye, gam[:, sls[hh]], 0.0) + bots[hh][:, :hd]
        g_ref[hh] = bots[hh][:, hd:]
    q_ref[...] = jnp.concatenate([rt[:, sl] + top[:, :hd] for sl, top in zip(sls, tops)], axis=1)
    yl_ref[...] = jnp.concatenate([top[:, hd:] for top in tops], axis=1)


def rwkv_chunk_prep(r, logw, kk, a, k, v, *, reverse):
    B, L, W = r.shape
    T = RW_CHUNK
    nh = W // HEAD_DIM
    nc = L // T
    blk = pl.BlockSpec((None, T, W), lambda b, c: (b, c, 0))
    mat = pl.BlockSpec((None, None, nh, HEAD_DIM, HEAD_DIM), lambda b, c: (b, c, 0, 0, 0))
    mat_shape = jax.ShapeDtypeStruct((B, nc, nh, HEAD_DIM, HEAD_DIM), F32)
    seq_shape = jax.ShapeDtypeStruct((B, L, W), F32)
    return pl.pallas_call(
        functools.partial(_rwkv_prep_body, reverse=reverse),
        grid=(B, nc),
        in_specs=[blk] * 6,
        out_specs=[mat, mat, blk, blk],
        out_shape=[mat_shape, mat_shape, seq_shape, seq_shape],
        compiler_params=_params(("parallel", "parallel")),
        name="rwkv_chunk_prep",
    )(r, logw, kk, a, k, v)


def _rwkv_scan_body(p_ref, g_ref, q_ref, yl_ref, z0_ref, y_ref, zf_ref, z_scr):
    c = pl.program_id(1)

    @pl.when(c == 0)
    def _():
        z_scr[...] = z0_ref[...]

    hd = HEAD_DIM
    n_heads = z_scr.shape[0]
    q = q_ref[...]
    ys = []
    for h in range(n_heads):
        z = z_scr[h]
        ys.append(_dot(q[:, h * hd:(h + 1) * hd], z))
        z_scr[h] = _dot(p_ref[h], z) + g_ref[h]
    y_ref[...] = jnp.concatenate(ys, axis=1) + yl_ref[...]

    @pl.when(c == pl.num_programs(1) - 1)
    def _():
        zf_ref[...] = z_scr[...]


def rwkv_chunk_scan(p, g, qh, yl, z0, *, reverse):
    B, nc, nh = p.shape[:3]
    L, W = qh.shape[1:]
    T = L // nc
    cidx = (lambda c: nc - 1 - c) if reverse else (lambda c: c)
    mat = pl.BlockSpec((None, None, nh, HEAD_DIM, HEAD_DIM), lambda b, c: (b, cidx(c), 0, 0, 0))
    seq = pl.BlockSpec((None, T, W), lambda b, c: (b, cidx(c), 0))
    st = pl.BlockSpec((None, nh, HEAD_DIM, HEAD_DIM), lambda b, c: (b, 0, 0, 0))
    return pl.pallas_call(
        _rwkv_scan_body,
        grid=(B, nc),
        in_specs=[mat, mat, seq, seq, st],
        out_specs=[seq, st],
        out_shape=[jax.ShapeDtypeStruct((B, L, W), F32), jax.ShapeDtypeStruct(z0.shape, F32)],
        scratch_shapes=[pltpu.VMEM((nh, HEAD_DIM, HEAD_DIM), F32)],
        compiler_params=_params(("parallel", "arbitrary")),
        name="rwkv_chunk_scan",
    )(p, g, qh, yl, z0)


def _head_sum(x):
    lt = 2 * HEAD_DIM
    gi = lax.broadcasted_iota(jnp.int32, (lt, lt), 0) // HEAD_DIM
    gj = lax.broadcasted_iota(jnp.int32, (lt, lt), 1) // HEAD_DIM
    bd = (gi == gj).astype(BF16)
    hi = x.astype(BF16)
    lo = (x - hi.astype(F32)).astype(BF16)
    tiles = [jnp.dot(hi[:, t * lt:(t + 1) * lt], bd, preferred_element_type=F32)
             + jnp.dot(lo[:, t * lt:(t + 1) * lt], bd, preferred_element_type=F32) for t in range(x.shape[1] // lt)]
    return tiles[0] if len(tiles) == 1 else jnp.concatenate(tiles, axis=1)


def _rwkv_pre_body(x_ref, xp_ref, xn_ref, mu_ref, wt_ref, w0_ref, a0_ref, kk_ref, ka_ref, *out_refs, tiles_per_seq):
    i = pl.program_id(0)
    x = x_ref[...]
    tm, feat = x.shape
    w = kk_ref.shape[-1]
    tail = feat - 3 * w
    row = lax.broadcasted_iota(jnp.int32, (tm, feat), 0)
    pos = i % tiles_per_seq
    prev_row = jnp.where(pos == 0, 0.0, xp_ref[7:8, :])
    next_row = jnp.where(pos == tiles_per_seq - 1, 0.0, xn_ref[0:1, :])
    lane = lax.broadcasted_iota(jnp.int32, (tm, tail), 1)
    kscale = kk_ref[...]
    kmix = ka_ref[...]
    for d in range(2):
        if d == 0:
            sh = jnp.where(row == 0, prev_row, pltpu.roll(x, 1, axis=0))
        else:
            sh = jnp.where(row == tm - 1, next_row, pltpu.roll(x, tm - 1, axis=0))
        xs = x + mu_ref[d] * (sh - x)
        r, k, v, t = xs[:, :w], xs[:, w:2 * w], xs[:, 2 * w:3 * w], xs[:, 3 * w:]
        act = jnp.where(lane < RW_DECAY_RANK, jnp.tanh(t),
                        jnp.where(lane < RW_DECAY_RANK + RW_A_RANK, t, jax.nn.sigmoid(t)))
        hi = act.astype(BF16)
        lo = (act - hi.astype(F32)).astype(BF16)
        z = jnp.dot(hi, wt_ref[d], preferred_element_type=F32) + jnp.dot(lo, wt_ref[d], preferred_element_type=F32)
        logw = -math.exp(-0.5) * jax.nn.sigmoid(w0_ref[d] + z[:, :w])
        a = jax.nn.sigmoid(a0_ref[d] + z[:, w:2 * w])
        g = z[:, 2 * w:]
        kx = k * kscale
        kk = kx * lax.rsqrt(_head_sum(kx * kx) + 1e-12)
        k2 = k * (1.0 + (a - 1.0) * kmix)
        for ref, val in zip(out_refs[7 * d:7 * d + 7], (r, logw, kk, a, k2, v, g)):
            ref[...] = val


def rwkv_pre(feats, mu, wt, w0, a0, k_k, k_a, rows_per_seq, width):
    m = feats.shape[0]
    feat = mu.shape[-1]
    tm = _row_tile(rows_per_seq, 256)
    tiles_per_seq = rows_per_seq // tm
    r8 = tm // 8
    full = lambda shape: pl.BlockSpec(shape, lambda i: (0,) * len(shape))
    outs = pl.pallas_call(
        functools.partial(_rwkv_pre_body, tiles_per_seq=tiles_per_seq),
        grid=(m // tm,),
        in_specs=[
            pl.BlockSpec((tm, feat), lambda i: (i, 0)),
            pl.BlockSpec((8, feat), lambda i: (jnp.maximum(i * r8 - 1, 0), 0)),
            pl.BlockSpec((8, feat), lambda i: (jnp.minimum((i + 1) * r8, m // 8 - 1), 0)),
            full(mu.shape), full(wt.shape), full(w0.shape), full(a0.shape), full(k_k.shape), full(k_a.shape),
        ],
        out_specs=[pl.BlockSpec((tm, width), lambda i: (i, 0))] * 14,
        out_shape=[jax.ShapeDtypeStruct((m, width), F32)] * 14,
        compiler_params=_params(("parallel",)),
        name="rwkv_pre",
    )(feats, feats, feats, mu, wt, w0, a0, k_k, k_a)
    return outs[:7], outs[7:]


def _rwkv_readout_body(*refs):
    dirs, (rk_ref, lnw_ref, lnb_ref, o_ref) = (refs[0:5], refs[5:10]), refs[10:]
    inv = 1.0 / HEAD_DIM
    acc = None
    for y_ref, r_ref, k_ref, v_ref, g_ref in dirs:
        y = y_ref[...]
        c = y - _head_sum(y) * inv
        yn = c * lax.rsqrt(_head_sum(c * c) * inv + RW_GN_EPS) * lnw_ref[...] + lnb_ref[...]
        bonus = _head_sum(r_ref[...] * k_ref[...] * rk_ref[...]) * v_ref[...]
        val = (yn + bonus) * g_ref[...]
        acc = val if acc is None else acc + val
    o_ref[...] = acc


def rwkv_readout(fwd, bwd, r_k, ln_w, ln_b):
    m, w = fwd[0].shape
    tm = _row_tile(m, 512)
    blk = pl.BlockSpec((tm, w), lambda i: (i, 0))
    par = pl.BlockSpec((1, w), lambda i: (0, 0))
    return pl.pallas_call(
        _rwkv_readout_body,
        grid=(m // tm,),
        in_specs=[blk] * 10 + [par] * 3,
        out_specs=blk,
        out_shape=jax.ShapeDtypeStruct((m, w), F32),
        compiler_params=_params(("parallel",)),
        name="rwkv_readout",
    )(*fwd, *bwd, r_k, ln_w, ln_b)


def _head_prep_body(*refs, rope):
    if rope:
        x_ref, g_ref, mk_ref, sc_ref, cos_ref, sin_ref, o_ref = refs
    else:
        x_ref, g_ref, mk_ref, sc_ref, o_ref = refs
    x = x_ref[...]
    tm, wb = x.shape
    lt = 2 * HEAD_DIM
    gid = lax.broadcasted_iota(jnp.int32, (lt, lt), 0) // HEAD_DIM
    gjd = lax.broadcasted_iota(jnp.int32, (lt, lt), 1) // HEAD_DIM
    bd = (gid == gjd).astype(BF16)
    xx = x * x
    hi = xx.astype(BF16)
    lo = (xx - hi.astype(F32)).astype(BF16)
    ms = jnp.concatenate(
        [jnp.dot(hi[:, t * lt:(t + 1) * lt], bd, preferred_element_type=F32)
         + jnp.dot(lo[:, t * lt:(t + 1) * lt], bd, preferred_element_type=F32) for t in range(wb // lt)], axis=1)
    y = x * lax.rsqrt(ms * (1.0 / HEAD_DIM) + NORM_EPS) * g_ref[...]
    if rope:
        half = HEAD_DIM // 2
        lane = lax.broadcasted_iota(jnp.int32, (tm, wb), 1)
        partner = jnp.where(lane % HEAD_DIM < half, pltpu.roll(y, wb - half, axis=1), pltpu.roll(y, half, axis=1))
        y = y * cos_ref[...] + partner * sin_ref[...]
    y = jnp.where(mk_ref[...] > 0.0, y, x)
    o_ref[...] = (y * sc_ref[...]).astype(o_ref.dtype)


def head_prep(p, col0, wb, gain, mask, scale, tables, rows_per_seq):
    m = p.shape[0]
    nj = gain.shape[0]
    tm = _row_tile(rows_per_seq, 512)
    cb0 = col0 // wb
    tiles_per_seq = rows_per_seq // tm
    par = pl.BlockSpec((None, 1, wb), lambda i, j: (j, 0, 0))
    in_specs = [pl.BlockSpec((tm, wb), lambda i, j: (i, cb0 + j)), par, par, par]
    args = [p, gain, mask, scale]
    if tables is not None:
        tab = pl.BlockSpec((tm, wb), lambda i, j: (i % tiles_per_seq, 0))
        in_specs += [tab, tab]
        args += list(tables)
    return pl.pallas_call(
        functools.partial(_head_prep_body, rope=tables is not None),
        grid=(m // tm, nj),
        in_specs=in_specs,
        out_specs=pl.BlockSpec((tm, wb), lambda i, j: (i, j)),
        out_shape=jax.ShapeDtypeStruct((m, nj * wb), BF16),
        compiler_params=_params(("parallel", "parallel")),
        name="head_prep",
    )(*args)


def _diff_attn_body(lam_ref, q_ref, kt_ref, v_ref, subg_ref, o_ref, q2_scr, m_scr, acc_scr, *, out_scale):
    ki = pl.program_id(3)
    tq = q_ref.shape[0]

    @pl.when(ki == 0)
    def _():
        q = q_ref[...]
        lane = lax.broadcasted_iota(jnp.int32, q.shape, 1)
        q2_scr[0:tq, :] = jnp.where(lane < HEAD_DIM, q, jnp.zeros_like(q))
        q2_scr[tq:2 * tq, :] = jnp.where(lane >= HEAD_DIM, q, jnp.zeros_like(q))
        m_scr[...] = jnp.full_like(m_scr, -jnp.inf)
        acc_scr[...] = jnp.zeros_like(acc_scr)

    kt = kt_ref[...]
    v = jnp.concatenate([v_ref[...], jnp.ones(v_ref.shape, BF16)], axis=1)
    rc = min(DF_ROW_CHUNK, 2 * tq)
    n_chunks = 2 * tq // rc
    score = lambda c: jnp.dot(q2_scr[pl.ds(c * rc, rc), :], kt, preferred_element_type=F32)
    s_next = score(0)
    for c in range(n_chunks):
        rows = pl.ds(c * rc, rc)
        s = s_next
        if c + 1 < n_chunks:
            s_next = score(c + 1)
        m_prev = m_scr[rows, :]
        m_new = jnp.maximum(m_prev, jnp.max(s, axis=-1, keepdims=True))
        alpha = jnp.exp2(m_prev - m_new)
        p = jnp.exp2(s - m_new)
        acc_scr[rows, :] = alpha * acc_scr[rows, :] + jnp.dot(p.astype(BF16), v, preferred_element_type=F32)
        m_scr[rows, :] = m_new

    @pl.when(ki == pl.num_programs(3) - 1)
    def _():
        hw = o_ref.shape[-1]
        o = acc_scr[:, 0:hw] / acc_scr[:, hw:hw + 1]
        a = o[0:tq, :] - lam_ref[0, 0] * o[tq:2 * tq, :]
        y = a * lax.rsqrt(jnp.mean(a * a, axis=-1, keepdims=True) + NORM_EPS)
        o_ref[...] = (y * subg_ref[...] * out_scale).astype(o_ref.dtype)


def _key_tile(k, cap):
    best = 128
    t = 128
    while t <= min(k, cap):
        if k % t == 0:
            best = t
        t += 128
    return best


def diff_attention(q, q_blk0, kt, v, v_blk0, sub_g, lam, out_scale, *, tq=1024, tk_cap=1280):
    B, L = q.shape[:2]
    W, K = kt.shape[1:]
    hw = 2 * HEAD_DIM
    tq = _row_tile(L, tq)
    tk = _key_tile(K, tk_cap)
    return pl.pallas_call(
        functools.partial(_diff_attn_body, out_scale=out_scale),
        grid=(B, W // hw, L // tq, K // tk),
        in_specs=[
            pl.BlockSpec(memory_space=pltpu.SMEM),
            pl.BlockSpec((None, tq, hw), lambda b, h, i, j: (b, i, q_blk0 + h)),
            pl.BlockSpec((None, hw, tk), lambda b, h, i, j: (b, h, j)),
            pl.BlockSpec((None, tk, hw), lambda b, h, i, j: (b, j, v_blk0 + h)),
            pl.BlockSpec((1, hw), lambda b, h, i, j: (0, 0)),
        ],
        out_specs=pl.BlockSpec((None, tq, hw), lambda b, h, i, j: (b, i, h)),
        out_shape=jax.ShapeDtypeStruct((B, L, W), F32),
        scratch_shapes=[pltpu.VMEM((2 * tq, hw), BF16), pltpu.VMEM((2 * tq, 1), F32),
                        pltpu.VMEM((2 * tq, 2 * hw), F32)],
        compiler_params=_params(("parallel", "parallel", "parallel", "arbitrary")),
        name="diff_attention",
    )(lam, q, kt, v, sub_g)


def _swa_finish(parts, vals, sink, o_ref):
    m = sink
    for s in parts:
        m = jnp.maximum(m, jnp.max(s, axis=-1, keepdims=True))
    denom = jnp.exp(sink - m)
    acc = None
    for s, v in zip(parts, vals):
        p = jnp.exp(s - m)
        denom = denom + jnp.sum(p, axis=-1, keepdims=True)
        pv = jnp.dot(p.astype(BF16), v, preferred_element_type=F32)
        acc = pv if acc is None else acc + pv
    o = acc / denom
    o_ref[...] = o.reshape(o_ref.shape).astype(o_ref.dtype)


def _swa_band_body(q_ref, kp_ref, kn_ref, kx_ref, kc_ref, vp_ref, vn_ref, vx_ref, vc_ref, sink_ref, o_ref):
    n = pl.program_id(2)
    nb = pl.num_programs(2)
    grp, blk, hd = q_ref.shape
    q = q_ref[...].reshape(grp * blk, hd)
    iq = lax.broadcasted_iota(jnp.int32, (grp * blk, blk), 0) % blk
    j = lax.broadcasted_iota(jnp.int32, (grp * blk, blk), 1)
    s_prev = jnp.where((iq + blk - j <= SWA_WINDOW) & (n > 0), _dot_nt_bf16(q, kp_ref[...]), NEG_INF)
    s_cur = jnp.where(jnp.abs(iq - j) <= SWA_WINDOW, _dot_nt_bf16(q, kn_ref[...]), NEG_INF)
    s_next = jnp.where((j + blk - iq <= SWA_WINDOW) & (n < nb - 1), _dot_nt_bf16(q, kx_ref[...]), NEG_INF)
    s_ctx = _dot_nt_bf16(q, kc_ref[...])
    _swa_finish([s_prev, s_cur, s_next, s_ctx], [vp_ref[...], vn_ref[...], vx_ref[...], vc_ref[...]],
                sink_ref[...], o_ref)


def _swa_ctx_body(q_ref, kc_ref, vc_ref, sink_ref, o_ref):
    grp, blk, hd = q_ref.shape
    q = q_ref[...].reshape(grp * blk, hd)
    _swa_finish([_dot_nt_bf16(q, kc_ref[...])], [vc_ref[...]], sink_ref[...], o_ref)


def _dot_nt_bf16(a, b):
    return lax.dot_general(a, b, _NT, preferred_element_type=F32)


def swa_attention(q, k, v, kc, vc, sink_rows):
    B, kvh, grp, L, hd = q.shape
    C = kc.shape[2]
    blk = SWA_BLOCK
    nb = L // blk
    qspec = pl.BlockSpec((None, None, grp, blk, hd), lambda b, h, n: (b, h, 0, n, 0))
    prev = pl.BlockSpec((None, None, blk, hd), lambda b, h, n: (b, h, jnp.maximum(n - 1, 0), 0))
    cur = pl.BlockSpec((None, None, blk, hd), lambda b, h, n: (b, h, n, 0))
    nxt = pl.BlockSpec((None, None, blk, hd), lambda b, h, n: (b, h, jnp.minimum(n + 1, nb - 1), 0))
    cspec = pl.BlockSpec((None, None, C, hd), lambda b, h, n: (b, h, 0, 0))
    sspec = pl.BlockSpec((None, grp * blk, 1), lambda b, h, n: (h, 0, 0))
    return pl.pallas_call(
        _swa_band_body,
        grid=(B, kvh, nb),
        in_specs=[qspec, prev, cur, nxt, cspec, prev, cur, nxt, cspec, sspec],
        out_specs=qspec,
        out_shape=jax.ShapeDtypeStruct(q.shape, F32),
        compiler_params=_params(("parallel", "parallel", "parallel")),
        name="swa_attention",
    )(q, k, k, k, kc, v, v, v, vc, sink_rows)


def swa_context_attention(q, kc, vc, sink_rows):
    B, kvh, grp, L, hd = q.shape
    C = kc.shape[2]
    blk = SWA_BLOCK
    qspec = pl.BlockSpec((None, None, grp, blk, hd), lambda b, h, n: (b, h, 0, n, 0))
    cspec = pl.BlockSpec((None, None, C, hd), lambda b, h, n: (b, h, 0, 0))
    sspec = pl.BlockSpec((None, grp * blk, 1), lambda b, h, n: (h, 0, 0))
    return pl.pallas_call(
        _swa_ctx_body,
        grid=(B, kvh, L // blk),
        in_specs=[qspec, cspec, cspec, sspec],
        out_specs=qspec,
        out_shape=jax.ShapeDtypeStruct(q.shape, F32),
        compiler_params=_params(("parallel", "parallel", "parallel")),
        name="swa_context_attention",
    )(q, kc, vc, sink_rows)


HY_COL_GROUP = 8
HY_K2_GROUP = 8


def _hy_fwd_outer_body(x_ref, fc_ref, fsn_ref, o_ref):
    half, g, ch = x_ref.shape
    for j in range(g):
        x = x_ref[:, j, :].astype(BF16)
        cols = slice(j * ch, (j + 1) * ch)
        o_ref[0, :, cols] = jnp.dot(fc_ref[...], x, preferred_element_type=F32).astype(o_ref.dtype)
        o_ref[1, :, cols] = jnp.dot(fsn_ref[...], x, preferred_element_type=F32).astype(o_ref.dtype)


def _hy_spectral_body(a_ref, m_ref, mi_ref, h_ref, o_ref):
    n1 = a_ref.shape[2]
    for j in range(a_ref.shape[1]):
        b = jnp.concatenate([a_ref[0, j], a_ref[1, j]], axis=0)
        x = jnp.dot(m_ref[j], b, preferred_element_type=F32)
        xr, xi = x[:n1], x[n1:]
        hr, hi = h_ref[0, j], h_ref[1, j]
        y = jnp.concatenate([xr * hr - xi * hi, xr * hi + xi * hr], axis=0).astype(BF16)
        c = jnp.dot(mi_ref[j], y, preferred_element_type=F32)
        o_ref[0, j] = c[:n1].astype(o_ref.dtype)
        o_ref[1, j] = c[n1:].astype(o_ref.dtype)


def _hy_inv_outer_body(c_ref, gc_ref, gsn_ref, z_ref, gate_ref, skip_ref, o_ref):
    half, g, ch = z_ref.shape
    y = (jnp.dot(gc_ref[...], c_ref[0], preferred_element_type=F32)
         + jnp.dot(gsn_ref[...], c_ref[1], preferred_element_type=F32))
    for j in range(g):
        o_ref[:, j, :] = gate_ref[:, j, :] * (y[:, j * ch:(j + 1) * ch] + skip_ref[...] * z_ref[:, j, :])


def _hy_tables(n):
    n2 = 1 << (n.bit_length() // 2)
    n1 = n // n2
    two_pi = 2.0 * math.pi
    i2 = jnp.arange(n2, dtype=jnp.int32)
    ang2 = ((i2[:, None] * i2[None, :]) % n2).astype(F32) * (two_pi / n2)
    c2, s2 = jnp.cos(ang2), jnp.sin(ang2)
    half = n2 // 2
    fwd_c, fwd_sn = c2[:, :half].astype(BF16), (-s2[:, :half]).astype(BF16)
    inv_c, inv_sn = c2[:half, :].astype(BF16), (-s2[:half, :]).astype(BF16)
    i1 = jnp.arange(n1, dtype=jnp.int32)
    kk = n2 * i1[None, :, None] + i2[:, None, None]
    th = ((kk * i1[None, None, :]) % n).astype(F32) * (two_pi / n)
    ct, st = jnp.cos(th), jnp.sin(th)
    m_big = jnp.concatenate([jnp.concatenate([ct, st], axis=2), jnp.concatenate([-st, ct], axis=2)], axis=1)
    ctt, stt = jnp.swapaxes(ct, 1, 2), jnp.swapaxes(st, 1, 2)
    mi_big = jnp.concatenate([jnp.concatenate([ctt, -stt], axis=2), jnp.concatenate([stt, ctt], axis=2)], axis=1)
    return n1, n2, fwd_c, fwd_sn, inv_c, inv_sn, m_big.astype(BF16), mi_big.astype(BF16)


def _hy_spectrum_planes(spec, n, n1, n2):
    L = n // 2
    full = jnp.concatenate([spec, jnp.conj(spec[1:L][::-1])], axis=0) * (1.0 / n)
    o, c = full.shape[1:]
    full = jnp.transpose(full.reshape(n1, n2, o, c), (2, 1, 0, 3))
    return jnp.stack([jnp.real(full), jnp.imag(full)], axis=1).astype(F32)


def hyena_long_conv(z, gate, skip, h_planes, tables):
    B, L, C = z.shape
    n1, n2, fwd_c, fwd_sn, inv_c, inv_sn, m_big, mi_big = tables
    half = n2 // 2
    g = min(HY_COL_GROUP, n1)
    kb = min(HY_K2_GROUP, n2)
    gc = g * C
    z4 = z.reshape(B, half, n1, C)
    gate4 = gate.reshape(B, half, n1, C)
    seq_blk = pl.BlockSpec((None, half, g, C), lambda b, j: (b, 0, j, 0))
    plane = jax.ShapeDtypeStruct((B, 2, n2, n1 * C), BF16)
    plane_blk = pl.BlockSpec((None, 2, n2, gc), lambda b, j: (b, 0, 0, j))
    full_mat = lambda shape: pl.BlockSpec(shape, lambda b, j: (0,) * len(shape))
    a = pl.pallas_call(
        _hy_fwd_outer_body,
        grid=(B, n1 // g),
        in_specs=[seq_blk, full_mat((n2, half)), full_mat((n2, half))],
        out_specs=plane_blk,
        out_shape=plane,
        compiler_params=_params(("parallel", "parallel")),
        name="hy_fwd_outer",
    )(z4, fwd_c, fwd_sn)
    cc = pl.pallas_call(
        _hy_spectral_body,
        grid=(B, n2 // kb),
        in_specs=[
            pl.BlockSpec((None, 2, kb, n1, C), lambda b, j: (b, 0, j, 0, 0)),
            pl.BlockSpec((kb, 2 * n1, 2 * n1), lambda b, j: (j, 0, 0)),
            pl.BlockSpec((kb, 2 * n1, 2 * n1), lambda b, j: (j, 0, 0)),
            pl.BlockSpec((2, kb, n1, C), lambda b, j: (0, j, 0, 0)),
        ],
        out_specs=pl.BlockSpec((None, 2, kb, n1, C), lambda b, j: (b, 0, j, 0, 0)),
        out_shape=jax.ShapeDtypeStruct((B, 2, n2, n1, C), BF16),
        compiler_params=_params(("parallel", "parallel")),
        name="hy_spectral",
    )(a.reshape(B, 2, n2, n1, C), m_big, mi_big, h_planes)
    skip_row = skip.astype(F32).reshape(1, C)
    out = pl.pallas_call(
        _hy_inv_outer_body,
        grid=(B, n1 // g),
        in_specs=[
            plane_blk,
            full_mat((half, n2)), full_mat((half, n2)),
            seq_blk, seq_blk, full_mat((1, C)),
        ],
        out_specs=seq_blk,
        out_shape=jax.ShapeDtypeStruct((B, half, n1, C), F32),
        compiler_params=_params(("parallel", "parallel")),
        name="hy_inv_outer",
    )(cc.reshape(B, 2, n2, n1 * C), inv_c, inv_sn, z4, gate4, skip_row)
    return out.reshape(B, L, C)


def _split2(x):
    hi = x.astype(BF16)
    return hi, (x - hi.astype(F32)).astype(BF16)


def _hy_filt_outer_body(x_ref, fc_ref, fsn_ref, o_ref):
    hi, lo = _split2(x_ref[...])
    for plane, f_ref in enumerate((fc_ref, fsn_ref)):
        o_ref[plane] = (jnp.dot(f_ref[...], hi, preferred_element_type=F32)
                        + jnp.dot(f_ref[...], lo, preferred_element_type=F32))


def _hy_filt_inner_body(a_ref, m_ref, o_ref):
    n1 = a_ref.shape[2]
    n_order, ch = o_ref.shape[0], o_ref.shape[-1]
    for j in range(a_ref.shape[1]):
        hi, lo = _split2(jnp.concatenate([a_ref[0, j], a_ref[1, j]], axis=0))
        x = jnp.dot(m_ref[j], hi, preferred_element_type=F32) + jnp.dot(m_ref[j], lo, preferred_element_type=F32)
        xr, xi = x[:n1], x[n1:]
        for o in range(n_order):
            f, b = slice(2 * o * ch, (2 * o + 1) * ch), slice((2 * o + 1) * ch, (2 * o + 2) * ch)
            o_ref[o, 0, j] = xr[:, f] + xr[:, b]
            o_ref[o, 1, j] = xi[:, f] - xi[:, b]


def hyena_filter_planes(hs, tables, n_order, ch):
    L, hf = hs.shape
    n1, n2, fwd_c, fwd_sn, _, _, m_big, _ = tables
    half = n2 // 2
    g = min(2, n1)
    kb = min(2, n2)
    full_mat = lambda shape: pl.BlockSpec(shape, lambda j: (0,) * len(shape))
    a = pl.pallas_call(
        _hy_filt_outer_body,
        grid=(n1 // g,),
        in_specs=[pl.BlockSpec((half, g * hf), lambda j: (0, j)), full_mat((n2, half)), full_mat((n2, half))],
        out_specs=pl.BlockSpec((2, n2, g * hf), lambda j: (0, 0, j)),
        out_shape=jax.ShapeDtypeStruct((2, n2, n1 * hf), F32),
        compiler_params=_params(("parallel",)),
        name="hy_filt_outer",
    )(hs.reshape(half, n1 * hf), fwd_c, fwd_sn)
    return pl.pallas_call(
        _hy_filt_inner_body,
        grid=(n2 // kb,),
        in_specs=[pl.BlockSpec((2, kb, n1, hf), lambda j: (0, j, 0, 0)),
                  pl.BlockSpec((kb, 2 * n1, 2 * n1), lambda j: (j, 0, 0))],
        out_specs=pl.BlockSpec((n_order, 2, kb, n1, ch), lambda j: (0, 0, j, 0, 0)),
        out_shape=jax.ShapeDtypeStruct((n_order, 2, n2, n1, ch), F32),
        compiler_params=_params(("parallel",)),
        name="hy_filt_inner",
    )(a.reshape(2, n2, n1, hf), m_big)


def _hyena_filter_taps(L, ch, f_w1, f_b1, f_w2, f_b2, f_w3, f_b3, f_freq, f_decay):
    t = jnp.arange(L, dtype=F32) / max(L - 1, 1)
    ang = 2 * math.pi * t[:, None] * jnp.arange(1, HY_POS_BANDS + 1, dtype=F32)
    feat = jnp.concatenate([t[:, None], jnp.sin(ang), jnp.cos(ang)], axis=-1)
    hp = lax.Precision.HIGHEST
    h = jnp.sin(f_freq * (jnp.dot(feat, f_w1, precision=hp) + f_b1))
    h = jnp.sin(f_freq * (jnp.dot(h, f_w2, precision=hp) + f_b2))
    h = jnp.dot(h, f_w3, precision=hp) + f_b3
    h = h * jnp.exp(-f_decay * t[:, None])
    h = h.reshape(L, HY_ORDER, 2, ch)
    keep = jnp.ones((L, 1, 2, 1), F32).at[0, 0, 1, 0].set(0.0)
    h = h * keep
    l1 = jnp.sum(jnp.abs(h), axis=(0, 2), keepdims=True)
    return (h / (l1 * (2.0 * L))).reshape(L, HY_ORDER * 2 * ch)


def _hyena(pa, conv_w, conv_b, f_w1, f_b1, f_w2, f_b2, f_w3, f_b3, f_freq, f_decay, skip):
    B, L, _ = pa.shape
    ch = skip.shape[-1]
    n = 2 * L
    u = _short_conv3(pa, conv_w, conv_b).astype(F32)
    v, x1, x2 = jnp.split(u, 3, axis=-1)
    tables = _hy_tables(n)
    taps = _hyena_filter_taps(L, ch, f_w1, f_b1, f_w2, f_b2, f_w3, f_b3, f_freq, f_decay)
    h = hyena_filter_planes(taps, tables, HY_ORDER, ch)
    z = hyena_long_conv(v, x1, skip[0], h[0], tables)
    return hyena_long_conv(z, x2, skip[1], h[1], tables)


def _rms_norm(x, g):
    xf = x.astype(F32)
    y = xf * lax.rsqrt(jnp.mean(xf * xf, axis=-1, keepdims=True) + NORM_EPS)
    return (y * g.astype(F32)).astype(x.dtype)


def _rope_tables(rows):
    n_freq = HEAD_DIM // 4
    inv = ROPE_BASE ** (-jnp.arange(n_freq, dtype=F32) / n_freq)
    row = jnp.repeat(jnp.arange(rows, dtype=F32), GRID_W)
    col = jnp.tile(jnp.arange(GRID_W, dtype=F32), rows)
    ang = jnp.concatenate([row[:, None] * inv, col[:, None] * inv], axis=-1)
    return jnp.cos(ang), jnp.sin(ang)


def _apply_rope(x, cos, sin):
    shp = (x.shape[1],) + (1,) * (x.ndim - 3) + (HEAD_DIM // 2,)
    c, s = cos.reshape(shp), sin.reshape(shp)
    x1, x2 = jnp.split(x.astype(F32), 2, axis=-1)
    return jnp.concatenate([x1 * c - x2 * s, x2 * c + x1 * s], axis=-1).astype(x.dtype)


def _short_conv3(x, w, b):
    xp = jnp.pad(x, ((0, 0), (1, 1), (0, 0)))
    return xp[:, :-2] * w[0] + xp[:, 1:-1] * w[1] + xp[:, 2:] * w[2] + b


def _hyena_filter_spectra(L, ch, f_w1, f_b1, f_w2, f_b2, f_w3, f_b3, f_freq, f_decay):
    t = jnp.arange(L, dtype=F32) / max(L - 1, 1)
    ang = 2 * math.pi * t[:, None] * jnp.arange(1, HY_POS_BANDS + 1, dtype=F32)
    feat = jnp.concatenate([t[:, None], jnp.sin(ang), jnp.cos(ang)], axis=-1)
    hp = lax.Precision.HIGHEST
    h = jnp.sin(f_freq * (jnp.dot(feat, f_w1, precision=hp) + f_b1))
    h = jnp.sin(f_freq * (jnp.dot(h, f_w2, precision=hp) + f_b2))
    h = jnp.dot(h, f_w3, precision=hp) + f_b3
    h = h * jnp.exp(-f_decay * t[:, None])
    h = h.reshape(L, HY_ORDER, 2, ch)
    h_fwd, h_bwd = h[:, :, 0], h[:, :, 1]
    l1 = jnp.sum(jnp.abs(h_fwd), axis=0) + jnp.sum(jnp.abs(h_bwd[1:]), axis=0)
    kern = jnp.concatenate([h_fwd, jnp.zeros((1, HY_ORDER, ch), F32), h_bwd[1:][::-1]], axis=0) / l1
    return jnp.fft.rfft(kern, axis=0)


def _fft_long_conv(z, spec, skip):
    L = z.shape[1]
    zf = jnp.fft.rfft(z, n=2 * L, axis=1)
    y = jnp.fft.irfft(zf * spec[None], n=2 * L, axis=1)[:, :L]
    return y + skip * z


def _hyena_branch(pa, conv_w, conv_b, spec, skip):
    u = _short_conv3(pa, conv_w, conv_b).astype(F32)
    v, x1, x2 = jnp.split(u, 3, axis=-1)
    z = x1 * _fft_long_conv(v, spec[:, 0], skip[0])
    return x2 * _fft_long_conv(z, spec[:, 1], skip[1])


def _swa_project(pb, q_g, k_g, n_heads):
    B, L = pb.shape[:2]
    q, k, v = jnp.split(pb, [n_heads * HEAD_DIM, (n_heads + SWA_KV_HEADS) * HEAD_DIM], axis=-1)
    q = _rms_norm(q.reshape(B, L, n_heads, HEAD_DIM), q_g)
    k = _rms_norm(k.reshape(B, L, SWA_KV_HEADS, HEAD_DIM), k_g)
    return q, k, v.reshape(B, L, SWA_KV_HEADS, HEAD_DIM)


def _sink_softmax(s, sink):
    m = jnp.maximum(jnp.max(s, axis=-1, keepdims=True), sink)
    p = jnp.exp(s - m)
    return p / (jnp.sum(p, axis=-1, keepdims=True) + jnp.exp(sink - m))


def _swa_context(qc, kc, vc, sink):
    B, C, n_heads = qc.shape[:3]
    grp = n_heads // SWA_KV_HEADS
    qg = qc.reshape(B, C, SWA_KV_HEADS, grp, HEAD_DIM)
    s = jnp.einsum('bqhgd,bkhd->bhgqk', qg, kc).astype(F32) * HEAD_DIM ** -0.5
    p = _sink_softmax(s, sink.astype(F32).reshape(1, SWA_KV_HEADS, grp, 1, 1))
    o = jnp.einsum('bhgqk,bkhd->bqhgd', p.astype(vc.dtype), vc)
    return o.reshape(B, C, n_heads * HEAD_DIM)


def _swa_latent(q, k, v, kc, vc, sink):
    B, L, n_heads = q.shape[:3]
    grp = n_heads // SWA_KV_HEADS
    nb = L // SWA_BLOCK
    qb = q.reshape(B, nb, SWA_BLOCK, SWA_KV_HEADS, grp, HEAD_DIM)

    def band(t):
        tb = t.reshape(B, nb, SWA_BLOCK, SWA_KV_HEADS, HEAD_DIM)
        tp = jnp.pad(tb, ((0, 0), (1, 1), (0, 0), (0, 0), (0, 0)))
        return jnp.concatenate([tp[:, :-2], tp[:, 1:-1], tp[:, 2:]], axis=2)

    kb, vb = band(k), band(v)
    scale = HEAD_DIM ** -0.5
    s_loc = jnp.einsum('bnqhgd,bnkhd->bnhgqk', qb, kb).astype(F32) * scale
    s_ctx = jnp.einsum('bnqhgd,bchd->bnhgqc', qb, kc).astype(F32) * scale
    q_rel = jnp.arange(SWA_BLOCK)[:, None] + SWA_BLOCK
    k_rel = jnp.arange(3 * SWA_BLOCK)[None, :]
    k_abs = (jnp.arange(nb)[:, None, None] - 1) * SWA_BLOCK + k_rel[None]
    valid = (jnp.abs(q_rel - k_rel) <= SWA_WINDOW)[None] & (k_abs >= 0) & (k_abs < L)
    s_loc = jnp.where(valid[None, :, None, None], s_loc, NEG_INF)
    s = jnp.concatenate([s_loc, s_ctx], axis=-1)
    p = _sink_softmax(s, sink.astype(F32).reshape(1, 1, SWA_KV_HEADS, grp, 1, 1)).astype(v.dtype)
    o = (jnp.einsum('bnhgqk,bnkhd->bnqhgd', p[..., :3 * SWA_BLOCK], vb)
         + jnp.einsum('bnhgqc,bchd->bnqhgd', p[..., 3 * SWA_BLOCK:], vc))
    return o.reshape(B, L, n_heads * HEAD_DIM)


def _token_shift(x, reverse):
    if reverse:
        return jnp.pad(x, ((0, 0), (0, 1), (0, 0)))[:, 1:]
    return jnp.pad(x, ((0, 0), (1, 0), (0, 0)))[:, :-1]


def _head_l2norm(x, n_heads):
    B, L, C = x.shape
    xh = x.reshape(B, L, n_heads, HEAD_DIM)
    xh = xh * lax.rsqrt(jnp.sum(xh * xh, axis=-1, keepdims=True) + 1e-12)
    return xh.reshape(B, L, C)


def _rwkv_prepare(feats, reverse, width, mu, w0, w2, a0, a2, g2, k_k, k_a):
    n_heads = width // HEAD_DIM
    xs = feats + mu * (_token_shift(feats, reverse) - feats)
    splits = (width, 2 * width, 3 * width, 3 * width + RW_DECAY_RANK, 3 * width + RW_DECAY_RANK + RW_A_RANK)
    r, k, v, wd, ad, gd = jnp.split(xs, splits, axis=-1)
    logw = -jnp.exp(-jax.nn.softplus(-(w0 + jnp.tanh(wd) @ w2)) - 0.5)
    a = jax.nn.sigmoid(a0 + ad @ a2)
    g = jax.nn.sigmoid(gd) @ g2
    kk = _head_l2norm(k * k_k, n_heads)
    k = k * (1 + (a - 1) * k_a)
    return r, logw, kk, a, k, v, g


def _wkv7(r, logw, kk, a, k, v, z0, reverse):
    p, g, qh, yl = rwkv_chunk_prep(r, logw, kk, a, k, v, reverse=reverse)
    y, z_fin = rwkv_chunk_scan(p, g, qh, yl, z0, reverse=reverse)
    return z_fin, y


def _rwkv_readout(y, r, k, v, g, r_k, ln_w, ln_b):
    B, L, C = y.shape
    n_heads = C // HEAD_DIM
    yh = y.reshape(B, L, n_heads, HEAD_DIM)
    mean = jnp.mean(yh, axis=-1, keepdims=True)
    var = jnp.mean(jnp.square(yh - mean), axis=-1, keepdims=True)
    yn = ((yh - mean) * lax.rsqrt(var + RW_GN_EPS)).reshape(B, L, C) * ln_w + ln_b
    bonus = jnp.sum((r * k * r_k).reshape(B, L, n_heads, HEAD_DIM), axis=-1, keepdims=True)
    bonus = (bonus * v.reshape(B, L, n_heads, HEAD_DIM)).reshape(B, L, C)
    return (yn + bonus) * g


def _rwkv_branch(f_ctx, f_lat, width, mu, w0, w2, a0, a2, g2, k_k, k_a, r_k, ln_w, ln_b, need_ctx):
    B = f_lat.shape[0]
    n_heads = width // HEAD_DIM
    y_lat = 0.0
    y_ctx = 0.0 if need_ctx else None
    for d in range(2):
        rev = d == 1
        dir_args = (width, mu[d], w0[d], w2[d], a0[d], a2[d], g2[d], k_k, k_a)
        r_c, w_c, kk_c, a_c, k_c, v_c, g_c = _rwkv_prepare(f_ctx, rev, *dir_args)
        z0 = jnp.zeros((B, n_heads, HEAD_DIM, HEAD_DIM), F32)
        z_ctx, o_c = _wkv7(r_c, w_c, kk_c, a_c, k_c, v_c, z0, rev)
        r_l, w_l, kk_l, a_l, k_l, v_l, g_l = _rwkv_prepare(f_lat, rev, *dir_args)
        _, o_l = _wkv7(r_l, w_l, kk_l, a_l, k_l, v_l, z_ctx, rev)
        y_lat = y_lat + _rwkv_readout(o_l, r_l, k_l, v_l, g_l, r_k, ln_w, ln_b)
        if need_ctx:
            y_ctx = y_ctx + _rwkv_readout(o_c, r_c, k_c, v_c, g_c, r_k, ln_w, ln_b)
    return y_lat, y_ctx


def _rwkv_branch2(f_ctx, f_lat, B, L, C, width, mu, w0, w2, a0, a2, g2, k_k, k_a, r_k, ln_w, ln_b, need_ctx):
    n_heads = width // HEAD_DIM
    feat = mu.shape[-1]
    dr, ar = RW_DECAY_RANK, RW_A_RANK
    wt = jnp.zeros((2, feat - 3 * width, 3 * width), F32)
    wt = wt.at[:, :dr, :width].set(w2).at[:, dr:dr + ar, width:2 * width].set(a2).at[:, dr + ar:, 2 * width:].set(g2)
    row = lambda t: t.astype(F32).reshape(1, width)
    args = (mu.astype(F32).reshape(2, 1, feat), wt.astype(BF16), w0.astype(F32).reshape(2, 1, width),
            a0.astype(F32).reshape(2, 1, width), row(k_k), row(k_a))
    pre_c = rwkv_pre(f_ctx, *args, C, width)
    pre_l = rwkv_pre(f_lat, *args, L, width)
    read_l, read_c = [], []
    for d in range(2):
        rev = d == 1
        seq = lambda ts, n: [t.reshape(B, n, width) for t in ts[:6]]
        z0 = jnp.zeros((B, n_heads, HEAD_DIM, HEAD_DIM), F32)
        z_ctx, o_c = _wkv7(*seq(pre_c[d], C), z0, rev)
        _, o_l = _wkv7(*seq(pre_l[d], L), z_ctx, rev)
        pick = lambda o, ts, n: (o.reshape(B * n, width), ts[0], ts[4], ts[5], ts[6])
        read_l.append(pick(o_l, pre_l[d], L))
        read_c.append(pick(o_c, pre_c[d], C))
    tail = (row(r_k), row(ln_w), row(ln_b))
    y_lat = rwkv_readout(read_l[0], read_l[1], *tail)
    y_ctx = rwkv_readout(read_c[0], read_c[1], *tail) if need_ctx else None
    return y_lat, y_ctx


def _diff_project(pd, q_g, k_g):
    B, L, w3 = pd.shape
    n_heads = w3 // (3 * 2 * HEAD_DIM)
    q, k, v = jnp.split(pd, 3, axis=-1)
    q = _rms_norm(q.reshape(B, L, n_heads, 2, HEAD_DIM), q_g)
    k = _rms_norm(k.reshape(B, L, n_heads, 2, HEAD_DIM), k_g)
    return q, k, v.reshape(B, L, n_heads, 2 * HEAD_DIM)


def _diff_maps(q, k_all, v_all, lam):
    s = jnp.einsum('bqhid,bkhid->bhiqk', q, k_all).astype(F32) * HEAD_DIM ** -0.5
    p = jax.nn.softmax(s, axis=-1)
    a = p[:, :, 0] - lam * p[:, :, 1]
    return jnp.einsum('bhqk,bkhe->bqhe', a.astype(v_all.dtype), v_all)


def _diff_latent(q, k, v, kc, vc, lam):
    B, L, n_heads = q.shape[:3]
    nb = L // DF_BLOCK
    k_all = jnp.concatenate([kc, k], axis=1)
    v_all = jnp.concatenate([vc, v], axis=1)
    qb = jnp.moveaxis(q.reshape(B, nb, DF_BLOCK, n_heads, 2, HEAD_DIM), 1, 0)
    o = lax.map(lambda q_blk: _diff_maps(q_blk, k_all, v_all, lam), qb)
    return jnp.moveaxis(o, 0, 1).reshape(B, L, n_heads, 2 * HEAD_DIM)


def _diff_readout(o, sub_g, lam_init):
    B, L = o.shape[:2]
    return (_rms_norm(o, sub_g) * (1 - lam_init)).reshape(B, L, -1)


def _swa_branch(pb_l, pb_c, cos, sin, q_g, k_g, sink, need_ctx):
    B, L = pb_l.shape[:2]
    C = pb_c.shape[1]
    n_heads = pb_l.shape[-1] // HEAD_DIM - 2 * SWA_KV_HEADS
    grp = n_heads // SWA_KV_HEADS
    scale = HEAD_DIM ** -0.5
    q_l, k_l, v_l = _swa_project(pb_l, q_g, k_g, n_heads)
    q_l, k_l = _apply_rope(q_l, cos, sin), _apply_rope(k_l, cos, sin)
    q_c, k_c, v_c = _swa_project(pb_c, q_g, k_g, n_heads)

    def q_layout(q, n):
        return jnp.transpose((q * scale).astype(BF16).reshape(B, n, SWA_KV_HEADS, grp, HEAD_DIM), (0, 2, 3, 1, 4))

    def kv_layout(t):
        return jnp.transpose(t.astype(BF16), (0, 2, 1, 3))

    def o_layout(o, n):
        return jnp.transpose(o, (0, 3, 1, 2, 4)).reshape(B, n, n_heads * HEAD_DIM)

    sink_rows = jnp.repeat(sink.astype(F32).reshape(SWA_KV_HEADS, grp), SWA_BLOCK, axis=1)[..., None]
    kc, vc = kv_layout(k_c), kv_layout(v_c)
    y_l = o_layout(swa_attention(q_layout(q_l, L), kv_layout(k_l), kv_layout(v_l), kc, vc, sink_rows), L)
    y_c = None
    if need_ctx:
        y_c = o_layout(swa_context_attention(q_layout(q_c, C), kc, vc, sink_rows), C)
    return y_l, y_c


def _diff_branch(pd_l, pd_c, cos, sin, q_g, k_g, sub_g, lam, lam_init, need_ctx):
    B, L = pd_l.shape[:2]
    C = pd_c.shape[1]
    W = pd_l.shape[-1] // 3
    scale = HEAD_DIM ** -0.5 * math.log2(math.e)
    dq_l, dk_l, dv_l = _diff_project(pd_l, q_g, k_g)
    dq_l, dk_l = _apply_rope(dq_l, cos, sin), _apply_rope(dk_l, cos, sin)
    dq_c, dk_c, dv_c = _diff_project(pd_c, q_g, k_g)
    flat = lambda t, n: t.reshape(B, n, W)
    q_l = (flat(dq_l, L) * scale).astype(BF16)
    k_c, v_c = flat(dk_c, C).astype(BF16), flat(dv_c, C).astype(BF16)
    kt_c = jnp.transpose(k_c, (0, 2, 1))
    kt_all = jnp.concatenate([kt_c, jnp.transpose(flat(dk_l, L).astype(BF16), (0, 2, 1))], axis=2)
    v_all = jnp.concatenate([v_c, flat(dv_l, L).astype(BF16)], axis=1)
    sub = sub_g.astype(F32).reshape(1, 2 * HEAD_DIM)
    lam2 = lam.astype(F32).reshape(1, 1)
    y_l = diff_attention(q_l, kt_all, v_all, sub, lam2, 1.0 - lam_init)
    y_c = None
    if need_ctx:
        q_c = (flat(dq_c, C) * scale).astype(BF16)
        y_c = diff_attention(q_c, kt_c, v_c, sub, lam2, 1.0 - lam_init)
    return y_l, y_c


def _rope_lane_tables(rows, wb):
    cos, sin = _rope_tables(rows)
    reps = wb // (HEAD_DIM // 2)
    sign = jnp.where(jnp.arange(wb) % HEAD_DIM < HEAD_DIM // 2, -1.0, 1.0).astype(F32)
    return jnp.tile(cos, (1, reps)), jnp.tile(sin, (1, reps)) * sign


def _lane_params(parts, wb):
    cols = [jnp.tile(jnp.asarray(val, F32).reshape(-1), n // jnp.asarray(val).size) for val, n in parts]
    return jnp.concatenate(cols).reshape(-1, 1, wb)


def _swa_branch2(p_l, p_c, col0, B, L, C, tables, q_g, k_g, sink, need_ctx):
    n_heads = sink.shape[0]
    grp = n_heads // SWA_KV_HEADS
    qw, kw = n_heads * HEAD_DIM, SWA_KV_HEADS * HEAD_DIM
    wb = 2 * kw
    gain = _lane_params([(q_g, qw), (k_g, kw), (1.0, kw)], wb)
    mask = _lane_params([(1.0, qw), (1.0, kw), (0.0, kw)], wb)
    scale = _lane_params([(HEAD_DIM ** -0.5, qw), (1.0, kw), (1.0, kw)], wb)
    o_l = head_prep(p_l, col0, wb, gain, mask, scale, tables, L).reshape(B, L, qw + 2 * kw)
    o_c = head_prep(p_c, col0, wb, gain, mask, scale, None, B * C).reshape(B, C, qw + 2 * kw)

    def q_layout(o, n):
        return jnp.transpose(o[:, :, :qw].reshape(B, n, SWA_KV_HEADS, grp, HEAD_DIM), (0, 2, 3, 1, 4))

    def kv_layout(t, n):
        return jnp.transpose(t.reshape(B, n, SWA_KV_HEADS, HEAD_DIM), (0, 2, 1, 3))

    def o_layout(o, n):
        return jnp.transpose(o, (0, 3, 1, 2, 4)).reshape(B * n, qw)

    sink_rows = jnp.repeat(sink.astype(F32).reshape(SWA_KV_HEADS, grp), SWA_BLOCK, axis=1)[..., None]
    kc, vc = kv_layout(o_c[:, :, qw:qw + kw], C), kv_layout(o_c[:, :, qw + kw:], C)
    y_l = o_layout(swa_attention(q_layout(o_l, L), kv_layout(o_l[:, :, qw:qw + kw], L),
                                 kv_layout(o_l[:, :, qw + kw:], L), kc, vc, sink_rows), L)
    y_c = o_layout(swa_context_attention(q_layout(o_c, C), kc, vc, sink_rows), C) if need_ctx else None
    return y_l, y_c


def _diff_branch2(p_l, p_c, col0, B, L, C, tables, q_g, k_g, sub_g, lam, lam_init, need_ctx):
    W = tables[0].shape[1]
    hw = 2 * HEAD_DIM
    q_scale = HEAD_DIM ** -0.5 * math.log2(math.e)
    gain = _lane_params([(q_g, W), (k_g, W), (1.0, W)], W)
    mask = _lane_params([(1.0, W), (1.0, W), (0.0, W)], W)
    scale = _lane_params([(q_scale, W), (1.0, W), (1.0, W)], W)
    o_l = head_prep(p_l, col0, W, gain, mask, scale, tables, L).reshape(B, L, 3 * W)
    o_c = head_prep(p_c, col0, W, gain, mask, scale, None, B * C).reshape(B, C, 3 * W)
    kv_all = jnp.concatenate([o_c, o_l], axis=1)
    kt_all = jnp.transpose(kv_all[:, :, W:2 * W], (0, 2, 1))
    sub = sub_g.astype(F32).reshape(1, hw)
    lam2 = lam.astype(F32).reshape(1, 1)
    v_blk0 = 2 * W // hw
    y_l = diff_attention(o_l, 0, kt_all, kv_all, v_blk0, sub, lam2, 1.0 - lam_init).reshape(B * L, W)
    y_c = None
    if need_ctx:
        y_c = diff_attention(o_c, 0, kt_all[:, :, :C], o_c, v_blk0, sub, lam2, 1.0 - lam_init).reshape(B * C, W)
    return y_l, y_c


def _moe_route(logits_g, logits_e, n_tok):
    g_idx = jnp.argmax(logits_g, axis=-1)
    g_prob = jnp.take_along_axis(jax.nn.softmax(logits_g, axis=-1), g_idx[:, None], axis=-1)[:, 0]
    e_logits = logits_e.reshape(n_tok, MOE_GROUPS, MOE_PER_GROUP)
    e_logits = jnp.take_along_axis(e_logits, g_idx[:, None, None], axis=1)[:, 0]
    top_p, top_e = lax.top_k(jax.nn.softmax(e_logits, axis=-1), MOE_TOP_K)
    weights = g_prob[:, None] * top_p / jnp.sum(top_p, axis=-1, keepdims=True)
    flat_e = (g_idx[:, None] * MOE_PER_GROUP + top_e).reshape(-1).astype(jnp.int32)
    n_assign = n_tok * MOE_TOP_K
    order = jnp.argsort(flat_e).astype(jnp.int32)
    se = flat_e[order]
    counts = jnp.sum(flat_e[:, None] == jnp.arange(MOE_EXPERTS, dtype=jnp.int32)[None, :], axis=0, dtype=jnp.int32)
    padded = (counts + MOE_BLOCK - 1) // MOE_BLOCK * MOE_BLOCK
    pad_end = jnp.cumsum(padded)
    pad_start = pad_end - padded
    start = jnp.cumsum(counts) - counts
    dest = (pad_start[se] + jnp.arange(n_assign, dtype=jnp.int32) - start[se]).astype(jnp.int32)
    n_blocks = -(-n_assign // MOE_BLOCK) + MOE_EXPERTS
    P = n_blocks * MOE_BLOCK
    p = jnp.arange(P, dtype=jnp.int32)
    e_p = jnp.minimum(jnp.sum(p[:, None] >= pad_end[None, :], axis=1, dtype=jnp.int32), MOE_EXPERTS - 1)
    off = p - pad_start[e_p]
    src = jnp.clip(start[e_p] + off, 0, n_assign - 1)
    tok_buf = jnp.where(off < counts[e_p], order[src] // MOE_TOP_K, n_tok).astype(jnp.int32)
    blk_e = e_p[::MOE_BLOCK]
    n_used = (pad_end[-1] // MOE_BLOCK).astype(jnp.int32).reshape(1)
    _, slot = lax.sort((order, dest), num_keys=1)
    return tok_buf, blk_e, n_used, slot.reshape(n_tok, MOE_TOP_K), weights


def _hier_moe(tokens_bf16, logits, w1, w3, w2):
    n_tok, d = tokens_bf16.shape
    logits_g = logits[:, :MOE_GROUPS]
    logits_e = logits[:, MOE_GROUPS:MOE_GROUPS + MOE_EXPERTS]
    tok_buf, blk_e, n_used, slot, weights = _moe_route(logits_g, logits_e, n_tok)
    x_pad = jnp.concatenate([tokens_bf16, jnp.zeros((1, d), BF16)], axis=0)
    xb = x_pad[tok_buf]
    yb = moe_experts(xb, blk_e, n_used, w1, w3, w2)
    wts = weights.astype(F32)
    return (yb[slot[:, 0]].astype(F32) * wts[:, 0:1] + yb[slot[:, 1]].astype(F32) * wts[:, 1:2])


def _modulate(x, g, shift, scale):
    return _rms_norm(x, g) * (1 + scale) + shift


def kernel(x, c, ctx, c_ctx, mod_w, mod_b, norm1_g, norm2_g, w_in, hy_conv_w, hy_conv_b, hy_f_w1, hy_f_b1, hy_f_w2, hy_f_b2, hy_f_w3, hy_f_b3, hy_f_freq, hy_f_decay, hy_skip, swa_q_g, swa_k_g, swa_sink, rw_mu, rw_w0, rw_w2, rw_a0, rw_a2, rw_g2, rw_k_k, rw_k_a, rw_r_k, rw_ln_w, rw_ln_b, df_q_g, df_k_g, df_lq1, df_lk1, df_lq2, df_lk2, df_sub_g, w_gate, b_gate, w_branch, w_out, moe_rg_w, moe_rg_b, moe_re_w, moe_re_b, moe_w1, moe_w3, moe_w2):
    B, L, D = x.shape
    C = ctx.shape[1]
    depth = mod_w.shape[0]
    bw = D // N_BRANCH
    a_cols = 3 * bw
    swa_heads = bw // HEAD_DIM
    b_cols = (swa_heads + 2 * SWA_KV_HEADS) * HEAD_DIM
    c_cols = 3 * bw + RW_DECAY_RANK + RW_A_RANK + RW_G_RANK
    in_splits = (a_cols, a_cols + b_cols, a_cols + b_cols + c_cols)
    a_off, b_off, d_off = c_cols, c_cols + a_cols, c_cols + a_cols + b_cols
    rows = L // GRID_W
    swa_tables = _rope_lane_tables(rows, 2 * SWA_KV_HEADS * HEAD_DIM)
    diff_tables = _rope_lane_tables(rows, bw)
    hp = lax.Precision.HIGHEST

    x_lat = x.reshape(B * L, D)
    x_ctx = ctx.reshape(B * C, D)
    for l in range(depth):
        need_ctx = l < depth - 1
        mod_lat = jnp.dot(jax.nn.silu(c), mod_w[l], precision=hp) + mod_b[l]
        mod_ctx = jnp.dot(jax.nn.silu(c_ctx)[None], mod_w[l], precision=hp) + mod_b[l]
        sh1, sc1, g1, sh2, sc2, g2 = [t[:, None, :] for t in jnp.split(mod_lat, 6, axis=-1)]
        csh1, csc1, cg1, csh2, csc2, cg2 = [t[:, None, :] for t in jnp.split(mod_ctx, 6, axis=-1)]

        w_l = w_in[l]
        w_in_b = jnp.concatenate([w_l[:, in_splits[1]:in_splits[2]], w_l[:, :in_splits[1]], w_l[:, in_splits[2]:]],
                                 axis=1).astype(BF16)
        p2_lat, h_lat = norm_proj(x_lat, norm1_g[l][None], sc1, sh1, w_in_b, rows_per_mod=L)
        p2_ctx, h_ctx = norm_proj(x_ctx, norm1_g[l][None], csc1, csh1, w_in_b, rows_per_mod=B * C)
        pa_l = p2_lat.reshape(B, L, -1)[..., a_off:a_off + a_cols]
        pa_c = p2_ctx.reshape(B, C, -1)[..., a_off:a_off + a_cols]

        hy_args = (hy_conv_w[l], hy_conv_b[l], hy_f_w1[l], hy_f_b1[l], hy_f_w2[l], hy_f_b2[l], hy_f_w3[l],
                   hy_f_b3[l], hy_f_freq[l], hy_f_decay[l], hy_skip[l])
        ya_l = _hyena(pa_l, *hy_args)

        yb_l, yb_c = _swa_branch2(p2_lat, p2_ctx, b_off, B, L, C, swa_tables, swa_q_g[l], swa_k_g[l],
                                  swa_sink[l], need_ctx)

        yc_l, yc_c = _rwkv_branch2(p2_ctx, p2_lat, B, L, C, bw, rw_mu[l], rw_w0[l], rw_w2[l], rw_a0[l], rw_a2[l],
                                   rw_g2[l], rw_k_k[l], rw_k_a[l], rw_r_k[l], rw_ln_w[l], rw_ln_b[l], need_ctx)

        lam_init = 0.8 - 0.6 * math.exp(-0.3 * l)
        lam = (jnp.exp(jnp.sum(df_lq1[l] * df_lk1[l])) - jnp.exp(jnp.sum(df_lq2[l] * df_lk2[l])) + lam_init)
        yd_l, yd_c = _diff_branch2(p2_lat, p2_ctx, d_off, B, L, C, diff_tables, df_q_g[l], df_k_g[l],
                                   df_sub_g[l], lam, lam_init, need_ctx)

        wg_b = w_gate[l].astype(BF16)
        bg = b_gate[l][:, None, :]
        wb_b = w_branch[l].astype(BF16)
        wo_b = w_out[l].astype(BF16)
        ys_l = [t.reshape(B * L, bw) for t in (ya_l, yb_l, yc_l, yd_l)]
        acc_l = merge_gated(h_lat, ys_l, wg_b, bg, wb_b)
        x_lat = resid_proj(x_lat, acc_l, wo_b, g1, rows_per_mod=L)
        if need_ctx:
            ya_c = _hyena(pa_c, *hy_args)
            ys_c = [t.reshape(B * C, bw) for t in (ya_c, yb_c, yc_c, yd_c)]
            acc_c = merge_gated(h_ctx, ys_c, wg_b, bg, wb_b)
            x_ctx = resid_proj(x_ctx, acc_c, wo_b, cg1, rows_per_mod=B * C)

        w1_b, w3_b, w2_b = moe_w1[l].astype(BF16), moe_w3[l].astype(BF16), moe_w2[l].astype(BF16)
        n_route = MOE_GROUPS + MOE_EXPERTS
        wr = jnp.zeros((D, ROUTER_LANES), F32).at[:, :MOE_GROUPS].set(moe_rg_w[l]).at[:, MOE_GROUPS:n_route].set(moe_re_w[l])
        br = jnp.zeros((1, ROUTER_LANES), F32).at[0, :MOE_GROUPS].set(moe_rg_b[l]).at[0, MOE_GROUPS:n_route].set(moe_re_b[l])
        g2n = norm2_g[l][None]
        hm_lat, lg_lat = norm_route(x_lat, g2n, sc2, sh2, wr, br, rows_per_mod=L)
        if need_ctx:
            hm_ctx, lg_ctx = norm_route(x_ctx, g2n, csc2, csh2, wr, br, rows_per_mod=B * C)
            out = _hier_moe(jnp.concatenate([hm_ctx, hm_lat], axis=0), jnp.concatenate([lg_ctx, lg_lat], axis=0),
                            w1_b, w3_b, w2_b)
            x_ctx = x_ctx + (cg2 * out[:B * C].reshape(1, B * C, D)).reshape(B * C, D)
            x_lat = x_lat + (g2 * out[B * C:].reshape(B, L, D)).reshape(B * L, D)
        else:
            out = _hier_moe(hm_lat, lg_lat, w1_b, w3_b, w2_b)
            x_lat = x_lat + (g2 * out.reshape(B, L, D)).reshape(B * L, D)
    return x_lat.reshape(B, L, D)
```

```python
import functools
import math

import jax
import jax.numpy as jnp
from jax import lax
from jax.experimental import pallas as pl
from jax.experimental.pallas import tpu as pltpu

F32 = jnp.float32
BF16 = jnp.bfloat16

GRID_W = 64
HEAD_DIM = 64
ROPE_BASE = 10000.0
NORM_EPS = 1e-6
NEG_INF = -1e30
N_BRANCH = 4
HY_ORDER = 2
HY_POS_BANDS = 8
SWA_KV_HEADS = 2
SWA_WINDOW = 128
SWA_BLOCK = 128
RW_DECAY_RANK = 64
RW_A_RANK = 64
RW_G_RANK = 128
RW_GN_EPS = 64e-5
DF_BLOCK = 128
DF_ROW_CHUNK = 256
MOE_GROUPS = 4
MOE_PER_GROUP = 8
MOE_EXPERTS = MOE_GROUPS * MOE_PER_GROUP
MOE_TOP_K = 2
MOE_BLOCK = 256

VMEM_LIMIT_BYTES = 56 * 1024 * 1024


def _row_tile(m, pref):
    t = min(pref, m)
    while m % t:
        t //= 2
    return t


def _params(sem):
    return pltpu.CompilerParams(dimension_semantics=sem, vmem_limit_bytes=VMEM_LIMIT_BYTES)


def _norm_proj_body(x_ref, g_ref, sc_ref, sh_ref, w_ref, p_ref, h_ref, h_scr):
    @pl.when(pl.program_id(1) == 0)
    def _():
        x = x_ref[...].astype(F32)
        y = x * lax.rsqrt(jnp.mean(x * x, axis=-1, keepdims=True) + NORM_EPS)
        h = y * g_ref[...] * (1.0 + sc_ref[...]) + sh_ref[...]
        h_scr[...] = h.astype(BF16)
        h_ref[...] = h_scr[...]

    p_ref[...] = jnp.dot(h_scr[...], w_ref[...], preferred_element_type=F32).astype(p_ref.dtype)


def norm_proj(x, g, scale, shift, w, *, rows_per_mod, tm=1024, tn=512, out_dtype=F32):
    m, d = x.shape
    n = w.shape[1]
    tm = _row_tile(rows_per_mod, tm)
    tn = _row_tile(n, tn)
    tiles_per_mod = rows_per_mod // tm
    mod_map = lambda i, j: (i // tiles_per_mod, 0, 0)
    return pl.pallas_call(
        _norm_proj_body,
        grid=(m // tm, n // tn),
        in_specs=[
            pl.BlockSpec((tm, d), lambda i, j: (i, 0)),
            pl.BlockSpec((1, d), lambda i, j: (0, 0)),
            pl.BlockSpec((None, 1, d), mod_map),
            pl.BlockSpec((None, 1, d), mod_map),
            pl.BlockSpec((d, tn), lambda i, j: (0, j)),
        ],
        out_specs=[
            pl.BlockSpec((tm, tn), lambda i, j: (i, j)),
            pl.BlockSpec((tm, d), lambda i, j: (i, 0)),
        ],
        out_shape=[jax.ShapeDtypeStruct((m, n), out_dtype), jax.ShapeDtypeStruct((m, d), BF16)],
        scratch_shapes=[pltpu.VMEM((tm, d), BF16)],
        compiler_params=_params(("parallel", "arbitrary")),
        name="norm_proj",
    )(x, g, scale, shift, w)


ROUTER_LANES = 128


def _norm_route_body(x_ref, g_ref, sc_ref, sh_ref, wr_ref, br_ref, h_ref, lg_ref):
    x = x_ref[...].astype(F32)
    y = x * lax.rsqrt(jnp.mean(x * x, axis=-1, keepdims=True) + NORM_EPS)
    h = y * g_ref[...] * (1.0 + sc_ref[...]) + sh_ref[...]
    h_ref[...] = h.astype(h_ref.dtype)
    lg_ref[...] = jnp.dot(h, wr_ref[...], preferred_element_type=F32, precision=lax.Precision.HIGHEST) + br_ref[...]


def norm_route(x, g, scale, shift, wr, br, *, rows_per_mod, tm=512):
    m, d = x.shape
    tm = _row_tile(rows_per_mod, tm)
    tiles_per_mod = rows_per_mod // tm
    mod_map = lambda i: (i // tiles_per_mod, 0, 0)
    return pl.pallas_call(
        _norm_route_body,
        grid=(m // tm,),
        in_specs=[
            pl.BlockSpec((tm, d), lambda i: (i, 0)),
            pl.BlockSpec((1, d), lambda i: (0, 0)),
            pl.BlockSpec((None, 1, d), mod_map),
            pl.BlockSpec((None, 1, d), mod_map),
            pl.BlockSpec((d, ROUTER_LANES), lambda i: (0, 0)),
            pl.BlockSpec((1, ROUTER_LANES), lambda i: (0, 0)),
        ],
        out_specs=[pl.BlockSpec((tm, d), lambda i: (i, 0)), pl.BlockSpec((tm, ROUTER_LANES), lambda i: (i, 0))],
        out_shape=[jax.ShapeDtypeStruct((m, d), BF16), jax.ShapeDtypeStruct((m, ROUTER_LANES), F32)],
        compiler_params=_params(("parallel",)),
        name="norm_route",
    )(x, g, scale, shift, wr, br)


def _merge_body(h_ref, y0_ref, y1_ref, y2_ref, y3_ref, wg_ref, bg_ref, wb_ref, o_ref):
    h = h_ref[...]
    acc = None
    for b, y_ref in enumerate((y0_ref, y1_ref, y2_ref, y3_ref)):
        gate = jax.nn.sigmoid(jnp.dot(h, wg_ref[b], preferred_element_type=F32) + bg_ref[b])
        val = gate * jnp.dot(y_ref[...].astype(BF16), wb_ref[b], preferred_element_type=F32)
        acc = val if acc is None else acc + val
    o_ref[...] = acc.astype(o_ref.dtype)


def merge_gated(h, ys, wg, bg, wb, *, tm=1024, tn=256):
    m, d = h.shape
    w = ys[0].shape[-1]
    tm = _row_tile(m, tm)
    tn = _row_tile(d, tn)
    yspec = pl.BlockSpec((tm, w), lambda i, j: (i, 0))
    return pl.pallas_call(
        _merge_body,
        grid=(m // tm, d // tn),
        in_specs=[
            pl.BlockSpec((tm, d), lambda i, j: (i, 0)),
            yspec, yspec, yspec, yspec,
            pl.BlockSpec((N_BRANCH, d, tn), lambda i, j: (0, 0, j)),
            pl.BlockSpec((N_BRANCH, 1, tn), lambda i, j: (0, 0, j)),
            pl.BlockSpec((N_BRANCH, w, tn), lambda i, j: (0, 0, j)),
        ],
        out_specs=pl.BlockSpec((tm, tn), lambda i, j: (i, j)),
        out_shape=jax.ShapeDtypeStruct((m, d), BF16),
        compiler_params=_params(("parallel", "arbitrary")),
        name="merge_gated",
    )(h, *ys, wg, bg, wb)


def _resid_proj_body(x_ref, a_ref, w_ref, gate_ref, o_ref):
    y = jnp.dot(a_ref[...], w_ref[...], preferred_element_type=F32)
    o_ref[...] = (x_ref[...].astype(F32) + gate_ref[...] * y).astype(o_ref.dtype)


def resid_proj(x, a, w, gate, *, rows_per_mod, tm=1024, tn=512):
    m, d = x.shape
    k = a.shape[1]
    tm = _row_tile(rows_per_mod, tm)
    tn = _row_tile(d, tn)
    tiles_per_mod = rows_per_mod // tm
    return pl.pallas_call(
        _resid_proj_body,
        grid=(m // tm, d // tn),
        in_specs=[
            pl.BlockSpec((tm, tn), lambda i, j: (i, j)),
            pl.BlockSpec((tm, k), lambda i, j: (i, 0)),
            pl.BlockSpec((k, tn), lambda i, j: (0, j)),
            pl.BlockSpec((None, 1, tn), lambda i, j: (i // tiles_per_mod, 0, j)),
        ],
        out_specs=pl.BlockSpec((tm, tn), lambda i, j: (i, j)),
        out_shape=jax.ShapeDtypeStruct((m, d), x.dtype),
        compiler_params=_params(("parallel", "arbitrary")),
        name="resid_proj",
    )(x, a, w, gate)


def _moe_body(blk_e_ref, n_used_ref, x_ref, w1_ref, w3_ref, w2_ref, o_ref):
    i = pl.program_id(0)

    @pl.when(i < n_used_ref[0])
    def _():
        x = x_ref[...]
        a = jnp.dot(x, w1_ref[...], preferred_element_type=F32)
        b = jnp.dot(x, w3_ref[...], preferred_element_type=F32)
        hdn = (a * jax.nn.sigmoid(a) * b).astype(BF16)
        o_ref[...] = jnp.dot(hdn, w2_ref[...], preferred_element_type=F32).astype(o_ref.dtype)

    @pl.when(i >= n_used_ref[0])
    def _():
        o_ref[...] = jnp.zeros_like(o_ref)


def moe_experts(xb, blk_e, n_used, w1, w3, w2):
    p, d = xb.shape
    hid = w1.shape[-1]
    n_blocks = p // MOE_BLOCK
    grid_spec = pltpu.PrefetchScalarGridSpec(
        num_scalar_prefetch=2,
        grid=(n_blocks,),
        in_specs=[
            pl.BlockSpec((MOE_BLOCK, d), lambda i, e, n: (i, 0)),
            pl.BlockSpec((None, d, hid), lambda i, e, n: (e[i], 0, 0)),
            pl.BlockSpec((None, d, hid), lambda i, e, n: (e[i], 0, 0)),
            pl.BlockSpec((None, hid, d), lambda i, e, n: (e[i], 0, 0)),
        ],
        out_specs=pl.BlockSpec((MOE_BLOCK, d), lambda i, e, n: (i, 0)),
    )
    return pl.pallas_call(
        _moe_body,
        grid_spec=grid_spec,
        out_shape=jax.ShapeDtypeStruct((p, d), BF16),
        compiler_params=_params(("arbitrary",)),
        name="moe_experts",
    )(blk_e, n_used, xb, w1, w3, w2)


RW_CHUNK = 64
_HI = lax.Precision.HIGHEST
_NT = (((1,), (1,)), ((), ()))
_TN = (((0,), (0,)), ((), ()))


def _dot(a, b, dims=None):
    if dims is None:
        return jnp.dot(a, b, preferred_element_type=F32, precision=_HI)
    return lax.dot_general(a, b, dims, preferred_element_type=F32, precision=_HI)


def _bdot(a, b, dims=None):
    a, b = a.astype(BF16), b.astype(BF16)
    if dims is None:
        return jnp.dot(a, b, preferred_element_type=F32)
    return lax.dot_general(a, b, dims, preferred_element_type=F32)


def _rwkv_prep_body(r_ref, lw_ref, kk_ref, a_ref, k_ref, v_ref, p_ref, g_ref, q_ref, yl_ref, *, reverse):
    T = r_ref.shape[0]
    hd = HEAD_DIM
    row = lax.broadcasted_iota(jnp.int32, (T, T), 0)
    col = lax.broadcasted_iota(jnp.int32, (T, T), 1)
    if reverse:
        strict, incl = col > row, col >= row
    else:
        strict, incl = col < row, col <= row
    lw = lw_ref[...]
    tri = incl.astype(BF16)
    lw1 = lw.astype(BF16)
    res1 = lw - lw1.astype(F32)
    lw2 = res1.astype(BF16)
    lw3 = (res1 - lw2.astype(F32)).astype(BF16)
    cum = (jnp.dot(tri, lw1, preferred_element_type=F32) + jnp.dot(tri, lw2, preferred_element_type=F32)
           + jnp.dot(tri, lw3, preferred_element_type=F32))
    total = jnp.sum(lw, axis=0, keepdims=True)
    e_in = jnp.exp(cum)
    e_ex = jnp.exp(cum - lw)
    e_ninv = jnp.exp(-cum)
    e_rem = jnp.exp(total - cum)
    gam = jnp.exp(total)
    kk = kk_ref[...]
    kka = kk * a_ref[...]
    k = k_ref[...]
    nt = -kk * e_ex
    rt = r_ref[...] * e_in
    at = kka * e_ninv
    kt = k * e_ninv
    ac = kka * e_rem
    kc = k * e_rem
    v = v_ref[...]
    eye = lax.broadcasted_iota(jnp.int32, (hd, hd), 0) == lax.broadcasted_iota(jnp.int32, (hd, hd), 1)
    n_heads = r_ref.shape[1] // hd
    zeros = jnp.zeros((T, hd), F32)
    steps = max(1, (T - 1).bit_length())
    heads = range(n_heads)
    sls = [slice(hh * hd, (hh + 1) * hd) for hh in heads]
    bigs = [_bdot(jnp.concatenate([nt[:, sl], rt[:, sl]], axis=0),
                  jnp.concatenate([at[:, sl], kt[:, sl]], axis=0), _NT) for sl in sls]
    a_ak = [jnp.where(strict, big[:T, T:], 0.0) for big in bigs]
    pws = [jnp.where(strict, big[:T, :T], 0.0) for big in bigs]
    lhs_top = [jnp.concatenate([jnp.where(incl, big[T:, :T], 0.0), jnp.where(incl, big[T:, T:], 0.0)], axis=1)
               for big in bigs]
    xs = [jnp.concatenate([nt[:, sl], _bdot(m, v[:, sl])], axis=1) for m, sl in zip(a_ak, sls)]
    for it in range(steps):
        if it < steps - 1:
            boths = [_bdot(pw, jnp.concatenate([x, pw], axis=1)) for pw, x in zip(pws, xs)]
            xs = [x + both[:, :2 * hd] for x, both in zip(xs, boths)]
            pws = [both[:, 2 * hd:] for both in boths]
        else:
            xs = [x + _bdot(pw, x) for pw, x in zip(pws, xs)]
    rhs = [jnp.concatenate([x, jnp.concatenate([zeros, v[:, sl]], axis=1)], axis=0) for x, sl in zip(xs, sls)]
    tops = [_bdot(lt, rh) for lt, rh in zip(lhs_top, rhs)]
    bots = [_bdot(jnp.concatenate([ac[:, sl], kc[:, sl]], axis=0), rh, _TN) for sl, rh in zip(sls, rhs)]
    for hh in heads:
        p_ref[hh] = jnp.where(eye, gam[:, sls[hh]], 0.0) + bots[hh][:, :hd]
        g_ref[hh] = bots[hh][:, hd:]
    q_ref[...] = jnp.concatenate([rt[:, sl] + top[:, :hd] for sl, top in zip(sls, tops)], axis=1)
    yl_ref[...] = jnp.concatenate([top[:, hd:] for top in tops], axis=1)


def rwkv_chunk_prep(r, logw, kk, a, k, v, *, reverse):
    B, L, W = r.shape
    T = RW_CHUNK
    nh = W // HEAD_DIM
    nc = L // T
    blk = pl.BlockSpec((None, T, W), lambda b, c: (b, c, 0))
    mat = pl.BlockSpec((None, None, nh, HEAD_DIM, HEAD_DIM), lambda b, c: (b, c, 0, 0, 0))
    mat_shape = jax.ShapeDtypeStruct((B, nc, nh, HEAD_DIM, HEAD_DIM), F32)
    seq_shape = jax.ShapeDtypeStruct((B, L, W), F32)
    return pl.pallas_call(
        functools.partial(_rwkv_prep_body, reverse=reverse),
        grid=(B, nc),
        in_specs=[blk] * 6,
        out_specs=[mat, mat, blk, blk],
        out_shape=[mat_shape, mat_shape, seq_shape, seq_shape],
        compiler_params=_params(("parallel", "parallel")),
        name="rwkv_chunk_prep",
    )(r, logw, kk, a, k, v)


def _rwkv_scan_body(p_ref, g_ref, q_ref, yl_ref, z0_ref, y_ref, zf_ref, z_scr):
    c = pl.program_id(1)

    @pl.when(c == 0)
    def _():
        z_scr[...] = z0_ref[...]

    hd = HEAD_DIM
    n_heads = z_scr.shape[0]
    q = q_ref[...]
    ys = []
    for h in range(n_heads):
        z = z_scr[h]
        ys.append(_bdot(q[:, h * hd:(h + 1) * hd], z))
        z_scr[h] = _dot(p_ref[h], z) + g_ref[h]
    y_ref[...] = jnp.concatenate(ys, axis=1) + yl_ref[...]

    @pl.when(c == pl.num_programs(1) - 1)
    def _():
        zf_ref[...] = z_scr[...]


def rwkv_chunk_scan(p, g, qh, yl, z0, *, reverse):
    B, nc, nh = p.shape[:3]
    L, W = qh.shape[1:]
    T = L // nc
    cidx = (lambda c: nc - 1 - c) if reverse else (lambda c: c)
    mat = pl.BlockSpec((None, None, nh, HEAD_DIM, HEAD_DIM), lambda b, c: (b, cidx(c), 0, 0, 0))
    seq = pl.BlockSpec((None, T, W), lambda b, c: (b, cidx(c), 0))
    st = pl.BlockSpec((None, nh, HEAD_DIM, HEAD_DIM), lambda b, c: (b, 0, 0, 0))
    return pl.pallas_call(
        _rwkv_scan_body,
        grid=(B, nc),
        in_specs=[mat, mat, seq, seq, st],
        out_specs=[seq, st],
        out_shape=[jax.ShapeDtypeStruct((B, L, W), F32), jax.ShapeDtypeStruct(z0.shape, F32)],
        scratch_shapes=[pltpu.VMEM((nh, HEAD_DIM, HEAD_DIM), F32)],
        compiler_params=_params(("parallel", "arbitrary")),
        name="rwkv_chunk_scan",
    )(p, g, qh, yl, z0)


def _head_sum(x):
    lt = 2 * HEAD_DIM
    gi = lax.broadcasted_iota(jnp.int32, (lt, lt), 0) // HEAD_DIM
    gj = lax.broadcasted_iota(jnp.int32, (lt, lt), 1) // HEAD_DIM
    bd = (gi == gj).astype(BF16)
    hi = x.astype(BF16)
    lo = (x - hi.astype(F32)).astype(BF16)
    tiles = [jnp.dot(hi[:, t * lt:(t + 1) * lt], bd, preferred_element_type=F32)
             + jnp.dot(lo[:, t * lt:(t + 1) * lt], bd, preferred_element_type=F32) for t in range(x.shape[1] // lt)]
    return tiles[0] if len(tiles) == 1 else jnp.concatenate(tiles, axis=1)


def _rwkv_pre_body(x_ref, xp_ref, xn_ref, mu_ref, wt_ref, w0_ref, a0_ref, kk_ref, ka_ref, *out_refs, tiles_per_seq):
    i = pl.program_id(0)
    x = x_ref[...]
    tm, feat = x.shape
    w = kk_ref.shape[-1]
    tail = feat - 3 * w
    row = lax.broadcasted_iota(jnp.int32, (tm, feat), 0)
    pos = i % tiles_per_seq
    prev_row = jnp.where(pos == 0, 0.0, xp_ref[7:8, :])
    next_row = jnp.where(pos == tiles_per_seq - 1, 0.0, xn_ref[0:1, :])
    lane = lax.broadcasted_iota(jnp.int32, (tm, tail), 1)
    kscale = kk_ref[...]
    kmix = ka_ref[...]
    for d in range(2):
        if d == 0:
            sh = jnp.where(row == 0, prev_row, pltpu.roll(x, 1, axis=0))
        else:
            sh = jnp.where(row == tm - 1, next_row, pltpu.roll(x, tm - 1, axis=0))
        xs = x + mu_ref[d] * (sh - x)
        r, k, v, t = xs[:, :w], xs[:, w:2 * w], xs[:, 2 * w:3 * w], xs[:, 3 * w:]
        act = jnp.where(lane < RW_DECAY_RANK, jnp.tanh(t),
                        jnp.where(lane < RW_DECAY_RANK + RW_A_RANK, t, jax.nn.sigmoid(t)))
        hi = act.astype(BF16)
        lo = (act - hi.astype(F32)).astype(BF16)
        z = jnp.dot(hi, wt_ref[d], preferred_element_type=F32) + jnp.dot(lo, wt_ref[d], preferred_element_type=F32)
        logw = -math.exp(-0.5) * jax.nn.sigmoid(w0_ref[d] + z[:, :w])
        a = jax.nn.sigmoid(a0_ref[d] + z[:, w:2 * w])
        g = z[:, 2 * w:]
        kx = k * kscale
        kk = kx * lax.rsqrt(_head_sum(kx * kx) + 1e-12)
        k2 = k * (1.0 + (a - 1.0) * kmix)
        for ref, val in zip(out_refs[7 * d:7 * d + 7], (r, logw, kk, a, k2, v, g)):
            ref[...] = val


def rwkv_pre(feats, mu, wt, w0, a0, k_k, k_a, rows_per_seq, width):
    m = feats.shape[0]
    feat = mu.shape[-1]
    tm = _row_tile(rows_per_seq, 256)
    tiles_per_seq = rows_per_seq // tm
    r8 = tm // 8
    full = lambda shape: pl.BlockSpec(shape, lambda i: (0,) * len(shape))
    outs = pl.pallas_call(
        functools.partial(_rwkv_pre_body, tiles_per_seq=tiles_per_seq),
        grid=(m // tm,),
        in_specs=[
            pl.BlockSpec((tm, feat), lambda i: (i, 0)),
            pl.BlockSpec((8, feat), lambda i: (jnp.maximum(i * r8 - 1, 0), 0)),
            pl.BlockSpec((8, feat), lambda i: (jnp.minimum((i + 1) * r8, m // 8 - 1), 0)),
            full(mu.shape), full(wt.shape), full(w0.shape), full(a0.shape), full(k_k.shape), full(k_a.shape),
        ],
        out_specs=[pl.BlockSpec((tm, width), lambda i: (i, 0))] * 14,
        out_shape=[jax.ShapeDtypeStruct((m, width), F32)] * 14,
        compiler_params=_params(("parallel",)),
        name="rwkv_pre",
    )(feats, feats, feats, mu, wt, w0, a0, k_k, k_a)
    return outs[:7], outs[7:]


def _rwkv_readout_body(*refs):
    dirs, (rk_ref, lnw_ref, lnb_ref, o_ref) = (refs[0:5], refs[5:10]), refs[10:]
    inv = 1.0 / HEAD_DIM
    acc = None
    for y_ref, r_ref, k_ref, v_ref, g_ref in dirs:
        y = y_ref[...]
        c = y - _head_sum(y) * inv
        yn = c * lax.rsqrt(_head_sum(c * c) * inv + RW_GN_EPS) * lnw_ref[...] + lnb_ref[...]
        bonus = _head_sum(r_ref[...] * k_ref[...] * rk_ref[...]) * v_ref[...]
        val = (yn + bonus) * g_ref[...]
        acc = val if acc is None else acc + val
    o_ref[...] = acc


def rwkv_readout(fwd, bwd, r_k, ln_w, ln_b):
    m, w = fwd[0].shape
    tm = _row_tile(m, 512)
    blk = pl.BlockSpec((tm, w), lambda i: (i, 0))
    par = pl.BlockSpec((1, w), lambda i: (0, 0))
    return pl.pallas_call(
        _rwkv_readout_body,
        grid=(m // tm,),
        in_specs=[blk] * 10 + [par] * 3,
        out_specs=blk,
        out_shape=jax.ShapeDtypeStruct((m, w), F32),
        compiler_params=_params(("parallel",)),
        name="rwkv_readout",
    )(*fwd, *bwd, r_k, ln_w, ln_b)


def _head_prep_body(*refs, rope):
    if rope:
        x_ref, g_ref, mk_ref, sc_ref, cos_ref, sin_ref, o_ref = refs
    else:
        x_ref, g_ref, mk_ref, sc_ref, o_ref = refs
    x = x_ref[...]
    tm, wb = x.shape
    lt = 2 * HEAD_DIM
    gid = lax.broadcasted_iota(jnp.int32, (lt, lt), 0) // HEAD_DIM
    gjd = lax.broadcasted_iota(jnp.int32, (lt, lt), 1) // HEAD_DIM
    bd = (gid == gjd).astype(BF16)
    xx = x * x
    hi = xx.astype(BF16)
    lo = (xx - hi.astype(F32)).astype(BF16)
    ms = jnp.concatenate(
        [jnp.dot(hi[:, t * lt:(t + 1) * lt], bd, preferred_element_type=F32)
         + jnp.dot(lo[:, t * lt:(t + 1) * lt], bd, preferred_element_type=F32) for t in range(wb // lt)], axis=1)
    y = x * lax.rsqrt(ms * (1.0 / HEAD_DIM) + NORM_EPS) * g_ref[...]
    if rope:
        half = HEAD_DIM // 2
        lane = lax.broadcasted_iota(jnp.int32, (tm, wb), 1)
        partner = jnp.where(lane % HEAD_DIM < half, pltpu.roll(y, wb - half, axis=1), pltpu.roll(y, half, axis=1))
        y = y * cos_ref[...] + partner * sin_ref[...]
    y = jnp.where(mk_ref[...] > 0.0, y, x)
    o_ref[...] = (y * sc_ref[...]).astype(o_ref.dtype)


def head_prep(p, col0, wb, gain, mask, scale, tables, rows_per_seq):
    m = p.shape[0]
    nj = gain.shape[0]
    tm = _row_tile(rows_per_seq, 512)
    cb0 = col0 // wb
    tiles_per_seq = rows_per_seq // tm
    par = pl.BlockSpec((None, 1, wb), lambda i, j: (j, 0, 0))
    in_specs = [pl.BlockSpec((tm, wb), lambda i, j: (i, cb0 + j)), par, par, par]
    args = [p, gain, mask, scale]
    if tables is not None:
        tab = pl.BlockSpec((tm, wb), lambda i, j: (i % tiles_per_seq, 0))
        in_specs += [tab, tab]
        args += list(tables)
    return pl.pallas_call(
        functools.partial(_head_prep_body, rope=tables is not None),
        grid=(m // tm, nj),
        in_specs=in_specs,
        out_specs=pl.BlockSpec((tm, wb), lambda i, j: (i, j)),
        out_shape=jax.ShapeDtypeStruct((m, nj * wb), BF16),
        compiler_params=_params(("parallel", "parallel")),
        name="head_prep",
    )(*args)


def _diff_attn_body(lam_ref, q_ref, kt_ref, v_ref, subg_ref, o_ref, q2_scr, m_scr, acc_scr, *, out_scale):
    ki = pl.program_id(3)
    tq = q_ref.shape[0]

    @pl.when(ki == 0)
    def _():
        q = q_ref[...]
        lane = lax.broadcasted_iota(jnp.int32, q.shape, 1)
        q2_scr[0:tq, :] = jnp.where(lane < HEAD_DIM, q, jnp.zeros_like(q))
        q2_scr[tq:2 * tq, :] = jnp.where(lane >= HEAD_DIM, q, jnp.zeros_like(q))
        m_scr[...] = jnp.full_like(m_scr, -jnp.inf)
        acc_scr[...] = jnp.zeros_like(acc_scr)

    kt = kt_ref[...]
    v = jnp.concatenate([v_ref[...], jnp.ones(v_ref.shape, BF16)], axis=1)
    rc = min(DF_ROW_CHUNK, 2 * tq)
    n_chunks = 2 * tq // rc
    score = lambda c: jnp.dot(q2_scr[pl.ds(c * rc, rc), :], kt, preferred_element_type=F32)
    s_next = score(0)
    for c in range(n_chunks):
        rows = pl.ds(c * rc, rc)
        s = s_next
        if c + 1 < n_chunks:
            s_next = score(c + 1)
        m_prev = m_scr[rows, :]
        m_new = jnp.maximum(m_prev, jnp.max(s, axis=-1, keepdims=True))
        alpha = jnp.exp2(m_prev - m_new)
        p = jnp.exp2(s - m_new)
        acc_scr[rows, :] = alpha * acc_scr[rows, :] + jnp.dot(p.astype(BF16), v, preferred_element_type=F32)
        m_scr[rows, :] = m_new

    @pl.when(ki == pl.num_programs(3) - 1)
    def _():
        hw = o_ref.shape[-1]
        o = acc_scr[:, 0:hw] / acc_scr[:, hw:hw + 1]
        a = o[0:tq, :] - lam_ref[0, 0] * o[tq:2 * tq, :]
        y = a * lax.rsqrt(jnp.mean(a * a, axis=-1, keepdims=True) + NORM_EPS)
        o_ref[...] = (y * subg_ref[...] * out_scale).astype(o_ref.dtype)


def _key_tile(k, cap):
    best = 128
    t = 128
    while t <= min(k, cap):
        if k % t == 0:
            best = t
        t += 128
    return best


def diff_attention(q, q_blk0, kt, v, v_blk0, sub_g, lam, out_scale, *, tq=1024, tk_cap=1664):
    B, L = q.shape[:2]
    W, K = kt.shape[1:]
    hw = 2 * HEAD_DIM
    tq = _row_tile(L, tq)
    tk = _key_tile(K, tk_cap)
    return pl.pallas_call(
        functools.partial(_diff_attn_body, out_scale=out_scale),
        grid=(B, W // hw, L // tq, K // tk),
        in_specs=[
            pl.BlockSpec(memory_space=pltpu.SMEM),
            pl.BlockSpec((None, tq, hw), lambda b, h, i, j: (b, i, q_blk0 + h)),
            pl.BlockSpec((None, hw, tk), lambda b, h, i, j: (b, h, j)),
            pl.BlockSpec((None, tk, hw), lambda b, h, i, j: (b, j, v_blk0 + h)),
            pl.BlockSpec((1, hw), lambda b, h, i, j: (0, 0)),
        ],
        out_specs=pl.BlockSpec((None, tq, hw), lambda b, h, i, j: (b, i, h)),
        out_shape=jax.ShapeDtypeStruct((B, L, W), F32),
        scratch_shapes=[pltpu.VMEM((2 * tq, hw), BF16), pltpu.VMEM((2 * tq, 1), F32),
                        pltpu.VMEM((2 * tq, 2 * hw), F32)],
        compiler_params=_params(("parallel", "parallel", "parallel", "arbitrary")),
        name="diff_attention",
    )(lam, q, kt, v, sub_g)


def _swa_finish(parts, vals, sink, o_ref):
    m = sink
    for s in parts:
        m = jnp.maximum(m, jnp.max(s, axis=-1, keepdims=True))
    denom = jnp.exp(sink - m)
    acc = None
    for s, v in zip(parts, vals):
        p = jnp.exp(s - m)
        denom = denom + jnp.sum(p, axis=-1, keepdims=True)
        pv = jnp.dot(p.astype(BF16), v, preferred_element_type=F32)
        acc = pv if acc is None else acc + pv
    o = acc / denom
    o_ref[...] = o.reshape(o_ref.shape).astype(o_ref.dtype)


def _swa_band_body(q_ref, kp_ref, kn_ref, kx_ref, kc_ref, vp_ref, vn_ref, vx_ref, vc_ref, sink_ref, o_ref):
    n = pl.program_id(2)
    nb = pl.num_programs(2)
    grp, blk, hd = q_ref.shape
    q = q_ref[...].reshape(grp * blk, hd)
    iq = lax.broadcasted_iota(jnp.int32, (grp * blk, blk), 0) % blk
    j = lax.broadcasted_iota(jnp.int32, (grp * blk, blk), 1)
    s_prev = jnp.where((iq + blk - j <= SWA_WINDOW) & (n > 0), _dot_nt_bf16(q, kp_ref[...]), NEG_INF)
    s_cur = jnp.where(jnp.abs(iq - j) <= SWA_WINDOW, _dot_nt_bf16(q, kn_ref[...]), NEG_INF)
    s_next = jnp.where((j + blk - iq <= SWA_WINDOW) & (n < nb - 1), _dot_nt_bf16(q, kx_ref[...]), NEG_INF)
    s_ctx = _dot_nt_bf16(q, kc_ref[...])
    _swa_finish([s_prev, s_cur, s_next, s_ctx], [vp_ref[...], vn_ref[...], vx_ref[...], vc_ref[...]],
                sink_ref[...], o_ref)


def _swa_ctx_body(q_ref, kc_ref, vc_ref, sink_ref, o_ref):
    grp, blk, hd = q_ref.shape
    q = q_ref[...].reshape(grp * blk, hd)
    _swa_finish([_dot_nt_bf16(q, kc_ref[...])], [vc_ref[...]], sink_ref[...], o_ref)


def _dot_nt_bf16(a, b):
    return lax.dot_general(a, b, _NT, preferred_element_type=F32)


def swa_attention(q, k, v, kc, vc, sink_rows):
    B, kvh, grp, L, hd = q.shape
    C = kc.shape[2]
    blk = SWA_BLOCK
    nb = L // blk
    qspec = pl.BlockSpec((None, None, grp, blk, hd), lambda b, h, n: (b, h, 0, n, 0))
    prev = pl.BlockSpec((None, None, blk, hd), lambda b, h, n: (b, h, jnp.maximum(n - 1, 0), 0))
    cur = pl.BlockSpec((None, None, blk, hd), lambda b, h, n: (b, h, n, 0))
    nxt = pl.BlockSpec((None, None, blk, hd), lambda b, h, n: (b, h, jnp.minimum(n + 1, nb - 1), 0))
    cspec = pl.BlockSpec((None, None, C, hd), lambda b, h, n: (b, h, 0, 0))
    sspec = pl.BlockSpec((None, grp * blk, 1), lambda b, h, n: (h, 0, 0))
    return pl.pallas_call(
        _swa_band_body,
        grid=(B, kvh, nb),
        in_specs=[qspec, prev, cur, nxt, cspec, prev, cur, nxt, cspec, sspec],
        out_specs=qspec,
        out_shape=jax.ShapeDtypeStruct(q.shape, F32),
        compiler_params=_params(("parallel", "parallel", "parallel")),
        name="swa_attention",
    )(q, k, k, k, kc, v, v, v, vc, sink_rows)


def swa_context_attention(q, kc, vc, sink_rows):
    B, kvh, grp, L, hd = q.shape
    C = kc.shape[2]
    blk = SWA_BLOCK
    qspec = pl.BlockSpec((None, None, grp, blk, hd), lambda b, h, n: (b, h, 0, n, 0))
    cspec = pl.BlockSpec((None, None, C, hd), lambda b, h, n: (b, h, 0, 0))
    sspec = pl.BlockSpec((None, grp * blk, 1), lambda b, h, n: (h, 0, 0))
    return pl.pallas_call(
        _swa_ctx_body,
        grid=(B, kvh, L // blk),
        in_specs=[qspec, cspec, cspec, sspec],
        out_specs=qspec,
        out_shape=jax.ShapeDtypeStruct(q.shape, F32),
        compiler_params=_params(("parallel", "parallel", "parallel")),
        name="swa_context_attention",
    )(q, kc, vc, sink_rows)


HY_COL_GROUP = 8
HY_K2_GROUP = 8


def _hy_fwd_outer_body(x_ref, fc_ref, fsn_ref, o_ref):
    half, g, ch = x_ref.shape
    for j in range(g):
        x = x_ref[:, j, :].astype(BF16)
        cols = slice(j * ch, (j + 1) * ch)
        o_ref[0, :, cols] = jnp.dot(fc_ref[...], x, preferred_element_type=F32).astype(o_ref.dtype)
        o_ref[1, :, cols] = jnp.dot(fsn_ref[...], x, preferred_element_type=F32).astype(o_ref.dtype)


def _hy_spectral_body(a_ref, m_ref, mi_ref, h_ref, o_ref):
    n1 = a_ref.shape[2]
    for j in range(a_ref.shape[1]):
        b = jnp.concatenate([a_ref[0, j], a_ref[1, j]], axis=0)
        x = jnp.dot(m_ref[j], b, preferred_element_type=F32)
        xr, xi = x[:n1], x[n1:]
        hr, hi = h_ref[0, j], h_ref[1, j]
        y = jnp.concatenate([xr * hr - xi * hi, xr * hi + xi * hr], axis=0).astype(BF16)
        c = jnp.dot(mi_ref[j], y, preferred_element_type=F32)
        o_ref[0, j] = c[:n1].astype(o_ref.dtype)
        o_ref[1, j] = c[n1:].astype(o_ref.dtype)


def _hy_inv_outer_body(c_ref, gc_ref, gsn_ref, z_ref, gate_ref, skip_ref, o_ref):
    half, g, ch = z_ref.shape
    y = (jnp.dot(gc_ref[...], c_ref[0], preferred_element_type=F32)
         + jnp.dot(gsn_ref[...], c_ref[1], preferred_element_type=F32))
    for j in range(g):
        o_ref[:, j, :] = gate_ref[:, j, :] * (y[:, j * ch:(j + 1) * ch] + skip_ref[...] * z_ref[:, j, :])


def _hy_tables(n):
    n2 = 1 << (n.bit_length() // 2)
    n1 = n // n2
    two_pi = 2.0 * math.pi
    i2 = jnp.arange(n2, dtype=jnp.int32)
    ang2 = ((i2[:, None] * i2[None, :]) % n2).astype(F32) * (two_pi / n2)
    c2, s2 = jnp.cos(ang2), jnp.sin(ang2)
    half = n2 // 2
    fwd_c, fwd_sn = c2[:, :half].astype(BF16), (-s2[:, :half]).astype(BF16)
    inv_c, inv_sn = c2[:half, :].astype(BF16), (-s2[:half, :]).astype(BF16)
    i1 = jnp.arange(n1, dtype=jnp.int32)
    kk = n2 * i1[None, :, None] + i2[:, None, None]
    th = ((kk * i1[None, None, :]) % n).astype(F32) * (two_pi / n)
    ct, st = jnp.cos(th), jnp.sin(th)
    m_big = jnp.concatenate([jnp.concatenate([ct, st], axis=2), jnp.concatenate([-st, ct], axis=2)], axis=1)
    ctt, stt = jnp.swapaxes(ct, 1, 2), jnp.swapaxes(st, 1, 2)
    mi_big = jnp.concatenate([jnp.concatenate([ctt, -stt], axis=2), jnp.concatenate([stt, ctt], axis=2)], axis=1)
    return n1, n2, fwd_c, fwd_sn, inv_c, inv_sn, m_big.astype(BF16), mi_big.astype(BF16)


def _hy_spectrum_planes(spec, n, n1, n2):
    L = n // 2
    full = jnp.concatenate([spec, jnp.conj(spec[1:L][::-1])], axis=0) * (1.0 / n)
    o, c = full.shape[1:]
    full = jnp.transpose(full.reshape(n1, n2, o, c), (2, 1, 0, 3))
    return jnp.stack([jnp.real(full), jnp.imag(full)], axis=1).astype(F32)


def hyena_long_conv(z, gate, skip, h_planes, tables):
    B, L, C = z.shape
    n1, n2, fwd_c, fwd_sn, inv_c, inv_sn, m_big, mi_big = tables
    half = n2 // 2
    g = min(HY_COL_GROUP, n1)
    kb = min(HY_K2_GROUP, n2)
    gc = g * C
    z4 = z.reshape(B, half, n1, C)
    gate4 = gate.reshape(B, half, n1, C)
    seq_blk = pl.BlockSpec((None, half, g, C), lambda b, j: (b, 0, j, 0))
    plane = jax.ShapeDtypeStruct((B, 2, n2, n1 * C), BF16)
    plane_blk = pl.BlockSpec((None, 2, n2, gc), lambda b, j: (b, 0, 0, j))
    full_mat = lambda shape: pl.BlockSpec(shape, lambda b, j: (0,) * len(shape))
    a = pl.pallas_call(
        _hy_fwd_outer_body,
        grid=(B, n1 // g),
        in_specs=[seq_blk, full_mat((n2, half)), full_mat((n2, half))],
        out_specs=plane_blk,
        out_shape=plane,
        compiler_params=_params(("parallel", "parallel")),
        name="hy_fwd_outer",
    )(z4, fwd_c, fwd_sn)
    cc = pl.pallas_call(
        _hy_spectral_body,
        grid=(B, n2 // kb),
        in_specs=[
            pl.BlockSpec((None, 2, kb, n1, C), lambda b, j: (b, 0, j, 0, 0)),
            pl.BlockSpec((kb, 2 * n1, 2 * n1), lambda b, j: (j, 0, 0)),
            pl.BlockSpec((kb, 2 * n1, 2 * n1), lambda b, j: (j, 0, 0)),
            pl.BlockSpec((2, kb, n1, C), lambda b, j: (0, j, 0, 0)),
        ],
        out_specs=pl.BlockSpec((None, 2, kb, n1, C), lambda b, j: (b, 0, j, 0, 0)),
        out_shape=jax.ShapeDtypeStruct((B, 2, n2, n1, C), BF16),
        compiler_params=_params(("parallel", "parallel")),
        name="hy_spectral",
    )(a.reshape(B, 2, n2, n1, C), m_big, mi_big, h_planes)
    skip_row = skip.astype(F32).reshape(1, C)
    out = pl.pallas_call(
        _hy_inv_outer_body,
        grid=(B, n1 // g),
        in_specs=[
            plane_blk,
            full_mat((half, n2)), full_mat((half, n2)),
            seq_blk, seq_blk, full_mat((1, C)),
        ],
        out_specs=seq_blk,
        out_shape=jax.ShapeDtypeStruct((B, half, n1, C), F32),
        compiler_params=_params(("parallel", "parallel")),
        name="hy_inv_outer",
    )(cc.reshape(B, 2, n2, n1 * C), inv_c, inv_sn, z4, gate4, skip_row)
    return out.reshape(B, L, C)


def _split2(x):
    hi = x.astype(BF16)
    return hi, (x - hi.astype(F32)).astype(BF16)


def _hy_filt_outer_body(x_ref, fc_ref, fsn_ref, o_ref):
    hi, lo = _split2(x_ref[...])
    for plane, f_ref in enumerate((fc_ref, fsn_ref)):
        o_ref[plane] = (jnp.dot(f_ref[...], hi, preferred_element_type=F32)
                        + jnp.dot(f_ref[...], lo, preferred_element_type=F32))


def _hy_filt_inner_body(a_ref, m_ref, o_ref):
    n1 = a_ref.shape[2]
    n_order, ch = o_ref.shape[0], o_ref.shape[-1]
    for j in range(a_ref.shape[1]):
        hi, lo = _split2(jnp.concatenate([a_ref[0, j], a_ref[1, j]], axis=0))
        x = jnp.dot(m_ref[j], hi, preferred_element_type=F32) + jnp.dot(m_ref[j], lo, preferred_element_type=F32)
        xr, xi = x[:n1], x[n1:]
        for o in range(n_order):
            f, b = slice(2 * o * ch, (2 * o + 1) * ch), slice((2 * o + 1) * ch, (2 * o + 2) * ch)
            o_ref[o, 0, j] = xr[:, f] + xr[:, b]
            o_ref[o, 1, j] = xi[:, f] - xi[:, b]


def hyena_filter_planes(hs, tables, n_order, ch):
    L, hf = hs.shape
    n1, n2, fwd_c, fwd_sn, _, _, m_big, _ = tables
    half = n2 // 2
    g = min(2, n1)
    kb = min(2, n2)
    full_mat = lambda shape: pl.BlockSpec(shape, lambda j: (0,) * len(shape))
    a = pl.pallas_call(
        _hy_filt_outer_body,
        grid=(n1 // g,),
        in_specs=[pl.BlockSpec((half, g * hf), lambda j: (0, j)), full_mat((n2, half)), full_mat((n2, half))],
        out_specs=pl.BlockSpec((2, n2, g * hf), lambda j: (0, 0, j)),
        out_shape=jax.ShapeDtypeStruct((2, n2, n1 * hf), F32),
        compiler_params=_params(("parallel",)),
        name="hy_filt_outer",
    )(hs.reshape(half, n1 * hf), fwd_c, fwd_sn)
    return pl.pallas_call(
        _hy_filt_inner_body,
        grid=(n2 // kb,),
        in_specs=[pl.BlockSpec((2, kb, n1, hf), lambda j: (0, j, 0, 0)),
                  pl.BlockSpec((kb, 2 * n1, 2 * n1), lambda j: (j, 0, 0))],
        out_specs=pl.BlockSpec((n_order, 2, kb, n1, ch), lambda j: (0, 0, j, 0, 0)),
        out_shape=jax.ShapeDtypeStruct((n_order, 2, n2, n1, ch), F32),
        compiler_params=_params(("parallel",)),
        name="hy_filt_inner",
    )(a.reshape(2, n2, n1, hf), m_big)


def _hyena_filter_taps(L, ch, f_w1, f_b1, f_w2, f_b2, f_w3, f_b3, f_freq, f_decay):
    t = jnp.arange(L, dtype=F32) / max(L - 1, 1)
    ang = 2 * math.pi * t[:, None] * jnp.arange(1, HY_POS_BANDS + 1, dtype=F32)
    feat = jnp.concatenate([t[:, None], jnp.sin(ang), jnp.cos(ang)], axis=-1)
    hp = lax.Precision.HIGHEST
    h = jnp.sin(f_freq * (jnp.dot(feat, f_w1, precision=hp) + f_b1))
    h = jnp.sin(f_freq * (jnp.dot(h, f_w2, precision=hp) + f_b2))
    h = jnp.dot(h, f_w3, precision=hp) + f_b3
    h = h * jnp.exp(-f_decay * t[:, None])
    h = h.reshape(L, HY_ORDER, 2, ch)
    keep = jnp.ones((L, 1, 2, 1), F32).at[0, 0, 1, 0].set(0.0)
    h = h * keep
    l1 = jnp.sum(jnp.abs(h), axis=(0, 2), keepdims=True)
    return (h / (l1 * (2.0 * L))).reshape(L, HY_ORDER * 2 * ch)


def _hyena(pa, conv_w, conv_b, f_w1, f_b1, f_w2, f_b2, f_w3, f_b3, f_freq, f_decay, skip):
    B, L, _ = pa.shape
    ch = skip.shape[-1]
    n = 2 * L
    u = _short_conv3(pa, conv_w, conv_b).astype(F32)
    v, x1, x2 = jnp.split(u, 3, axis=-1)
    tables = _hy_tables(n)
    taps = _hyena_filter_taps(L, ch, f_w1, f_b1, f_w2, f_b2, f_w3, f_b3, f_freq, f_decay)
    h = hyena_filter_planes(taps, tables, HY_ORDER, ch)
    z = hyena_long_conv(v, x1, skip[0], h[0], tables)
    return hyena_long_conv(z, x2, skip[1], h[1], tables)


def _rms_norm(x, g):
    xf = x.astype(F32)
    y = xf * lax.rsqrt(jnp.mean(xf * xf, axis=-1, keepdims=True) + NORM_EPS)
    return (y * g.astype(F32)).astype(x.dtype)


def _rope_tables(rows):
    n_freq = HEAD_DIM // 4
    inv = ROPE_BASE ** (-jnp.arange(n_freq, dtype=F32) / n_freq)
    row = jnp.repeat(jnp.arange(rows, dtype=F32), GRID_W)
    col = jnp.tile(jnp.arange(GRID_W, dtype=F32), rows)
    ang = jnp.concatenate([row[:, None] * inv, col[:, None] * inv], axis=-1)
    return jnp.cos(ang), jnp.sin(ang)


def _apply_rope(x, cos, sin):
    shp = (x.shape[1],) + (1,) * (x.ndim - 3) + (HEAD_DIM // 2,)
    c, s = cos.reshape(shp), sin.reshape(shp)
    x1, x2 = jnp.split(x.astype(F32), 2, axis=-1)
    return jnp.concatenate([x1 * c - x2 * s, x2 * c + x1 * s], axis=-1).astype(x.dtype)


def _short_conv3(x, w, b):
    xp = jnp.pad(x, ((0, 0), (1, 1), (0, 0)))
    return xp[:, :-2] * w[0] + xp[:, 1:-1] * w[1] + xp[:, 2:] * w[2] + b


def _hyena_filter_spectra(L, ch, f_w1, f_b1, f_w2, f_b2, f_w3, f_b3, f_freq, f_decay):
    t = jnp.arange(L, dtype=F32) / max(L - 1, 1)
    ang = 2 * math.pi * t[:, None] * jnp.arange(1, HY_POS_BANDS + 1, dtype=F32)
    feat = jnp.concatenate([t[:, None], jnp.sin(ang), jnp.cos(ang)], axis=-1)
    hp = lax.Precision.HIGHEST
    h = jnp.sin(f_freq * (jnp.dot(feat, f_w1, precision=hp) + f_b1))
    h = jnp.sin(f_freq * (jnp.dot(h, f_w2, precision=hp) + f_b2))
    h = jnp.dot(h, f_w3, precision=hp) + f_b3
    h = h * jnp.exp(-f_decay * t[:, None])
    h = h.reshape(L, HY_ORDER, 2, ch)
    h_fwd, h_bwd = h[:, :, 0], h[:, :, 1]
    l1 = jnp.sum(jnp.abs(h_fwd), axis=0) + jnp.sum(jnp.abs(h_bwd[1:]), axis=0)
    kern = jnp.concatenate([h_fwd, jnp.zeros((1, HY_ORDER, ch), F32), h_bwd[1:][::-1]], axis=0) / l1
    return jnp.fft.rfft(kern, axis=0)


def _fft_long_conv(z, spec, skip):
    L = z.shape[1]
    zf = jnp.fft.rfft(z, n=2 * L, axis=1)
    y = jnp.fft.irfft(zf * spec[None], n=2 * L, axis=1)[:, :L]
    return y + skip * z


def _hyena_branch(pa, conv_w, conv_b, spec, skip):
    u = _short_conv3(pa, conv_w, conv_b).astype(F32)
    v, x1, x2 = jnp.split(u, 3, axis=-1)
    z = x1 * _fft_long_conv(v, spec[:, 0], skip[0])
    return x2 * _fft_long_conv(z, spec[:, 1], skip[1])


def _swa_project(pb, q_g, k_g, n_heads):
    B, L = pb.shape[:2]
    q, k, v = jnp.split(pb, [n_heads * HEAD_DIM, (n_heads + SWA_KV_HEADS) * HEAD_DIM], axis=-1)
    q = _rms_norm(q.reshape(B, L, n_heads, HEAD_DIM), q_g)
    k = _rms_norm(k.reshape(B, L, SWA_KV_HEADS, HEAD_DIM), k_g)
    return q, k, v.reshape(B, L, SWA_KV_HEADS, HEAD_DIM)


def _sink_softmax(s, sink):
    m = jnp.maximum(jnp.max(s, axis=-1, keepdims=True), sink)
    p = jnp.exp(s - m)
    return p / (jnp.sum(p, axis=-1, keepdims=True) + jnp.exp(sink - m))


def _swa_context(qc, kc, vc, sink):
    B, C, n_heads = qc.shape[:3]
    grp = n_heads // SWA_KV_HEADS
    qg = qc.reshape(B, C, SWA_KV_HEADS, grp, HEAD_DIM)
    s = jnp.einsum('bqhgd,bkhd->bhgqk', qg, kc).astype(F32) * HEAD_DIM ** -0.5
    p = _sink_softmax(s, sink.astype(F32).reshape(1, SWA_KV_HEADS, grp, 1, 1))
    o = jnp.einsum('bhgqk,bkhd->bqhgd', p.astype(vc.dtype), vc)
    return o.reshape(B, C, n_heads * HEAD_DIM)


def _swa_latent(q, k, v, kc, vc, sink):
    B, L, n_heads = q.shape[:3]
    grp = n_heads // SWA_KV_HEADS
    nb = L // SWA_BLOCK
    qb = q.reshape(B, nb, SWA_BLOCK, SWA_KV_HEADS, grp, HEAD_DIM)

    def band(t):
        tb = t.reshape(B, nb, SWA_BLOCK, SWA_KV_HEADS, HEAD_DIM)
        tp = jnp.pad(tb, ((0, 0), (1, 1), (0, 0), (0, 0), (0, 0)))
        return jnp.concatenate([tp[:, :-2], tp[:, 1:-1], tp[:, 2:]], axis=2)

    kb, vb = band(k), band(v)
    scale = HEAD_DIM ** -0.5
    s_loc = jnp.einsum('bnqhgd,bnkhd->bnhgqk', qb, kb).astype(F32) * scale
    s_ctx = jnp.einsum('bnqhgd,bchd->bnhgqc', qb, kc).astype(F32) * scale
    q_rel = jnp.arange(SWA_BLOCK)[:, None] + SWA_BLOCK
    k_rel = jnp.arange(3 * SWA_BLOCK)[None, :]
    k_abs = (jnp.arange(nb)[:, None, None] - 1) * SWA_BLOCK + k_rel[None]
    valid = (jnp.abs(q_rel - k_rel) <= SWA_WINDOW)[None] & (k_abs >= 0) & (k_abs < L)
    s_loc = jnp.where(valid[None, :, None, None], s_loc, NEG_INF)
    s = jnp.concatenate([s_loc, s_ctx], axis=-1)
    p = _sink_softmax(s, sink.astype(F32).reshape(1, 1, SWA_KV_HEADS, grp, 1, 1)).astype(v.dtype)
    o = (jnp.einsum('bnhgqk,bnkhd->bnqhgd', p[..., :3 * SWA_BLOCK], vb)
         + jnp.einsum('bnhgqc,bchd->bnqhgd', p[..., 3 * SWA_BLOCK:], vc))
    return o.reshape(B, L, n_heads * HEAD_DIM)


def _token_shift(x, reverse):
    if reverse:
        return jnp.pad(x, ((0, 0), (0, 1), (0, 0)))[:, 1:]
    return jnp.pad(x, ((0, 0), (1, 0), (0, 0)))[:, :-1]


def _head_l2norm(x, n_heads):
    B, L, C = x.shape
    xh = x.reshape(B, L, n_heads, HEAD_DIM)
    xh = xh * lax.rsqrt(jnp.sum(xh * xh, axis=-1, keepdims=True) + 1e-12)
    return xh.reshape(B, L, C)


def _rwkv_prepare(feats, reverse, width, mu, w0, w2, a0, a2, g2, k_k, k_a):
    n_heads = width // HEAD_DIM
    xs = feats + mu * (_token_shift(feats, reverse) - feats)
    splits = (width, 2 * width, 3 * width, 3 * width + RW_DECAY_RANK, 3 * width + RW_DECAY_RANK + RW_A_RANK)
    r, k, v, wd, ad, gd = jnp.split(xs, splits, axis=-1)
    logw = -jnp.exp(-jax.nn.softplus(-(w0 + jnp.tanh(wd) @ w2)) - 0.5)
    a = jax.nn.sigmoid(a0 + ad @ a2)
    g = jax.nn.sigmoid(gd) @ g2
    kk = _head_l2norm(k * k_k, n_heads)
    k = k * (1 + (a - 1) * k_a)
    return r, logw, kk, a, k, v, g


def _wkv7(r, logw, kk, a, k, v, z0, reverse):
    p, g, qh, yl = rwkv_chunk_prep(r, logw, kk, a, k, v, reverse=reverse)
    y, z_fin = rwkv_chunk_scan(p, g, qh, yl, z0, reverse=reverse)
    return z_fin, y


def _rwkv_readout(y, r, k, v, g, r_k, ln_w, ln_b):
    B, L, C = y.shape
    n_heads = C // HEAD_DIM
    yh = y.reshape(B, L, n_heads, HEAD_DIM)
    mean = jnp.mean(yh, axis=-1, keepdims=True)
    var = jnp.mean(jnp.square(yh - mean), axis=-1, keepdims=True)
    yn = ((yh - mean) * lax.rsqrt(var + RW_GN_EPS)).reshape(B, L, C) * ln_w + ln_b
    bonus = jnp.sum((r * k * r_k).reshape(B, L, n_heads, HEAD_DIM), axis=-1, keepdims=True)
    bonus = (bonus * v.reshape(B, L, n_heads, HEAD_DIM)).reshape(B, L, C)
    return (yn + bonus) * g


def _rwkv_branch(f_ctx, f_lat, width, mu, w0, w2, a0, a2, g2, k_k, k_a, r_k, ln_w, ln_b, need_ctx):
    B = f_lat.shape[0]
    n_heads = width // HEAD_DIM
    y_lat = 0.0
    y_ctx = 0.0 if need_ctx else None
    for d in range(2):
        rev = d == 1
        dir_args = (width, mu[d], w0[d], w2[d], a0[d], a2[d], g2[d], k_k, k_a)
        r_c, w_c, kk_c, a_c, k_c, v_c, g_c = _rwkv_prepare(f_ctx, rev, *dir_args)
        z0 = jnp.zeros((B, n_heads, HEAD_DIM, HEAD_DIM), F32)
        z_ctx, o_c = _wkv7(r_c, w_c, kk_c, a_c, k_c, v_c, z0, rev)
        r_l, w_l, kk_l, a_l, k_l, v_l, g_l = _rwkv_prepare(f_lat, rev, *dir_args)
        _, o_l = _wkv7(r_l, w_l, kk_l, a_l, k_l, v_l, z_ctx, rev)
        y_lat = y_lat + _rwkv_readout(o_l, r_l, k_l, v_l, g_l, r_k, ln_w, ln_b)
        if need_ctx:
            y_ctx = y_ctx + _rwkv_readout(o_c, r_c, k_c, v_c, g_c, r_k, ln_w, ln_b)
    return y_lat, y_ctx


def _rwkv_branch2(f_ctx, f_lat, B, L, C, width, mu, w0, w2, a0, a2, g2, k_k, k_a, r_k, ln_w, ln_b, need_ctx):
    n_heads = width // HEAD_DIM
    feat = mu.shape[-1]
    dr, ar = RW_DECAY_RANK, RW_A_RANK
    wt = jnp.zeros((2, feat - 3 * width, 3 * width), F32)
    wt = wt.at[:, :dr, :width].set(w2).at[:, dr:dr + ar, width:2 * width].set(a2).at[:, dr + ar:, 2 * width:].set(g2)
    row = lambda t: t.astype(F32).reshape(1, width)
    args = (mu.astype(F32).reshape(2, 1, feat), wt.astype(BF16), w0.astype(F32).reshape(2, 1, width),
            a0.astype(F32).reshape(2, 1, width), row(k_k), row(k_a))
    pre_c = rwkv_pre(f_ctx, *args, C, width)
    pre_l = rwkv_pre(f_lat, *args, L, width)
    read_l, read_c = [], []
    for d in range(2):
        rev = d == 1
        seq = lambda ts, n: [t.reshape(B, n, width) for t in ts[:6]]
        z0 = jnp.zeros((B, n_heads, HEAD_DIM, HEAD_DIM), F32)
        z_ctx, o_c = _wkv7(*seq(pre_c[d], C), z0, rev)
        _, o_l = _wkv7(*seq(pre_l[d], L), z_ctx, rev)
        pick = lambda o, ts, n: (o.reshape(B * n, width), ts[0], ts[4], ts[5], ts[6])
        read_l.append(pick(o_l, pre_l[d], L))
        read_c.append(pick(o_c, pre_c[d], C))
    tail = (row(r_k), row(ln_w), row(ln_b))
    y_lat = rwkv_readout(read_l[0], read_l[1], *tail)
    y_ctx = rwkv_readout(read_c[0], read_c[1], *tail) if need_ctx else None
    return y_lat, y_ctx


def _diff_project(pd, q_g, k_g):
    B, L, w3 = pd.shape
    n_heads = w3 // (3 * 2 * HEAD_DIM)
    q, k, v = jnp.split(pd, 3, axis=-1)
    q = _rms_norm(q.reshape(B, L, n_heads, 2, HEAD_DIM), q_g)
    k = _rms_norm(k.reshape(B, L, n_heads, 2, HEAD_DIM), k_g)
    return q, k, v.reshape(B, L, n_heads, 2 * HEAD_DIM)


def _diff_maps(q, k_all, v_all, lam):
    s = jnp.einsum('bqhid,bkhid->bhiqk', q, k_all).astype(F32) * HEAD_DIM ** -0.5
    p = jax.nn.softmax(s, axis=-1)
    a = p[:, :, 0] - lam * p[:, :, 1]
    return jnp.einsum('bhqk,bkhe->bqhe', a.astype(v_all.dtype), v_all)


def _diff_latent(q, k, v, kc, vc, lam):
    B, L, n_heads = q.shape[:3]
    nb = L // DF_BLOCK
    k_all = jnp.concatenate([kc, k], axis=1)
    v_all = jnp.concatenate([vc, v], axis=1)
    qb = jnp.moveaxis(q.reshape(B, nb, DF_BLOCK, n_heads, 2, HEAD_DIM), 1, 0)
    o = lax.map(lambda q_blk: _diff_maps(q_blk, k_all, v_all, lam), qb)
    return jnp.moveaxis(o, 0, 1).reshape(B, L, n_heads, 2 * HEAD_DIM)


def _diff_readout(o, sub_g, lam_init):
    B, L = o.shape[:2]
    return (_rms_norm(o, sub_g) * (1 - lam_init)).reshape(B, L, -1)


def _swa_branch(pb_l, pb_c, cos, sin, q_g, k_g, sink, need_ctx):
    B, L = pb_l.shape[:2]
    C = pb_c.shape[1]
    n_heads = pb_l.shape[-1] // HEAD_DIM - 2 * SWA_KV_HEADS
    grp = n_heads // SWA_KV_HEADS
    scale = HEAD_DIM ** -0.5
    q_l, k_l, v_l = _swa_project(pb_l, q_g, k_g, n_heads)
    q_l, k_l = _apply_rope(q_l, cos, sin), _apply_rope(k_l, cos, sin)
    q_c, k_c, v_c = _swa_project(pb_c, q_g, k_g, n_heads)

    def q_layout(q, n):
        return jnp.transpose((q * scale).astype(BF16).reshape(B, n, SWA_KV_HEADS, grp, HEAD_DIM), (0, 2, 3, 1, 4))

    def kv_layout(t):
        return jnp.transpose(t.astype(BF16), (0, 2, 1, 3))

    def o_layout(o, n):
        return jnp.transpose(o, (0, 3, 1, 2, 4)).reshape(B, n, n_heads * HEAD_DIM)

    sink_rows = jnp.repeat(sink.astype(F32).reshape(SWA_KV_HEADS, grp), SWA_BLOCK, axis=1)[..., None]
    kc, vc = kv_layout(k_c), kv_layout(v_c)
    y_l = o_layout(swa_attention(q_layout(q_l, L), kv_layout(k_l), kv_layout(v_l), kc, vc, sink_rows), L)
    y_c = None
    if need_ctx:
        y_c = o_layout(swa_context_attention(q_layout(q_c, C), kc, vc, sink_rows), C)
    return y_l, y_c


def _diff_branch(pd_l, pd_c, cos, sin, q_g, k_g, sub_g, lam, lam_init, need_ctx):
    B, L = pd_l.shape[:2]
    C = pd_c.shape[1]
    W = pd_l.shape[-1] // 3
    scale = HEAD_DIM ** -0.5 * math.log2(math.e)
    dq_l, dk_l, dv_l = _diff_project(pd_l, q_g, k_g)
    dq_l, dk_l = _apply_rope(dq_l, cos, sin), _apply_rope(dk_l, cos, sin)
    dq_c, dk_c, dv_c = _diff_project(pd_c, q_g, k_g)
    flat = lambda t, n: t.reshape(B, n, W)
    q_l = (flat(dq_l, L) * scale).astype(BF16)
    k_c, v_c = flat(dk_c, C).astype(BF16), flat(dv_c, C).astype(BF16)
    kt_c = jnp.transpose(k_c, (0, 2, 1))
    kt_all = jnp.concatenate([kt_c, jnp.transpose(flat(dk_l, L).astype(BF16), (0, 2, 1))], axis=2)
    v_all = jnp.concatenate([v_c, flat(dv_l, L).astype(BF16)], axis=1)
    sub = sub_g.astype(F32).reshape(1, 2 * HEAD_DIM)
    lam2 = lam.astype(F32).reshape(1, 1)
    y_l = diff_attention(q_l, kt_all, v_all, sub, lam2, 1.0 - lam_init)
    y_c = None
    if need_ctx:
        q_c = (flat(dq_c, C) * scale).astype(BF16)
        y_c = diff_attention(q_c, kt_c, v_c, sub, lam2, 1.0 - lam_init)
    return y_l, y_c


def _rope_lane_tables(rows, wb):
    cos, sin = _rope_tables(rows)
    reps = wb // (HEAD_DIM // 2)
    sign = jnp.where(jnp.arange(wb) % HEAD_DIM < HEAD_DIM // 2, -1.0, 1.0).astype(F32)
    return jnp.tile(cos, (1, reps)), jnp.tile(sin, (1, reps)) * sign


def _lane_params(parts, wb):
    cols = [jnp.tile(jnp.asarray(val, F32).reshape(-1), n // jnp.asarray(val).size) for val, n in parts]
    return jnp.concatenate(cols).reshape(-1, 1, wb)


def _swa_branch2(p_l, p_c, col0, B, L, C, tables, q_g, k_g, sink, need_ctx):
    n_heads = sink.shape[0]
    grp = n_heads // SWA_KV_HEADS
    qw, kw = n_heads * HEAD_DIM, SWA_KV_HEADS * HEAD_DIM
    wb = 2 * kw
    gain = _lane_params([(q_g, qw), (k_g, kw), (1.0, kw)], wb)
    mask = _lane_params([(1.0, qw), (1.0, kw), (0.0, kw)], wb)
    scale = _lane_params([(HEAD_DIM ** -0.5, qw), (1.0, kw), (1.0, kw)], wb)
    o_l = head_prep(p_l, col0, wb, gain, mask, scale, tables, L).reshape(B, L, qw + 2 * kw)
    o_c = head_prep(p_c, col0, wb, gain, mask, scale, None, B * C).reshape(B, C, qw + 2 * kw)

    def q_layout(o, n):
        return jnp.transpose(o[:, :, :qw].reshape(B, n, SWA_KV_HEADS, grp, HEAD_DIM), (0, 2, 3, 1, 4))

    def kv_layout(t, n):
        return jnp.transpose(t.reshape(B, n, SWA_KV_HEADS, HEAD_DIM), (0, 2, 1, 3))

    def o_layout(o, n):
        return jnp.transpose(o, (0, 3, 1, 2, 4)).reshape(B * n, qw)

    sink_rows = jnp.repeat(sink.astype(F32).reshape(SWA_KV_HEADS, grp), SWA_BLOCK, axis=1)[..., None]
    kc, vc = kv_layout(o_c[:, :, qw:qw + kw], C), kv_layout(o_c[:, :, qw + kw:], C)
    y_l = o_layout(swa_attention(q_layout(o_l, L), kv_layout(o_l[:, :, qw:qw + kw], L),
                                 kv_layout(o_l[:, :, qw + kw:], L), kc, vc, sink_rows), L)
    y_c = o_layout(swa_context_attention(q_layout(o_c, C), kc, vc, sink_rows), C) if need_ctx else None
    return y_l, y_c


def _diff_branch2(p_l, p_c, col0, B, L, C, tables, q_g, k_g, sub_g, lam, lam_init, need_ctx):
    W = tables[0].shape[1]
    hw = 2 * HEAD_DIM
    q_scale = HEAD_DIM ** -0.5 * math.log2(math.e)
    gain = _lane_params([(q_g, W), (k_g, W), (1.0, W)], W)
    mask = _lane_params([(1.0, W), (1.0, W), (0.0, W)], W)
    scale = _lane_params([(q_scale, W), (1.0, W), (1.0, W)], W)
    o_l = head_prep(p_l, col0, W, gain, mask, scale, tables, L).reshape(B, L, 3 * W)
    o_c = head_prep(p_c, col0, W, gain, mask, scale, None, B * C).reshape(B, C, 3 * W)
    kv_all = jnp.concatenate([o_c, o_l], axis=1)
    kt_all = jnp.transpose(kv_all[:, :, W:2 * W], (0, 2, 1))
    sub = sub_g.astype(F32).reshape(1, hw)
    lam2 = lam.astype(F32).reshape(1, 1)
    v_blk0 = 2 * W // hw
    y_l = diff_attention(o_l, 0, kt_all, kv_all, v_blk0, sub, lam2, 1.0 - lam_init).reshape(B * L, W)
    y_c = None
    if need_ctx:
        y_c = diff_attention(o_c, 0, kt_all[:, :, :C], o_c, v_blk0, sub, lam2, 1.0 - lam_init).reshape(B * C, W)
    return y_l, y_c


def _moe_route(logits_g, logits_e, n_tok):
    g_idx = jnp.argmax(logits_g, axis=-1)
    g_prob = jnp.take_along_axis(jax.nn.softmax(logits_g, axis=-1), g_idx[:, None], axis=-1)[:, 0]
    e_logits = logits_e.reshape(n_tok, MOE_GROUPS, MOE_PER_GROUP)
    e_logits = jnp.take_along_axis(e_logits, g_idx[:, None, None], axis=1)[:, 0]
    top_p, top_e = lax.top_k(jax.nn.softmax(e_logits, axis=-1), MOE_TOP_K)
    weights = g_prob[:, None] * top_p / jnp.sum(top_p, axis=-1, keepdims=True)
    flat_e = (g_idx[:, None] * MOE_PER_GROUP + top_e).reshape(-1).astype(jnp.int32)
    n_assign = n_tok * MOE_TOP_K
    order = jnp.argsort(flat_e).astype(jnp.int32)
    se = flat_e[order]
    counts = jnp.sum(flat_e[:, None] == jnp.arange(MOE_EXPERTS, dtype=jnp.int32)[None, :], axis=0, dtype=jnp.int32)
    padded = (counts + MOE_BLOCK - 1) // MOE_BLOCK * MOE_BLOCK
    pad_end = jnp.cumsum(padded)
    pad_start = pad_end - padded
    start = jnp.cumsum(counts) - counts
    dest = (pad_start[se] + jnp.arange(n_assign, dtype=jnp.int32) - start[se]).astype(jnp.int32)
    n_blocks = -(-n_assign // MOE_BLOCK) + MOE_EXPERTS
    P = n_blocks * MOE_BLOCK
    p = jnp.arange(P, dtype=jnp.int32)
    e_p = jnp.minimum(jnp.sum(p[:, None] >= pad_end[None, :], axis=1, dtype=jnp.int32), MOE_EXPERTS - 1)
    off = p - pad_start[e_p]
    src = jnp.clip(start[e_p] + off, 0, n_assign - 1)
    tok_buf = jnp.where(off < counts[e_p], order[src] // MOE_TOP_K, n_tok).astype(jnp.int32)
    blk_e = e_p[::MOE_BLOCK]
    n_used = (pad_end[-1] // MOE_BLOCK).astype(jnp.int32).reshape(1)
    _, slot = lax.sort((order, dest), num_keys=1)
    return tok_buf, blk_e, n_used, slot.reshape(n_tok, MOE_TOP_K), weights


def _hier_moe(tokens_bf16, logits, w1, w3, w2):
    n_tok, d = tokens_bf16.shape
    logits_g = logits[:, :MOE_GROUPS]
    logits_e = logits[:, MOE_GROUPS:MOE_GROUPS + MOE_EXPERTS]
    tok_buf, blk_e, n_used, slot, weights = _moe_route(logits_g, logits_e, n_tok)
    x_pad = jnp.concatenate([tokens_bf16, jnp.zeros((1, d), BF16)], axis=0)
    xb = x_pad[tok_buf]
    yb = moe_experts(xb, blk_e, n_used, w1, w3, w2)
    wts = weights.astype(F32)
    return (yb[slot[:, 0]].astype(F32) * wts[:, 0:1] + yb[slot[:, 1]].astype(F32) * wts[:, 1:2])


def _modulate(x, g, shift, scale):
    return _rms_norm(x, g) * (1 + scale) + shift


def kernel(x, c, ctx, c_ctx, mod_w, mod_b, norm1_g, norm2_g, w_in, hy_conv_w, hy_conv_b, hy_f_w1, hy_f_b1, hy_f_w2, hy_f_b2, hy_f_w3, hy_f_b3, hy_f_freq, hy_f_decay, hy_skip, swa_q_g, swa_k_g, swa_sink, rw_mu, rw_w0, rw_w2, rw_a0, rw_a2, rw_g2, rw_k_k, rw_k_a, rw_r_k, rw_ln_w, rw_ln_b, df_q_g, df_k_g, df_lq1, df_lk1, df_lq2, df_lk2, df_sub_g, w_gate, b_gate, w_branch, w_out, moe_rg_w, moe_rg_b, moe_re_w, moe_re_b, moe_w1, moe_w3, moe_w2):
    B, L, D = x.shape
    C = ctx.shape[1]
    depth = mod_w.shape[0]
    bw = D // N_BRANCH
    a_cols = 3 * bw
    swa_heads = bw // HEAD_DIM
    b_cols = (swa_heads + 2 * SWA_KV_HEADS) * HEAD_DIM
    c_cols = 3 * bw + RW_DECAY_RANK + RW_A_RANK + RW_G_RANK
    in_splits = (a_cols, a_cols + b_cols, a_cols + b_cols + c_cols)
    a_off, b_off, d_off = c_cols, c_cols + a_cols, c_cols + a_cols + b_cols
    rows = L // GRID_W
    swa_tables = _rope_lane_tables(rows, 2 * SWA_KV_HEADS * HEAD_DIM)
    diff_tables = _rope_lane_tables(rows, bw)
    hp = lax.Precision.HIGHEST

    x_lat = x.reshape(B * L, D)
    x_ctx = ctx.reshape(B * C, D)
    for l in range(depth):
        need_ctx = l < depth - 1
        mod_lat = jnp.dot(jax.nn.silu(c), mod_w[l], precision=hp) + mod_b[l]
        mod_ctx = jnp.dot(jax.nn.silu(c_ctx)[None], mod_w[l], precision=hp) + mod_b[l]
        sh1, sc1, g1, sh2, sc2, g2 = [t[:, None, :] for t in jnp.split(mod_lat, 6, axis=-1)]
        csh1, csc1, cg1, csh2, csc2, cg2 = [t[:, None, :] for t in jnp.split(mod_ctx, 6, axis=-1)]

        w_l = w_in[l]
        w_in_b = jnp.concatenate([w_l[:, in_splits[1]:in_splits[2]], w_l[:, :in_splits[1]], w_l[:, in_splits[2]:]],
                                 axis=1).astype(BF16)
        p2_lat, h_lat = norm_proj(x_lat, norm1_g[l][None], sc1, sh1, w_in_b, rows_per_mod=L)
        p2_ctx, h_ctx = norm_proj(x_ctx, norm1_g[l][None], csc1, csh1, w_in_b, rows_per_mod=B * C)
        pa_l = p2_lat.reshape(B, L, -1)[..., a_off:a_off + a_cols]
        pa_c = p2_ctx.reshape(B, C, -1)[..., a_off:a_off + a_cols]

        hy_args = (hy_conv_w[l], hy_conv_b[l], hy_f_w1[l], hy_f_b1[l], hy_f_w2[l], hy_f_b2[l], hy_f_w3[l],
                   hy_f_b3[l], hy_f_freq[l], hy_f_decay[l], hy_skip[l])
        ya_l = _hyena(pa_l, *hy_args)

        yb_l, yb_c = _swa_branch2(p2_lat, p2_ctx, b_off, B, L, C, swa_tables, swa_q_g[l], swa_k_g[l],
                                  swa_sink[l], need_ctx)

        yc_l, yc_c = _rwkv_branch2(p2_ctx, p2_lat, B, L, C, bw, rw_mu[l], rw_w0[l], rw_w2[l], rw_a0[l], rw_a2[l],
                                   rw_g2[l], rw_k_k[l], rw_k_a[l], rw_r_k[l], rw_ln_w[l], rw_ln_b[l], need_ctx)

        lam_init = 0.8 - 0.6 * math.exp(-0.3 * l)
        lam = (jnp.exp(jnp.sum(df_lq1[l] * df_lk1[l])) - jnp.exp(jnp.sum(df_lq2[l] * df_lk2[l])) + lam_init)
        yd_l, yd_c = _diff_branch2(p2_lat, p2_ctx, d_off, B, L, C, diff_tables, df_q_g[l], df_k_g[l],
                                   df_sub_g[l], lam, lam_init, need_ctx)

        wg_b = w_gate[l].astype(BF16)
        bg = b_gate[l][:, None, :]
        wb_b = w_branch[l].astype(BF16)
        wo_b = w_out[l].astype(BF16)
        ys_l = [t.reshape(B * L, bw) for t in (ya_l, yb_l, yc_l, yd_l)]
        acc_l = merge_gated(h_lat, ys_l, wg_b, bg, wb_b)
        x_lat = resid_proj(x_lat, acc_l, wo_b, g1, rows_per_mod=L)
        if need_ctx:
            ya_c = _hyena(pa_c, *hy_args)
            ys_c = [t.reshape(B * C, bw) for t in (ya_c, yb_c, yc_c, yd_c)]
            acc_c = merge_gated(h_ctx, ys_c, wg_b, bg, wb_b)
            x_ctx = resid_proj(x_ctx, acc_c, wo_b, cg1, rows_per_mod=B * C)

        w1_b, w3_b, w2_b = moe_w1[l].astype(BF16), moe_w3[l].astype(BF16), moe_w2[l].astype(BF16)
        n_route = MOE_GROUPS + MOE_EXPERTS
        wr = jnp.zeros((D, ROUTER_LANES), F32).at[:, :MOE_GROUPS].set(moe_rg_w[l]).at[:, MOE_GROUPS:n_route].set(moe_re_w[l])
        br = jnp.zeros((1, ROUTER_LANES), F32).at[0, :MOE_GROUPS].set(moe_rg_b[l]).at[0, MOE_GROUPS:n_route].set(moe_re_b[l])
        g2n = norm2_g[l][None]
        hm_lat, lg_lat = norm_route(x_lat, g2n, sc2, sh2, wr, br, rows_per_mod=L)
        if need_ctx:
            hm_ctx, lg_ctx = norm_route(x_ctx, g2n, csc2, csh2, wr, br, rows_per_mod=B * C)
            out = _hier_moe(jnp.concatenate([hm_ctx, hm_lat], axis=0), jnp.concatenate([lg_ctx, lg_lat], axis=0),
                            w1_b, w3_b, w2_b)
            x_ctx = x_ctx + (cg2 * out[:B * C].reshape(1, B * C, D)).reshape(B * C, D)
            x_lat = x_lat + (g2 * out[B * C:].reshape(B, L, D)).reshape(B * L, D)
        else:
            out = _hier_moe(hm_lat, lg_lat, w1_b, w3_b, w2_b)
            x_lat = x_lat + (g2 * out.reshape(B, L, D)).reshape(B * L, D)
    return x_lat.reshape(B, L, D)
```
